```python
import functools
import jax, jax.numpy as jnp
from jax import lax
import numpy as np

D_MODEL = 1024
BATCH = 1
SEQ = 16384
DEPTH = 1
DEC_BATCH = 128
DEC_SEQ = 8
PAST_LEN = 16384
PAGE_SIZE = 128

HEAD_DIM = 64
ATTN_HEADS = 8
KV_HEADS = 2
GROUP = ATTN_HEADS // KV_HEADS
ATTN_WIDTH = ATTN_HEADS * HEAD_DIM
KV_WIDTH = KV_HEADS * HEAD_DIM
WINDOW = 128
BLOCK = WINDOW
M_HEADS = 4
M_HEAD_DIM = 128
M_WIDTH = M_HEADS * M_HEAD_DIM
M_CHUNK = 128
MIX_WIDTH = ATTN_WIDTH + M_WIDTH
IN_WIDTH = ATTN_WIDTH + 2 * KV_WIDTH + 4 * M_WIDTH + 2 * M_HEADS
D_FF = -(-8 * D_MODEL // (3 * 256)) * 256
EPS = 1e-6

kernel_name = "hymba_swa_sink_mlstm_decode_step"


def _rms_norm(x, g):
    xf = x.astype(jnp.float32)
    y = xf * lax.rsqrt(jnp.mean(xf * xf, axis=-1, keepdims=True) + EPS)
    return (y * g.astype(jnp.float32)).astype(x.dtype)


def _project(h, w_in):
    p = jnp.einsum('bsd,de->bse', h, w_in)
    sizes = (ATTN_WIDTH, KV_WIDTH, KV_WIDTH, M_WIDTH, M_WIDTH, M_WIDTH, M_WIDTH, M_HEADS, M_HEADS)
    offs = np.cumsum(sizes)[:-1].tolist()
    return jnp.split(p, offs, axis=-1)


def _sink_softmax_attend(q, k, v, mask, sink):
    s = jnp.einsum('bnqhgd,bnkhd->bnhgqk', q, k).astype(jnp.float32) * (HEAD_DIM ** -0.5)
    s = jnp.where(mask[None, :, None, None], s, -jnp.inf)
    sink_l = sink.astype(jnp.float32).reshape(KV_HEADS, GROUP)[None, None, :, :, None, None]
    m = jnp.maximum(jnp.max(s, axis=-1, keepdims=True), sink_l)
    p = jnp.exp(s - m)
    p = p / (jnp.sum(p, axis=-1, keepdims=True) + jnp.exp(sink_l - m))
    return jnp.einsum('bnhgqk,bnkhd->bnqhgd', p.astype(v.dtype), v)


def _swa_prompt(q, k, v, sink):
    B, S = q.shape[:2]
    nb = S // BLOCK
    qb = q.reshape(B, nb, BLOCK, KV_HEADS, GROUP, HEAD_DIM)
    kb = k.reshape(B, nb, BLOCK, KV_HEADS, HEAD_DIM)
    vb = v.reshape(B, nb, BLOCK, KV_HEADS, HEAD_DIM)
    kk = jnp.concatenate([jnp.concatenate([jnp.zeros_like(kb[:, :1]), kb[:, :-1]], axis=1), kb], axis=2)
    vv = jnp.concatenate([jnp.concatenate([jnp.zeros_like(vb[:, :1]), vb[:, :-1]], axis=1), vb], axis=2)
    blk = jnp.arange(nb)[:, None]
    qpos = blk * BLOCK + jnp.arange(BLOCK)[None, :]
    kpos = (blk - 1) * BLOCK + jnp.arange(2 * BLOCK)[None, :]
    diff = qpos[:, :, None] - kpos[:, None, :]
    mask = (diff >= 0) & (diff < WINDOW) & (kpos[:, None, :] >= 0)
    o = _sink_softmax_attend(qb, kk, vv, mask, sink)
    return o.reshape(B, S, ATTN_WIDTH), k[:, S - WINDOW:], v[:, S - WINDOW:]


def _swa_sample(cache_k, cache_v, q, k, v, sink):
    B, T = q.shape[:2]
    kk = jnp.concatenate([cache_k, k], axis=1)
    vv = jnp.concatenate([cache_v, v], axis=1)
    diff = (WINDOW + jnp.arange(T))[:, None] - jnp.arange(WINDOW + T)[None, :]
    mask = ((diff >= 0) & (diff < WINDOW))[None]
    o = _sink_softmax_attend(q[:, None], kk[:, None], vv[:, None], mask, sink)
    return o.reshape(B, T, ATTN_WIDTH), kk[:, T:], vv[:, T:]


def _mlstm_chunk(carry, xs):
    C, n, m = carry
    q, k, v, ig, lf = xs
    L = q.shape[2]
    b = jnp.cumsum(lf, axis=-1)
    causal = jnp.tril(jnp.ones((L, L), dtype=bool))
    D = jnp.where(causal, b[..., :, None] - b[..., None, :] + ig[..., None, :], -jnp.inf)
    inter = b + m[..., None]
    m_t = jnp.maximum(inter, jnp.max(D, axis=-1))
    w_intra = jnp.exp(D - m_t[..., None])
    w_inter = jnp.exp(inter - m_t)
    sqk = w_intra * jnp.einsum('bhtd,bhsd->bhts', q, k)
    num = w_inter[..., None] * jnp.einsum('bhtd,bhde->bhte', q, C) + jnp.einsum('bhts,bhse->bhte', sqk, v)
    nq = w_inter * jnp.einsum('bhtd,bhd->bht', q, n) + jnp.sum(sqk, axis=-1)
    h = num / jnp.maximum(jnp.abs(nq), jnp.exp(-m_t))[..., None]
    m_end = m_t[..., -1]
    dec = jnp.exp(b[..., -1] + m - m_end)
    w_end = jnp.exp(b[..., -1:] - b + ig - m_end[..., None])
    C_new = dec[..., None, None] * C + jnp.einsum('bhs,bhsd,bhse->bhde', w_end, k, v)
    n_new = dec[..., None] * n + jnp.einsum('bhs,bhsd->bhd', w_end, k)
    return (C_new, n_new, m_end), h


def _mlstm(q, k, v, ig, lf, C0, n0, m0, chunk):
    B, S, H, Dh = q.shape
    nc = S // chunk

    def chunks(a):
        a = a.reshape((B, nc, chunk) + a.shape[2:])
        return jnp.swapaxes(jnp.moveaxis(a, 1, 0), 2, 3)

    xs = (chunks(q), chunks(k), chunks(v), chunks(ig), chunks(lf))
    carry0 = (C0.astype(jnp.float32), n0.astype(jnp.float32), m0.astype(jnp.float32))
    (C, n, m), h = lax.scan(_mlstm_chunk, carry0, xs)
    h = jnp.moveaxis(jnp.swapaxes(h, 2, 3), 0, 1).reshape(B, S, H, Dh)
    return h, C, n, m


def _block(x, attn_fn, C0, n0, m0, chunk, w_in, b_i, b_f, attn_sink, m_norm, w_out,
           g_pre_mix, g_post_mix, g_pre_ffn, g_post_ffn, w_gate, w_up, w_down):
    B, S = x.shape[:2]
    f32 = jnp.float32
    h = _rms_norm(x, g_pre_mix)
    qa, ka, va, qm, km, vm, om, ip, fp = _project(h, w_in)
    attn_o, k_state, v_state = attn_fn(qa.reshape(B, S, KV_HEADS, GROUP, HEAD_DIM),
                                       ka.reshape(B, S, KV_HEADS, HEAD_DIM),
                                       va.reshape(B, S, KV_HEADS, HEAD_DIM), attn_sink)
    qm = qm.reshape(B, S, M_HEADS, M_HEAD_DIM).astype(f32)
    km = km.reshape(B, S, M_HEADS, M_HEAD_DIM).astype(f32) * (M_HEAD_DIM ** -0.5)
    vm = vm.reshape(B, S, M_HEADS, M_HEAD_DIM).astype(f32)
    ig = ip.astype(f32) + b_i.astype(f32)
    lf = jax.nn.log_sigmoid(fp.astype(f32) + b_f.astype(f32))
    hm, C, n, m = _mlstm(qm, km, vm, ig, lf, C0, n0, m0, chunk)
    hm = _rms_norm(hm, m_norm.reshape(M_HEADS, M_HEAD_DIM)).reshape(B, S, M_WIDTH)
    hm = jax.nn.sigmoid(om.astype(f32)) * hm
    mix = jnp.concatenate([attn_o.astype(x.dtype), hm.astype(x.dtype)], axis=-1)
    x = x + _rms_norm(jnp.einsum('bse,ed->bsd', mix, w_out), g_post_mix)
    f = _rms_norm(x, g_pre_ffn)
    ff = jax.nn.silu(jnp.einsum('bsd,df->bsf', f, w_gate)) * jnp.einsum('bsd,df->bsf', f, w_up)
    x = x + _rms_norm(jnp.einsum('bsf,fd->bsd', ff, w_down), g_post_ffn)
    return x, (k_state, v_state, C.astype(C0.dtype), n.astype(n0.dtype), m.astype(m0.dtype))


def setup_inputs(seed: int = 0) -> dict:
    key = jax.random.key(seed)
    ks = jax.random.split(key, 20)

    def nrm(k, shape, scale):
        return jax.random.normal(k, shape, jnp.float32) * scale

    return {
        'x_prompt': nrm(ks[0], (BATCH, SEQ, D_MODEL), 1.0),
        'x_sample': nrm(ks[1], (DEC_BATCH, DEC_SEQ, D_MODEL), 1.0),
        'cache_k': nrm(ks[2], (DEPTH, DEC_BATCH, WINDOW, KV_HEADS, HEAD_DIM), 1.0),
        'cache_v': nrm(ks[3], (DEPTH, DEC_BATCH, WINDOW, KV_HEADS, HEAD_DIM), 1.0),
        'state_C': nrm(ks[4], (DEPTH, DEC_BATCH, M_HEADS, M_HEAD_DIM, M_HEAD_DIM), 0.1),
        'state_n': nrm(ks[5], (DEPTH, DEC_BATCH, M_HEADS, M_HEAD_DIM), 0.1),
        'state_m': nrm(ks[6], (DEPTH, DEC_BATCH, M_HEADS), 0.5),
        'w_in': nrm(ks[7], (DEPTH, D_MODEL, IN_WIDTH), D_MODEL ** -0.5),
        'b_i': nrm(ks[8], (DEPTH, M_HEADS), 0.1),
        'b_f': 3.0 + nrm(ks[9], (DEPTH, M_HEADS), 0.5),
        'attn_sink': nrm(ks[10], (DEPTH, ATTN_HEADS), 0.5),
        'm_norm': 1.0 + nrm(ks[11], (DEPTH, M_WIDTH), 0.02),
        'w_out': nrm(ks[12], (DEPTH, MIX_WIDTH, D_MODEL), MIX_WIDTH ** -0.5),
        'g_pre_mix': 1.0 + nrm(ks[13], (DEPTH, D_MODEL), 0.02),
        'g_post_mix': 1.0 + nrm(ks[14], (DEPTH, D_MODEL), 0.02),
        'g_pre_ffn': 1.0 + nrm(ks[15], (DEPTH, D_MODEL), 0.02),
        'g_post_ffn': 1.0 + nrm(ks[16], (DEPTH, D_MODEL), 0.02),
        'w_gate': nrm(ks[17], (DEPTH, D_MODEL, D_FF), D_MODEL ** -0.5),
        'w_up': nrm(ks[18], (DEPTH, D_MODEL, D_FF), D_MODEL ** -0.5),
        'w_down': nrm(ks[19], (DEPTH, D_FF, D_MODEL), D_FF ** -0.5),
    }


def reference(x_prompt, x_sample, cache_k, cache_v, state_C, state_n, state_m,
              w_in, b_i, b_f, attn_sink, m_norm, w_out,
              g_pre_mix, g_post_mix, g_pre_ffn, g_post_ffn, w_gate, w_up, w_down):
    xp, xs = x_prompt, x_sample
    prompt_states, sample_states = [], []
    for l in range(DEPTH):
        params = (w_in[l], b_i[l], b_f[l], attn_sink[l], m_norm[l], w_out[l],
                  g_pre_mix[l], g_post_mix[l], g_pre_ffn[l], g_post_ffn[l], w_gate[l], w_up[l], w_down[l])
        Bp = xp.shape[0]
        C0 = jnp.zeros((Bp, M_HEADS, M_HEAD_DIM, M_HEAD_DIM), jnp.float32)
        n0 = jnp.zeros((Bp, M_HEADS, M_HEAD_DIM), jnp.float32)
        m0 = jnp.zeros((Bp, M_HEADS), jnp.float32)
        xp, sp = _block(xp, _swa_prompt, C0, n0, m0, M_CHUNK, *params)
        xs, ss = _block(xs, functools.partial(_swa_sample, cache_k[l], cache_v[l]),
                        state_C[l], state_n[l], state_m[l], xs.shape[1], *params)
        prompt_states.append(sp)
        sample_states.append(ss)

    def stack(states, i):
        return jnp.stack([s[i] for s in states], axis=0)

    k_prompt, v_prompt = stack(prompt_states, 0), stack(prompt_states, 1)
    C_prompt, n_prompt, m_prompt = stack(prompt_states, 2), stack(prompt_states, 3), stack(prompt_states, 4)
    k_sample, v_sample = stack(sample_states, 0), stack(sample_states, 1)
    C_sample, n_sample, m_sample = stack(sample_states, 2), stack(sample_states, 3), stack(sample_states, 4)
    return (xp, xs, k_prompt, v_prompt, C_prompt, n_prompt, m_prompt,
            k_sample, v_sample, C_sample, n_sample, m_sample)
```

```python
import functools

import jax
import jax.numpy as jnp
from jax import lax
from jax.experimental import pallas as pl
from jax.experimental.pallas import tpu as pltpu

F32 = jnp.float32
BF16 = jnp.bfloat16
HIGHEST = lax.Precision.HIGHEST

D_MODEL = 1024
HEAD_DIM = 64
ATTN_HEADS = 8
KV_HEADS = 2
ATTN_WIDTH = ATTN_HEADS * HEAD_DIM
KV_WIDTH = KV_HEADS * HEAD_DIM
WINDOW = 128
M_HEADS = 4
M_HEAD_DIM = 128
M_WIDTH = M_HEADS * M_HEAD_DIM
D_FF = 2816
EPS = 1e-6

LANES = 128
GATE_PAD = LANES
IN_MAIN = ATTN_WIDTH + 2 * KV_WIDTH + 4 * M_WIDTH
IN_PAD = IN_MAIN + GATE_PAD
VMEM_LIMIT = 56 * 1024 * 1024

TOKEN_TILE = 512
ATTN_BLOCKS = 4
MLSTM_CHUNKS = 2
FFN_CHUNK = 1408
SAMPLE_ATTN_BATCH = 8
SAMPLE_MLSTM_BATCH = 16


def _rms(x, g):
    return x * lax.rsqrt(jnp.mean(x * x, axis=-1, keepdims=True) + EPS) * g


def _const_spec(shape):
    nd = len(shape)
    return pl.BlockSpec(shape, lambda i: (0,) * nd, pipeline_mode=pl.Buffered(1))


def _params(semantics):
    return pltpu.CompilerParams(dimension_semantics=semantics, vmem_limit_bytes=VMEM_LIMIT)


def _inproj_kernel(x_ref, g_ref, w_ref, qa_ref, ka_ref, va_ref, qm_ref, km_ref, vm_ref, om_ref, gt_ref):
    h = _rms(x_ref[...], g_ref[...]).astype(BF16)

    def proj(off, n):
        return jnp.dot(h, w_ref[:, off:off + n], preferred_element_type=F32)

    off = 0
    qa_ref[...] = (proj(off, ATTN_WIDTH) * (HEAD_DIM ** -0.5)).astype(qa_ref.dtype)
    off += ATTN_WIDTH
    ka_ref[...] = proj(off, KV_WIDTH)
    off += KV_WIDTH
    va_ref[...] = proj(off, KV_WIDTH)
    off += KV_WIDTH
    qm_ref[...] = proj(off, M_WIDTH).astype(qm_ref.dtype)
    off += M_WIDTH
    km_ref[...] = (proj(off, M_WIDTH) * (M_HEAD_DIM ** -0.5)).astype(km_ref.dtype)
    off += M_WIDTH
    vm_ref[...] = proj(off, M_WIDTH).astype(vm_ref.dtype)
    off += M_WIDTH
    om_ref[...] = proj(off, M_WIDTH)
    off += M_WIDTH
    gt_ref[...] = proj(off, GATE_PAD)


def _inproj(x, g_pre, w_pad, act_dtype):
    m = x.shape[0]
    tm = TOKEN_TILE
    row = lambda n: pl.BlockSpec((tm, n), lambda i: (i, 0))
    widths = (ATTN_WIDTH, KV_WIDTH, KV_WIDTH, M_WIDTH, M_WIDTH, M_WIDTH, M_WIDTH, GATE_PAD)
    dtypes = (act_dtype, F32, F32, act_dtype, act_dtype, act_dtype, F32, F32)
    return pl.pallas_call(
        _inproj_kernel,
        grid=(m // tm,),
        in_specs=[row(D_MODEL), _const_spec((1, D_MODEL)), _const_spec((D_MODEL, IN_PAD))],
        out_specs=[row(n) for n in widths],
        out_shape=[jax.ShapeDtypeStruct((m, n), d) for n, d in zip(widths, dtypes)],
        compiler_params=_params(("parallel",)),
        name="inproj",
    )(x, g_pre, w_pad)


def _split_kv(x, kv):
    lane = lax.broadcasted_iota(jnp.int32, x.shape, 1)
    lo = lane < HEAD_DIM
    xr = pltpu.roll(x, HEAD_DIM, axis=1)
    zero = jnp.zeros_like(x)
    if kv == 0:
        a, b = jnp.where(lo, x, zero), jnp.where(lo, zero, xr)
    else:
        a, b = jnp.where(lo, xr, zero), jnp.where(lo, zero, x)
    return a.astype(BF16), b.astype(BF16)


def _blockdiag(x0, x1, kv):
    a0, b0 = _split_kv(x0, kv)
    a1, b1 = _split_kv(x1, kv)
    return jnp.concatenate([a0, b0, a1, b1], axis=0)


def _pair_attention(q2, kbd, vbd, mask0, mask1, sink_a, sink_b):
    s = lax.dot_general(q2, kbd, (((1,), (1,)), ((), ())), preferred_element_type=F32)
    s0 = jnp.where(mask0, s[:, :2 * LANES], -jnp.inf)
    s1 = jnp.where(mask1, s[:, 2 * LANES:], -jnp.inf)

    def stats(c0, c1, sink):
        m = jnp.maximum(jnp.maximum(jnp.max(c0, axis=-1, keepdims=True), jnp.max(c1, axis=-1, keepdims=True)), sink)
        p0 = jnp.exp(c0 - m)
        p1 = jnp.exp(c1 - m)
        den = jnp.sum(p0, axis=-1, keepdims=True) + jnp.sum(p1, axis=-1, keepdims=True) + jnp.exp(sink - m)
        return p0, p1, 1.0 / den

    pa0, pa1, ra = stats(s0[:, :LANES], s1[:, :LANES], sink_a)
    pb0, pb1, rb = stats(s0[:, LANES:], s1[:, LANES:], sink_b)
    pcat = jnp.concatenate([pa0, pb0, pa1, pb1], axis=1).astype(BF16)
    o = jnp.dot(pcat, vbd, preferred_element_type=F32)
    lane = lax.broadcasted_iota(jnp.int32, o.shape, 1)
    return o * jnp.where(lane < HEAD_DIM, ra, rb)


def _attn_prompt_kernel(sink_ref, q_ref, k_ref, v_ref, kp_ref, vp_ref, o_ref):
    first = pl.program_id(0) == 0
    r = lax.broadcasted_iota(jnp.int32, (WINDOW, 2 * LANES), 0)
    c = lax.broadcasted_iota(jnp.int32, (WINDOW, 2 * LANES), 1) & (LANES - 1)
    own = c <= r
    prev = c > r
    prev_first = c > r + jnp.where(first, jnp.int32(LANES), jnp.int32(0))
    for blk in range(ATTN_BLOCKS):
        rows = slice(blk * WINDOW, (blk + 1) * WINDOW)
        k_own, v_own = k_ref[rows, :], v_ref[rows, :]
        if blk == 0:
            k_prev, v_prev, mprev = kp_ref[...], vp_ref[...], prev_first
        else:
            prows = slice((blk - 1) * WINDOW, blk * WINDOW)
            k_prev, v_prev, mprev = k_ref[prows, :], v_ref[prows, :], prev
        for kv in range(KV_HEADS):
            kbd = _blockdiag(k_own, k_prev, kv)
            vbd = _blockdiag(v_own, v_prev, kv)
            for jj in range(2):
                j = 2 * kv + jj
                cols = slice(j * LANES, (j + 1) * LANES)
                o = _pair_attention(q_ref[rows, cols], kbd, vbd, own, mprev,
                                    sink_ref[0, 2 * j], sink_ref[0, 2 * j + 1])
                o_ref[rows, cols] = o.astype(o_ref.dtype)


def _attn_prompt(sink, qa, ka, va):
    m = qa.shape[0]
    tm = ATTN_BLOCKS * WINDOW
    row = lambda n: pl.BlockSpec((tm, n), lambda i: (i, 0))
    prev = pl.BlockSpec((WINDOW, KV_WIDTH), lambda i: (jnp.maximum(i * ATTN_BLOCKS - 1, 0), 0))
    return pl.pallas_call(
        _attn_prompt_kernel,
        grid=(m // tm,),
        in_specs=[pl.BlockSpec(memory_space=pltpu.SMEM), row(ATTN_WIDTH), row(KV_WIDTH), row(KV_WIDTH), prev, prev],
        out_specs=row(ATTN_WIDTH),
        out_shape=jax.ShapeDtypeStruct((m, ATTN_WIDTH), BF16),
        compiler_params=_params(("parallel",)),
        name="attn_prompt",
    )(sink, qa, ka, va, ka, va)


def _attn_sample_kernel(sink_ref, q_ref, kn_ref, vn_ref, ck_ref, cv_ref, o_ref, ko_ref, vo_ref, *, dec_seq):
    t = dec_seq
    rows16 = 2 * t
    r = lax.broadcasted_iota(jnp.int32, (rows16, 2 * LANES), 0) & (t - 1)
    c = lax.broadcasted_iota(jnp.int32, (rows16, 2 * LANES), 1) & (LANES - 1)
    mask_cache = c > r
    mask_new = c <= r
    pad = jnp.zeros((WINDOW - t, KV_WIDTH), F32)
    outs = [[] for _ in range(ATTN_HEADS // 2)]
    for b in range(SAMPLE_ATTN_BATCH):
        rows = slice(b * t, (b + 1) * t)
        ck, cv = ck_ref[b], cv_ref[b]
        kn, vn = kn_ref[rows, :], vn_ref[rows, :]
        ko_ref[b] = jnp.concatenate([ck[t:, :], kn], axis=0)
        vo_ref[b] = jnp.concatenate([cv[t:, :], vn], axis=0)
        kn_pad = jnp.concatenate([kn, pad], axis=0)
        vn_pad = jnp.concatenate([vn, pad], axis=0)
        q = q_ref[rows, :]
        for kv in range(KV_HEADS):
            kbd = _blockdiag(ck, kn_pad, kv)
            vbd = _blockdiag(cv, vn_pad, kv)
            j0, j1 = 2 * kv, 2 * kv + 1
            q2 = jnp.concatenate([q[:, j0 * LANES:(j0 + 1) * LANES], q[:, j1 * LANES:(j1 + 1) * LANES]], axis=0)
            ri = lax.broadcasted_iota(jnp.int32, (rows16, 1), 0)
            sink_a = jnp.where(ri < t, sink_ref[0, 2 * j0], sink_ref[0, 2 * j1])
            sink_b = jnp.where(ri < t, sink_ref[0, 2 * j0 + 1], sink_ref[0, 2 * j1 + 1])
            o = _pair_attention(q2.astype(BF16), kbd, vbd, mask_cache, mask_new, sink_a, sink_b)
            outs[j0].append(o[:t, :])
            outs[j1].append(o[t:, :])
    for j, parts in enumerate(outs):
        o_ref[:, j * LANES:(j + 1) * LANES] = jnp.concatenate(parts, axis=0).astype(o_ref.dtype)


def _attn_sample(sink, qa, ka, va, cache_k, cache_v, dec_seq):
    nb = cache_k.shape[0]
    bb = SAMPLE_ATTN_BATCH
    row = lambda n: pl.BlockSpec((bb * dec_seq, n), lambda i: (i, 0))
    cache = pl.BlockSpec((bb, WINDOW, KV_WIDTH), lambda i: (i, 0, 0))
    return pl.pallas_call(
        functools.partial(_attn_sample_kernel, dec_seq=dec_seq),
        grid=(nb // bb,),
        in_specs=[pl.BlockSpec(memory_space=pltpu.SMEM), row(ATTN_WIDTH), row(KV_WIDTH), row(KV_WIDTH), cache, cache],
        out_specs=[row(ATTN_WIDTH), cache, cache],
        out_shape=[jax.ShapeDtypeStruct((nb * dec_seq, ATTN_WIDTH), BF16),
                   jax.ShapeDtypeStruct(cache_k.shape, F32), jax.ShapeDtypeStruct(cache_v.shape, F32)],
        compiler_params=_params(("parallel",)),
        name="attn_sample",
    )(sink, qa, ka, va, cache_k, cache_v)


def _gates(g_blk, bias_row, cum):
    pre = g_blk + bias_row
    lane = lax.broadcasted_iota(jnp.int32, pre.shape, 1)
    a = jnp.where(lane < M_HEADS, pre, jax.nn.log_sigmoid(pre))
    b = jnp.dot(cum, a, precision=HIGHEST, preferred_element_type=F32)
    return a, b, a.T, b.T


def _col(x, j):
    return jnp.broadcast_to(x[:, j:j + 1], x.shape)


def _mlstm_intra(q, k, v, a, b, at, bt, h, mask, m_prev, last):
    bc, ic = _col(b, M_HEADS + h), _col(a, h)
    br, ir = bt[M_HEADS + h:M_HEADS + h + 1, :], at[h:h + 1, :]
    d = jnp.where(mask, (bc - br) + ir, -jnp.inf)
    inter = bc + m_prev
    m_t = jnp.maximum(inter, jnp.max(d, axis=-1, keepdims=True))
    w_intra = jnp.exp(d - m_t)
    w_inter = jnp.exp(inter - m_t)
    s = lax.dot_general(q, k, (((1,), (1,)), ((), ())), preferred_element_type=F32)
    sqk = w_intra * s
    intra = jnp.dot(sqk.astype(BF16), v, preferred_element_type=F32)
    sq_sum = jnp.sum(sqk, axis=-1, keepdims=True)
    m_end = last(m_t)
    bl = last(bc)
    dec = jnp.exp(bl + m_prev - m_end)
    w_end = jnp.exp((bl - bc) + ic - m_end)
    kw = k.astype(F32) * w_end
    return w_inter, intra, sq_sum, m_t, m_end, dec, kw


def _head_out(num, nq, m_t, mnorm_row, om):
    hh = num / jnp.maximum(jnp.abs(nq), jnp.exp(-m_t))
    y = hh * lax.rsqrt(jnp.mean(hh * hh, axis=-1, keepdims=True) + EPS) * mnorm_row
    return (jax.nn.sigmoid(om) * y).astype(BF16)


def _mlstm_prompt_kernel(q_ref, k_ref, v_ref, om_ref, g_ref, bias_ref, mnorm_ref,
                         o_ref, c_out, n_out, m_out, c_s, n_s, m_s):
    step = pl.program_id(0)

    @pl.when(step == 0)
    def _():
        c_s[...] = jnp.zeros_like(c_s)
        n_s[...] = jnp.zeros_like(n_s)
        m_s[...] = jnp.zeros_like(m_s)

    ln = WINDOW
    r = lax.broadcasted_iota(jnp.int32, (ln, ln), 0)
    c = lax.broadcasted_iota(jnp.int32, (ln, ln), 1)
    causal = c <= r
    cum = causal.astype(F32)
    last = lambda x: jnp.broadcast_to(x[ln - 1:ln, :], x.shape)
    for ch in range(MLSTM_CHUNKS):
        rows = slice(ch * ln, (ch + 1) * ln)
        a, b, at, bt = _gates(g_ref[rows, :], bias_ref[...], cum)
        for h in range(M_HEADS):
            cols = slice(h * M_HEAD_DIM, (h + 1) * M_HEAD_DIM)
            q, k, v = q_ref[rows, cols], k_ref[rows, cols], v_ref[rows, cols]
            m_prev = m_s[h:h + 1, :]
            w_inter, intra, sq_sum, m_t, m_end, dec, kw = _mlstm_intra(q, k, v, a, b, at, bt, h, causal, m_prev, last)
            c_old = c_s[h]
            n_old = n_s[h:h + 1, :]
            num = w_inter * jnp.dot(q, c_old.astype(BF16), preferred_element_type=F32) + intra
            nq = w_inter * jnp.sum(q.astype(F32) * n_old, axis=-1, keepdims=True) + sq_sum
            o_ref[rows, cols] = _head_out(num, nq, m_t, mnorm_ref[:, cols], om_ref[rows, cols])
            c_s[h] = dec * c_old + jnp.dot(kw.T.astype(BF16), v, preferred_element_type=F32)
            n_s[h:h + 1, :] = dec[0:1, :] * n_old + jnp.sum(kw, axis=0, keepdims=True)
            m_s[h:h + 1, :] = m_end[0:1, :]

    @pl.when(step == pl.num_programs(0) - 1)
    def _():
        c_out[...] = c_s[...]
        n_out[...] = n_s[...]
        m_out[...] = m_s[...]


def _mlstm_prompt(qm, km, vm, om, gt, bias_row, mnorm):
    m = qm.shape[0]
    tm = MLSTM_CHUNKS * WINDOW
    row = lambda n: pl.BlockSpec((tm, n), lambda i: (i, 0))
    whole = lambda shape: pl.BlockSpec(shape, lambda i: (0,) * len(shape))
    c_shape = (M_HEADS, M_HEAD_DIM, M_HEAD_DIM)
    s_shape = (8, LANES)
    return pl.pallas_call(
        _mlstm_prompt_kernel,
        grid=(m // tm,),
        in_specs=[row(M_WIDTH), row(M_WIDTH), row(M_WIDTH), row(M_WIDTH), row(GATE_PAD),
                  whole((1, GATE_PAD)), whole((1, M_WIDTH))],
        out_specs=[row(M_WIDTH), whole(c_shape), whole(s_shape), whole(s_shape)],
        out_shape=[jax.ShapeDtypeStruct((m, M_WIDTH), BF16), jax.ShapeDtypeStruct(c_shape, F32),
                   jax.ShapeDtypeStruct(s_shape, F32), jax.ShapeDtypeStruct(s_shape, F32)],
        scratch_shapes=[pltpu.VMEM(c_shape, F32), pltpu.VMEM(s_shape, F32), pltpu.VMEM(s_shape, F32)],
        compiler_params=_params(("arbitrary",)),
        name="mlstm_prompt",
    )(qm, km, vm, om, gt, bias_row, mnorm)


def _mlstm_sample_kernel(q_ref, k_ref, v_ref, om_ref, g_ref, m0_ref, bias_ref, mnorm_ref, c_ref, n_ref,
                         o_ref, c_out, n_out, m_out, *, dec_seq):
    t = dec_seq
    bb = SAMPLE_MLSTM_BATCH
    ln = bb * t
    r = lax.broadcasted_iota(jnp.int32, (ln, ln), 0)
    c = lax.broadcasted_iota(jnp.int32, (ln, ln), 1)
    shift = t.bit_length() - 1
    same = (r >> shift) == (c >> shift)
    mask = jnp.logical_and(same, c <= r)
    cum = mask.astype(F32)
    expand = (c == (r >> shift)).astype(F32)
    gather = ((c >> shift) == r).astype(F32)
    is_last = (r & (t - 1)) == t - 1
    is_first = (r & (t - 1)) == 0

    def last(x):
        y = jnp.where(is_last, x, 0.0)
        step = 1
        while step < t:
            y = y + pltpu.roll(y, ln - step, axis=0)
            step *= 2
        return y

    a, b, at, bt = _gates(g_ref[...], bias_ref[...], cum)
    m0 = m0_ref[...]
    m_cols = jnp.zeros((ln, LANES), F32)
    lane = lax.broadcasted_iota(jnp.int32, (ln, LANES), 1)
    zpad = jnp.zeros((ln - bb, M_HEAD_DIM), F32)
    for h in range(M_HEADS):
        cols = slice(h * M_HEAD_DIM, (h + 1) * M_HEAD_DIM)
        qf = q_ref[:, cols]
        q, k, v = qf.astype(BF16), k_ref[:, cols].astype(BF16), v_ref[:, cols].astype(BF16)
        m_prev = _col(m0, h)
        w_inter, intra, sq_sum, m_t, m_end, dec, kw = _mlstm_intra(q, k, v, a, b, at, bt, h, mask, m_prev, last)
        n_old = n_ref[:, cols]
        n_exp = jnp.dot(expand, jnp.concatenate([n_old, zpad], axis=0), precision=HIGHEST,
                        preferred_element_type=F32)
        qc = jnp.concatenate(
            [jnp.dot(qf[s * t:(s + 1) * t, :], c_ref[s * M_HEADS + h].astype(BF16).astype(F32),
                     preferred_element_type=F32) for s in range(bb)], axis=0)
        num = w_inter * qc + intra
        nq = w_inter * jnp.sum(qf * n_exp, axis=-1, keepdims=True) + sq_sum
        o_ref[:, cols] = _head_out(num, nq, m_t, mnorm_ref[:, cols], om_ref[:, cols])
        kwt = kw.T.astype(BF16)
        for s in range(bb):
            lhs = jnp.where((c >> shift) == s, kwt, jnp.zeros_like(kwt))
            upd = jnp.dot(lhs, v, preferred_element_type=F32)
            c_out[s * M_HEADS + h] = dec[s * t:s * t + 1, :] * c_ref[s * M_HEADS + h] + upd
        n_new = jnp.dot(gather, jnp.where(is_first, dec * n_exp, 0.0) + kw, precision=HIGHEST,
                        preferred_element_type=F32)
        n_out[:, cols] = n_new[:bb, :]
        m_cols = jnp.where(lane == h, m_end, m_cols)
    m_out[...] = m_cols


def _mlstm_sample(qm, km, vm, om, gt, m0_rows, bias_row, mnorm, c_in, n_in, dec_seq):
    m = qm.shape[0]
    bb = SAMPLE_MLSTM_BATCH
    tm = bb * dec_seq
    nb = m // dec_seq
    row = lambda n: pl.BlockSpec((tm, n), lambda i: (i, 0))
    whole = lambda shape: pl.BlockSpec(shape, lambda i: (0,) * len(shape))
    c_spec = pl.BlockSpec((bb * M_HEADS, M_HEAD_DIM, M_HEAD_DIM), lambda i: (i, 0, 0))
    n_spec = pl.BlockSpec((bb, M_WIDTH), lambda i: (i, 0))
    return pl.pallas_call(
        functools.partial(_mlstm_sample_kernel, dec_seq=dec_seq),
        grid=(nb // bb,),
        in_specs=[row(M_WIDTH), row(M_WIDTH), row(M_WIDTH), row(M_WIDTH), row(GATE_PAD), row(LANES),
                  whole((1, GATE_PAD)), whole((1, M_WIDTH)), c_spec, n_spec],
        out_specs=[row(M_WIDTH), c_spec, n_spec, row(LANES)],
        out_shape=[jax.ShapeDtypeStruct((m, M_WIDTH), BF16), jax.ShapeDtypeStruct(c_in.shape, F32),
                   jax.ShapeDtypeStruct(n_in.shape, F32), jax.ShapeDtypeStruct((m, LANES), F32)],
        compiler_params=_params(("parallel",)),
        name="mlstm_sample",
    )(qm, km, vm, om, gt, m0_rows, bias_row, mnorm, c_in, n_in)


def _out_ffn_kernel(x_ref, ma_ref, mm_ref, wo_ref, g1_ref, g2_ref, wg_ref, wu_ref, wd_ref, g3_ref, o_ref):
    y = (jnp.dot(ma_ref[...], wo_ref[:ATTN_WIDTH, :], preferred_element_type=F32)
         + jnp.dot(mm_ref[...], wo_ref[ATTN_WIDTH:, :], preferred_element_type=F32))
    x1 = x_ref[...] + _rms(y, g1_ref[...])
    f = _rms(x1, g2_ref[...]).astype(BF16)
    acc = None
    for off in range(0, D_FF, FFN_CHUNK):
        g = jnp.dot(f, wg_ref[:, off:off + FFN_CHUNK], preferred_element_type=F32)
        u = jnp.dot(f, wu_ref[:, off:off + FFN_CHUNK], preferred_element_type=F32)
        act = (g * jax.nn.sigmoid(g) * u).astype(BF16)
        part = jnp.dot(act, wd_ref[off:off + FFN_CHUNK, :], preferred_element_type=F32)
        acc = part if acc is None else acc + part
    o_ref[...] = x1 + _rms(acc, g3_ref[...])


def _out_ffn(x, mix_a, mix_m, w_out, g_post_mix, g_pre_ffn, w_gate, w_up, w_down, g_post_ffn):
    m = x.shape[0]
    tm = TOKEN_TILE
    row = lambda n: pl.BlockSpec((tm, n), lambda i: (i, 0))
    vec = _const_spec((1, D_MODEL))
    return pl.pallas_call(
        _out_ffn_kernel,
        grid=(m // tm,),
        in_specs=[row(D_MODEL), row(ATTN_WIDTH), row(M_WIDTH), _const_spec((D_MODEL, D_MODEL)), vec, vec,
                  _const_spec((D_MODEL, D_FF)), _const_spec((D_MODEL, D_FF)), _const_spec((D_FF, D_MODEL)), vec],
        out_specs=row(D_MODEL),
        out_shape=jax.ShapeDtypeStruct((m, D_MODEL), F32),
        compiler_params=_params(("parallel",)),
        name="out_ffn",
    )(x, mix_a, mix_m, w_out, g_post_mix, g_pre_ffn, w_gate, w_up, w_down, g_post_ffn)


def _layer(xp, xs, cache_k, cache_v, state_c, state_n, state_m, w_in, b_i, b_f, attn_sink, m_norm, w_out,
           g_pre_mix, g_post_mix, g_pre_ffn, g_post_ffn, w_gate, w_up, w_down):
    bp, sp, _ = xp.shape
    bs, ts, _ = xs.shape
    assert bp == 1 and sp % (ATTN_BLOCKS * WINDOW) == 0 and sp % TOKEN_TILE == 0
    assert ts & (ts - 1) == 0 and (bs * ts) % TOKEN_TILE == 0 and bs % SAMPLE_MLSTM_BATCH == 0

    w_pad = jnp.pad(w_in, ((0, 0), (0, IN_PAD - w_in.shape[1]))).astype(BF16)
    row = lambda v: v.reshape(1, -1)
    bias_row = jnp.pad(jnp.concatenate([b_i, b_f]), (0, GATE_PAD - 2 * M_HEADS)).reshape(1, GATE_PAD)
    sink = row(attn_sink)
    wo, wg, wu, wd = w_out.astype(BF16), w_gate.astype(BF16), w_up.astype(BF16), w_down.astype(BF16)
    ffn = (wo, row(g_post_mix), row(g_pre_ffn), wg, wu, wd, row(g_post_ffn))

    x2 = xp.reshape(sp, D_MODEL)
    qa, ka, va, qm, km, vm, om, gt = _inproj(x2, row(g_pre_mix), w_pad, BF16)
    mix_a = _attn_prompt(sink, qa, ka, va)
    mix_m, c_p, n_p, m_p = _mlstm_prompt(qm, km, vm, om, gt, bias_row, row(m_norm))
    yp = _out_ffn(x2, mix_a, mix_m, *ffn).reshape(xp.shape)
    k_p = ka[sp - WINDOW:].reshape(1, WINDOW, KV_HEADS, HEAD_DIM)
    v_p = va[sp - WINDOW:].reshape(1, WINDOW, KV_HEADS, HEAD_DIM)
    c_p = c_p.reshape(1, M_HEADS, M_HEAD_DIM, M_HEAD_DIM)
    n_p = n_p[:M_HEADS].reshape(1, M_HEADS, M_HEAD_DIM)
    m_p = m_p[:M_HEADS, 0].reshape(1, M_HEADS)

    x2 = xs.reshape(bs * ts, D_MODEL)
    qa, ka, va, qm, km, vm, om, gt = _inproj(x2, row(g_pre_mix), w_pad, F32)
    mix_a, k_s, v_s = _attn_sample(sink, qa, ka, va, cache_k.reshape(bs, WINDOW, KV_WIDTH),
                                   cache_v.reshape(bs, WINDOW, KV_WIDTH), ts)
    m0_rows = jnp.pad(jnp.repeat(state_m, ts, axis=0), ((0, 0), (0, LANES - M_HEADS)))
    mix_m, c_s, n_s, m_rows = _mlstm_sample(qm, km, vm, om, gt, m0_rows, bias_row, row(m_norm),
                                            state_c.reshape(bs * M_HEADS, M_HEAD_DIM, M_HEAD_DIM),
                                            state_n.reshape(bs, M_WIDTH), ts)
    ys = _out_ffn(x2, mix_a, mix_m, *ffn).reshape(xs.shape)
    k_s = k_s.reshape(bs, WINDOW, KV_HEADS, HEAD_DIM)
    v_s = v_s.reshape(bs, WINDOW, KV_HEADS, HEAD_DIM)
    c_s = c_s.reshape(bs, M_HEADS, M_HEAD_DIM, M_HEAD_DIM)
    n_s = n_s.reshape(bs, M_HEADS, M_HEAD_DIM)
    m_s = m_rows[ts - 1::ts, :M_HEADS]
    return yp, ys, (k_p, v_p, c_p, n_p, m_p), (k_s, v_s, c_s, n_s, m_s)


def kernel(x_prompt, x_sample, cache_k, cache_v, state_C, state_n, state_m, w_in, b_i, b_f, attn_sink, m_norm,
           w_out, g_pre_mix, g_post_mix, g_pre_ffn, g_post_ffn, w_gate, w_up, w_down):
    depth = w_in.shape[0]
    xp, xs = x_prompt, x_sample
    prompt_states, sample_states = [], []
    for l in range(depth):
        xp, xs, st_p, st_s = _layer(xp, xs, cache_k[l], cache_v[l], state_C[l], state_n[l], state_m[l],
                                    w_in[l], b_i[l], b_f[l], attn_sink[l], m_norm[l], w_out[l],
                                    g_pre_mix[l], g_post_mix[l], g_pre_ffn[l], g_post_ffn[l],
                                    w_gate[l], w_up[l], w_down[l])
        prompt_states.append(st_p)
        sample_states.append(st_s)
    stack = lambda states, i: jnp.stack([s[i] for s in states], axis=0)
    return (xp, xs) + tuple(stack(prompt_states, i) for i in range(5)) + tuple(stack(sample_states, i) for i in range(5))
```

```python
import functools

import jax
import jax.numpy as jnp
from jax import lax
from jax.experimental import pallas as pl
from jax.experimental.pallas import tpu as pltpu

F32 = jnp.float32
BF16 = jnp.bfloat16
HIGHEST = lax.Precision.HIGHEST

D_MODEL = 1024
HEAD_DIM = 64
ATTN_HEADS = 8
KV_HEADS = 2
GROUP = ATTN_HEADS // KV_HEADS
ATTN_WIDTH = ATTN_HEADS * HEAD_DIM
KV_WIDTH = KV_HEADS * HEAD_DIM
WINDOW = 128
M_HEADS = 4
M_HEAD_DIM = 128
M_WIDTH = M_HEADS * M_HEAD_DIM
D_FF = 2816
EPS = 1e-6

LANES = 128
SUBLANES = 8
GATE_PAD = LANES
GATE_ROWS = 2 * SUBLANES
IN_MAIN = ATTN_WIDTH + 2 * KV_WIDTH + 4 * M_WIDTH
IN_PAD = IN_MAIN + GATE_PAD
VMEM_LIMIT = 56 * 1024 * 1024

HEAD_ORDER = tuple(h for j in range(GROUP) for h in (j, j + GROUP))

TOKEN_TILE = 512
ATTN_BLOCKS = 4
MLSTM_CHUNKS = 4
FFN_CHUNK = 1408
SAMPLE_ATTN_BATCH = 8
SAMPLE_MLSTM_BATCH = 16


def _rms(x, g):
    return x * lax.rsqrt(jnp.mean(x * x, axis=-1, keepdims=True) + EPS) * g


def _const_spec(shape):
    nd = len(shape)
    return pl.BlockSpec(shape, lambda i: (0,) * nd, pipeline_mode=pl.Buffered(1))


def _params(semantics):
    return pltpu.CompilerParams(dimension_semantics=semantics, vmem_limit_bytes=VMEM_LIMIT)


def _inproj_kernel(x_ref, g_ref, w_ref, wgt_ref, qa_ref, ka_ref, va_ref, qm_ref, km_ref, vm_ref, om_ref, gt_ref,
                   *, gate_major):
    h = _rms(x_ref[...], g_ref[...]).astype(BF16)

    def proj(off, n):
        return jnp.dot(h, w_ref[:, off:off + n], preferred_element_type=F32)

    off = 0
    qa_ref[...] = (proj(off, ATTN_WIDTH) * (HEAD_DIM ** -0.5)).astype(qa_ref.dtype)
    off += ATTN_WIDTH
    ka_ref[...] = proj(off, KV_WIDTH)
    off += KV_WIDTH
    va_ref[...] = proj(off, KV_WIDTH)
    off += KV_WIDTH
    qm_ref[...] = proj(off, M_WIDTH).astype(qm_ref.dtype)
    off += M_WIDTH
    km = proj(off, M_WIDTH) * (M_HEAD_DIM ** -0.5)
    km_ref[...] = (km.T if gate_major else km).astype(km_ref.dtype)
    off += M_WIDTH
    vm_ref[...] = proj(off, M_WIDTH).astype(vm_ref.dtype)
    off += M_WIDTH
    om_ref[...] = proj(off, M_WIDTH)
    off += M_WIDTH
    if gate_major:
        gt_ref[...] = lax.dot_general(wgt_ref[...], h, (((1,), (1,)), ((), ())), preferred_element_type=F32)
    else:
        gt_ref[...] = proj(off, GATE_PAD)


def _inproj(x, g_pre, w_pad, w_gates_t, gate_major):
    m = x.shape[0]
    tm = TOKEN_TILE
    act = BF16 if gate_major else F32
    row = lambda n: pl.BlockSpec((tm, n), lambda i: (i, 0))
    col = lambda n: pl.BlockSpec((n, tm), lambda i: (0, i))
    widths = (ATTN_WIDTH, KV_WIDTH, KV_WIDTH, M_WIDTH, M_WIDTH, M_WIDTH, M_WIDTH, GATE_PAD)
    dtypes = (act, F32, F32, act, act, act, F32, F32)
    out_specs = [row(n) for n in widths]
    out_shape = [jax.ShapeDtypeStruct((m, n), d) for n, d in zip(widths, dtypes)]
    if gate_major:
        out_specs[4], out_shape[4] = col(M_WIDTH), jax.ShapeDtypeStruct((M_WIDTH, m), act)
        out_specs[7], out_shape[7] = col(GATE_ROWS), jax.ShapeDtypeStruct((GATE_ROWS, m), F32)
    return pl.pallas_call(
        functools.partial(_inproj_kernel, gate_major=gate_major),
        grid=(m // tm,),
        in_specs=[row(D_MODEL), _const_spec((1, D_MODEL)), _const_spec((D_MODEL, IN_PAD)),
                  _const_spec((GATE_ROWS, D_MODEL))],
        out_specs=out_specs,
        out_shape=out_shape,
        compiler_params=_params(("parallel",)),
        name="inproj",
    )(x, g_pre, w_pad, w_gates_t)


def _stack_heads(q_tiles):
    lane = lax.broadcasted_iota(jnp.int32, q_tiles[0].shape, 1)
    lo = lane < HEAD_DIM
    zero = jnp.zeros_like(q_tiles[0])
    parts = []
    for qt in q_tiles:
        parts += [jnp.where(lo, qt, zero), jnp.where(lo, zero, qt)]
    return jnp.concatenate(parts, axis=0)


def _unstack_heads(o, rows):
    lane = lax.broadcasted_iota(jnp.int32, (rows, LANES), 1)
    lo = lane < HEAD_DIM
    return [jnp.where(lo, o[(2 * j) * rows:(2 * j + 1) * rows, :], o[(2 * j + 1) * rows:(2 * j + 2) * rows, :])
            for j in range(GROUP)]


def _sink_rows(sink_ref, rows):
    return jnp.concatenate([jnp.full((rows, LANES), sink_ref[0, h], F32) for h in HEAD_ORDER], axis=0)


def _attn_prompt_kernel(sink_ref, q_ref, k_ref, v_ref, kp_ref, vp_ref, o_ref):
    first = pl.program_id(0) == 0
    nrow = ATTN_HEADS * WINDOW
    r = lax.broadcasted_iota(jnp.int32, (nrow, WINDOW), 0) & (WINDOW - 1)
    c = lax.broadcasted_iota(jnp.int32, (nrow, WINDOW), 1)
    own = c <= r
    sink = _sink_rows(sink_ref, WINDOW)
    no_prev = jnp.where(first, -jnp.inf, 0.0)
    ones = jnp.ones((2 * WINDOW, LANES), BF16)
    blocks = [slice(blk * WINDOW, (blk + 1) * WINDOW) for blk in range(ATTN_BLOCKS)]
    scores, v_augs = [], []
    for blk, rows in enumerate(blocks):
        if blk == 0:
            k_prev, v_prev = kp_ref[...], vp_ref[...]
        else:
            k_prev, v_prev = k_ref[blocks[blk - 1], :], v_ref[blocks[blk - 1], :]
        kcat = jnp.concatenate([k_ref[rows, :], k_prev], axis=0).astype(BF16)
        vcat = jnp.concatenate([v_ref[rows, :], v_prev], axis=0).astype(BF16)
        v_augs.append(jnp.concatenate([vcat, ones], axis=1))
        qs = _stack_heads([q_ref[rows, j * LANES:(j + 1) * LANES] for j in range(GROUP)])
        scores.append(lax.dot_general(qs, kcat, (((1,), (1,)), ((), ())), preferred_element_type=F32))
    probs, maxes = [], []
    for blk, s in enumerate(scores):
        s_prev = s[:, WINDOW:]
        if blk == 0:
            s_prev = s_prev + no_prev
        sc = jnp.where(own, s[:, :WINDOW], s_prev)
        m = jnp.maximum(jnp.max(sc, axis=-1, keepdims=True), sink)
        p = jnp.exp(sc - m)
        zero = jnp.zeros_like(p)
        probs.append(jnp.concatenate([jnp.where(own, p, zero), jnp.where(own, zero, p)], axis=1).astype(BF16))
        maxes.append(m)
    for rows, p2, v_aug, m in zip(blocks, probs, v_augs, maxes):
        o = jnp.dot(p2, v_aug, preferred_element_type=F32)
        o = o[:, :LANES] * (1.0 / (o[:, LANES:] + jnp.exp(sink - m)))
        for j, tile in enumerate(_unstack_heads(o, WINDOW)):
            o_ref[rows, j * LANES:(j + 1) * LANES] = tile.astype(o_ref.dtype)


def _attn_prompt(sink, qa, ka, va):
    m = qa.shape[0]
    tm = ATTN_BLOCKS * WINDOW
    row = lambda n: pl.BlockSpec((tm, n), lambda i: (i, 0))
    prev = pl.BlockSpec((WINDOW, KV_WIDTH), lambda i: (jnp.maximum(i * ATTN_BLOCKS - 1, 0), 0))
    return pl.pallas_call(
        _attn_prompt_kernel,
        grid=(m // tm,),
        in_specs=[pl.BlockSpec(memory_space=pltpu.SMEM), row(ATTN_WIDTH), row(KV_WIDTH), row(KV_WIDTH), prev, prev],
        out_specs=row(ATTN_WIDTH),
        out_shape=jax.ShapeDtypeStruct((m, ATTN_WIDTH), BF16),
        compiler_params=_params(("parallel",)),
        name="attn_prompt",
    )(sink, qa, ka, va, ka, va)


def _attn_sample_kernel(sink_ref, q_ref, kn_ref, vn_ref, ck_ref, cv_ref, o_ref, ko_ref, vo_ref, *, dec_seq):
    t = dec_seq
    bb = SAMPLE_ATTN_BATCH
    nrow = ATTN_HEADS * t
    ts, ns = t.bit_length() - 1, nrow.bit_length() - 1
    r_c = lax.broadcasted_iota(jnp.int32, (nrow, WINDOW), 0) & (t - 1)
    c_c = lax.broadcasted_iota(jnp.int32, (nrow, WINDOW), 1)
    vis_cache = c_c > r_c
    r_n = lax.broadcasted_iota(jnp.int32, (bb * nrow, bb * t), 0)
    c_n = lax.broadcasted_iota(jnp.int32, (bb * nrow, bb * t), 1)
    vis_new = jnp.logical_and((r_n >> ns) == (c_n >> ts), (c_n & (t - 1)) <= (r_n & (t - 1)))
    sink = _sink_rows(sink_ref, t)
    nt = (((1,), (1,)), ((), ()))
    kn_all, vn_all = kn_ref[...], vn_ref[...]
    qs, s_c = [], []
    for b in range(bb):
        rows = slice(b * t, (b + 1) * t)
        ck, cv = ck_ref[b], cv_ref[b]
        ko_ref[b] = jnp.concatenate([ck[t:, :], kn_all[rows, :]], axis=0)
        vo_ref[b] = jnp.concatenate([cv[t:, :], vn_all[rows, :]], axis=0)
        qs.append(_stack_heads([q_ref[rows, j * LANES:(j + 1) * LANES] for j in range(GROUP)]).astype(BF16))
        s_c.append(jnp.where(vis_cache, lax.dot_general(qs[b], ck.astype(BF16), nt, preferred_element_type=F32),
                             -jnp.inf))
    s_n = jnp.where(vis_new, lax.dot_general(jnp.concatenate(qs, axis=0), kn_all.astype(BF16), nt,
                                             preferred_element_type=F32), -jnp.inf)
    p_c, p_n, rden = [], [], []
    for b in range(bb):
        s_nb = s_n[b * nrow:(b + 1) * nrow, :]
        m = jnp.maximum(jnp.maximum(jnp.max(s_c[b], axis=-1, keepdims=True), jnp.max(s_nb, axis=-1, keepdims=True)),
                        sink)
        p_c.append(jnp.exp(s_c[b] - m))
        p_n.append(jnp.exp(s_nb - m[:, :bb * t]))
        rden.append(1.0 / (jnp.sum(p_c[b], axis=-1, keepdims=True) + jnp.sum(p_n[b], axis=-1, keepdims=True)
                           + jnp.exp(sink - m)))
    o_n = jnp.dot(jnp.concatenate(p_n, axis=0).astype(BF16), vn_all.astype(BF16), preferred_element_type=F32)
    outs = [[] for _ in range(GROUP)]
    for b in range(bb):
        o = (jnp.dot(p_c[b].astype(BF16), cv_ref[b].astype(BF16), preferred_element_type=F32)
             + o_n[b * nrow:(b + 1) * nrow, :]) * rden[b]
        for j, tile in enumerate(_unstack_heads(o, t)):
            outs[j].append(tile)
    for j, parts in enumerate(outs):
        o_ref[:, j * LANES:(j + 1) * LANES] = jnp.concatenate(parts, axis=0).astype(o_ref.dtype)


def _attn_sample(sink, qa, ka, va, cache_k, cache_v, dec_seq):
    nb = cache_k.shape[0]
    bb = SAMPLE_ATTN_BATCH
    row = lambda n: pl.BlockSpec((bb * dec_seq, n), lambda i: (i, 0))
    cache = pl.BlockSpec((bb, WINDOW, KV_WIDTH), lambda i: (i, 0, 0))
    return pl.pallas_call(
        functools.partial(_attn_sample_kernel, dec_seq=dec_seq),
        grid=(nb // bb,),
        in_specs=[pl.BlockSpec(memory_space=pltpu.SMEM), row(ATTN_WIDTH), row(KV_WIDTH), row(KV_WIDTH), cache, cache],
        out_specs=[row(ATTN_WIDTH), cache, cache],
        out_shape=[jax.ShapeDtypeStruct((nb * dec_seq, ATTN_WIDTH), BF16),
                   jax.ShapeDtypeStruct(cache_k.shape, F32), jax.ShapeDtypeStruct(cache_v.shape, F32)],
        compiler_params=_params(("parallel",)),
        name="attn_sample",
    )(sink, qa, ka, va, cache_k, cache_v)


def _head_out(hh, mnorm_row, om):
    y = hh * lax.rsqrt(jnp.mean(hh * hh, axis=-1, keepdims=True) + EPS) * mnorm_row
    return (jax.nn.sigmoid(om) * y).astype(BF16)


def _mlstm_prompt_kernel(q_ref, kt_ref, v_ref, om_ref, g_ref, bias_ref, mnorm_ref, o_ref, c_out, m_out, c_s, m_s):
    step = pl.program_id(0)

    @pl.when(step == 0)
    def _():
        c_s[...] = jnp.zeros_like(c_s)
        m_s[...] = jnp.zeros_like(m_s)

    ln = WINDOW
    r = lax.broadcasted_iota(jnp.int32, (ln, ln), 0)
    c = lax.broadcasted_iota(jnp.int32, (ln, ln), 1)
    causal = c <= r
    upper = (r <= c).astype(F32)
    lane8 = lax.broadcasted_iota(jnp.int32, (SUBLANES, ln), 1)
    ones_col = (c == 0).astype(BF16)
    zrows = jnp.zeros((ln - 3 * SUBLANES, ln), F32)
    pre = []
    for ch in range(MLSTM_CHUNKS):
        rows = slice(ch * ln, (ch + 1) * ln)
        gi = g_ref[0:SUBLANES, rows] + bias_ref[0:SUBLANES, :]
        lf = jax.nn.log_sigmoid(g_ref[SUBLANES:GATE_ROWS, rows] + bias_ref[SUBLANES:GATE_ROWS, :])
        b = jnp.dot(lf, upper, precision=HIGHEST, preferred_element_type=F32)
        g = gi - b
        cm0 = g
        sh = 1
        while sh < ln:
            cm0 = jnp.maximum(cm0, jnp.where(lane8 >= sh, pltpu.roll(cm0, sh, axis=1), -jnp.inf))
            sh *= 2
        b_last = jnp.broadcast_to(b[:, ln - 1:ln], b.shape)
        g_max = jnp.broadcast_to(cm0[:, ln - 1:ln], b.shape)
        pre.append((b, g, cm0, b_last, g_max, (b_last - b) + gi))
    m_prev = m_s[...]
    for ch in range(MLSTM_CHUNKS):
        rows = slice(ch * ln, (ch + 1) * ln)
        b, g, cm0, b_last, g_max, w_end_arg = pre[ch]
        cm = jnp.maximum(cm0, m_prev)
        w_inter = jnp.exp(m_prev - cm)
        e_negm = jnp.exp(-(b + cm))
        m_end = b_last + jnp.maximum(g_max, m_prev)
        dec = jnp.exp(b_last + m_prev - m_end)
        w_end = jnp.exp(w_end_arg - m_end)
        m_prev = m_end
        cols_t = jnp.concatenate([cm, w_inter, e_negm, zrows], axis=0).T
        hcols = [slice(h * M_HEAD_DIM, (h + 1) * M_HEAD_DIM) for h in range(M_HEADS)]
        qs = [q_ref[rows, hc] for hc in hcols]
        kts = [kt_ref[hc, rows] for hc in hcols]
        v_augs = [jnp.concatenate([v_ref[rows, hc], ones_col], axis=1) for hc in hcols]
        s_qk = [jnp.dot(q, kt, preferred_element_type=F32) for q, kt in zip(qs, kts)]
        c_olds = [c_s[h] for h in range(M_HEADS)]
        q_c = [jnp.dot(q, c_old.astype(BF16), preferred_element_type=F32) for q, c_old in zip(qs, c_olds)]
        nums = []
        for h in range(M_HEADS):
            w_intra = jnp.exp(jnp.where(causal, g[h:h + 1, :] - cols_t[:, h:h + 1], -jnp.inf))
            sqk = (w_intra * s_qk[h]).astype(BF16)
            wi_col = cols_t[:, SUBLANES + h:SUBLANES + h + 1]
            nums.append(wi_col * q_c[h] + jnp.dot(sqk, v_augs[h], preferred_element_type=F32))
        for h in range(M_HEADS):
            kwt = (kts[h].astype(F32) * w_end[h:h + 1, :]).astype(BF16)
            dec_row = jnp.concatenate([dec[h:h + 1, :], dec[h:h + 1, :]], axis=1)
            c_s[h] = dec_row * c_olds[h] + jnp.dot(kwt, v_augs[h], preferred_element_type=F32)
        for h in range(M_HEADS):
            en_col = cols_t[:, 2 * SUBLANES + h:2 * SUBLANES + h + 1]
            den = jnp.maximum(jnp.abs(nums[h][:, M_HEAD_DIM:M_HEAD_DIM + 1]), en_col)
            hh = nums[h][:, :M_HEAD_DIM] * (1.0 / den)
            o_ref[rows, hcols[h]] = _head_out(hh, mnorm_ref[:, hcols[h]], om_ref[rows, hcols[h]])
    m_s[...] = m_prev

    @pl.when(step == pl.num_programs(0) - 1)
    def _():
        c_out[...] = c_s[...]
        m_out[...] = m_s[...]


def _mlstm_prompt(qm, kt, vm, om, gt, bias_cols, mnorm):
    m = qm.shape[0]
    tm = MLSTM_CHUNKS * WINDOW
    row = lambda n: pl.BlockSpec((tm, n), lambda i: (i, 0))
    col = lambda n: pl.BlockSpec((n, tm), lambda i: (0, i))
    whole = lambda shape: pl.BlockSpec(shape, lambda i: (0,) * len(shape))
    c_shape = (M_HEADS, M_HEAD_DIM, 2 * M_HEAD_DIM)
    s_shape = (SUBLANES, LANES)
    return pl.pallas_call(
        _mlstm_prompt_kernel,
        grid=(m // tm,),
        in_specs=[row(M_WIDTH), col(M_WIDTH), row(M_WIDTH), row(M_WIDTH), col(GATE_ROWS),
                  whole((GATE_ROWS, LANES)), whole((1, M_WIDTH))],
        out_specs=[row(M_WIDTH), whole(c_shape), whole(s_shape)],
        out_shape=[jax.ShapeDtypeStruct((m, M_WIDTH), BF16), jax.ShapeDtypeStruct(c_shape, F32),
                   jax.ShapeDtypeStruct(s_shape, F32)],
        scratch_shapes=[pltpu.VMEM(c_shape, F32), pltpu.VMEM(s_shape, F32)],
        compiler_params=_params(("arbitrary",)),
        name="mlstm_prompt",
    )(qm, kt, vm, om, gt, bias_cols, mnorm)


def _gates(g_blk, bias_row, cum):
    pre = g_blk + bias_row
    lane = lax.broadcasted_iota(jnp.int32, pre.shape, 1)
    a = jnp.where(lane < M_HEADS, pre, jax.nn.log_sigmoid(pre))
    b = jnp.dot(cum, a, precision=HIGHEST, preferred_element_type=F32)
    return a, b, a.T, b.T


def _col(x, j):
    return jnp.broadcast_to(x[:, j:j + 1], x.shape)


def _mlstm_intra(q, k, v, a, b, at, bt, h, mask, m_prev, last):
    bc, ic = _col(b, M_HEADS + h), _col(a, h)
    br, ir = bt[M_HEADS + h:M_HEADS + h + 1, :], at[h:h + 1, :]
    d = jnp.where(mask, (bc - br) + ir, -jnp.inf)
    inter = bc + m_prev
    m_t = jnp.maximum(inter, jnp.max(d, axis=-1, keepdims=True))
    w_intra = jnp.exp(d - m_t)
    w_inter = jnp.exp(inter - m_t)
    s = lax.dot_general(q, k, (((1,), (1,)), ((), ())), preferred_element_type=F32)
    sqk = w_intra * s
    intra = jnp.dot(sqk.astype(BF16), v, preferred_element_type=F32)
    sq_sum = jnp.sum(sqk, axis=-1, keepdims=True)
    m_end = last(m_t)
    bl = last(bc)
    dec = jnp.exp(bl + m_prev - m_end)
    w_end = jnp.exp((bl - bc) + ic - m_end)
    kw = k.astype(F32) * w_end
    return w_inter, intra, sq_sum, m_t, m_end, dec, kw


def _mlstm_sample_kernel(q_ref, k_ref, v_ref, om_ref, g_ref, m0_ref, bias_ref, mnorm_ref, c_ref, n_ref,
                         o_ref, c_out, n_out, m_out, *, dec_seq):
    t = dec_seq
    bb = SAMPLE_MLSTM_BATCH
    ln = bb * t
    r = lax.broadcasted_iota(jnp.int32, (ln, ln), 0)
    c = lax.broadcasted_iota(jnp.int32, (ln, ln), 1)
    shift = t.bit_length() - 1
    same = (r >> shift) == (c >> shift)
    mask = jnp.logical_and(same, c <= r)
    cum = mask.astype(F32)
    expand = (c == (r >> shift)).astype(F32)
    gather = ((c >> shift) == r).astype(F32)
    is_last = (r & (t - 1)) == t - 1
    is_first = (r & (t - 1)) == 0

    def last(x):
        y = jnp.where(is_last, x, 0.0)
        step = 1
        while step < t:
            y = y + pltpu.roll(y, ln - step, axis=0)
            step *= 2
        return y

    a, b, at, bt = _gates(g_ref[...], bias_ref[...], cum)
    m0 = m0_ref[...]
    m_cols = jnp.zeros((ln, LANES), F32)
    lane = lax.broadcasted_iota(jnp.int32, (ln, LANES), 1)
    zpad = jnp.zeros((ln - bb, M_HEAD_DIM), F32)
    for h in range(M_HEADS):
        cols = slice(h * M_HEAD_DIM, (h + 1) * M_HEAD_DIM)
        qf = q_ref[:, cols]
        q, k, v = qf.astype(BF16), k_ref[:, cols].astype(BF16), v_ref[:, cols].astype(BF16)
        m_prev = _col(m0, h)
        w_inter, intra, sq_sum, m_t, m_end, dec, kw = _mlstm_intra(q, k, v, a, b, at, bt, h, mask, m_prev, last)
        n_old = n_ref[:, cols]
        n_exp = jnp.dot(expand, jnp.concatenate([n_old, zpad], axis=0), precision=HIGHEST,
                        preferred_element_type=F32)
        qc = jnp.concatenate(
            [jnp.dot(qf[s * t:(s + 1) * t, :], c_ref[s * M_HEADS + h].astype(BF16).astype(F32),
                     preferred_element_type=F32) for s in range(bb)], axis=0)
        num = w_inter * qc + intra
        nq = w_inter * jnp.sum(qf * n_exp, axis=-1, keepdims=True) + sq_sum
        hh = num / jnp.maximum(jnp.abs(nq), jnp.exp(-m_t))
        o_ref[:, cols] = _head_out(hh, mnorm_ref[:, cols], om_ref[:, cols])
        kwt = kw.T.astype(BF16)
        for s in range(bb):
            lhs = jnp.where((c >> shift) == s, kwt, jnp.zeros_like(kwt))
            upd = jnp.dot(lhs, v, preferred_element_type=F32)
            c_out[s * M_HEADS + h] = dec[s * t:s * t + 1, :] * c_ref[s * M_HEADS + h] + upd
        n_new = jnp.dot(gather, jnp.where(is_first, dec * n_exp, 0.0) + kw, precision=HIGHEST,
                        preferred_element_type=F32)
        n_out[:, cols] = n_new[:bb, :]
        m_cols = jnp.where(lane == h, m_end, m_cols)
    m_out[...] = m_cols


def _mlstm_sample(qm, km, vm, om, gt, m0_rows, bias_row, mnorm, c_in, n_in, dec_seq):
    m = qm.shape[0]
    bb = SAMPLE_MLSTM_BATCH
    tm = bb * dec_seq
    nb = m // dec_seq
    row = lambda n: pl.BlockSpec((tm, n), lambda i: (i, 0))
    whole = lambda shape: pl.BlockSpec(shape, lambda i: (0,) * len(shape))
    c_spec = pl.BlockSpec((bb * M_HEADS, M_HEAD_DIM, M_HEAD_DIM), lambda i: (i, 0, 0))
    n_spec = pl.BlockSpec((bb, M_WIDTH), lambda i: (i, 0))
    return pl.pallas_call(
        functools.partial(_mlstm_sample_kernel, dec_seq=dec_seq),
        grid=(nb // bb,),
        in_specs=[row(M_WIDTH), row(M_WIDTH), row(M_WIDTH), row(M_WIDTH), row(GATE_PAD), row(LANES),
                  whole((1, GATE_PAD)), whole((1, M_WIDTH)), c_spec, n_spec],
        out_specs=[row(M_WIDTH), c_spec, n_spec, row(LANES)],
        out_shape=[jax.ShapeDtypeStruct((m, M_WIDTH), BF16), jax.ShapeDtypeStruct(c_in.shape, F32),
                   jax.ShapeDtypeStruct(n_in.shape, F32), jax.ShapeDtypeStruct((m, LANES), F32)],
        compiler_params=_params(("parallel",)),
        name="mlstm_sample",
    )(qm, km, vm, om, gt, m0_rows, bias_row, mnorm, c_in, n_in)


def _out_ffn_kernel(x_ref, ma_ref, mm_ref, wo_ref, g1_ref, g2_ref, wg_ref, wu_ref, wd_ref, g3_ref, o_ref):
    y = (jnp.dot(ma_ref[...], wo_ref[:ATTN_WIDTH, :], preferred_element_type=F32)
         + jnp.dot(mm_ref[...], wo_ref[ATTN_WIDTH:, :], preferred_element_type=F32))
    x1 = x_ref[...] + _rms(y, g1_ref[...])
    f = _rms(x1, g2_ref[...]).astype(BF16)
    acc = None
    for off in range(0, D_FF, FFN_CHUNK):
        g = jnp.dot(f, wg_ref[:, off:off + FFN_CHUNK], preferred_element_type=F32)
        u = jnp.dot(f, wu_ref[:, off:off + FFN_CHUNK], preferred_element_type=F32)
        act = (g * jax.nn.sigmoid(g) * u).astype(BF16)
        part = jnp.dot(act, wd_ref[off:off + FFN_CHUNK, :], preferred_element_type=F32)
        acc = part if acc is None else acc + part
    o_ref[...] = x1 + _rms(acc, g3_ref[...])


def _out_ffn(x, mix_a, mix_m, w_out, g_post_mix, g_pre_ffn, w_gate, w_up, w_down, g_post_ffn):
    m = x.shape[0]
    tm = TOKEN_TILE
    row = lambda n: pl.BlockSpec((tm, n), lambda i: (i, 0))
    vec = _const_spec((1, D_MODEL))
    return pl.pallas_call(
        _out_ffn_kernel,
        grid=(m // tm,),
        in_specs=[row(D_MODEL), row(ATTN_WIDTH), row(M_WIDTH), _const_spec((D_MODEL, D_MODEL)), vec, vec,
                  _const_spec((D_MODEL, D_FF)), _const_spec((D_MODEL, D_FF)), _const_spec((D_FF, D_MODEL)), vec],
        out_specs=row(D_MODEL),
        out_shape=jax.ShapeDtypeStruct((m, D_MODEL), F32),
        compiler_params=_params(("parallel",)),
        name="out_ffn",
    )(x, mix_a, mix_m, w_out, g_post_mix, g_pre_ffn, w_gate, w_up, w_down, g_post_ffn)


def _permute_heads(w, axis):
    shape = w.shape
    w = w.reshape(shape[:axis] + (ATTN_HEADS, HEAD_DIM) + shape[axis + 1:])
    w = jnp.take(w, jnp.array(HEAD_ORDER), axis=axis)
    return w.reshape(shape)


def _layer(xp, xs, cache_k, cache_v, state_c, state_n, state_m, w_in, b_i, b_f, attn_sink, m_norm, w_out,
           g_pre_mix, g_post_mix, g_pre_ffn, g_post_ffn, w_gate, w_up, w_down):
    bp, sp, _ = xp.shape
    bs, ts, _ = xs.shape
    assert bp == 1 and sp % (ATTN_BLOCKS * WINDOW) == 0 and sp % TOKEN_TILE == 0
    assert ts & (ts - 1) == 0 and (bs * ts) % TOKEN_TILE == 0 and bs % SAMPLE_MLSTM_BATCH == 0

    row = lambda v: v.reshape(1, -1)
    w_q = _permute_heads(w_in[:, :ATTN_WIDTH], 1)
    w_pad = jnp.concatenate([w_q, w_in[:, ATTN_WIDTH:], jnp.zeros((D_MODEL, IN_PAD - w_in.shape[1]), F32)],
                            axis=1).astype(BF16)
    zrows = jnp.zeros((SUBLANES - M_HEADS, D_MODEL), F32)
    w_gates_t = jnp.concatenate([w_in[:, IN_MAIN:IN_MAIN + M_HEADS].T, zrows,
                                 w_in[:, IN_MAIN + M_HEADS:IN_MAIN + 2 * M_HEADS].T, zrows], axis=0).astype(BF16)
    bias_row = jnp.pad(jnp.concatenate([b_i, b_f]), (0, GATE_PAD - 2 * M_HEADS)).reshape(1, GATE_PAD)
    zb = jnp.zeros((SUBLANES - M_HEADS,), F32)
    bias_cols = jnp.broadcast_to(jnp.concatenate([b_i, zb, b_f, zb])[:, None], (GATE_ROWS, LANES))
    sink = row(attn_sink)
    wo = jnp.concatenate([_permute_heads(w_out[:ATTN_WIDTH], 0), w_out[ATTN_WIDTH:]], axis=0).astype(BF16)
    wg, wu, wd = w_gate.astype(BF16), w_up.astype(BF16), w_down.astype(BF16)
    ffn = (wo, row(g_post_mix), row(g_pre_ffn), wg, wu, wd, row(g_post_ffn))

    x2 = xp.reshape(sp, D_MODEL)
    qa, ka, va, qm, kt, vm, om, gt = _inproj(x2, row(g_pre_mix), w_pad, w_gates_t, True)
    mix_a = _attn_prompt(sink, qa, ka, va)
    mix_m, c_aug, m_p = _mlstm_prompt(qm, kt, vm, om, gt, bias_cols, row(m_norm))
    yp = _out_ffn(x2, mix_a, mix_m, *ffn).reshape(xp.shape)
    k_p = ka[sp - WINDOW:].reshape(1, WINDOW, KV_HEADS, HEAD_DIM)
    v_p = va[sp - WINDOW:].reshape(1, WINDOW, KV_HEADS, HEAD_DIM)
    c_p = c_aug[:, :, :M_HEAD_DIM].reshape(1, M_HEADS, M_HEAD_DIM, M_HEAD_DIM)
    n_p = c_aug[:, :, M_HEAD_DIM].reshape(1, M_HEADS, M_HEAD_DIM)
    m_p = m_p[:M_HEADS, 0].reshape(1, M_HEADS)

    x2 = xs.reshape(bs * ts, D_MODEL)
    qa, ka, va, qm, km, vm, om, gt = _inproj(x2, row(g_pre_mix), w_pad, w_gates_t, False)
    mix_a, k_s, v_s = _attn_sample(sink, qa, ka, va, cache_k.reshape(bs, WINDOW, KV_WIDTH),
                                   cache_v.reshape(bs, WINDOW, KV_WIDTH), ts)
    m0_rows = jnp.pad(jnp.repeat(state_m, ts, axis=0), ((0, 0), (0, LANES - M_HEADS)))
    mix_m, c_s, n_s, m_rows = _mlstm_sample(qm, km, vm, om, gt, m0_rows, bias_row, row(m_norm),
                                            state_c.reshape(bs * M_HEADS, M_HEAD_DIM, M_HEAD_DIM),
                                            state_n.reshape(bs, M_WIDTH), ts)
    ys = _out_ffn(x2, mix_a, mix_m, *ffn).reshape(xs.shape)
    k_s = k_s.reshape(bs, WINDOW, KV_HEADS, HEAD_DIM)
    v_s = v_s.reshape(bs, WINDOW, KV_HEADS, HEAD_DIM)
    c_s = c_s.reshape(bs, M_HEADS, M_HEAD_DIM, M_HEAD_DIM)
    n_s = n_s.reshape(bs, M_HEADS, M_HEAD_DIM)
    m_s = m_rows[ts - 1::ts, :M_HEADS]
    return yp, ys, (k_p, v_p, c_p, n_p, m_p), (k_s, v_s, c_s, n_s, m_s)


def kernel(x_prompt, x_sample, cache_k, cache_v, state_C, state_n, state_m, w_in, b_i, b_f, attn_sink, m_norm,
           w_out, g_pre_mix, g_post_mix, g_pre_ffn, g_post_ffn, w_gate, w_up, w_down):
    depth = w_in.shape[0]
    xp, xs = x_prompt, x_sample
    prompt_states, sample_states = [], []
    for l in range(depth):
        xp, xs, st_p, st_s = _layer(xp, xs, cache_k[l], cache_v[l], state_C[l], state_n[l], state_m[l],
                                    w_in[l], b_i[l], b_f[l], attn_sink[l], m_norm[l], w_out[l],
                                    g_pre_mix[l], g_post_mix[l], g_pre_ffn[l], g_post_ffn[l],
                                    w_gate[l], w_up[l], w_down[l])
        prompt_states.append(st_p)
        sample_states.append(st_s)
    stack = lambda states, i: jnp.stack([s[i] for s in states], axis=0)
    return (xp, xs) + tuple(stack(prompt_states, i) for i in range(5)) + tuple(stack(sample_states, i) for i in range(5))
```

```python
import functools

import jax
import jax.numpy as jnp
from jax import lax
from jax.experimental import pallas as pl
from jax.experimental.pallas import tpu as pltpu

F32 = jnp.float32
BF16 = jnp.bfloat16
HIGHEST = lax.Precision.HIGHEST

D_MODEL = 1024
HEAD_DIM = 64
ATTN_HEADS = 8
KV_HEADS = 2
GROUP = ATTN_HEADS // KV_HEADS
ATTN_WIDTH = ATTN_HEADS * HEAD_DIM
KV_WIDTH = KV_HEADS * HEAD_DIM
WINDOW = 128
M_HEADS = 4
M_HEAD_DIM = 128
M_WIDTH = M_HEADS * M_HEAD_DIM
D_FF = 2816
EPS = 1e-6

LANES = 128
SUBLANES = 8
GATE_PAD = LANES
GATE_ROWS = 2 * SUBLANES
BF16_SUBLANES = 16
AUG_ROWS = M_HEAD_DIM + BF16_SUBLANES
IN_MAIN = ATTN_WIDTH + 2 * KV_WIDTH + 4 * M_WIDTH
IN_PAD = IN_MAIN + GATE_PAD
VMEM_LIMIT = 56 * 1024 * 1024

HEAD_ORDER = tuple(h for j in range(GROUP) for h in (j, j + GROUP))

TOKEN_TILE = 512
ATTN_BLOCKS = 4
MLSTM_CHUNKS = 4
FFN_CHUNK = 1408
ROW_GROUPS = 2
SAMPLE_ATTN_BATCH = 8
SAMPLE_MLSTM_BATCH = 16


def _rms(x, g):
    return x * lax.rsqrt(jnp.mean(x * x, axis=-1, keepdims=True) + EPS) * g


def _const_spec(shape):
    nd = len(shape)
    return pl.BlockSpec(shape, lambda i: (0,) * nd, pipeline_mode=pl.Buffered(1))


def _params(semantics):
    return pltpu.CompilerParams(dimension_semantics=semantics, vmem_limit_bytes=VMEM_LIMIT)


def _inproj_kernel(x_ref, g_ref, w_ref, wgt_ref, qa_ref, ka_ref, va_ref, qm_ref, km_ref, vm_ref, om_ref, gt_ref,
                   *, gate_major):
    tm = x_ref.shape[0]
    step = tm // ROW_GROUPS
    for r0 in range(0, tm, step):
        rs = slice(r0, r0 + step)
        h = _rms(x_ref[rs, :], g_ref[...]).astype(BF16)

        def proj(off, n):
            return jnp.dot(h, w_ref[:, off:off + n], preferred_element_type=F32)

        def put(ref, val):
            if gate_major:
                ref[:, rs] = val.T.astype(ref.dtype)
            else:
                ref[rs, :] = val.astype(ref.dtype)

        off = 0
        qa_ref[rs, :] = (proj(off, ATTN_WIDTH) * (HEAD_DIM ** -0.5)).astype(qa_ref.dtype)
        off += ATTN_WIDTH
        ka_ref[rs, :] = proj(off, KV_WIDTH)
        off += KV_WIDTH
        va_ref[rs, :] = proj(off, KV_WIDTH)
        off += KV_WIDTH
        put(qm_ref, proj(off, M_WIDTH))
        off += M_WIDTH
        km_ref[rs, :] = (proj(off, M_WIDTH) * (M_HEAD_DIM ** -0.5)).astype(km_ref.dtype)
        off += M_WIDTH
        put(vm_ref, proj(off, M_WIDTH))
        off += M_WIDTH
        om_ref[rs, :] = proj(off, M_WIDTH)
        off += M_WIDTH
        if gate_major:
            gt_ref[:, rs] = lax.dot_general(wgt_ref[...], h, (((1,), (1,)), ((), ())), preferred_element_type=F32)
        else:
            gt_ref[rs, :] = proj(off, GATE_PAD)


def _inproj(x, g_pre, w_pad, w_gates_t, gate_major):
    m = x.shape[0]
    tm = TOKEN_TILE
    act = BF16 if gate_major else F32
    row = lambda n: pl.BlockSpec((tm, n), lambda i: (i, 0))
    col = lambda n: pl.BlockSpec((n, tm), lambda i: (0, i))
    widths = (ATTN_WIDTH, KV_WIDTH, KV_WIDTH, M_WIDTH, M_WIDTH, M_WIDTH, M_WIDTH, GATE_PAD)
    dtypes = (act, F32, F32, act, act, act, F32, F32)
    out_specs = [row(n) for n in widths]
    out_shape = [jax.ShapeDtypeStruct((m, n), d) for n, d in zip(widths, dtypes)]
    if gate_major:
        for i, n in ((3, M_WIDTH), (5, M_WIDTH), (7, GATE_ROWS)):
            out_specs[i], out_shape[i] = col(n), jax.ShapeDtypeStruct((n, m), dtypes[i])
    return pl.pallas_call(
        functools.partial(_inproj_kernel, gate_major=gate_major),
        grid=(m // tm,),
        in_specs=[row(D_MODEL), _const_spec((1, D_MODEL)), _const_spec((D_MODEL, IN_PAD)),
                  _const_spec((GATE_ROWS, D_MODEL))],
        out_specs=out_specs,
        out_shape=out_shape,
        compiler_params=_params(("parallel",)),
        name="inproj",
    )(x, g_pre, w_pad, w_gates_t)


def _stack_heads(q_tiles):
    lane = lax.broadcasted_iota(jnp.int32, q_tiles[0].shape, 1)
    lo = lane < HEAD_DIM
    zero = jnp.zeros_like(q_tiles[0])
    parts = []
    for qt in q_tiles:
        parts += [jnp.where(lo, qt, zero), jnp.where(lo, zero, qt)]
    return jnp.concatenate(parts, axis=0)


def _unstack_heads(o, rows):
    lane = lax.broadcasted_iota(jnp.int32, (rows, LANES), 1)
    lo = lane < HEAD_DIM
    return [jnp.where(lo, o[(2 * j) * rows:(2 * j + 1) * rows, :], o[(2 * j + 1) * rows:(2 * j + 2) * rows, :])
            for j in range(GROUP)]


def _sink_rows(sink_ref, rows):
    return jnp.concatenate([jnp.full((rows, LANES), sink_ref[0, h], F32) for h in HEAD_ORDER], axis=0)


def _attn_prompt_kernel(sink_ref, q_ref, k_ref, v_ref, kp_ref, vp_ref, o_ref):
    first = pl.program_id(0) == 0
    nrow = ATTN_HEADS * WINDOW
    r = lax.broadcasted_iota(jnp.int32, (nrow, WINDOW), 0) & (WINDOW - 1)
    c = lax.broadcasted_iota(jnp.int32, (nrow, WINDOW), 1)
    own = c <= r
    sink = _sink_rows(sink_ref, WINDOW)
    no_prev = jnp.where(first, -jnp.inf, 0.0)
    ones = jnp.ones((2 * WINDOW, LANES), BF16)
    blocks = [slice(blk * WINDOW, (blk + 1) * WINDOW) for blk in range(ATTN_BLOCKS)]
    scores, v_augs = [], []
    for blk, rows in enumerate(blocks):
        if blk == 0:
            k_prev, v_prev = kp_ref[...], vp_ref[...]
        else:
            k_prev, v_prev = k_ref[blocks[blk - 1], :], v_ref[blocks[blk - 1], :]
        kcat = jnp.concatenate([k_ref[rows, :], k_prev], axis=0).astype(BF16)
        vcat = jnp.concatenate([v_ref[rows, :], v_prev], axis=0).astype(BF16)
        v_augs.append(jnp.concatenate([vcat, ones], axis=1))
        qs = _stack_heads([q_ref[rows, j * LANES:(j + 1) * LANES] for j in range(GROUP)])
        scores.append(lax.dot_general(qs, kcat, (((1,), (1,)), ((), ())), preferred_element_type=F32))
    probs, maxes = [], []
    for blk, s in enumerate(scores):
        s_prev = s[:, WINDOW:]
        if blk == 0:
            s_prev = s_prev + no_prev
        sc = jnp.where(own, s[:, :WINDOW], s_prev)
        m = jnp.maximum(jnp.max(sc, axis=-1, keepdims=True), sink)
        p = jnp.exp(sc - m)
        zero = jnp.zeros_like(p)
        probs.append(jnp.concatenate([jnp.where(own, p, zero), jnp.where(own, zero, p)], axis=1).astype(BF16))
        maxes.append(m)
    for rows, p2, v_aug, m in zip(blocks, probs, v_augs, maxes):
        o = jnp.dot(p2, v_aug, preferred_element_type=F32)
        o = o[:, :LANES] * (1.0 / (o[:, LANES:] + jnp.exp(sink - m)))
        for j, tile in enumerate(_unstack_heads(o, WINDOW)):
            o_ref[rows, j * LANES:(j + 1) * LANES] = tile.astype(o_ref.dtype)


def _attn_prompt(sink, qa, ka, va):
    m = qa.shape[0]
    tm = ATTN_BLOCKS * WINDOW
    row = lambda n: pl.BlockSpec((tm, n), lambda i: (i, 0))
    prev = pl.BlockSpec((WINDOW, KV_WIDTH), lambda i: (jnp.maximum(i * ATTN_BLOCKS - 1, 0), 0))
    return pl.pallas_call(
        _attn_prompt_kernel,
        grid=(m // tm,),
        in_specs=[pl.BlockSpec(memory_space=pltpu.SMEM), row(ATTN_WIDTH), row(KV_WIDTH), row(KV_WIDTH), prev, prev],
        out_specs=row(ATTN_WIDTH),
        out_shape=jax.ShapeDtypeStruct((m, ATTN_WIDTH), BF16),
        compiler_params=_params(("parallel",)),
        name="attn_prompt",
    )(sink, qa, ka, va, ka, va)


def _attn_sample_kernel(sink_ref, q_ref, kn_ref, vn_ref, ck_ref, cv_ref, o_ref, ko_ref, vo_ref, *, dec_seq):
    t = dec_seq
    bb = SAMPLE_ATTN_BATCH
    nrow = ATTN_HEADS * t
    ts, ns = t.bit_length() - 1, nrow.bit_length() - 1
    r_c = lax.broadcasted_iota(jnp.int32, (nrow, WINDOW), 0) & (t - 1)
    c_c = lax.broadcasted_iota(jnp.int32, (nrow, WINDOW), 1)
    vis_cache = c_c > r_c
    r_n = lax.broadcasted_iota(jnp.int32, (bb * nrow, bb * t), 0)
    c_n = lax.broadcasted_iota(jnp.int32, (bb * nrow, bb * t), 1)
    vis_new = jnp.logical_and((r_n >> ns) == (c_n >> ts), (c_n & (t - 1)) <= (r_n & (t - 1)))
    sink = _sink_rows(sink_ref, t)
    nt = (((1,), (1,)), ((), ()))
    kn_all, vn_all = kn_ref[...], vn_ref[...]
    qs, s_c = [], []
    for b in range(bb):
        rows = slice(b * t, (b + 1) * t)
        ck, cv = ck_ref[b], cv_ref[b]
        ko_ref[b] = jnp.concatenate([ck[t:, :], kn_all[rows, :]], axis=0)
        vo_ref[b] = jnp.concatenate([cv[t:, :], vn_all[rows, :]], axis=0)
        qs.append(_stack_heads([q_ref[rows, j * LANES:(j + 1) * LANES] for j in range(GROUP)]).astype(BF16))
        s_c.append(jnp.where(vis_cache, lax.dot_general(qs[b], ck.astype(BF16), nt, preferred_element_type=F32),
                             -jnp.inf))
    s_n = jnp.where(vis_new, lax.dot_general(jnp.concatenate(qs, axis=0), kn_all.astype(BF16), nt,
                                             preferred_element_type=F32), -jnp.inf)
    p_c, p_n, rden = [], [], []
    for b in range(bb):
        s_nb = s_n[b * nrow:(b + 1) * nrow, :]
        m = jnp.maximum(jnp.maximum(jnp.max(s_c[b], axis=-1, keepdims=True), jnp.max(s_nb, axis=-1, keepdims=True)),
                        sink)
        p_c.append(jnp.exp(s_c[b] - m))
        p_n.append(jnp.exp(s_nb - m[:, :bb * t]))
        rden.append(1.0 / (jnp.sum(p_c[b], axis=-1, keepdims=True) + jnp.sum(p_n[b], axis=-1, keepdims=True)
                           + jnp.exp(sink - m)))
    o_n = jnp.dot(jnp.concatenate(p_n, axis=0).astype(BF16), vn_all.astype(BF16), preferred_element_type=F32)
    outs = [[] for _ in range(GROUP)]
    for b in range(bb):
        o = (jnp.dot(p_c[b].astype(BF16), cv_ref[b].astype(BF16), preferred_element_type=F32)
             + o_n[b * nrow:(b + 1) * nrow, :]) * rden[b]
        for j, tile in enumerate(_unstack_heads(o, t)):
            outs[j].append(tile)
    for j, parts in enumerate(outs):
        o_ref[:, j * LANES:(j + 1) * LANES] = jnp.concatenate(parts, axis=0).astype(o_ref.dtype)


def _attn_sample(sink, qa, ka, va, cache_k, cache_v, dec_seq):
    nb = cache_k.shape[0]
    bb = SAMPLE_ATTN_BATCH
    row = lambda n: pl.BlockSpec((bb * dec_seq, n), lambda i: (i, 0))
    cache = pl.BlockSpec((bb, WINDOW, KV_WIDTH), lambda i: (i, 0, 0))
    return pl.pallas_call(
        functools.partial(_attn_sample_kernel, dec_seq=dec_seq),
        grid=(nb // bb,),
        in_specs=[pl.BlockSpec(memory_space=pltpu.SMEM), row(ATTN_WIDTH), row(KV_WIDTH), row(KV_WIDTH), cache, cache],
        out_specs=[row(ATTN_WIDTH), cache, cache],
        out_shape=[jax.ShapeDtypeStruct((nb * dec_seq, ATTN_WIDTH), BF16),
                   jax.ShapeDtypeStruct(cache_k.shape, F32), jax.ShapeDtypeStruct(cache_v.shape, F32)],
        compiler_params=_params(("parallel",)),
        name="attn_sample",
    )(sink, qa, ka, va, cache_k, cache_v)


def _head_out(hh, mnorm_row, om):
    y = hh * lax.rsqrt(jnp.mean(hh * hh, axis=-1, keepdims=True) + EPS) * mnorm_row
    return (jax.nn.sigmoid(om) * y).astype(BF16)


def _mlstm_prompt_kernel(qt_ref, k_ref, vt_ref, om_ref, g_ref, bias_ref, mnorm_ref, o_ref, c_out, m_out, c_s, m_s):
    step = pl.program_id(0)

    @pl.when(step == 0)
    def _():
        c_s[...] = jnp.zeros_like(c_s)
        m_s[...] = jnp.zeros_like(m_s)

    ln = WINDOW
    r = lax.broadcasted_iota(jnp.int32, (ln, ln), 0)
    c = lax.broadcasted_iota(jnp.int32, (ln, ln), 1)
    causal_t = r <= c
    upper = causal_t.astype(F32)
    lane8 = lax.broadcasted_iota(jnp.int32, (SUBLANES, ln), 1)
    ones_rows = jnp.ones((AUG_ROWS - M_HEAD_DIM, ln), BF16)
    zrows = jnp.zeros((ln - SUBLANES, ln), F32)
    hcols = [slice(h * M_HEAD_DIM, (h + 1) * M_HEAD_DIM) for h in range(M_HEADS)]
    chunks = [slice(ch * ln, (ch + 1) * ln) for ch in range(MLSTM_CHUNKS)]
    pre, scores = [], []
    for rows in chunks:
        gi = g_ref[0:SUBLANES, rows] + bias_ref[0:SUBLANES, :]
        lf = jax.nn.log_sigmoid(g_ref[SUBLANES:GATE_ROWS, rows] + bias_ref[SUBLANES:GATE_ROWS, :])
        b = jnp.dot(lf, upper, precision=HIGHEST, preferred_element_type=F32)
        g = gi - b
        cm0 = g
        sh = 1
        while sh < ln:
            cm0 = jnp.maximum(cm0, jnp.where(lane8 >= sh, pltpu.roll(cm0, sh, axis=1), -jnp.inf))
            sh *= 2
        b_last = jnp.broadcast_to(b[:, ln - 1:ln], b.shape)
        g_max = jnp.broadcast_to(cm0[:, ln - 1:ln], b.shape)
        g_cols = jnp.concatenate([g, zrows], axis=0).T
        pre.append((b, g_cols, cm0, b_last, g_max, (b_last - b) + gi))
        scores.append([jnp.dot(k_ref[rows, hc], qt_ref[hc, rows], preferred_element_type=F32) for hc in hcols])
    m_prev = m_s[...]
    scal = []
    for b, _, cm0, b_last, g_max, w_end_arg in pre:
        cm = jnp.maximum(cm0, m_prev)
        m_end = b_last + jnp.maximum(g_max, m_prev)
        scal.append((cm, jnp.exp(m_prev - cm), jnp.exp(-(b + cm)), jnp.exp(b_last + m_prev - m_end),
                     jnp.exp(w_end_arg - m_end)))
        m_prev = m_end
    m_s[...] = m_prev
    gated = []
    for rows, s_t, (_, g_cols, *_), (cm, _, _, _, w_end) in zip(chunks, scores, pre, scal):
        vts = [jnp.concatenate([vt_ref[hc, rows], ones_rows], axis=0) for hc in hcols]
        sqks = [(jnp.exp(jnp.where(causal_t, g_cols[:, h:h + 1] - cm[h:h + 1, :], -jnp.inf)) * s_t[h]).astype(BF16)
                for h in range(M_HEADS)]
        kvws = [(vts[h].astype(F32) * w_end[h:h + 1, :]).astype(BF16) for h in range(M_HEADS)]
        gated.append((vts, sqks, kvws))
    mem = [c_s[h] for h in range(M_HEADS)]
    for rows, (vts, sqks, kvws), (_, w_inter, e_negm, dec, _) in zip(chunks, gated, scal):
        upd = [jnp.dot(kvws[h], k_ref[rows, hcols[h]], preferred_element_type=F32) for h in range(M_HEADS)]
        intra = [jnp.dot(vts[h], sqks[h], preferred_element_type=F32) for h in range(M_HEADS)]
        inter = [jnp.dot(mem[h].astype(BF16), qt_ref[hcols[h], rows], preferred_element_type=F32)
                 for h in range(M_HEADS)]
        for h in range(M_HEADS):
            num = inter[h] * w_inter[h:h + 1, :] + intra[h]
            mem[h] = dec[h:h + 1, :] * mem[h] + upd[h]
            den = jnp.maximum(jnp.abs(num[M_HEAD_DIM:M_HEAD_DIM + 1, :]), e_negm[h:h + 1, :])
            hh = num[:M_HEAD_DIM, :] * (1.0 / den)
            y = hh * lax.rsqrt(jnp.mean(hh * hh, axis=0, keepdims=True) + EPS)
            o_ref[rows, hcols[h]] = (jax.nn.sigmoid(om_ref[rows, hcols[h]])
                                     * (y.T * mnorm_ref[:, hcols[h]])).astype(o_ref.dtype)
    for h in range(M_HEADS):
        c_s[h] = mem[h]

    @pl.when(step == pl.num_programs(0) - 1)
    def _():
        c_out[...] = c_s[...]
        m_out[...] = m_s[...]


def _mlstm_prompt(qt, km, vt, om, gt, bias_cols, mnorm):
    m = km.shape[0]
    tm = MLSTM_CHUNKS * WINDOW
    row = lambda n: pl.BlockSpec((tm, n), lambda i: (i, 0))
    col = lambda n: pl.BlockSpec((n, tm), lambda i: (0, i))
    whole = lambda shape: pl.BlockSpec(shape, lambda i: (0,) * len(shape))
    c_shape = (M_HEADS, AUG_ROWS, M_HEAD_DIM)
    s_shape = (SUBLANES, LANES)
    return pl.pallas_call(
        _mlstm_prompt_kernel,
        grid=(m // tm,),
        in_specs=[col(M_WIDTH), row(M_WIDTH), col(M_WIDTH), row(M_WIDTH), col(GATE_ROWS),
                  whole((GATE_ROWS, LANES)), whole((1, M_WIDTH))],
        out_specs=[row(M_WIDTH), whole(c_shape), whole(s_shape)],
        out_shape=[jax.ShapeDtypeStruct((m, M_WIDTH), BF16), jax.ShapeDtypeStruct(c_shape, F32),
                   jax.ShapeDtypeStruct(s_shape, F32)],
        scratch_shapes=[pltpu.VMEM(c_shape, F32), pltpu.VMEM(s_shape, F32)],
        compiler_params=_params(("arbitrary",)),
        name="mlstm_prompt",
    )(qt, km, vt, om, gt, bias_cols, mnorm)


def _gates(g_blk, bias_row, cum):
    pre = g_blk + bias_row
    lane = lax.broadcasted_iota(jnp.int32, pre.shape, 1)
    a = jnp.where(lane < M_HEADS, pre, jax.nn.log_sigmoid(pre))
    b = jnp.dot(cum, a, precision=HIGHEST, preferred_element_type=F32)
    return a, b, a.T, b.T


def _col(x, j):
    return jnp.broadcast_to(x[:, j:j + 1], x.shape)


def _mlstm_intra(q, k, v, a, b, at, bt, h, mask, m_prev, last):
    bc, ic = _col(b, M_HEADS + h), _col(a, h)
    br, ir = bt[M_HEADS + h:M_HEADS + h + 1, :], at[h:h + 1, :]
    d = jnp.where(mask, (bc - br) + ir, -jnp.inf)
    inter = bc + m_prev
    m_t = jnp.maximum(inter, jnp.max(d, axis=-1, keepdims=True))
    w_intra = jnp.exp(d - m_t)
    w_inter = jnp.exp(inter - m_t)
    s = lax.dot_general(q, k, (((1,), (1,)), ((), ())), preferred_element_type=F32)
    sqk = w_intra * s
    intra = jnp.dot(sqk.astype(BF16), v, preferred_element_type=F32)
    sq_sum = jnp.sum(sqk, axis=-1, keepdims=True)
    m_end = last(m_t)
    bl = last(bc)
    dec = jnp.exp(bl + m_prev - m_end)
    w_end = jnp.exp((bl - bc) + ic - m_end)
    kw = k.astype(F32) * w_end
    return w_inter, intra, sq_sum, m_t, m_end, dec, kw


def _mlstm_sample_kernel(q_ref, k_ref, v_ref, om_ref, g_ref, m0_ref, bias_ref, mnorm_ref, c_ref, n_ref,
                         o_ref, c_out, n_out, m_out, *, dec_seq):
    t = dec_seq
    bb = SAMPLE_MLSTM_BATCH
    ln = bb * t
    r = lax.broadcasted_iota(jnp.int32, (ln, ln), 0)
    c = lax.broadcasted_iota(jnp.int32, (ln, ln), 1)
    shift = t.bit_length() - 1
    same = (r >> shift) == (c >> shift)
    mask = jnp.logical_and(same, c <= r)
    cum = mask.astype(F32)
    expand = (c == (r >> shift)).astype(F32)
    gather = ((c >> shift) == r).astype(F32)
    is_last = (r & (t - 1)) == t - 1
    is_first = (r & (t - 1)) == 0

    def last(x):
        y = jnp.where(is_last, x, 0.0)
        step = 1
        while step < t:
            y = y + pltpu.roll(y, ln - step, axis=0)
            step *= 2
        return y

    a, b, at, bt = _gates(g_ref[...], bias_ref[...], cum)
    m0 = m0_ref[...]
    m_cols = jnp.zeros((ln, LANES), F32)
    lane = lax.broadcasted_iota(jnp.int32, (ln, LANES), 1)
    zpad = jnp.zeros((ln - bb, M_HEAD_DIM), F32)
    for h in range(M_HEADS):
        cols = slice(h * M_HEAD_DIM, (h + 1) * M_HEAD_DIM)
        qf = q_ref[:, cols]
        q, k, v = qf.astype(BF16), k_ref[:, cols].astype(BF16), v_ref[:, cols].astype(BF16)
        m_prev = _col(m0, h)
        w_inter, intra, sq_sum, m_t, m_end, dec, kw = _mlstm_intra(q, k, v, a, b, at, bt, h, mask, m_prev, last)
        n_old = n_ref[:, cols]
        n_exp = jnp.dot(expand, jnp.concatenate([n_old, zpad], axis=0), precision=HIGHEST,
                        preferred_element_type=F32)
        qc = jnp.concatenate(
            [jnp.dot(qf[s * t:(s + 1) * t, :], c_ref[s * M_HEADS + h].astype(BF16).astype(F32),
                     preferred_element_type=F32) for s in range(bb)], axis=0)
        num = w_inter * qc + intra
        nq = w_inter * jnp.sum(qf * n_exp, axis=-1, keepdims=True) + sq_sum
        hh = num / jnp.maximum(jnp.abs(nq), jnp.exp(-m_t))
        o_ref[:, cols] = _head_out(hh, mnorm_ref[:, cols], om_ref[:, cols])
        kwt = kw.T.astype(BF16)
        for s in range(bb):
            lhs = jnp.where((c >> shift) == s, kwt, jnp.zeros_like(kwt))
            upd = jnp.dot(lhs, v, preferred_element_type=F32)
            c_out[s * M_HEADS + h] = dec[s * t:s * t + 1, :] * c_ref[s * M_HEADS + h] + upd
        n_new = jnp.dot(gather, jnp.where(is_first, dec * n_exp, 0.0) + kw, precision=HIGHEST,
                        preferred_element_type=F32)
        n_out[:, cols] = n_new[:bb, :]
        m_cols = jnp.where(lane == h, m_end, m_cols)
    m_out[...] = m_cols


def _mlstm_sample(qm, km, vm, om, gt, m0_rows, bias_row, mnorm, c_in, n_in, dec_seq):
    m = qm.shape[0]
    bb = SAMPLE_MLSTM_BATCH
    tm = bb * dec_seq
    nb = m // dec_seq
    row = lambda n: pl.BlockSpec((tm, n), lambda i: (i, 0))
    whole = lambda shape: pl.BlockSpec(shape, lambda i: (0,) * len(shape))
    c_spec = pl.BlockSpec((bb * M_HEADS, M_HEAD_DIM, M_HEAD_DIM), lambda i: (i, 0, 0))
    n_spec = pl.BlockSpec((bb, M_WIDTH), lambda i: (i, 0))
    return pl.pallas_call(
        functools.partial(_mlstm_sample_kernel, dec_seq=dec_seq),
        grid=(nb // bb,),
        in_specs=[row(M_WIDTH), row(M_WIDTH), row(M_WIDTH), row(M_WIDTH), row(GATE_PAD), row(LANES),
                  whole((1, GATE_PAD)), whole((1, M_WIDTH)), c_spec, n_spec],
        out_specs=[row(M_WIDTH), c_spec, n_spec, row(LANES)],
        out_shape=[jax.ShapeDtypeStruct((m, M_WIDTH), BF16), jax.ShapeDtypeStruct(c_in.shape, F32),
                   jax.ShapeDtypeStruct(n_in.shape, F32), jax.ShapeDtypeStruct((m, LANES), F32)],
        compiler_params=_params(("parallel",)),
        name="mlstm_sample",
    )(qm, km, vm, om, gt, m0_rows, bias_row, mnorm, c_in, n_in)


def _out_ffn_kernel(x_ref, ma_ref, mm_ref, wo_ref, g1_ref, g2_ref, wg_ref, wu_ref, wd_ref, g3_ref, o_ref):
    tm = x_ref.shape[0]
    groups = [slice(r, r + tm // ROW_GROUPS) for r in range(0, tm, tm // ROW_GROUPS)]
    ys = [jnp.dot(ma_ref[rs, :], wo_ref[:ATTN_WIDTH, :], preferred_element_type=F32)
          + jnp.dot(mm_ref[rs, :], wo_ref[ATTN_WIDTH:, :], preferred_element_type=F32) for rs in groups]
    x1s = [x_ref[rs, :] + _rms(y, g1_ref[...]) for rs, y in zip(groups, ys)]
    fs = [_rms(x1, g2_ref[...]).astype(BF16) for x1 in x1s]
    accs = [None] * ROW_GROUPS
    for off in range(0, D_FF, FFN_CHUNK):
        acts = []
        for f in fs:
            g = jnp.dot(f, wg_ref[:, off:off + FFN_CHUNK], preferred_element_type=F32)
            u = jnp.dot(f, wu_ref[:, off:off + FFN_CHUNK], preferred_element_type=F32)
            acts.append((g * jax.nn.sigmoid(g) * u).astype(BF16))
        for i, act in enumerate(acts):
            part = jnp.dot(act, wd_ref[off:off + FFN_CHUNK, :], preferred_element_type=F32)
            accs[i] = part if accs[i] is None else accs[i] + part
    for rs, x1, acc in zip(groups, x1s, accs):
        o_ref[rs, :] = x1 + _rms(acc, g3_ref[...])


def _out_ffn(x, mix_a, mix_m, w_out, g_post_mix, g_pre_ffn, w_gate, w_up, w_down, g_post_ffn):
    m = x.shape[0]
    tm = TOKEN_TILE
    row = lambda n: pl.BlockSpec((tm, n), lambda i: (i, 0))
    vec = _const_spec((1, D_MODEL))
    return pl.pallas_call(
        _out_ffn_kernel,
        grid=(m // tm,),
        in_specs=[row(D_MODEL), row(ATTN_WIDTH), row(M_WIDTH), _const_spec((D_MODEL, D_MODEL)), vec, vec,
                  _const_spec((D_MODEL, D_FF)), _const_spec((D_MODEL, D_FF)), _const_spec((D_FF, D_MODEL)), vec],
        out_specs=row(D_MODEL),
        out_shape=jax.ShapeDtypeStruct((m, D_MODEL), F32),
        compiler_params=_params(("parallel",)),
        name="out_ffn",
    )(x, mix_a, mix_m, w_out, g_post_mix, g_pre_ffn, w_gate, w_up, w_down, g_post_ffn)


def _permute_heads(w, axis):
    shape = w.shape
    w = w.reshape(shape[:axis] + (ATTN_HEADS, HEAD_DIM) + shape[axis + 1:])
    w = jnp.take(w, jnp.array(HEAD_ORDER), axis=axis)
    return w.reshape(shape)


def _layer(xp, xs, cache_k, cache_v, state_c, state_n, state_m, w_in, b_i, b_f, attn_sink, m_norm, w_out,
           g_pre_mix, g_post_mix, g_pre_ffn, g_post_ffn, w_gate, w_up, w_down):
    bp, sp, _ = xp.shape
    bs, ts, _ = xs.shape
    assert bp == 1 and sp % (ATTN_BLOCKS * WINDOW) == 0 and sp % TOKEN_TILE == 0
    assert ts & (ts - 1) == 0 and (bs * ts) % TOKEN_TILE == 0 and bs % SAMPLE_MLSTM_BATCH == 0

    row = lambda v: v.reshape(1, -1)
    w_q = _permute_heads(w_in[:, :ATTN_WIDTH], 1)
    w_pad = jnp.concatenate([w_q, w_in[:, ATTN_WIDTH:], jnp.zeros((D_MODEL, IN_PAD - w_in.shape[1]), F32)],
                            axis=1).astype(BF16)
    zrows = jnp.zeros((SUBLANES - M_HEADS, D_MODEL), F32)
    w_gates_t = jnp.concatenate([w_in[:, IN_MAIN:IN_MAIN + M_HEADS].T, zrows,
                                 w_in[:, IN_MAIN + M_HEADS:IN_MAIN + 2 * M_HEADS].T, zrows], axis=0).astype(BF16)
    bias_row = jnp.pad(jnp.concatenate([b_i, b_f]), (0, GATE_PAD - 2 * M_HEADS)).reshape(1, GATE_PAD)
    zb = jnp.zeros((SUBLANES - M_HEADS,), F32)
    bias_cols = jnp.broadcast_to(jnp.concatenate([b_i, zb, b_f, zb])[:, None], (GATE_ROWS, LANES))
    sink = row(attn_sink)
    wo = jnp.concatenate([_permute_heads(w_out[:ATTN_WIDTH], 0), w_out[ATTN_WIDTH:]], axis=0).astype(BF16)
    wg, wu, wd = w_gate.astype(BF16), w_up.astype(BF16), w_down.astype(BF16)
    ffn = (wo, row(g_post_mix), row(g_pre_ffn), wg, wu, wd, row(g_post_ffn))

    x2 = xp.reshape(sp, D_MODEL)
    qa, ka, va, qt, km, vt, om, gt = _inproj(x2, row(g_pre_mix), w_pad, w_gates_t, True)
    mix_a = _attn_prompt(sink, qa, ka, va)
    mix_m, c_aug, m_p = _mlstm_prompt(qt, km, vt, om, gt, bias_cols, row(m_norm))
    yp = _out_ffn(x2, mix_a, mix_m, *ffn).reshape(xp.shape)
    k_p = ka[sp - WINDOW:].reshape(1, WINDOW, KV_HEADS, HEAD_DIM)
    v_p = va[sp - WINDOW:].reshape(1, WINDOW, KV_HEADS, HEAD_DIM)
    c_p = jnp.swapaxes(c_aug[:, :M_HEAD_DIM, :], 1, 2).reshape(1, M_HEADS, M_HEAD_DIM, M_HEAD_DIM)
    n_p = c_aug[:, M_HEAD_DIM, :].reshape(1, M_HEADS, M_HEAD_DIM)
    m_p = m_p[:M_HEADS, 0].reshape(1, M_HEADS)

    x2 = xs.reshape(bs * ts, D_MODEL)
    qa, ka, va, qm, km, vm, om, gt = _inproj(x2, row(g_pre_mix), w_pad, w_gates_t, False)
    mix_a, k_s, v_s = _attn_sample(sink, qa, ka, va, cache_k.reshape(bs, WINDOW, KV_WIDTH),
                                   cache_v.reshape(bs, WINDOW, KV_WIDTH), ts)
    m0_rows = jnp.pad(jnp.repeat(state_m, ts, axis=0), ((0, 0), (0, LANES - M_HEADS)))
    mix_m, c_s, n_s, m_rows = _mlstm_sample(qm, km, vm, om, gt, m0_rows, bias_row, row(m_norm),
                                            state_c.reshape(bs * M_HEADS, M_HEAD_DIM, M_HEAD_DIM),
                                            state_n.reshape(bs, M_WIDTH), ts)
    ys = _out_ffn(x2, mix_a, mix_m, *ffn).reshape(xs.shape)
    k_s = k_s.reshape(bs, WINDOW, KV_HEADS, HEAD_DIM)
    v_s = v_s.reshape(bs, WINDOW, KV_HEADS, HEAD_DIM)
    c_s = c_s.reshape(bs, M_HEADS, M_HEAD_DIM, M_HEAD_DIM)
    n_s = n_s.reshape(bs, M_HEADS, M_HEAD_DIM)
    m_s = m_rows[ts - 1::ts, :M_HEADS]
    return yp, ys, (k_p, v_p, c_p, n_p, m_p), (k_s, v_s, c_s, n_s, m_s)


def kernel(x_prompt, x_sample, cache_k, cache_v, state_C, state_n, state_m, w_in, b_i, b_f, attn_sink, m_norm,
           w_out, g_pre_mix, g_post_mix, g_pre_ffn, g_post_ffn, w_gate, w_up, w_down):
    depth = w_in.shape[0]
    xp, xs = x_prompt, x_sample
    prompt_states, sample_states = [], []
    for l in range(depth):
        xp, xs, st_p, st_s = _layer(xp, xs, cache_k[l], cache_v[l], state_C[l], state_n[l], state_m[l],
                                    w_in[l], b_i[l], b_f[l], attn_sink[l], m_norm[l], w_out[l],
                                    g_pre_mix[l], g_post_mix[l], g_pre_ffn[l], g_post_ffn[l],
                                    w_gate[l], w_up[l], w_down[l])
        prompt_states.append(st_p)
        sample_states.append(st_s)
    stack = lambda states, i: jnp.stack([s[i] for s in states], axis=0)
    return (xp, xs) + tuple(stack(prompt_states, i) for i in range(5)) + tuple(stack(sample_states, i) for i in range(5))
```

```python
import functools

import jax
import jax.numpy as jnp
from jax import lax
from jax.experimental import pallas as pl
from jax.experimental.pallas import tpu as pltpu

F32 = jnp.float32
BF16 = jnp.bfloat16
HIGHEST = lax.Precision.HIGHEST

D_MODEL = 1024
HEAD_DIM = 64
ATTN_HEADS = 8
KV_HEADS = 2
GROUP = ATTN_HEADS // KV_HEADS
ATTN_WIDTH = ATTN_HEADS * HEAD_DIM
KV_WIDTH = KV_HEADS * HEAD_DIM
WINDOW = 128
M_HEADS = 4
M_HEAD_DIM = 128
M_WIDTH = M_HEADS * M_HEAD_DIM
D_FF = 2816
EPS = 1e-6

LANES = 128
SUBLANES = 8
GATE_PAD = LANES
GATE_ROWS = 2 * SUBLANES
BF16_SUBLANES = 16
AUG_ROWS = M_HEAD_DIM + BF16_SUBLANES
IN_MAIN = ATTN_WIDTH + 2 * KV_WIDTH + 4 * M_WIDTH
IN_PAD = IN_MAIN + GATE_PAD
VMEM_LIMIT = 56 * 1024 * 1024

HEAD_ORDER = tuple(h for j in range(GROUP) for h in (j, j + GROUP))

TOKEN_TILE = 512
FFN_CHUNK = 1408
ROW_GROUPS = 2
SAMPLE_ATTN_BATCH = 8
SAMPLE_MLSTM_BATCH = 16


def _rms(x, g):
    return x * lax.rsqrt(jnp.mean(x * x, axis=-1, keepdims=True) + EPS) * g


def _const_spec(shape):
    nd = len(shape)
    return pl.BlockSpec(shape, lambda i: (0,) * nd, pipeline_mode=pl.Buffered(1))


def _params(semantics):
    return pltpu.CompilerParams(dimension_semantics=semantics, vmem_limit_bytes=VMEM_LIMIT)


OFF_QA, OFF_KA, OFF_VA = 0, ATTN_WIDTH, ATTN_WIDTH + KV_WIDTH
OFF_QM = ATTN_WIDTH + 2 * KV_WIDTH
OFF_KM, OFF_VM, OFF_OM = OFF_QM + M_WIDTH, OFF_QM + 2 * M_WIDTH, OFF_QM + 3 * M_WIDTH


def _inproj_kernel(x_ref, g_ref, w_ref, qa_ref, ka_ref, va_ref, qm_ref, km_ref, vm_ref, om_ref, gt_ref):
    tm = x_ref.shape[0]
    step = tm // ROW_GROUPS
    for r0 in range(0, tm, step):
        rs = slice(r0, r0 + step)
        h = _rms(x_ref[rs, :], g_ref[...]).astype(BF16)

        def proj(off, n):
            return jnp.dot(h, w_ref[:, off:off + n], preferred_element_type=F32)

        qa_ref[rs, :] = proj(OFF_QA, ATTN_WIDTH) * (HEAD_DIM ** -0.5)
        ka_ref[rs, :] = proj(OFF_KA, KV_WIDTH)
        va_ref[rs, :] = proj(OFF_VA, KV_WIDTH)
        qm_ref[rs, :] = proj(OFF_QM, M_WIDTH)
        km_ref[rs, :] = proj(OFF_KM, M_WIDTH) * (M_HEAD_DIM ** -0.5)
        vm_ref[rs, :] = proj(OFF_VM, M_WIDTH)
        om_ref[rs, :] = proj(OFF_OM, M_WIDTH)
        gt_ref[rs, :] = proj(IN_MAIN, GATE_PAD)


def _inproj(x, g_pre, w_pad):
    m = x.shape[0]
    tm = TOKEN_TILE
    row = lambda n: pl.BlockSpec((tm, n), lambda i: (i, 0))
    widths = (ATTN_WIDTH, KV_WIDTH, KV_WIDTH, M_WIDTH, M_WIDTH, M_WIDTH, M_WIDTH, GATE_PAD)
    return pl.pallas_call(
        _inproj_kernel,
        grid=(m // tm,),
        in_specs=[row(D_MODEL), _const_spec((1, D_MODEL)), _const_spec((D_MODEL, IN_PAD))],
        out_specs=[row(n) for n in widths],
        out_shape=[jax.ShapeDtypeStruct((m, n), F32) for n in widths],
        compiler_params=_params(("parallel",)),
        name="inproj",
    )(x, g_pre, w_pad)


def _stack_heads(q_tiles):
    lane = lax.broadcasted_iota(jnp.int32, q_tiles[0].shape, 1)
    lo = lane < HEAD_DIM
    zero = jnp.zeros_like(q_tiles[0])
    parts = []
    for qt in q_tiles:
        parts += [jnp.where(lo, qt, zero), jnp.where(lo, zero, qt)]
    return jnp.concatenate(parts, axis=0)


def _unstack_heads(o, rows):
    lane = lax.broadcasted_iota(jnp.int32, (rows, LANES), 1)
    lo = lane < HEAD_DIM
    return [jnp.where(lo, o[(2 * j) * rows:(2 * j + 1) * rows, :], o[(2 * j + 1) * rows:(2 * j + 2) * rows, :])
            for j in range(GROUP)]


def _sink_rows(sink_ref, rows):
    return jnp.concatenate([jnp.full((rows, LANES), sink_ref[0, h], F32) for h in HEAD_ORDER], axis=0)


def _attn_sample_kernel(sink_ref, q_ref, kn_ref, vn_ref, ck_ref, cv_ref, o_ref, ko_ref, vo_ref, *, dec_seq):
    t = dec_seq
    bb = SAMPLE_ATTN_BATCH
    nrow = ATTN_HEADS * t
    ts, ns = t.bit_length() - 1, nrow.bit_length() - 1
    r_c = lax.broadcasted_iota(jnp.int32, (nrow, WINDOW), 0) & (t - 1)
    c_c = lax.broadcasted_iota(jnp.int32, (nrow, WINDOW), 1)
    vis_cache = c_c > r_c
    r_n = lax.broadcasted_iota(jnp.int32, (bb * nrow, bb * t), 0)
    c_n = lax.broadcasted_iota(jnp.int32, (bb * nrow, bb * t), 1)
    vis_new = jnp.logical_and((r_n >> ns) == (c_n >> ts), (c_n & (t - 1)) <= (r_n & (t - 1)))
    sink = _sink_rows(sink_ref, t)
    nt = (((1,), (1,)), ((), ()))
    kn_all, vn_all = kn_ref[...], vn_ref[...]
    qs, s_c = [], []
    for b in range(bb):
        rows = slice(b * t, (b + 1) * t)
        ck, cv = ck_ref[b], cv_ref[b]
        ko_ref[b] = jnp.concatenate([ck[t:, :], kn_all[rows, :]], axis=0)
        vo_ref[b] = jnp.concatenate([cv[t:, :], vn_all[rows, :]], axis=0)
        qs.append(_stack_heads([q_ref[rows, j * LANES:(j + 1) * LANES] for j in range(GROUP)]).astype(BF16))
        s_c.append(jnp.where(vis_cache, lax.dot_general(qs[b], ck.astype(BF16), nt, preferred_element_type=F32),
                             -jnp.inf))
    s_n = jnp.where(vis_new, lax.dot_general(jnp.concatenate(qs, axis=0), kn_all.astype(BF16), nt,
                                             preferred_element_type=F32), -jnp.inf)
    p_c, p_n, rden = [], [], []
    for b in range(bb):
        s_nb = s_n[b * nrow:(b + 1) * nrow, :]
        m = jnp.maximum(jnp.maximum(jnp.max(s_c[b], axis=-1, keepdims=True), jnp.max(s_nb, axis=-1, keepdims=True)),
                        sink)
        p_c.append(jnp.exp(s_c[b] - m))
        p_n.append(jnp.exp(s_nb - m[:, :bb * t]))
        rden.append(1.0 / (jnp.sum(p_c[b], axis=-1, keepdims=True) + jnp.sum(p_n[b], axis=-1, keepdims=True)
                           + jnp.exp(sink - m)))
    o_n = jnp.dot(jnp.concatenate(p_n, axis=0).astype(BF16), vn_all.astype(BF16), preferred_element_type=F32)
    outs = [[] for _ in range(GROUP)]
    for b in range(bb):
        o = (jnp.dot(p_c[b].astype(BF16), cv_ref[b].astype(BF16), preferred_element_type=F32)
             + o_n[b * nrow:(b + 1) * nrow, :]) * rden[b]
        for j, tile in enumerate(_unstack_heads(o, t)):
            outs[j].append(tile)
    for j, parts in enumerate(outs):
        o_ref[:, j * LANES:(j + 1) * LANES] = jnp.concatenate(parts, axis=0).astype(o_ref.dtype)


def _attn_sample(sink, qa, ka, va, cache_k, cache_v, dec_seq):
    nb = cache_k.shape[0]
    bb = SAMPLE_ATTN_BATCH
    row = lambda n: pl.BlockSpec((bb * dec_seq, n), lambda i: (i, 0))
    cache = pl.BlockSpec((bb, WINDOW, KV_WIDTH), lambda i: (i, 0, 0))
    return pl.pallas_call(
        functools.partial(_attn_sample_kernel, dec_seq=dec_seq),
        grid=(nb // bb,),
        in_specs=[pl.BlockSpec(memory_space=pltpu.SMEM), row(ATTN_WIDTH), row(KV_WIDTH), row(KV_WIDTH), cache, cache],
        out_specs=[row(ATTN_WIDTH), cache, cache],
        out_shape=[jax.ShapeDtypeStruct((nb * dec_seq, ATTN_WIDTH), BF16),
                   jax.ShapeDtypeStruct(cache_k.shape, F32), jax.ShapeDtypeStruct(cache_v.shape, F32)],
        compiler_params=_params(("parallel",)),
        name="attn_sample",
    )(sink, qa, ka, va, cache_k, cache_v)


def _head_out(hh, mnorm_row, om):
    y = hh * lax.rsqrt(jnp.mean(hh * hh, axis=-1, keepdims=True) + EPS) * mnorm_row
    return (jax.nn.sigmoid(om) * y).astype(BF16)


def _mixer_prompt_kernel(sink_ref, x_ref, g_ref, w_ref, wgt_ref, bias_ref, mnorm_ref,
                         mix_ref, kw_ref, vw_ref, c_ref, m_ref):
    step = pl.program_id(0)

    @pl.when(step == 0)
    def _():
        kw_ref[...] = jnp.zeros_like(kw_ref)
        vw_ref[...] = jnp.zeros_like(vw_ref)
        c_ref[...] = jnp.zeros_like(c_ref)
        m_ref[...] = jnp.zeros_like(m_ref)

    tm = x_ref.shape[0]
    ln = WINDOW
    groups = [slice(r0, r0 + tm // ROW_GROUPS) for r0 in range(0, tm, tm // ROW_GROUPS)]
    blocks = [slice(r0, r0 + ln) for r0 in range(0, tm, ln)]
    hcols = [slice(h * M_HEAD_DIM, (h + 1) * M_HEAD_DIM) for h in range(M_HEADS)]
    nt = (((1,), (1,)), ((), ()))
    hs = [_rms(x_ref[rs, :], g_ref[...]).astype(BF16) for rs in groups]

    def proj(off, n):
        return jnp.concatenate([jnp.dot(h, w_ref[:, off:off + n], preferred_element_type=F32) for h in hs], axis=0)

    qt = proj(OFF_QM, M_WIDTH).T.astype(BF16)
    km = (proj(OFF_KM, M_WIDTH) * (M_HEAD_DIM ** -0.5)).astype(BF16)
    vt = proj(OFF_VM, M_WIDTH).T.astype(BF16)
    gates = jnp.concatenate([lax.dot_general(wgt_ref[...], h, nt, preferred_element_type=F32) for h in hs], axis=1)
    r = lax.broadcasted_iota(jnp.int32, (ln, ln), 0)
    c = lax.broadcasted_iota(jnp.int32, (ln, ln), 1)
    causal_t = r <= c
    upper = causal_t.astype(F32)
    lane8 = lax.broadcasted_iota(jnp.int32, (SUBLANES, ln), 1)
    ones_rows = jnp.ones((AUG_ROWS - M_HEAD_DIM, ln), BF16)
    zrows = jnp.zeros((ln - SUBLANES, ln), F32)
    pre, scores_m = [], []
    for rows in blocks:
        gi = gates[0:SUBLANES, rows] + bias_ref[0:SUBLANES, :]
        lf = jax.nn.log_sigmoid(gates[SUBLANES:GATE_ROWS, rows] + bias_ref[SUBLANES:GATE_ROWS, :])
        b = jnp.dot(lf, upper, precision=HIGHEST, preferred_element_type=F32)
        g = gi - b
        cm0 = g
        sh = 1
        while sh < ln:
            cm0 = jnp.maximum(cm0, jnp.where(lane8 >= sh, pltpu.roll(cm0, sh, axis=1), -jnp.inf))
            sh *= 2
        b_last = jnp.broadcast_to(b[:, ln - 1:ln], b.shape)
        g_max = jnp.broadcast_to(cm0[:, ln - 1:ln], b.shape)
        g_cols = jnp.concatenate([g, zrows], axis=0).T
        pre.append((b, g_cols, cm0, b_last, g_max, (b_last - b) + gi))
        scores_m.append([jnp.dot(km[rows, hc], qt[hc, rows], preferred_element_type=F32) for hc in hcols])

    qa = (proj(OFF_QA, ATTN_WIDTH) * (HEAD_DIM ** -0.5)).astype(BF16)
    ka = proj(OFF_KA, KV_WIDTH)
    va = proj(OFF_VA, KV_WIDTH)
    nrow = ATTN_HEADS * ln
    ra = lax.broadcasted_iota(jnp.int32, (nrow, ln), 0) & (ln - 1)
    ca = lax.broadcasted_iota(jnp.int32, (nrow, ln), 1)
    own = ca <= ra
    sink = _sink_rows(sink_ref, ln)
    no_prev = jnp.where(step == 0, -jnp.inf, 0.0)
    ones = jnp.ones((2 * ln, LANES), BF16)
    scores_a, v_augs = [], []
    for blk, rows in enumerate(blocks):
        if blk == 0:
            k_prev, v_prev = kw_ref[...], vw_ref[...]
        else:
            k_prev, v_prev = ka[blocks[blk - 1], :], va[blocks[blk - 1], :]
        kcat = jnp.concatenate([ka[rows, :], k_prev], axis=0).astype(BF16)
        vcat = jnp.concatenate([va[rows, :], v_prev], axis=0).astype(BF16)
        v_augs.append(jnp.concatenate([vcat, ones], axis=1))
        qs = _stack_heads([qa[rows, j * LANES:(j + 1) * LANES] for j in range(GROUP)])
        scores_a.append(lax.dot_general(qs, kcat, nt, preferred_element_type=F32))
    kw_ref[...] = ka[blocks[-1], :]
    vw_ref[...] = va[blocks[-1], :]

    m_prev = m_ref[...]
    scal = []
    for b, _, cm0, b_last, g_max, w_end_arg in pre:
        cm = jnp.maximum(cm0, m_prev)
        m_end = b_last + jnp.maximum(g_max, m_prev)
        scal.append((cm, jnp.exp(m_prev - cm), jnp.exp(-(b + cm)), jnp.exp(b_last + m_prev - m_end),
                     jnp.exp(w_end_arg - m_end)))
        m_prev = m_end
    m_ref[...] = m_prev
    gated = []
    for rows, s_t, (_, g_cols, *_), (cm, _, _, _, w_end) in zip(blocks, scores_m, pre, scal):
        vts = [jnp.concatenate([vt[hc, rows], ones_rows], axis=0) for hc in hcols]
        sqks = [(jnp.exp(jnp.where(causal_t, g_cols[:, h:h + 1] - cm[h:h + 1, :], -jnp.inf)) * s_t[h]).astype(BF16)
                for h in range(M_HEADS)]
        kvws = [(vts[h].astype(F32) * w_end[h:h + 1, :]).astype(BF16) for h in range(M_HEADS)]
        gated.append((vts, sqks, kvws))
    probs, maxes = [], []
    for blk, s in enumerate(scores_a):
        s_prev = s[:, ln:]
        if blk == 0:
            s_prev = s_prev + no_prev
        sc = jnp.where(own, s[:, :ln], s_prev)
        mx = jnp.maximum(jnp.max(sc, axis=-1, keepdims=True), sink)
        p = jnp.exp(sc - mx)
        zero = jnp.zeros_like(p)
        probs.append(jnp.concatenate([jnp.where(own, p, zero), jnp.where(own, zero, p)], axis=1).astype(BF16))
        maxes.append(mx)
    om = proj(OFF_OM, M_WIDTH)

    mem = [c_ref[h] for h in range(M_HEADS)]
    for rows, (vts, sqks, kvws), (_, w_inter, e_negm, dec, _), p2, v_aug, mx in zip(
            blocks, gated, scal, probs, v_augs, maxes):
        upd = [jnp.dot(kvws[h], km[rows, hcols[h]], preferred_element_type=F32) for h in range(M_HEADS)]
        intra = [jnp.dot(vts[h], sqks[h], preferred_element_type=F32) for h in range(M_HEADS)]
        inter = [jnp.dot(mem[h].astype(BF16), qt[hcols[h], rows], preferred_element_type=F32)
                 for h in range(M_HEADS)]
        o = jnp.dot(p2, v_aug, preferred_element_type=F32)
        o = o[:, :LANES] * (1.0 / (o[:, LANES:] + jnp.exp(sink - mx)))
        for j, tile in enumerate(_unstack_heads(o, ln)):
            mix_ref[rows, j * LANES:(j + 1) * LANES] = tile.astype(mix_ref.dtype)
        for h in range(M_HEADS):
            num = inter[h] * w_inter[h:h + 1, :] + intra[h]
            mem[h] = dec[h:h + 1, :] * mem[h] + upd[h]
            den = jnp.maximum(jnp.abs(num[M_HEAD_DIM:M_HEAD_DIM + 1, :]), e_negm[h:h + 1, :])
            hh = num[:M_HEAD_DIM, :] * (1.0 / den)
            y = hh * lax.rsqrt(jnp.mean(hh * hh, axis=0, keepdims=True) + EPS)
            mcols = slice(ATTN_WIDTH + h * M_HEAD_DIM, ATTN_WIDTH + (h + 1) * M_HEAD_DIM)
            mix_ref[rows, mcols] = (jax.nn.sigmoid(om[rows, hcols[h]])
                                    * (y.T * mnorm_ref[:, hcols[h]])).astype(mix_ref.dtype)
    for h in range(M_HEADS):
        c_ref[h] = mem[h]


def _mixer_prompt(sink, x, g_pre, w_pad, w_gates_t, bias_cols, mnorm):
    m = x.shape[0]
    tm = TOKEN_TILE
    row = lambda n: pl.BlockSpec((tm, n), lambda i: (i, 0))
    whole = lambda shape: pl.BlockSpec(shape, lambda i: (0,) * len(shape))
    c_shape = (M_HEADS, AUG_ROWS, M_HEAD_DIM)
    w_shape = (WINDOW, KV_WIDTH)
    s_shape = (SUBLANES, LANES)
    return pl.pallas_call(
        _mixer_prompt_kernel,
        grid=(m // tm,),
        in_specs=[pl.BlockSpec(memory_space=pltpu.SMEM), row(D_MODEL), _const_spec((1, D_MODEL)),
                  _const_spec((D_MODEL, IN_PAD)), _const_spec((GATE_ROWS, D_MODEL)),
                  _const_spec((GATE_ROWS, LANES)), _const_spec((1, M_WIDTH))],
        out_specs=[row(ATTN_WIDTH + M_WIDTH), whole(w_shape), whole(w_shape), whole(c_shape), whole(s_shape)],
        out_shape=[jax.ShapeDtypeStruct((m, ATTN_WIDTH + M_WIDTH), BF16), jax.ShapeDtypeStruct(w_shape, F32),
                   jax.ShapeDtypeStruct(w_shape, F32), jax.ShapeDtypeStruct(c_shape, F32),
                   jax.ShapeDtypeStruct(s_shape, F32)],
        compiler_params=_params(("arbitrary",)),
        name="mixer_prompt",
    )(sink, x, g_pre, w_pad, w_gates_t, bias_cols, mnorm)


def _gates(g_blk, bias_row, cum):
    pre = g_blk + bias_row
    lane = lax.broadcasted_iota(jnp.int32, pre.shape, 1)
    a = jnp.where(lane < M_HEADS, pre, jax.nn.log_sigmoid(pre))
    b = jnp.dot(cum, a, precision=HIGHEST, preferred_element_type=F32)
    return a, b, a.T, b.T


def _col(x, j):
    return jnp.broadcast_to(x[:, j:j + 1], x.shape)


def _mlstm_intra(q, k, v, a, b, at, bt, h, mask, m_prev, last):
    bc, ic = _col(b, M_HEADS + h), _col(a, h)
    br, ir = bt[M_HEADS + h:M_HEADS + h + 1, :], at[h:h + 1, :]
    d = jnp.where(mask, (bc - br) + ir, -jnp.inf)
    inter = bc + m_prev
    m_t = jnp.maximum(inter, jnp.max(d, axis=-1, keepdims=True))
    w_intra = jnp.exp(d - m_t)
    w_inter = jnp.exp(inter - m_t)
    s = lax.dot_general(q, k, (((1,), (1,)), ((), ())), preferred_element_type=F32)
    sqk = w_intra * s
    intra = jnp.dot(sqk.astype(BF16), v, preferred_element_type=F32)
    sq_sum = jnp.sum(sqk, axis=-1, keepdims=True)
    m_end = last(m_t)
    bl = last(bc)
    dec = jnp.exp(bl + m_prev - m_end)
    w_end = jnp.exp((bl - bc) + ic - m_end)
    kw = k.astype(F32) * w_end
    return w_inter, intra, sq_sum, m_t, m_end, dec, kw


def _mlstm_sample_kernel(q_ref, k_ref, v_ref, om_ref, g_ref, m0_ref, bias_ref, mnorm_ref, c_ref, n_ref,
                         o_ref, c_out, n_out, m_out, *, dec_seq):
    t = dec_seq
    bb = SAMPLE_MLSTM_BATCH
    ln = bb * t
    r = lax.broadcasted_iota(jnp.int32, (ln, ln), 0)
    c = lax.broadcasted_iota(jnp.int32, (ln, ln), 1)
    shift = t.bit_length() - 1
    same = (r >> shift) == (c >> shift)
    mask = jnp.logical_and(same, c <= r)
    cum = mask.astype(F32)
    expand = (c == (r >> shift)).astype(F32)
    gather = ((c >> shift) == r).astype(F32)
    is_last = (r & (t - 1)) == t - 1
    is_first = (r & (t - 1)) == 0

    def last(x):
        y = jnp.where(is_last, x, 0.0)
        step = 1
        while step < t:
            y = y + pltpu.roll(y, ln - step, axis=0)
            step *= 2
        return y

    a, b, at, bt = _gates(g_ref[...], bias_ref[...], cum)
    m0 = m0_ref[...]
    m_cols = jnp.zeros((ln, LANES), F32)
    lane = lax.broadcasted_iota(jnp.int32, (ln, LANES), 1)
    zpad = jnp.zeros((ln - bb, M_HEAD_DIM), F32)
    for h in range(M_HEADS):
        cols = slice(h * M_HEAD_DIM, (h + 1) * M_HEAD_DIM)
        qf = q_ref[:, cols]
        q, k, v = qf.astype(BF16), k_ref[:, cols].astype(BF16), v_ref[:, cols].astype(BF16)
        m_prev = _col(m0, h)
        w_inter, intra, sq_sum, m_t, m_end, dec, kw = _mlstm_intra(q, k, v, a, b, at, bt, h, mask, m_prev, last)
        n_old = n_ref[:, cols]
        n_exp = jnp.dot(expand, jnp.concatenate([n_old, zpad], axis=0), precision=HIGHEST,
                        preferred_element_type=F32)
        qc = jnp.concatenate(
            [jnp.dot(qf[s * t:(s + 1) * t, :], c_ref[s * M_HEADS + h].astype(BF16).astype(F32),
                     preferred_element_type=F32) for s in range(bb)], axis=0)
        num = w_inter * qc + intra
        nq = w_inter * jnp.sum(qf * n_exp, axis=-1, keepdims=True) + sq_sum
        hh = num / jnp.maximum(jnp.abs(nq), jnp.exp(-m_t))
        o_ref[:, cols] = _head_out(hh, mnorm_ref[:, cols], om_ref[:, cols])
        kwt = kw.T.astype(BF16)
        for s in range(bb):
            lhs = jnp.where((c >> shift) == s, kwt, jnp.zeros_like(kwt))
            upd = jnp.dot(lhs, v, preferred_element_type=F32)
            c_out[s * M_HEADS + h] = dec[s * t:s * t + 1, :] * c_ref[s * M_HEADS + h] + upd
        n_new = jnp.dot(gather, jnp.where(is_first, dec * n_exp, 0.0) + kw, precision=HIGHEST,
                        preferred_element_type=F32)
        n_out[:, cols] = n_new[:bb, :]
        m_cols = jnp.where(lane == h, m_end, m_cols)
    m_out[...] = m_cols


def _mlstm_sample(qm, km, vm, om, gt, m0_rows, bias_row, mnorm, c_in, n_in, dec_seq):
    m = qm.shape[0]
    bb = SAMPLE_MLSTM_BATCH
    tm = bb * dec_seq
    nb = m // dec_seq
    row = lambda n: pl.BlockSpec((tm, n), lambda i: (i, 0))
    whole = lambda shape: pl.BlockSpec(shape, lambda i: (0,) * len(shape))
    c_spec = pl.BlockSpec((bb * M_HEADS, M_HEAD_DIM, M_HEAD_DIM), lambda i: (i, 0, 0))
    n_spec = pl.BlockSpec((bb, M_WIDTH), lambda i: (i, 0))
    return pl.pallas_call(
        functools.partial(_mlstm_sample_kernel, dec_seq=dec_seq),
        grid=(nb // bb,),
        in_specs=[row(M_WIDTH), row(M_WIDTH), row(M_WIDTH), row(M_WIDTH), row(GATE_PAD), row(LANES),
                  whole((1, GATE_PAD)), whole((1, M_WIDTH)), c_spec, n_spec],
        out_specs=[row(M_WIDTH), c_spec, n_spec, row(LANES)],
        out_shape=[jax.ShapeDtypeStruct((m, M_WIDTH), BF16), jax.ShapeDtypeStruct(c_in.shape, F32),
                   jax.ShapeDtypeStruct(n_in.shape, F32), jax.ShapeDtypeStruct((m, LANES), F32)],
        compiler_params=_params(("parallel",)),
        name="mlstm_sample",
    )(qm, km, vm, om, gt, m0_rows, bias_row, mnorm, c_in, n_in)


def _out_ffn_kernel(x_ref, mix_ref, wo_ref, g1_ref, g2_ref, wg_ref, wu_ref, wd_ref, g3_ref, o_ref):
    tm = x_ref.shape[0]
    groups = [slice(r, r + tm // ROW_GROUPS) for r in range(0, tm, tm // ROW_GROUPS)]
    ys = [jnp.dot(mix_ref[rs, :], wo_ref[...], preferred_element_type=F32) for rs in groups]
    x1s = [x_ref[rs, :] + _rms(y, g1_ref[...]) for rs, y in zip(groups, ys)]
    fs = [_rms(x1, g2_ref[...]).astype(BF16) for x1 in x1s]
    accs = [None] * ROW_GROUPS
    for off in range(0, D_FF, FFN_CHUNK):
        acts = []
        for f in fs:
            g = jnp.dot(f, wg_ref[:, off:off + FFN_CHUNK], preferred_element_type=F32)
            u = jnp.dot(f, wu_ref[:, off:off + FFN_CHUNK], preferred_element_type=F32)
            acts.append((g * jax.nn.sigmoid(g) * u).astype(BF16))
        for i, act in enumerate(acts):
            part = jnp.dot(act, wd_ref[off:off + FFN_CHUNK, :], preferred_element_type=F32)
            accs[i] = part if accs[i] is None else accs[i] + part
    for rs, x1, acc in zip(groups, x1s, accs):
        o_ref[rs, :] = x1 + _rms(acc, g3_ref[...])


def _out_ffn(x, mix, w_out, g_post_mix, g_pre_ffn, w_gate, w_up, w_down, g_post_ffn):
    m = x.shape[0]
    tm = TOKEN_TILE
    row = lambda n: pl.BlockSpec((tm, n), lambda i: (i, 0))
    vec = _const_spec((1, D_MODEL))
    return pl.pallas_call(
        _out_ffn_kernel,
        grid=(m // tm,),
        in_specs=[row(D_MODEL), row(ATTN_WIDTH + M_WIDTH), _const_spec((D_MODEL, D_MODEL)), vec, vec,
                  _const_spec((D_MODEL, D_FF)), _const_spec((D_MODEL, D_FF)), _const_spec((D_FF, D_MODEL)), vec],
        out_specs=row(D_MODEL),
        out_shape=jax.ShapeDtypeStruct((m, D_MODEL), F32),
        compiler_params=_params(("parallel",)),
        name="out_ffn",
    )(x, mix, w_out, g_post_mix, g_pre_ffn, w_gate, w_up, w_down, g_post_ffn)


def _permute_heads(w, axis):
    shape = w.shape
    w = w.reshape(shape[:axis] + (ATTN_HEADS, HEAD_DIM) + shape[axis + 1:])
    w = jnp.take(w, jnp.array(HEAD_ORDER), axis=axis)
    return w.reshape(shape)


def _layer(xp, xs, cache_k, cache_v, state_c, state_n, state_m, w_in, b_i, b_f, attn_sink, m_norm, w_out,
           g_pre_mix, g_post_mix, g_pre_ffn, g_post_ffn, w_gate, w_up, w_down):
    bp, sp, _ = xp.shape
    bs, ts, _ = xs.shape
    assert bp == 1 and sp % TOKEN_TILE == 0 and TOKEN_TILE % WINDOW == 0
    assert ts & (ts - 1) == 0 and (bs * ts) % TOKEN_TILE == 0 and bs % SAMPLE_MLSTM_BATCH == 0

    row = lambda v: v.reshape(1, -1)
    w_q = _permute_heads(w_in[:, :ATTN_WIDTH], 1)
    w_pad = jnp.concatenate([w_q, w_in[:, ATTN_WIDTH:], jnp.zeros((D_MODEL, IN_PAD - w_in.shape[1]), F32)],
                            axis=1).astype(BF16)
    zrows = jnp.zeros((SUBLANES - M_HEADS, D_MODEL), F32)
    w_gates_t = jnp.concatenate([w_in[:, IN_MAIN:IN_MAIN + M_HEADS].T, zrows,
                                 w_in[:, IN_MAIN + M_HEADS:IN_MAIN + 2 * M_HEADS].T, zrows], axis=0).astype(BF16)
    bias_row = jnp.pad(jnp.concatenate([b_i, b_f]), (0, GATE_PAD - 2 * M_HEADS)).reshape(1, GATE_PAD)
    zb = jnp.zeros((SUBLANES - M_HEADS,), F32)
    bias_cols = jnp.broadcast_to(jnp.concatenate([b_i, zb, b_f, zb])[:, None], (GATE_ROWS, LANES))
    sink = row(attn_sink)
    wo = jnp.concatenate([_permute_heads(w_out[:ATTN_WIDTH], 0), w_out[ATTN_WIDTH:]], axis=0).astype(BF16)
    wg, wu, wd = w_gate.astype(BF16), w_up.astype(BF16), w_down.astype(BF16)
    ffn = (wo, row(g_post_mix), row(g_pre_ffn), wg, wu, wd, row(g_post_ffn))

    x2 = xp.reshape(sp, D_MODEL)
    mix, k_w, v_w, c_aug, m_p = _mixer_prompt(sink, x2, row(g_pre_mix), w_pad, w_gates_t, bias_cols, row(m_norm))
    yp = _out_ffn(x2, mix, *ffn).reshape(xp.shape)
    k_p = k_w.reshape(1, WINDOW, KV_HEADS, HEAD_DIM)
    v_p = v_w.reshape(1, WINDOW, KV_HEADS, HEAD_DIM)
    c_p = jnp.swapaxes(c_aug[:, :M_HEAD_DIM, :], 1, 2).reshape(1, M_HEADS, M_HEAD_DIM, M_HEAD_DIM)
    n_p = c_aug[:, M_HEAD_DIM, :].reshape(1, M_HEADS, M_HEAD_DIM)
    m_p = m_p[:M_HEADS, 0].reshape(1, M_HEADS)

    x2 = xs.reshape(bs * ts, D_MODEL)
    qa, ka, va, qm, km, vm, om, gt = _inproj(x2, row(g_pre_mix), w_pad)
    mix_a, k_s, v_s = _attn_sample(sink, qa, ka, va, cache_k.reshape(bs, WINDOW, KV_WIDTH),
                                   cache_v.reshape(bs, WINDOW, KV_WIDTH), ts)
    m0_rows = jnp.pad(jnp.repeat(state_m, ts, axis=0), ((0, 0), (0, LANES - M_HEADS)))
    mix_m, c_s, n_s, m_rows = _mlstm_sample(qm, km, vm, om, gt, m0_rows, bias_row, row(m_norm),
                                            state_c.reshape(bs * M_HEADS, M_HEAD_DIM, M_HEAD_DIM),
                                            state_n.reshape(bs, M_WIDTH), ts)
    ys = _out_ffn(x2, jnp.concatenate([mix_a, mix_m], axis=1), *ffn).reshape(xs.shape)
    k_s = k_s.reshape(bs, WINDOW, KV_HEADS, HEAD_DIM)
    v_s = v_s.reshape(bs, WINDOW, KV_HEADS, HEAD_DIM)
    c_s = c_s.reshape(bs, M_HEADS, M_HEAD_DIM, M_HEAD_DIM)
    n_s = n_s.reshape(bs, M_HEADS, M_HEAD_DIM)
    m_s = m_rows[ts - 1::ts, :M_HEADS]
    return yp, ys, (k_p, v_p, c_p, n_p, m_p), (k_s, v_s, c_s, n_s, m_s)


def kernel(x_prompt, x_sample, cache_k, cache_v, state_C, state_n, state_m, w_in, b_i, b_f, attn_sink, m_norm,
           w_out, g_pre_mix, g_post_mix, g_pre_ffn, g_post_ffn, w_gate, w_up, w_down):
    depth = w_in.shape[0]
    xp, xs = x_prompt, x_sample
    prompt_states, sample_states = [], []
    for l in range(depth):
        xp, xs, st_p, st_s = _layer(xp, xs, cache_k[l], cache_v[l], state_C[l], state_n[l], state_m[l],
                                    w_in[l], b_i[l], b_f[l], attn_sink[l], m_norm[l], w_out[l],
                                    g_pre_mix[l], g_post_mix[l], g_pre_ffn[l], g_post_ffn[l],
                                    w_gate[l], w_up[l], w_down[l])
        prompt_states.append(st_p)
        sample_states.append(st_s)
    stack = lambda states, i: jnp.stack([s[i] for s in states], axis=0)
    return (xp, xs) + tuple(stack(prompt_states, i) for i in range(5)) + tuple(stack(sample_states, i) for i in range(5))
```

```python
import functools

import jax
import jax.numpy as jnp
from jax import lax
from jax.experimental import pallas as pl
from jax.experimental.pallas import tpu as pltpu

F32 = jnp.float32
BF16 = jnp.bfloat16
HIGHEST = lax.Precision.HIGHEST

D_MODEL = 1024
HEAD_DIM = 64
ATTN_HEADS = 8
KV_HEADS = 2
GROUP = ATTN_HEADS // KV_HEADS
ATTN_WIDTH = ATTN_HEADS * HEAD_DIM
KV_WIDTH = KV_HEADS * HEAD_DIM
WINDOW = 128
M_HEADS = 4
M_HEAD_DIM = 128
M_WIDTH = M_HEADS * M_HEAD_DIM
D_FF = 2816
EPS = 1e-6

LANES = 128
SUBLANES = 8
GATE_PAD = LANES
GATE_ROWS = 2 * SUBLANES
BF16_SUBLANES = 16
AUG_ROWS = M_HEAD_DIM + BF16_SUBLANES
IN_MAIN = ATTN_WIDTH + 2 * KV_WIDTH + 4 * M_WIDTH
IN_PAD = IN_MAIN + GATE_PAD
VMEM_LIMIT = 56 * 1024 * 1024

HEAD_ORDER = tuple(h for j in range(GROUP) for h in (j, j + GROUP))

TOKEN_TILE = 512
FFN_CHUNK = 1408
ROW_GROUPS = 2
WEIGHT_SLAB = 32
SAMPLE_ATTN_BATCH = 8
SAMPLE_MLSTM_BATCH = 16


def _rms(x, g):
    return x * lax.rsqrt(jnp.mean(x * x, axis=-1, keepdims=True) + EPS) * g


def _const_spec(shape):
    nd = len(shape)
    return pl.BlockSpec(shape, lambda i: (0,) * nd, pipeline_mode=pl.Buffered(1))


def _params(semantics):
    return pltpu.CompilerParams(dimension_semantics=semantics, vmem_limit_bytes=VMEM_LIMIT)


OFF_QA, OFF_KA, OFF_VA = 0, ATTN_WIDTH, ATTN_WIDTH + KV_WIDTH
OFF_QM = ATTN_WIDTH + 2 * KV_WIDTH
OFF_KM, OFF_VM, OFF_OM = OFF_QM + M_WIDTH, OFF_QM + 2 * M_WIDTH, OFF_QM + 3 * M_WIDTH


def _inproj_kernel(x_ref, g_ref, w_ref, qa_ref, ka_ref, va_ref, qm_ref, km_ref, vm_ref, om_ref, gt_ref):
    tm = x_ref.shape[0]
    step = tm // ROW_GROUPS
    for r0 in range(0, tm, step):
        rs = slice(r0, r0 + step)
        h = _rms(x_ref[rs, :], g_ref[...]).astype(BF16)

        def proj(off, n):
            return jnp.dot(h, w_ref[:, off:off + n], preferred_element_type=F32)

        qa_ref[rs, :] = proj(OFF_QA, ATTN_WIDTH) * (HEAD_DIM ** -0.5)
        ka_ref[rs, :] = proj(OFF_KA, KV_WIDTH)
        va_ref[rs, :] = proj(OFF_VA, KV_WIDTH)
        qm_ref[rs, :] = proj(OFF_QM, M_WIDTH)
        km_ref[rs, :] = proj(OFF_KM, M_WIDTH) * (M_HEAD_DIM ** -0.5)
        vm_ref[rs, :] = proj(OFF_VM, M_WIDTH)
        om_ref[rs, :] = proj(OFF_OM, M_WIDTH)
        gt_ref[rs, :] = proj(IN_MAIN, GATE_PAD)


def _inproj(x, g_pre, w_pad):
    m = x.shape[0]
    tm = TOKEN_TILE
    row = lambda n: pl.BlockSpec((tm, n), lambda i: (i, 0))
    widths = (ATTN_WIDTH, KV_WIDTH, KV_WIDTH, M_WIDTH, M_WIDTH, M_WIDTH, M_WIDTH, GATE_PAD)
    return pl.pallas_call(
        _inproj_kernel,
        grid=(m // tm,),
        in_specs=[row(D_MODEL), _const_spec((1, D_MODEL)), _const_spec((D_MODEL, IN_PAD))],
        out_specs=[row(n) for n in widths],
        out_shape=[jax.ShapeDtypeStruct((m, n), F32) for n in widths],
        compiler_params=_params(("parallel",)),
        name="inproj",
    )(x, g_pre, w_pad)


def _stack_heads(q_tiles):
    lane = lax.broadcasted_iota(jnp.int32, q_tiles[0].shape, 1)
    lo = lane < HEAD_DIM
    zero = jnp.zeros_like(q_tiles[0])
    parts = []
    for qt in q_tiles:
        parts += [jnp.where(lo, qt, zero), jnp.where(lo, zero, qt)]
    return jnp.concatenate(parts, axis=0)


def _unstack_heads(o, rows):
    lane = lax.broadcasted_iota(jnp.int32, (rows, LANES), 1)
    lo = lane < HEAD_DIM
    return [jnp.where(lo, o[(2 * j) * rows:(2 * j + 1) * rows, :], o[(2 * j + 1) * rows:(2 * j + 2) * rows, :])
            for j in range(GROUP)]


def _sink_rows(sink_ref, rows):
    return jnp.concatenate([jnp.full((rows, LANES), sink_ref[0, h], F32) for h in HEAD_ORDER], axis=0)


def _attn_sample_kernel(sink_ref, q_ref, kn_ref, vn_ref, ck_ref, cv_ref, o_ref, ko_ref, vo_ref, *, dec_seq):
    t = dec_seq
    bb = SAMPLE_ATTN_BATCH
    nrow = ATTN_HEADS * t
    ts, ns = t.bit_length() - 1, nrow.bit_length() - 1
    r_c = lax.broadcasted_iota(jnp.int32, (nrow, WINDOW), 0) & (t - 1)
    c_c = lax.broadcasted_iota(jnp.int32, (nrow, WINDOW), 1)
    vis_cache = c_c > r_c
    r_n = lax.broadcasted_iota(jnp.int32, (bb * nrow, bb * t), 0)
    c_n = lax.broadcasted_iota(jnp.int32, (bb * nrow, bb * t), 1)
    vis_new = jnp.logical_and((r_n >> ns) == (c_n >> ts), (c_n & (t - 1)) <= (r_n & (t - 1)))
    sink = _sink_rows(sink_ref, t)
    nt = (((1,), (1,)), ((), ()))
    kn_all, vn_all = kn_ref[...], vn_ref[...]
    qs, s_c = [], []
    for b in range(bb):
        rows = slice(b * t, (b + 1) * t)
        ck, cv = ck_ref[b], cv_ref[b]
        ko_ref[b] = jnp.concatenate([ck[t:, :], kn_all[rows, :]], axis=0)
        vo_ref[b] = jnp.concatenate([cv[t:, :], vn_all[rows, :]], axis=0)
        qs.append(_stack_heads([q_ref[rows, j * LANES:(j + 1) * LANES] for j in range(GROUP)]).astype(BF16))
        s_c.append(jnp.where(vis_cache, lax.dot_general(qs[b], ck.astype(BF16), nt, preferred_element_type=F32),
                             -jnp.inf))
    s_n = jnp.where(vis_new, lax.dot_general(jnp.concatenate(qs, axis=0), kn_all.astype(BF16), nt,
                                             preferred_element_type=F32), -jnp.inf)
    p_c, p_n, rden = [], [], []
    for b in range(bb):
        s_nb = s_n[b * nrow:(b + 1) * nrow, :]
        m = jnp.maximum(jnp.maximum(jnp.max(s_c[b], axis=-1, keepdims=True), jnp.max(s_nb, axis=-1, keepdims=True)),
                        sink)
        p_c.append(jnp.exp(s_c[b] - m))
        p_n.append(jnp.exp(s_nb - m[:, :bb * t]))
        rden.append(1.0 / (jnp.sum(p_c[b], axis=-1, keepdims=True) + jnp.sum(p_n[b], axis=-1, keepdims=True)
                           + jnp.exp(sink - m)))
    o_n = jnp.dot(jnp.concatenate(p_n, axis=0).astype(BF16), vn_all.astype(BF16), preferred_element_type=F32)
    outs = [[] for _ in range(GROUP)]
    for b in range(bb):
        o = (jnp.dot(p_c[b].astype(BF16), cv_ref[b].astype(BF16), preferred_element_type=F32)
             + o_n[b * nrow:(b + 1) * nrow, :]) * rden[b]
        for j, tile in enumerate(_unstack_heads(o, t)):
            outs[j].append(tile)
    for j, parts in enumerate(outs):
        o_ref[:, j * LANES:(j + 1) * LANES] = jnp.concatenate(parts, axis=0).astype(o_ref.dtype)


def _attn_sample(sink, qa, ka, va, cache_k, cache_v, dec_seq):
    nb = cache_k.shape[0]
    bb = SAMPLE_ATTN_BATCH
    row = lambda n: pl.BlockSpec((bb * dec_seq, n), lambda i: (i, 0))
    cache = pl.BlockSpec((bb, WINDOW, KV_WIDTH), lambda i: (i, 0, 0))
    return pl.pallas_call(
        functools.partial(_attn_sample_kernel, dec_seq=dec_seq),
        grid=(nb // bb,),
        in_specs=[pl.BlockSpec(memory_space=pltpu.SMEM), row(ATTN_WIDTH), row(KV_WIDTH), row(KV_WIDTH), cache, cache],
        out_specs=[row(ATTN_WIDTH), cache, cache],
        out_shape=[jax.ShapeDtypeStruct((nb * dec_seq, ATTN_WIDTH), BF16),
                   jax.ShapeDtypeStruct(cache_k.shape, F32), jax.ShapeDtypeStruct(cache_v.shape, F32)],
        compiler_params=_params(("parallel",)),
        name="attn_sample",
    )(sink, qa, ka, va, cache_k, cache_v)


def _head_out(hh, mnorm_row, om):
    y = hh * lax.rsqrt(jnp.mean(hh * hh, axis=-1, keepdims=True) + EPS) * mnorm_row
    return (jax.nn.sigmoid(om) * y).astype(BF16)


def _mixer_prompt_kernel(sink_ref, bi_ref, bf_ref, x_ref, g_ref, win_ref, mnorm_ref, wg_ref, wu_ref, wd_ref, wo_ref,
                         mix_ref, kw_ref, vw_ref, c_ref, m_ref, w_ref, wgb_ref, wub_ref, wdb_ref, wob_ref, wgt_ref):
    step = pl.program_id(0)

    @pl.when(step == 0)
    def _():
        kw_ref[...] = jnp.zeros_like(kw_ref)
        vw_ref[...] = jnp.zeros_like(vw_ref)
        c_ref[...] = jnp.zeros_like(c_ref)
        m_ref[...] = jnp.zeros_like(m_ref)
        for j in range(GROUP):
            lo = win_ref[:, j * HEAD_DIM:(j + 1) * HEAD_DIM]
            hi = win_ref[:, (j + GROUP) * HEAD_DIM:(j + GROUP + 1) * HEAD_DIM]
            w_ref[:, j * LANES:(j + 1) * LANES] = jnp.concatenate([lo, hi], axis=1).astype(BF16)
        w_ref[:, ATTN_WIDTH:IN_MAIN] = win_ref[:, ATTN_WIDTH:IN_MAIN].astype(BF16)
        gw = jnp.concatenate([win_ref[:, IN_MAIN:IN_MAIN + 2 * M_HEADS],
                              jnp.zeros((D_MODEL, GATE_PAD - 2 * M_HEADS), F32)], axis=1)
        w_ref[:, IN_MAIN:] = gw.astype(BF16)
        wgt_ref[...] = gw.T[:GATE_ROWS, :].astype(BF16)

    wgb_ref[...] = wg_ref[...].astype(BF16)
    wub_ref[...] = wu_ref[...].astype(BF16)
    wdb_ref[...] = wd_ref[...].astype(BF16)
    wob_ref[...] = wo_ref[...].astype(BF16)

    tm = x_ref.shape[0]
    ln = WINDOW
    groups = [slice(r0, r0 + tm // ROW_GROUPS) for r0 in range(0, tm, tm // ROW_GROUPS)]
    blocks = [slice(r0, r0 + ln) for r0 in range(0, tm, ln)]
    hcols = [slice(h * M_HEAD_DIM, (h + 1) * M_HEAD_DIM) for h in range(M_HEADS)]
    nt = (((1,), (1,)), ((), ()))
    hs = [_rms(x_ref[rs, :], g_ref[...]).astype(BF16) for rs in groups]

    def proj(off, n):
        return jnp.concatenate([jnp.dot(h, w_ref[:, off:off + n], preferred_element_type=F32) for h in hs], axis=0)

    qt = proj(OFF_QM, M_WIDTH).T.astype(BF16)
    km = (proj(OFF_KM, M_WIDTH) * (M_HEAD_DIM ** -0.5)).astype(BF16)
    vt = proj(OFF_VM, M_WIDTH).T.astype(BF16)
    gates = jnp.concatenate([lax.dot_general(wgt_ref[...], h, nt, preferred_element_type=F32) for h in hs], axis=1)
    r = lax.broadcasted_iota(jnp.int32, (ln, ln), 0)
    c = lax.broadcasted_iota(jnp.int32, (ln, ln), 1)
    causal_t = r <= c
    upper = causal_t.astype(F32)
    lane8 = lax.broadcasted_iota(jnp.int32, (SUBLANES, ln), 1)
    ones_rows = jnp.ones((AUG_ROWS - M_HEAD_DIM, ln), BF16)
    zrows = jnp.zeros((ln - SUBLANES, ln), F32)
    row8 = lax.broadcasted_iota(jnp.int32, (SUBLANES, ln), 0)
    bias = jnp.zeros((SUBLANES, ln), F32)
    for h in range(M_HEADS):
        bias = jnp.where(row8 == h, bi_ref[0, h], jnp.where(row8 == M_HEADS + h, bf_ref[0, h], bias))
    pre, scores_m = [], []
    for rows in blocks:
        gi = gates[0:SUBLANES, rows] + bias
        b = pltpu.roll(jnp.dot(jax.nn.log_sigmoid(gi), upper, precision=HIGHEST, preferred_element_type=F32),
                       M_HEADS, axis=0)
        g = gi - b
        cm0 = g
        sh = 1
        while sh < ln:
            cm0 = jnp.maximum(cm0, jnp.where(lane8 >= sh, pltpu.roll(cm0, sh, axis=1), -jnp.inf))
            sh *= 2
        b_last = jnp.broadcast_to(b[:, ln - 1:ln], b.shape)
        g_max = jnp.broadcast_to(cm0[:, ln - 1:ln], b.shape)
        g_cols = jnp.concatenate([g, zrows], axis=0).T
        pre.append((b, g_cols, cm0, b_last, g_max, (b_last - b) + gi))
        scores_m.append([jnp.dot(km[rows, hc], qt[hc, rows], preferred_element_type=F32) for hc in hcols])

    qa = (proj(OFF_QA, ATTN_WIDTH) * (HEAD_DIM ** -0.5)).astype(BF16)
    ka = proj(OFF_KA, KV_WIDTH)
    va = proj(OFF_VA, KV_WIDTH)
    nrow = ATTN_HEADS * ln
    ra = lax.broadcasted_iota(jnp.int32, (nrow, ln), 0) & (ln - 1)
    ca = lax.broadcasted_iota(jnp.int32, (nrow, ln), 1)
    own = ca <= ra
    sink = _sink_rows(sink_ref, ln)
    no_prev = jnp.where(step == 0, -jnp.inf, 0.0)
    ones = jnp.ones((2 * ln, LANES), BF16)
    scores_a, v_augs = [], []
    for blk, rows in enumerate(blocks):
        if blk == 0:
            k_prev, v_prev = kw_ref[...], vw_ref[...]
        else:
            k_prev, v_prev = ka[blocks[blk - 1], :], va[blocks[blk - 1], :]
        kcat = jnp.concatenate([ka[rows, :], k_prev], axis=0).astype(BF16)
        vcat = jnp.concatenate([va[rows, :], v_prev], axis=0).astype(BF16)
        v_augs.append(jnp.concatenate([vcat, ones], axis=1))
        qs = _stack_heads([qa[rows, j * LANES:(j + 1) * LANES] for j in range(GROUP)])
        scores_a.append(lax.dot_general(qs, kcat, nt, preferred_element_type=F32))
    kw_ref[...] = ka[blocks[-1], :]
    vw_ref[...] = va[blocks[-1], :]

    m_prev = m_ref[...]
    scal = []
    for b, _, cm0, b_last, g_max, w_end_arg in pre:
        cm = jnp.maximum(cm0, m_prev)
        m_end = b_last + jnp.maximum(g_max, m_prev)
        scal.append((cm, jnp.exp(m_prev - cm), jnp.exp(-(b + cm)), jnp.exp(b_last + m_prev - m_end),
                     jnp.exp(w_end_arg - m_end)))
        m_prev = m_end
    m_ref[...] = m_prev
    gated = []
    for rows, s_t, (_, g_cols, *_), (cm, _, _, _, w_end) in zip(blocks, scores_m, pre, scal):
        vts = [jnp.concatenate([vt[hc, rows], ones_rows], axis=0) for hc in hcols]
        sqks = [(jnp.exp(jnp.where(causal_t, g_cols[:, h:h + 1] - cm[h:h + 1, :], -jnp.inf)) * s_t[h]).astype(BF16)
                for h in range(M_HEADS)]
        kvws = [(vts[h].astype(F32) * w_end[h:h + 1, :]).astype(BF16) for h in range(M_HEADS)]
        gated.append((vts, sqks, kvws))
    probs, maxes = [], []
    for blk, s in enumerate(scores_a):
        s_prev = s[:, ln:]
        if blk == 0:
            s_prev = s_prev + no_prev
        sc = jnp.where(own, s[:, :ln], s_prev)
        mx = jnp.maximum(jnp.max(sc, axis=-1, keepdims=True), sink)
        p = jnp.exp(sc - mx)
        zero = jnp.zeros_like(p)
        probs.append(jnp.concatenate([jnp.where(own, p, zero), jnp.where(own, zero, p)], axis=1).astype(BF16))
        maxes.append(mx)
    om = proj(OFF_OM, M_WIDTH)

    mem = [c_ref[h] for h in range(M_HEADS)]
    for rows, (vts, sqks, kvws), (_, w_inter, e_negm, dec, _), p2, v_aug, mx in zip(
            blocks, gated, scal, probs, v_augs, maxes):
        upd = [jnp.dot(kvws[h], km[rows, hcols[h]], preferred_element_type=F32) for h in range(M_HEADS)]
        intra = [jnp.dot(vts[h], sqks[h], preferred_element_type=F32) for h in range(M_HEADS)]
        inter = [jnp.dot(mem[h].astype(BF16), qt[hcols[h], rows], preferred_element_type=F32)
                 for h in range(M_HEADS)]
        o = jnp.dot(p2, v_aug, preferred_element_type=F32)
        o = o[:, :LANES] * (1.0 / (o[:, LANES:] + jnp.exp(sink - mx)))
        for j, tile in enumerate(_unstack_heads(o, ln)):
            mix_ref[rows, j * LANES:(j + 1) * LANES] = tile.astype(mix_ref.dtype)
        for h in range(M_HEADS):
            num = inter[h] * w_inter[h:h + 1, :] + intra[h]
            mem[h] = dec[h:h + 1, :] * mem[h] + upd[h]
            den = jnp.maximum(jnp.abs(num[M_HEAD_DIM:M_HEAD_DIM + 1, :]), e_negm[h:h + 1, :])
            hh = num[:M_HEAD_DIM, :] * (1.0 / den)
            y = hh * lax.rsqrt(jnp.mean(hh * hh, axis=0, keepdims=True) + EPS)
            mcols = slice(ATTN_WIDTH + h * M_HEAD_DIM, ATTN_WIDTH + (h + 1) * M_HEAD_DIM)
            mix_ref[rows, mcols] = (jax.nn.sigmoid(om[rows, hcols[h]])
                                    * (y.T * mnorm_ref[:, hcols[h]])).astype(mix_ref.dtype)
    for h in range(M_HEADS):
        c_ref[h] = mem[h]


def _out_row_block(i):
    per_head = (ATTN_WIDTH // ATTN_HEADS) // WEIGHT_SLAB
    j, part = i // per_head, i % per_head
    head = (j % 2) * GROUP + j // 2
    return jnp.where(i < ATTN_HEADS * per_head, head * per_head + part, i)


def _mixer_prompt(sink, b_i, b_f, x, g_pre, w_in, mnorm, w_gate, w_up, w_down, w_out):
    m = x.shape[0]
    tm = TOKEN_TILE
    steps = m // tm
    assert D_MODEL == steps * WEIGHT_SLAB and D_FF % (steps // 2) == 0 and (D_FF // (steps // 2)) % BF16_SUBLANES == 0
    down_slab = D_FF // (steps // 2)
    row = lambda n: pl.BlockSpec((tm, n), lambda i: (i, 0))
    whole = lambda shape: pl.BlockSpec(shape, lambda i: (0,) * len(shape))
    smem = pl.BlockSpec(memory_space=pltpu.SMEM)
    slab = lambda n: pl.BlockSpec((WEIGHT_SLAB, n), lambda i: (i, 0))
    down = pl.BlockSpec((down_slab, D_MODEL), lambda i: (i // 2, 0))
    c_shape = (M_HEADS, AUG_ROWS, M_HEAD_DIM)
    w_shape = (WINDOW, KV_WIDTH)
    s_shape = (SUBLANES, LANES)
    sds = jax.ShapeDtypeStruct
    return pl.pallas_call(
        _mixer_prompt_kernel,
        grid=(steps,),
        in_specs=[smem, smem, smem, row(D_MODEL), _const_spec((1, D_MODEL)), _const_spec(w_in.shape),
                  _const_spec((1, M_WIDTH)), slab(D_FF), slab(D_FF), down,
                  pl.BlockSpec((WEIGHT_SLAB, D_MODEL), lambda i: (_out_row_block(i), 0))],
        out_specs=[row(ATTN_WIDTH + M_WIDTH), whole(w_shape), whole(w_shape), whole(c_shape), whole(s_shape),
                   whole((D_MODEL, IN_PAD)), slab(D_FF), slab(D_FF), down, slab(D_MODEL)],
        out_shape=[sds((m, ATTN_WIDTH + M_WIDTH), BF16), sds(w_shape, F32), sds(w_shape, F32), sds(c_shape, F32),
                   sds(s_shape, F32), sds((D_MODEL, IN_PAD), BF16), sds(w_gate.shape, BF16), sds(w_up.shape, BF16),
                   sds(w_down.shape, BF16), sds(w_out.shape, BF16)],
        scratch_shapes=[pltpu.VMEM((GATE_ROWS, D_MODEL), BF16)],
        compiler_params=_params(("arbitrary",)),
        name="mixer_prompt",
    )(sink, b_i, b_f, x, g_pre, w_in, mnorm, w_gate, w_up, w_down, w_out)


def _gates(g_blk, bias_row, cum):
    pre = g_blk + bias_row
    lane = lax.broadcasted_iota(jnp.int32, pre.shape, 1)
    a = jnp.where(lane < M_HEADS, pre, jax.nn.log_sigmoid(pre))
    b = jnp.dot(cum, a, precision=HIGHEST, preferred_element_type=F32)
    return a, b, a.T, b.T


def _col(x, j):
    return jnp.broadcast_to(x[:, j:j + 1], x.shape)


def _mlstm_intra(q, k, v, a, b, at, bt, h, mask, m_prev, last):
    bc, ic = _col(b, M_HEADS + h), _col(a, h)
    br, ir = bt[M_HEADS + h:M_HEADS + h + 1, :], at[h:h + 1, :]
    d = jnp.where(mask, (bc - br) + ir, -jnp.inf)
    inter = bc + m_prev
    m_t = jnp.maximum(inter, jnp.max(d, axis=-1, keepdims=True))
    w_intra = jnp.exp(d - m_t)
    w_inter = jnp.exp(inter - m_t)
    s = lax.dot_general(q, k, (((1,), (1,)), ((), ())), preferred_element_type=F32)
    sqk = w_intra * s
    intra = jnp.dot(sqk.astype(BF16), v, preferred_element_type=F32)
    sq_sum = jnp.sum(sqk, axis=-1, keepdims=True)
    m_end = last(m_t)
    bl = last(bc)
    dec = jnp.exp(bl + m_prev - m_end)
    w_end = jnp.exp((bl - bc) + ic - m_end)
    kw = k.astype(F32) * w_end
    return w_inter, intra, sq_sum, m_t, m_end, dec, kw


def _mlstm_sample_kernel(q_ref, k_ref, v_ref, om_ref, g_ref, m0_ref, bias_ref, mnorm_ref, c_ref, n_ref,
                         o_ref, c_out, n_out, m_out, *, dec_seq):
    t = dec_seq
    bb = SAMPLE_MLSTM_BATCH
    ln = bb * t
    r = lax.broadcasted_iota(jnp.int32, (ln, ln), 0)
    c = lax.broadcasted_iota(jnp.int32, (ln, ln), 1)
    shift = t.bit_length() - 1
    same = (r >> shift) == (c >> shift)
    mask = jnp.logical_and(same, c <= r)
    cum = mask.astype(F32)
    expand = (c == (r >> shift)).astype(F32)
    gather = ((c >> shift) == r).astype(F32)
    is_last = (r & (t - 1)) == t - 1
    is_first = (r & (t - 1)) == 0

    def last(x):
        y = jnp.where(is_last, x, 0.0)
        step = 1
        while step < t:
            y = y + pltpu.roll(y, ln - step, axis=0)
            step *= 2
        return y

    a, b, at, bt = _gates(g_ref[...], bias_ref[...], cum)
    m0 = m0_ref[...]
    m_cols = jnp.zeros((ln, LANES), F32)
    lane = lax.broadcasted_iota(jnp.int32, (ln, LANES), 1)
    zpad = jnp.zeros((ln - bb, M_HEAD_DIM), F32)
    for h in range(M_HEADS):
        cols = slice(h * M_HEAD_DIM, (h + 1) * M_HEAD_DIM)
        qf = q_ref[:, cols]
        q, k, v = qf.astype(BF16), k_ref[:, cols].astype(BF16), v_ref[:, cols].astype(BF16)
        m_prev = _col(m0, h)
        w_inter, intra, sq_sum, m_t, m_end, dec, kw = _mlstm_intra(q, k, v, a, b, at, bt, h, mask, m_prev, last)
        n_old = n_ref[:, cols]
        n_exp = jnp.dot(expand, jnp.concatenate([n_old, zpad], axis=0), precision=HIGHEST,
                        preferred_element_type=F32)
        qc = jnp.concatenate(
            [jnp.dot(qf[s * t:(s + 1) * t, :], c_ref[s * M_HEADS + h].astype(BF16).astype(F32),
                     preferred_element_type=F32) for s in range(bb)], axis=0)
        num = w_inter * qc + intra
        nq = w_inter * jnp.sum(qf * n_exp, axis=-1, keepdims=True) + sq_sum
        hh = num / jnp.maximum(jnp.abs(nq), jnp.exp(-m_t))
        o_ref[:, cols] = _head_out(hh, mnorm_ref[:, cols], om_ref[:, cols])
        kwt = kw.T.astype(BF16)
        for s in range(bb):
            lhs = jnp.where((c >> shift) == s, kwt, jnp.zeros_like(kwt))
            upd = jnp.dot(lhs, v, preferred_element_type=F32)
            c_out[s * M_HEADS + h] = dec[s * t:s * t + 1, :] * c_ref[s * M_HEADS + h] + upd
        n_new = jnp.dot(gather, jnp.where(is_first, dec * n_exp, 0.0) + kw, precision=HIGHEST,
                        preferred_element_type=F32)
        n_out[:, cols] = n_new[:bb, :]
        m_cols = jnp.where(lane == h, m_end, m_cols)
    m_out[...] = m_cols


def _mlstm_sample(qm, km, vm, om, gt, m0_rows, bias_row, mnorm, c_in, n_in, dec_seq):
    m = qm.shape[0]
    bb = SAMPLE_MLSTM_BATCH
    tm = bb * dec_seq
    nb = m // dec_seq
    row = lambda n: pl.BlockSpec((tm, n), lambda i: (i, 0))
    whole = lambda shape: pl.BlockSpec(shape, lambda i: (0,) * len(shape))
    c_spec = pl.BlockSpec((bb * M_HEADS, M_HEAD_DIM, M_HEAD_DIM), lambda i: (i, 0, 0))
    n_spec = pl.BlockSpec((bb, M_WIDTH), lambda i: (i, 0))
    return pl.pallas_call(
        functools.partial(_mlstm_sample_kernel, dec_seq=dec_seq),
        grid=(nb // bb,),
        in_specs=[row(M_WIDTH), row(M_WIDTH), row(M_WIDTH), row(M_WIDTH), row(GATE_PAD), row(LANES),
                  whole((1, GATE_PAD)), whole((1, M_WIDTH)), c_spec, n_spec],
        out_specs=[row(M_WIDTH), c_spec, n_spec, row(LANES)],
        out_shape=[jax.ShapeDtypeStruct((m, M_WIDTH), BF16), jax.ShapeDtypeStruct(c_in.shape, F32),
                   jax.ShapeDtypeStruct(n_in.shape, F32), jax.ShapeDtypeStruct((m, LANES), F32)],
        compiler_params=_params(("parallel",)),
        name="mlstm_sample",
    )(qm, km, vm, om, gt, m0_rows, bias_row, mnorm, c_in, n_in)


def _out_ffn_kernel(x_ref, mix_ref, wo_ref, g1_ref, g2_ref, wg_ref, wu_ref, wd_ref, g3_ref, o_ref):
    tm = x_ref.shape[0]
    groups = [slice(r, r + tm // ROW_GROUPS) for r in range(0, tm, tm // ROW_GROUPS)]
    ys = [jnp.dot(mix_ref[rs, :], wo_ref[...], preferred_element_type=F32) for rs in groups]
    x1s = [x_ref[rs, :] + _rms(y, g1_ref[...]) for rs, y in zip(groups, ys)]
    fs = [_rms(x1, g2_ref[...]).astype(BF16) for x1 in x1s]
    accs = [None] * ROW_GROUPS
    for off in range(0, D_FF, FFN_CHUNK):
        acts = []
        for f in fs:
            g = jnp.dot(f, wg_ref[:, off:off + FFN_CHUNK], preferred_element_type=F32)
            u = jnp.dot(f, wu_ref[:, off:off + FFN_CHUNK], preferred_element_type=F32)
            acts.append((g * jax.nn.sigmoid(g) * u).astype(BF16))
        for i, act in enumerate(acts):
            part = jnp.dot(act, wd_ref[off:off + FFN_CHUNK, :], preferred_element_type=F32)
            accs[i] = part if accs[i] is None else accs[i] + part
    for rs, x1, acc in zip(groups, x1s, accs):
        o_ref[rs, :] = x1 + _rms(acc, g3_ref[...])


def _out_ffn(x, mix, w_out, g_post_mix, g_pre_ffn, w_gate, w_up, w_down, g_post_ffn):
    m = x.shape[0]
    tm = TOKEN_TILE
    row = lambda n: pl.BlockSpec((tm, n), lambda i: (i, 0))
    vec = _const_spec((1, D_MODEL))
    return pl.pallas_call(
        _out_ffn_kernel,
        grid=(m // tm,),
        in_specs=[row(D_MODEL), row(ATTN_WIDTH + M_WIDTH), _const_spec((D_MODEL, D_MODEL)), vec, vec,
                  _const_spec((D_MODEL, D_FF)), _const_spec((D_MODEL, D_FF)), _const_spec((D_FF, D_MODEL)), vec],
        out_specs=row(D_MODEL),
        out_shape=jax.ShapeDtypeStruct((m, D_MODEL), F32),
        compiler_params=_params(("parallel",)),
        name="out_ffn",
    )(x, mix, w_out, g_post_mix, g_pre_ffn, w_gate, w_up, w_down, g_post_ffn)


def _layer(xp, xs, cache_k, cache_v, state_c, state_n, state_m, w_in, b_i, b_f, attn_sink, m_norm, w_out,
           g_pre_mix, g_post_mix, g_pre_ffn, g_post_ffn, w_gate, w_up, w_down):
    bp, sp, _ = xp.shape
    bs, ts, _ = xs.shape
    assert bp == 1 and sp % TOKEN_TILE == 0 and TOKEN_TILE % WINDOW == 0
    assert ts & (ts - 1) == 0 and (bs * ts) % TOKEN_TILE == 0 and bs % SAMPLE_MLSTM_BATCH == 0

    row = lambda v: v.reshape(1, -1)
    bias_row = jnp.pad(jnp.concatenate([b_i, b_f]), (0, GATE_PAD - 2 * M_HEADS)).reshape(1, GATE_PAD)
    sink = row(attn_sink)

    x2 = xp.reshape(sp, D_MODEL)
    mix, k_w, v_w, c_aug, m_p, w_pad, wg, wu, wd, wo = _mixer_prompt(
        sink, row(b_i), row(b_f), x2, row(g_pre_mix), w_in, row(m_norm), w_gate, w_up, w_down, w_out)
    ffn = (wo, row(g_post_mix), row(g_pre_ffn), wg, wu, wd, row(g_post_ffn))
    yp = _out_ffn(x2, mix, *ffn).reshape(xp.shape)
    k_p = k_w.reshape(1, WINDOW, KV_HEADS, HEAD_DIM)
    v_p = v_w.reshape(1, WINDOW, KV_HEADS, HEAD_DIM)
    c_p = jnp.swapaxes(c_aug[:, :M_HEAD_DIM, :], 1, 2).reshape(1, M_HEADS, M_HEAD_DIM, M_HEAD_DIM)
    n_p = c_aug[:, M_HEAD_DIM, :].reshape(1, M_HEADS, M_HEAD_DIM)
    m_p = m_p[:M_HEADS, 0].reshape(1, M_HEADS)

    x2 = xs.reshape(bs * ts, D_MODEL)
    qa, ka, va, qm, km, vm, om, gt = _inproj(x2, row(g_pre_mix), w_pad)
    mix_a, k_s, v_s = _attn_sample(sink, qa, ka, va, cache_k.reshape(bs, WINDOW, KV_WIDTH),
                                   cache_v.reshape(bs, WINDOW, KV_WIDTH), ts)
    m0_rows = jnp.pad(jnp.repeat(state_m, ts, axis=0), ((0, 0), (0, LANES - M_HEADS)))
    mix_m, c_s, n_s, m_rows = _mlstm_sample(qm, km, vm, om, gt, m0_rows, bias_row, row(m_norm),
                                            state_c.reshape(bs * M_HEADS, M_HEAD_DIM, M_HEAD_DIM),
                                            state_n.reshape(bs, M_WIDTH), ts)
    ys = _out_ffn(x2, jnp.concatenate([mix_a, mix_m], axis=1), *ffn).reshape(xs.shape)
    k_s = k_s.reshape(bs, WINDOW, KV_HEADS, HEAD_DIM)
    v_s = v_s.reshape(bs, WINDOW, KV_HEADS, HEAD_DIM)
    c_s = c_s.reshape(bs, M_HEADS, M_HEAD_DIM, M_HEAD_DIM)
    n_s = n_s.reshape(bs, M_HEADS, M_HEAD_DIM)
    m_s = m_rows[ts - 1::ts, :M_HEADS]
    return yp, ys, (k_p, v_p, c_p, n_p, m_p), (k_s, v_s, c_s, n_s, m_s)


def kernel(x_prompt, x_sample, cache_k, cache_v, state_C, state_n, state_m, w_in, b_i, b_f, attn_sink, m_norm,
           w_out, g_pre_mix, g_post_mix, g_pre_ffn, g_post_ffn, w_gate, w_up, w_down):
    depth = w_in.shape[0]
    xp, xs = x_prompt, x_sample
    prompt_states, sample_states = [], []
    for l in range(depth):
        xp, xs, st_p, st_s = _layer(xp, xs, cache_k[l], cache_v[l], state_C[l], state_n[l], state_m[l],
                                    w_in[l], b_i[l], b_f[l], attn_sink[l], m_norm[l], w_out[l],
                                    g_pre_mix[l], g_post_mix[l], g_pre_ffn[l], g_post_ffn[l],
                                    w_gate[l], w_up[l], w_down[l])
        prompt_states.append(st_p)
        sample_states.append(st_s)
    stack = lambda states, i: jnp.stack([s[i] for s in states], axis=0)
    return (xp, xs) + tuple(stack(prompt_states, i) for i in range(5)) + tuple(stack(sample_states, i) for i in range(5))
```

```python
import functools

import jax
import jax.numpy as jnp
from jax import lax
from jax.experimental import pallas as pl
from jax.experimental.pallas import tpu as pltpu

F32 = jnp.float32
BF16 = jnp.bfloat16
HIGHEST = lax.Precision.HIGHEST

D_MODEL = 1024
HEAD_DIM = 64
ATTN_HEADS = 8
KV_HEADS = 2
GROUP = ATTN_HEADS // KV_HEADS
ATTN_WIDTH = ATTN_HEADS * HEAD_DIM
KV_WIDTH = KV_HEADS * HEAD_DIM
WINDOW = 128
M_HEADS = 4
M_HEAD_DIM = 128
M_WIDTH = M_HEADS * M_HEAD_DIM
D_FF = 2816
EPS = 1e-6

LANES = 128
SUBLANES = 8
GATE_PAD = LANES
BF16_SUBLANES = 16
AUG_ROWS = M_HEAD_DIM + BF16_SUBLANES
IN_MAIN = ATTN_WIDTH + 2 * KV_WIDTH + 4 * M_WIDTH
IN_PAD = IN_MAIN + GATE_PAD
VMEM_LIMIT = 56 * 1024 * 1024

HEAD_ORDER = tuple(h for j in range(GROUP) for h in (j, j + GROUP))

TOKEN_TILE = 512
FFN_CHUNK = 1408
ROW_GROUPS = 2
WEIGHT_SLAB = 32
SAMPLE_ATTN_BATCH = 16
SAMPLE_MLSTM_BATCH = 16


def _rms(x, g):
    return x * lax.rsqrt(jnp.mean(x * x, axis=-1, keepdims=True) + EPS) * g


def _const_spec(shape):
    nd = len(shape)
    return pl.BlockSpec(shape, lambda i: (0,) * nd, pipeline_mode=pl.Buffered(1))


def _params(semantics):
    return pltpu.CompilerParams(dimension_semantics=semantics, vmem_limit_bytes=VMEM_LIMIT)


OFF_QA, OFF_KA, OFF_VA = 0, ATTN_WIDTH, ATTN_WIDTH + KV_WIDTH
OFF_QM = ATTN_WIDTH + 2 * KV_WIDTH
OFF_KM, OFF_VM, OFF_OM = OFF_QM + M_WIDTH, OFF_QM + 2 * M_WIDTH, OFF_QM + 3 * M_WIDTH


def _inproj_kernel(x_ref, g_ref, w_ref, qa_ref, ka_ref, va_ref, qm_ref, km_ref, vm_ref, om_ref, gt_ref):
    tm = x_ref.shape[0]
    step = tm // ROW_GROUPS
    for r0 in range(0, tm, step):
        rs = slice(r0, r0 + step)
        h = _rms(x_ref[rs, :], g_ref[...]).astype(BF16)

        def proj(off, n):
            return jnp.dot(h, w_ref[:, off:off + n], preferred_element_type=F32)

        qa_ref[rs, :] = proj(OFF_QA, ATTN_WIDTH) * (HEAD_DIM ** -0.5)
        ka_ref[rs, :] = proj(OFF_KA, KV_WIDTH)
        va_ref[rs, :] = proj(OFF_VA, KV_WIDTH)
        qm_ref[rs, :] = proj(OFF_QM, M_WIDTH)
        km_ref[rs, :] = proj(OFF_KM, M_WIDTH) * (M_HEAD_DIM ** -0.5)
        vm_ref[rs, :] = proj(OFF_VM, M_WIDTH)
        om_ref[rs, :] = proj(OFF_OM, M_WIDTH)
        gt_ref[rs, :] = proj(IN_MAIN, GATE_PAD)


def _inproj(x, g_pre, w_pad):
    m = x.shape[0]
    tm = TOKEN_TILE
    row = lambda n: pl.BlockSpec((tm, n), lambda i: (i, 0))
    widths = (ATTN_WIDTH, KV_WIDTH, KV_WIDTH, M_WIDTH, M_WIDTH, M_WIDTH, M_WIDTH, GATE_PAD)
    return pl.pallas_call(
        _inproj_kernel,
        grid=(m // tm,),
        in_specs=[row(D_MODEL), _const_spec((1, D_MODEL)), _const_spec((D_MODEL, IN_PAD))],
        out_specs=[row(n) for n in widths],
        out_shape=[jax.ShapeDtypeStruct((m, n), F32) for n in widths],
        compiler_params=_params(("parallel",)),
        name="inproj",
    )(x, g_pre, w_pad)


def _stack_heads(q_tiles):
    lane = lax.broadcasted_iota(jnp.int32, q_tiles[0].shape, 1)
    lo = lane < HEAD_DIM
    zero = jnp.zeros_like(q_tiles[0])
    parts = []
    for qt in q_tiles:
        parts += [jnp.where(lo, qt, zero), jnp.where(lo, zero, qt)]
    return jnp.concatenate(parts, axis=0)


def _unstack_heads(o, rows):
    lane = lax.broadcasted_iota(jnp.int32, (rows, LANES), 1)
    lo = lane < HEAD_DIM
    return [jnp.where(lo, o[(2 * j) * rows:(2 * j + 1) * rows, :], o[(2 * j + 1) * rows:(2 * j + 2) * rows, :])
            for j in range(GROUP)]


def _sink_rows(sink_ref, rows):
    return jnp.concatenate([jnp.full((rows, LANES), sink_ref[0, h], F32) for h in HEAD_ORDER], axis=0)


def _attn_sample_kernel(sink_ref, q_ref, kn_ref, vn_ref, ck_ref, cv_ref, o_ref, ko_ref, vo_ref, *, dec_seq):
    t = dec_seq
    bb = SAMPLE_ATTN_BATCH
    nrow = ATTN_HEADS * t
    ts, ns = t.bit_length() - 1, nrow.bit_length() - 1
    r_c = lax.broadcasted_iota(jnp.int32, (nrow, WINDOW), 0) & (t - 1)
    c_c = lax.broadcasted_iota(jnp.int32, (nrow, WINDOW), 1)
    vis_cache = c_c > r_c
    r_n = lax.broadcasted_iota(jnp.int32, (bb * nrow, bb * t), 0)
    c_n = lax.broadcasted_iota(jnp.int32, (bb * nrow, bb * t), 1)
    vis_new = jnp.logical_and((r_n >> ns) == (c_n >> ts), (c_n & (t - 1)) <= (r_n & (t - 1)))
    sink = _sink_rows(sink_ref, t)
    nt = (((1,), (1,)), ((), ()))
    kn_all, vn_all = kn_ref[...], vn_ref[...]
    qs, s_c = [], []
    for b in range(bb):
        rows = slice(b * t, (b + 1) * t)
        ck, cv = ck_ref[b], cv_ref[b]
        ko_ref[b] = jnp.concatenate([ck[t:, :], kn_all[rows, :]], axis=0)
        vo_ref[b] = jnp.concatenate([cv[t:, :], vn_all[rows, :]], axis=0)
        qs.append(_stack_heads([q_ref[rows, j * LANES:(j + 1) * LANES] for j in range(GROUP)]).astype(BF16))
        s_c.append(jnp.where(vis_cache, lax.dot_general(qs[b], ck.astype(BF16), nt, preferred_element_type=F32),
                             -jnp.inf))
    s_n = jnp.where(vis_new, lax.dot_general(jnp.concatenate(qs, axis=0), kn_all.astype(BF16), nt,
                                             preferred_element_type=F32), -jnp.inf)
    p_c, p_n, rden = [], [], []
    for b in range(bb):
        s_nb = s_n[b * nrow:(b + 1) * nrow, :]
        m = jnp.maximum(jnp.maximum(jnp.max(s_c[b], axis=-1, keepdims=True), jnp.max(s_nb, axis=-1, keepdims=True)),
                        sink)
        p_c.append(jnp.exp(s_c[b] - m))
        p_n.append(jnp.exp(s_nb - m[:, :bb * t]))
        rden.append(1.0 / (jnp.sum(p_c[b], axis=-1, keepdims=True) + jnp.sum(p_n[b], axis=-1, keepdims=True)
                           + jnp.exp(sink - m)))
    o_n = jnp.dot(jnp.concatenate(p_n, axis=0).astype(BF16), vn_all.astype(BF16), preferred_element_type=F32)
    outs = [[] for _ in range(GROUP)]
    for b in range(bb):
        o = (jnp.dot(p_c[b].astype(BF16), cv_ref[b].astype(BF16), preferred_element_type=F32)
             + o_n[b * nrow:(b + 1) * nrow, :]) * rden[b]
        for j, tile in enumerate(_unstack_heads(o, t)):
            outs[j].append(tile)
    for j, parts in enumerate(outs):
        o_ref[:, j * LANES:(j + 1) * LANES] = jnp.concatenate(parts, axis=0).astype(o_ref.dtype)


def _attn_sample(sink, qa, ka, va, cache_k, cache_v, dec_seq):
    nb = cache_k.shape[0]
    bb = SAMPLE_ATTN_BATCH
    row = lambda n: pl.BlockSpec((bb * dec_seq, n), lambda i: (i, 0))
    cache = pl.BlockSpec((bb, WINDOW, KV_WIDTH), lambda i: (i, 0, 0))
    return pl.pallas_call(
        functools.partial(_attn_sample_kernel, dec_seq=dec_seq),
        grid=(nb // bb,),
        in_specs=[pl.BlockSpec(memory_space=pltpu.SMEM), row(ATTN_WIDTH), row(KV_WIDTH), row(KV_WIDTH), cache, cache],
        out_specs=[row(ATTN_WIDTH), cache, cache],
        out_shape=[jax.ShapeDtypeStruct((nb * dec_seq, ATTN_WIDTH), BF16),
                   jax.ShapeDtypeStruct(cache_k.shape, F32), jax.ShapeDtypeStruct(cache_v.shape, F32)],
        compiler_params=_params(("parallel",)),
        name="attn_sample",
    )(sink, qa, ka, va, cache_k, cache_v)


def _head_out(hh, mnorm_row, om):
    y = hh * lax.rsqrt(jnp.mean(hh * hh, axis=-1, keepdims=True) + EPS) * mnorm_row
    return (jax.nn.sigmoid(om) * y).astype(BF16)


def _mixer_prompt_kernel(sink_ref, bi_ref, bf_ref, x_ref, g_ref, win_ref, mnorm_ref, wg_ref, wu_ref, wd_ref, wo_ref,
                         mix_ref, kw_ref, vw_ref, c_ref, m_ref, w_ref, wgb_ref, wub_ref, wdb_ref, wob_ref):
    step = pl.program_id(0)

    @pl.when(step == 0)
    def _():
        kw_ref[...] = jnp.zeros_like(kw_ref)
        vw_ref[...] = jnp.zeros_like(vw_ref)
        c_ref[...] = jnp.zeros_like(c_ref)
        m_ref[...] = jnp.zeros_like(m_ref)
        for j in range(GROUP):
            lo = win_ref[:, j * HEAD_DIM:(j + 1) * HEAD_DIM]
            hi = win_ref[:, (j + GROUP) * HEAD_DIM:(j + GROUP + 1) * HEAD_DIM]
            w_ref[:, j * LANES:(j + 1) * LANES] = jnp.concatenate([lo, hi], axis=1).astype(BF16)
        w_ref[:, ATTN_WIDTH:IN_MAIN] = win_ref[:, ATTN_WIDTH:IN_MAIN].astype(BF16)
        gw = jnp.concatenate([win_ref[:, IN_MAIN:IN_MAIN + 2 * M_HEADS],
                              jnp.zeros((D_MODEL, GATE_PAD - 2 * M_HEADS), F32)], axis=1)
        w_ref[:, IN_MAIN:] = gw.astype(BF16)

    wgb_ref[...] = wg_ref[...].astype(BF16)
    wub_ref[...] = wu_ref[...].astype(BF16)
    wdb_ref[...] = wd_ref[...].astype(BF16)
    wob_ref[...] = wo_ref[...].astype(BF16)

    tm = x_ref.shape[0]
    ln = WINDOW
    groups = [slice(r0, r0 + tm // ROW_GROUPS) for r0 in range(0, tm, tm // ROW_GROUPS)]
    blocks = [slice(r0, r0 + ln) for r0 in range(0, tm, ln)]
    hcols = [slice(h * M_HEAD_DIM, (h + 1) * M_HEAD_DIM) for h in range(M_HEADS)]
    nt = (((1,), (1,)), ((), ()))
    hs = [_rms(x_ref[rs, :], g_ref[...]).astype(BF16) for rs in groups]

    def proj(off, n):
        return jnp.concatenate([jnp.dot(h, w_ref[:, off:off + n], preferred_element_type=F32) for h in hs], axis=0)

    qt = proj(OFF_QM, M_WIDTH).T.astype(BF16)
    km = (proj(OFF_KM, M_WIDTH) * (M_HEAD_DIM ** -0.5)).astype(BF16)
    vt = proj(OFF_VM, M_WIDTH).T.astype(BF16)
    gates = proj(IN_MAIN, GATE_PAD).T
    r = lax.broadcasted_iota(jnp.int32, (ln, ln), 0)
    c = lax.broadcasted_iota(jnp.int32, (ln, ln), 1)
    causal_t = r <= c
    upper = causal_t.astype(F32)
    lane8 = lax.broadcasted_iota(jnp.int32, (SUBLANES, ln), 1)
    ones_rows = jnp.ones((AUG_ROWS - M_HEAD_DIM, ln), BF16)
    zrows = jnp.zeros((ln - SUBLANES, ln), F32)
    row8 = lax.broadcasted_iota(jnp.int32, (SUBLANES, ln), 0)
    bias = jnp.zeros((SUBLANES, ln), F32)
    for h in range(M_HEADS):
        bias = jnp.where(row8 == h, bi_ref[0, h], jnp.where(row8 == M_HEADS + h, bf_ref[0, h], bias))
    pre, scores_m = [], []
    for rows in blocks:
        gi = gates[0:SUBLANES, rows] + bias
        b = pltpu.roll(jnp.dot(jax.nn.log_sigmoid(gi), upper, precision=HIGHEST, preferred_element_type=F32),
                       M_HEADS, axis=0)
        g = gi - b
        cm0 = g
        sh = 1
        while sh < ln:
            cm0 = jnp.maximum(cm0, jnp.where(lane8 >= sh, pltpu.roll(cm0, sh, axis=1), -jnp.inf))
            sh *= 2
        b_last = jnp.broadcast_to(b[:, ln - 1:ln], b.shape)
        g_max = jnp.broadcast_to(cm0[:, ln - 1:ln], b.shape)
        g_cols = jnp.concatenate([g, zrows], axis=0).T
        pre.append((b, g_cols, cm0, b_last, g_max, (b_last - b) + gi))
        scores_m.append([jnp.dot(km[rows, hc], qt[hc, rows], preferred_element_type=F32) for hc in hcols])

    qa = (proj(OFF_QA, ATTN_WIDTH) * (HEAD_DIM ** -0.5)).astype(BF16)
    ka = proj(OFF_KA, KV_WIDTH)
    va = proj(OFF_VA, KV_WIDTH)
    nrow = ATTN_HEADS * ln
    ra = lax.broadcasted_iota(jnp.int32, (nrow, ln), 0) & (ln - 1)
    ca = lax.broadcasted_iota(jnp.int32, (nrow, ln), 1)
    own = ca <= ra
    sink = _sink_rows(sink_ref, ln)
    no_prev = jnp.where(step == 0, -jnp.inf, 0.0)
    ones = jnp.ones((2 * ln, LANES), BF16)
    scores_a, v_augs = [], []
    for blk, rows in enumerate(blocks):
        if blk == 0:
            k_prev, v_prev = kw_ref[...], vw_ref[...]
        else:
            k_prev, v_prev = ka[blocks[blk - 1], :], va[blocks[blk - 1], :]
        kcat = jnp.concatenate([ka[rows, :], k_prev], axis=0).astype(BF16)
        vcat = jnp.concatenate([va[rows, :], v_prev], axis=0).astype(BF16)
        v_augs.append(jnp.concatenate([vcat, ones], axis=1))
        qs = _stack_heads([qa[rows, j * LANES:(j + 1) * LANES] for j in range(GROUP)])
        scores_a.append(lax.dot_general(qs, kcat, nt, preferred_element_type=F32))
    kw_ref[...] = ka[blocks[-1], :]
    vw_ref[...] = va[blocks[-1], :]

    m_prev = m_ref[...]
    scal = []
    for b, _, cm0, b_last, g_max, w_end_arg in pre:
        cm = jnp.maximum(cm0, m_prev)
        m_end = b_last + jnp.maximum(g_max, m_prev)
        scal.append((cm, jnp.exp(m_prev - cm), jnp.exp(-(b + cm)), jnp.exp(b_last + m_prev - m_end),
                     jnp.exp(w_end_arg - m_end)))
        m_prev = m_end
    m_ref[...] = m_prev
    gated = []
    for rows, s_t, (_, g_cols, *_), (cm, _, _, _, w_end) in zip(blocks, scores_m, pre, scal):
        vts = [jnp.concatenate([vt[hc, rows], ones_rows], axis=0) for hc in hcols]
        sqks = [(jnp.exp(jnp.where(causal_t, g_cols[:, h:h + 1] - cm[h:h + 1, :], -jnp.inf)) * s_t[h]).astype(BF16)
                for h in range(M_HEADS)]
        kvws = [(vts[h].astype(F32) * w_end[h:h + 1, :]).astype(BF16) for h in range(M_HEADS)]
        gated.append((vts, sqks, kvws))
    probs, maxes = [], []
    for blk, s in enumerate(scores_a):
        s_prev = s[:, ln:]
        if blk == 0:
            s_prev = s_prev + no_prev
        sc = jnp.where(own, s[:, :ln], s_prev)
        mx = jnp.maximum(jnp.max(sc, axis=-1, keepdims=True), sink)
        p = jnp.exp(sc - mx)
        zero = jnp.zeros_like(p)
        probs.append(jnp.concatenate([jnp.where(own, p, zero), jnp.where(own, zero, p)], axis=1).astype(BF16))
        maxes.append(mx)
    om = proj(OFF_OM, M_WIDTH)

    mem = [c_ref[h] for h in range(M_HEADS)]
    for rows, (vts, sqks, kvws), (_, w_inter, e_negm, dec, _), p2, v_aug, mx in zip(
            blocks, gated, scal, probs, v_augs, maxes):
        upd = [jnp.dot(kvws[h], km[rows, hcols[h]], preferred_element_type=F32) for h in range(M_HEADS)]
        intra = [jnp.dot(vts[h], sqks[h], preferred_element_type=F32) for h in range(M_HEADS)]
        inter = [jnp.dot(mem[h].astype(BF16), qt[hcols[h], rows], preferred_element_type=F32)
                 for h in range(M_HEADS)]
        o = jnp.dot(p2, v_aug, preferred_element_type=F32)
        o = o[:, :LANES] * (1.0 / (o[:, LANES:] + jnp.exp(sink - mx)))
        for j, tile in enumerate(_unstack_heads(o, ln)):
            mix_ref[rows, j * LANES:(j + 1) * LANES] = tile.astype(mix_ref.dtype)
        for h in range(M_HEADS):
            num = inter[h] * w_inter[h:h + 1, :] + intra[h]
            mem[h] = dec[h:h + 1, :] * mem[h] + upd[h]
            den = jnp.maximum(jnp.abs(num[M_HEAD_DIM:M_HEAD_DIM + 1, :]), e_negm[h:h + 1, :])
            hh = num[:M_HEAD_DIM, :] * (1.0 / den)
            y = hh * lax.rsqrt(jnp.mean(hh * hh, axis=0, keepdims=True) + EPS)
            mcols = slice(ATTN_WIDTH + h * M_HEAD_DIM, ATTN_WIDTH + (h + 1) * M_HEAD_DIM)
            mix_ref[rows, mcols] = (jax.nn.sigmoid(om[rows, hcols[h]])
                                    * (y.T * mnorm_ref[:, hcols[h]])).astype(mix_ref.dtype)
    for h in range(M_HEADS):
        c_ref[h] = mem[h]


def _out_row_block(i):
    per_head = (ATTN_WIDTH // ATTN_HEADS) // WEIGHT_SLAB
    j, part = i // per_head, i % per_head
    head = (j % 2) * GROUP + j // 2
    return jnp.where(i < ATTN_HEADS * per_head, head * per_head + part, i)


def _mixer_prompt(sink, b_i, b_f, x, g_pre, w_in, mnorm, w_gate, w_up, w_down, w_out):
    m = x.shape[0]
    tm = TOKEN_TILE
    steps = m // tm
    assert D_MODEL == steps * WEIGHT_SLAB and D_FF % (steps // 2) == 0 and (D_FF // (steps // 2)) % BF16_SUBLANES == 0
    down_slab = D_FF // (steps // 2)
    row = lambda n: pl.BlockSpec((tm, n), lambda i: (i, 0))
    whole = lambda shape: pl.BlockSpec(shape, lambda i: (0,) * len(shape))
    smem = pl.BlockSpec(memory_space=pltpu.SMEM)
    slab = lambda n: pl.BlockSpec((WEIGHT_SLAB, n), lambda i: (i, 0))
    down = pl.BlockSpec((down_slab, D_MODEL), lambda i: (i // 2, 0))
    c_shape = (M_HEADS, AUG_ROWS, M_HEAD_DIM)
    w_shape = (WINDOW, KV_WIDTH)
    s_shape = (SUBLANES, LANES)
    sds = jax.ShapeDtypeStruct
    return pl.pallas_call(
        _mixer_prompt_kernel,
        grid=(steps,),
        in_specs=[smem, smem, smem, row(D_MODEL), _const_spec((1, D_MODEL)), _const_spec(w_in.shape),
                  _const_spec((1, M_WIDTH)), slab(D_FF), slab(D_FF), down,
                  pl.BlockSpec((WEIGHT_SLAB, D_MODEL), lambda i: (_out_row_block(i), 0))],
        out_specs=[row(ATTN_WIDTH + M_WIDTH), whole(w_shape), whole(w_shape), whole(c_shape), whole(s_shape),
                   whole((D_MODEL, IN_PAD)), slab(D_FF), slab(D_FF), down, slab(D_MODEL)],
        out_shape=[sds((m, ATTN_WIDTH + M_WIDTH), BF16), sds(w_shape, F32), sds(w_shape, F32), sds(c_shape, F32),
                   sds(s_shape, F32), sds((D_MODEL, IN_PAD), BF16), sds(w_gate.shape, BF16), sds(w_up.shape, BF16),
                   sds(w_down.shape, BF16), sds(w_out.shape, BF16)],
        compiler_params=_params(("arbitrary",)),
        name="mixer_prompt",
    )(sink, b_i, b_f, x, g_pre, w_in, mnorm, w_gate, w_up, w_down, w_out)


def _gates(g_blk, bias_row, cum):
    pre = g_blk + bias_row
    lane = lax.broadcasted_iota(jnp.int32, pre.shape, 1)
    a = jnp.where(lane < M_HEADS, pre, jax.nn.log_sigmoid(pre))
    b = jnp.dot(cum, a, precision=HIGHEST, preferred_element_type=F32)
    return a, b, a.T, b.T


def _col(x, j):
    return jnp.broadcast_to(x[:, j:j + 1], x.shape)


def _mlstm_gated_scores(q, k, a, b, at, bt, h, mask, m_prev, last):
    bc, ic = _col(b, M_HEADS + h), _col(a, h)
    br, ir = bt[M_HEADS + h:M_HEADS + h + 1, :], at[h:h + 1, :]
    d = jnp.where(mask, (bc - br) + ir, -jnp.inf)
    inter = bc + m_prev
    m_t = jnp.maximum(inter, jnp.max(d, axis=-1, keepdims=True))
    w_inter = jnp.exp(inter - m_t)
    sqk = jnp.exp(d - m_t) * lax.dot_general(q, k, (((1,), (1,)), ((), ())), preferred_element_type=F32)
    m_end = last(m_t)
    bl = last(bc)
    dec = jnp.exp(bl + m_prev - m_end)
    w_end = jnp.exp((bl - bc) + ic - m_end)
    kw = k.astype(F32) * w_end
    return w_inter, sqk, m_t, m_end, dec, kw


def _mlstm_sample_kernel(q_ref, k_ref, v_ref, om_ref, g_ref, m0_ref, bias_ref, mnorm_ref, c_ref, n_ref,
                         o_ref, c_out, n_out, m_out, *, dec_seq):
    t = dec_seq
    bb = SAMPLE_MLSTM_BATCH
    ln = bb * t
    r = lax.broadcasted_iota(jnp.int32, (ln, ln), 0)
    c = lax.broadcasted_iota(jnp.int32, (ln, ln), 1)
    shift = t.bit_length() - 1
    same = (r >> shift) == (c >> shift)
    mask = jnp.logical_and(same, c <= r)
    cum = mask.astype(F32)
    expand = (c == (r >> shift)).astype(F32)
    gather = ((c >> shift) == r).astype(F32)
    is_last = (r & (t - 1)) == t - 1
    is_first = (r & (t - 1)) == 0

    def last(x):
        y = jnp.where(is_last, x, 0.0)
        step = 1
        while step < t:
            y = y + pltpu.roll(y, ln - step, axis=0)
            step *= 2
        return y

    a, b, at, bt = _gates(g_ref[...], bias_ref[...], cum)
    m0 = m0_ref[...]
    lane = lax.broadcasted_iota(jnp.int32, (ln, LANES), 1)
    zpad = jnp.zeros((ln - bb, M_HEAD_DIM), F32)
    heads = range(M_HEADS)
    hcols = [slice(h * M_HEAD_DIM, (h + 1) * M_HEAD_DIM) for h in heads]
    qfs = [q_ref[:, hc] for hc in hcols]
    vs = [v_ref[:, hc].astype(BF16) for hc in hcols]
    qcs = [jnp.concatenate(
        [jnp.dot(qfs[h][s * t:(s + 1) * t, :], c_ref[s * M_HEADS + h].astype(BF16).astype(F32),
                 preferred_element_type=F32) for s in range(bb)], axis=0) for h in heads]
    n_exps = [jnp.dot(expand, jnp.concatenate([n_ref[:, hc], zpad], axis=0), precision=HIGHEST,
                      preferred_element_type=F32) for hc in hcols]
    parts = [_mlstm_gated_scores(qfs[h].astype(BF16), k_ref[:, hcols[h]].astype(BF16), a, b, at, bt, h, mask,
                                 _col(m0, h), last) for h in heads]
    intras = [jnp.dot(parts[h][1].astype(BF16), vs[h], preferred_element_type=F32) for h in heads]
    m_cols = jnp.zeros((ln, LANES), F32)
    kwts = []
    for h in heads:
        w_inter, sqk, m_t, m_end, dec, kw = parts[h]
        num = w_inter * qcs[h] + intras[h]
        nq = w_inter * jnp.sum(qfs[h] * n_exps[h], axis=-1, keepdims=True) + jnp.sum(sqk, axis=-1, keepdims=True)
        hh = num / jnp.maximum(jnp.abs(nq), jnp.exp(-m_t))
        o_ref[:, hcols[h]] = _head_out(hh, mnorm_ref[:, hcols[h]], om_ref[:, hcols[h]])
        kwts.append(kw.T.astype(BF16))
        m_cols = jnp.where(lane == h, m_end, m_cols)
    m_out[...] = m_cols
    for h in heads:
        dec = parts[h][4]
        for s in range(bb):
            lhs = jnp.where((c >> shift) == s, kwts[h], jnp.zeros_like(kwts[h]))
            upd = jnp.dot(lhs, vs[h], preferred_element_type=F32)
            c_out[s * M_HEADS + h] = dec[s * t:s * t + 1, :] * c_ref[s * M_HEADS + h] + upd
    for h in heads:
        dec, kw = parts[h][4], parts[h][5]
        n_new = jnp.dot(gather, jnp.where(is_first, dec * n_exps[h], 0.0) + kw, precision=HIGHEST,
                        preferred_element_type=F32)
        n_out[:, hcols[h]] = n_new[:bb, :]


def _mlstm_sample(qm, km, vm, om, gt, m0_rows, bias_row, mnorm, c_in, n_in, dec_seq):
    m = qm.shape[0]
    bb = SAMPLE_MLSTM_BATCH
    tm = bb * dec_seq
    nb = m // dec_seq
    row = lambda n: pl.BlockSpec((tm, n), lambda i: (i, 0))
    whole = lambda shape: pl.BlockSpec(shape, lambda i: (0,) * len(shape))
    c_spec = pl.BlockSpec((bb * M_HEADS, M_HEAD_DIM, M_HEAD_DIM), lambda i: (i, 0, 0))
    n_spec = pl.BlockSpec((bb, M_WIDTH), lambda i: (i, 0))
    return pl.pallas_call(
        functools.partial(_mlstm_sample_kernel, dec_seq=dec_seq),
        grid=(nb // bb,),
        in_specs=[row(M_WIDTH), row(M_WIDTH), row(M_WIDTH), row(M_WIDTH), row(GATE_PAD), row(LANES),
                  whole((1, GATE_PAD)), whole((1, M_WIDTH)), c_spec, n_spec],
        out_specs=[row(M_WIDTH), c_spec, n_spec, row(LANES)],
        out_shape=[jax.ShapeDtypeStruct((m, M_WIDTH), BF16), jax.ShapeDtypeStruct(c_in.shape, F32),
                   jax.ShapeDtypeStruct(n_in.shape, F32), jax.ShapeDtypeStruct((m, LANES), F32)],
        compiler_params=_params(("parallel",)),
        name="mlstm_sample",
    )(qm, km, vm, om, gt, m0_rows, bias_row, mnorm, c_in, n_in)


def _out_ffn_kernel(x_ref, mix_ref, wo_ref, g1_ref, g2_ref, wg_ref, wu_ref, wd_ref, g3_ref, o_ref):
    tm = x_ref.shape[0]
    groups = [slice(r, r + tm // ROW_GROUPS) for r in range(0, tm, tm // ROW_GROUPS)]
    ys = [jnp.dot(mix_ref[rs, :], wo_ref[...], preferred_element_type=F32) for rs in groups]
    x1s = [x_ref[rs, :] + _rms(y, g1_ref[...]) for rs, y in zip(groups, ys)]
    fs = [_rms(x1, g2_ref[...]).astype(BF16) for x1 in x1s]
    accs = [None] * ROW_GROUPS
    for off in range(0, D_FF, FFN_CHUNK):
        acts = []
        for f in fs:
            g = jnp.dot(f, wg_ref[:, off:off + FFN_CHUNK], preferred_element_type=F32)
            u = jnp.dot(f, wu_ref[:, off:off + FFN_CHUNK], preferred_element_type=F32)
            acts.append((g * jax.nn.sigmoid(g) * u).astype(BF16))
        for i, act in enumerate(acts):
            part = jnp.dot(act, wd_ref[off:off + FFN_CHUNK, :], preferred_element_type=F32)
            accs[i] = part if accs[i] is None else accs[i] + part
    for rs, x1, acc in zip(groups, x1s, accs):
        o_ref[rs, :] = x1 + _rms(acc, g3_ref[...])


def _out_ffn(x, mix, w_out, g_post_mix, g_pre_ffn, w_gate, w_up, w_down, g_post_ffn):
    m = x.shape[0]
    tm = TOKEN_TILE
    row = lambda n: pl.BlockSpec((tm, n), lambda i: (i, 0))
    vec = _const_spec((1, D_MODEL))
    return pl.pallas_call(
        _out_ffn_kernel,
        grid=(m // tm,),
        in_specs=[row(D_MODEL), row(ATTN_WIDTH + M_WIDTH), _const_spec((D_MODEL, D_MODEL)), vec, vec,
                  _const_spec((D_MODEL, D_FF)), _const_spec((D_MODEL, D_FF)), _const_spec((D_FF, D_MODEL)), vec],
        out_specs=row(D_MODEL),
        out_shape=jax.ShapeDtypeStruct((m, D_MODEL), F32),
        compiler_params=_params(("parallel",)),
        name="out_ffn",
    )(x, mix, w_out, g_post_mix, g_pre_ffn, w_gate, w_up, w_down, g_post_ffn)


def _layer(xp, xs, cache_k, cache_v, state_c, state_n, state_m, w_in, b_i, b_f, attn_sink, m_norm, w_out,
           g_pre_mix, g_post_mix, g_pre_ffn, g_post_ffn, w_gate, w_up, w_down):
    bp, sp, _ = xp.shape
    bs, ts, _ = xs.shape
    assert bp == 1 and sp % TOKEN_TILE == 0 and TOKEN_TILE % WINDOW == 0
    assert ts & (ts - 1) == 0 and (bs * ts) % TOKEN_TILE == 0 and bs % SAMPLE_MLSTM_BATCH == 0

    row = lambda v: v.reshape(1, -1)
    bias_row = jnp.pad(jnp.concatenate([b_i, b_f]), (0, GATE_PAD - 2 * M_HEADS)).reshape(1, GATE_PAD)
    sink = row(attn_sink)

    x2 = xp.reshape(sp, D_MODEL)
    mix, k_w, v_w, c_aug, m_p, w_pad, wg, wu, wd, wo = _mixer_prompt(
        sink, row(b_i), row(b_f), x2, row(g_pre_mix), w_in, row(m_norm), w_gate, w_up, w_down, w_out)
    ffn = (wo, row(g_post_mix), row(g_pre_ffn), wg, wu, wd, row(g_post_ffn))
    yp = _out_ffn(x2, mix, *ffn).reshape(xp.shape)
    k_p = k_w.reshape(1, WINDOW, KV_HEADS, HEAD_DIM)
    v_p = v_w.reshape(1, WINDOW, KV_HEADS, HEAD_DIM)
    c_p = jnp.swapaxes(c_aug[:, :M_HEAD_DIM, :], 1, 2).reshape(1, M_HEADS, M_HEAD_DIM, M_HEAD_DIM)
    n_p = c_aug[:, M_HEAD_DIM, :].reshape(1, M_HEADS, M_HEAD_DIM)
    m_p = m_p[:M_HEADS, 0].reshape(1, M_HEADS)

    x2 = xs.reshape(bs * ts, D_MODEL)
    qa, ka, va, qm, km, vm, om, gt = _inproj(x2, row(g_pre_mix), w_pad)
    mix_a, k_s, v_s = _attn_sample(sink, qa, ka, va, cache_k.reshape(bs, WINDOW, KV_WIDTH),
                                   cache_v.reshape(bs, WINDOW, KV_WIDTH), ts)
    m0_rows = jnp.pad(jnp.repeat(state_m, ts, axis=0), ((0, 0), (0, LANES - M_HEADS)))
    mix_m, c_s, n_s, m_rows = _mlstm_sample(qm, km, vm, om, gt, m0_rows, bias_row, row(m_norm),
                                            state_c.reshape(bs * M_HEADS, M_HEAD_DIM, M_HEAD_DIM),
                                            state_n.reshape(bs, M_WIDTH), ts)
    ys = _out_ffn(x2, jnp.concatenate([mix_a, mix_m], axis=1), *ffn).reshape(xs.shape)
    k_s = k_s.reshape(bs, WINDOW, KV_HEADS, HEAD_DIM)
    v_s = v_s.reshape(bs, WINDOW, KV_HEADS, HEAD_DIM)
    c_s = c_s.reshape(bs, M_HEADS, M_HEAD_DIM, M_HEAD_DIM)
    n_s = n_s.reshape(bs, M_HEADS, M_HEAD_DIM)
    m_s = m_rows[ts - 1::ts, :M_HEADS]
    return yp, ys, (k_p, v_p, c_p, n_p, m_p), (k_s, v_s, c_s, n_s, m_s)


def kernel(x_prompt, x_sample, cache_k, cache_v, state_C, state_n, state_m, w_in, b_i, b_f, attn_sink, m_norm,
           w_out, g_pre_mix, g_post_mix, g_pre_ffn, g_post_ffn, w_gate, w_up, w_down):
    depth = w_in.shape[0]
    xp, xs = x_prompt, x_sample
    prompt_states, sample_states = [], []
    for l in range(depth):
        xp, xs, st_p, st_s = _layer(xp, xs, cache_k[l], cache_v[l], state_C[l], state_n[l], state_m[l],
                                    w_in[l], b_i[l], b_f[l], attn_sink[l], m_norm[l], w_out[l],
                                    g_pre_mix[l], g_post_mix[l], g_pre_ffn[l], g_post_ffn[l],
                                    w_gate[l], w_up[l], w_down[l])
        prompt_states.append(st_p)
        sample_states.append(st_s)
    stack = lambda states, i: jnp.stack([s[i] for s in states], axis=0)
    return (xp, xs) + tuple(stack(prompt_states, i) for i in range(5)) + tuple(stack(sample_states, i) for i in range(5))
```

```python
import functools

import jax
import jax.numpy as jnp
from jax import lax
from jax.experimental import pallas as pl
from jax.experimental.pallas import tpu as pltpu

F32 = jnp.float32
BF16 = jnp.bfloat16
HIGHEST = lax.Precision.HIGHEST

D_MODEL = 1024
HEAD_DIM = 64
ATTN_HEADS = 8
KV_HEADS = 2
GROUP = ATTN_HEADS // KV_HEADS
ATTN_WIDTH = ATTN_HEADS * HEAD_DIM
KV_WIDTH = KV_HEADS * HEAD_DIM
WINDOW = 128
M_HEADS = 4
M_HEAD_DIM = 128
M_WIDTH = M_HEADS * M_HEAD_DIM
D_FF = 2816
EPS = 1e-6

LANES = 128
SUBLANES = 8
GATE_PAD = LANES
BF16_SUBLANES = 16
AUG_ROWS = M_HEAD_DIM + BF16_SUBLANES
IN_MAIN = ATTN_WIDTH + 2 * KV_WIDTH + 4 * M_WIDTH
IN_PAD = IN_MAIN + GATE_PAD
VMEM_LIMIT = 56 * 1024 * 1024

HEAD_ORDER = tuple(h for j in range(GROUP) for h in (j, j + GROUP))

TOKEN_TILE = 512
FFN_CHUNK = 1408
ROW_GROUPS = 2
WEIGHT_SLAB = 32
MIXER_SLOT_REFS = 8
SAMPLE_ATTN_BATCH = 16
SAMPLE_MLSTM_BATCH = 16


def _rms(x, g):
    return x * lax.rsqrt(jnp.mean(x * x, axis=-1, keepdims=True) + EPS) * g


def _const_spec(shape):
    nd = len(shape)
    return pl.BlockSpec(shape, lambda i: (0,) * nd, pipeline_mode=pl.Buffered(1))


def _params(semantics):
    return pltpu.CompilerParams(dimension_semantics=semantics, vmem_limit_bytes=VMEM_LIMIT)


OFF_QA, OFF_KA, OFF_VA = 0, ATTN_WIDTH, ATTN_WIDTH + KV_WIDTH
OFF_QM = ATTN_WIDTH + 2 * KV_WIDTH
OFF_KM, OFF_VM, OFF_OM = OFF_QM + M_WIDTH, OFF_QM + 2 * M_WIDTH, OFF_QM + 3 * M_WIDTH


def _inproj_kernel(x_ref, g_ref, w_ref, qa_ref, ka_ref, va_ref, qm_ref, km_ref, vm_ref, om_ref, gt_ref):
    tm = x_ref.shape[0]
    step = tm // ROW_GROUPS
    for r0 in range(0, tm, step):
        rs = slice(r0, r0 + step)
        h = _rms(x_ref[rs, :], g_ref[...]).astype(BF16)

        def proj(off, n):
            return jnp.dot(h, w_ref[:, off:off + n], preferred_element_type=F32)

        qa_ref[rs, :] = proj(OFF_QA, ATTN_WIDTH) * (HEAD_DIM ** -0.5)
        ka_ref[rs, :] = proj(OFF_KA, KV_WIDTH)
        va_ref[rs, :] = proj(OFF_VA, KV_WIDTH)
        qm_ref[rs, :] = proj(OFF_QM, M_WIDTH)
        km_ref[rs, :] = proj(OFF_KM, M_WIDTH) * (M_HEAD_DIM ** -0.5)
        vm_ref[rs, :] = proj(OFF_VM, M_WIDTH)
        om_ref[rs, :] = proj(OFF_OM, M_WIDTH)
        gt_ref[rs, :] = proj(IN_MAIN, GATE_PAD)


def _inproj(x, g_pre, w_pad):
    m = x.shape[0]
    tm = TOKEN_TILE
    row = lambda n: pl.BlockSpec((tm, n), lambda i: (i, 0))
    widths = (ATTN_WIDTH, KV_WIDTH, KV_WIDTH, M_WIDTH, M_WIDTH, M_WIDTH, M_WIDTH, GATE_PAD)
    return pl.pallas_call(
        _inproj_kernel,
        grid=(m // tm,),
        in_specs=[row(D_MODEL), _const_spec((1, D_MODEL)), _const_spec((D_MODEL, IN_PAD))],
        out_specs=[row(n) for n in widths],
        out_shape=[jax.ShapeDtypeStruct((m, n), F32) for n in widths],
        compiler_params=_params(("parallel",)),
        name="inproj",
    )(x, g_pre, w_pad)


def _stack_heads(q_tiles):
    lane = lax.broadcasted_iota(jnp.int32, q_tiles[0].shape, 1)
    lo = lane < HEAD_DIM
    zero = jnp.zeros_like(q_tiles[0])
    parts = []
    for qt in q_tiles:
        parts += [jnp.where(lo, qt, zero), jnp.where(lo, zero, qt)]
    return jnp.concatenate(parts, axis=0)


def _unstack_heads(o, rows):
    lane = lax.broadcasted_iota(jnp.int32, (rows, LANES), 1)
    lo = lane < HEAD_DIM
    return [jnp.where(lo, o[(2 * j) * rows:(2 * j + 1) * rows, :], o[(2 * j + 1) * rows:(2 * j + 2) * rows, :])
            for j in range(GROUP)]


def _sink_rows(sink_ref, rows):
    return jnp.concatenate([jnp.full((rows, LANES), sink_ref[0, h], F32) for h in HEAD_ORDER], axis=0)


def _attn_sample_kernel(sink_ref, q_ref, kn_ref, vn_ref, ck_ref, cv_ref, o_ref, ko_ref, vo_ref, *, dec_seq):
    t = dec_seq
    bb = SAMPLE_ATTN_BATCH
    nrow = ATTN_HEADS * t
    ts, ns = t.bit_length() - 1, nrow.bit_length() - 1
    r_c = lax.broadcasted_iota(jnp.int32, (nrow, WINDOW), 0) & (t - 1)
    c_c = lax.broadcasted_iota(jnp.int32, (nrow, WINDOW), 1)
    vis_cache = c_c > r_c
    r_n = lax.broadcasted_iota(jnp.int32, (bb * nrow, bb * t), 0)
    c_n = lax.broadcasted_iota(jnp.int32, (bb * nrow, bb * t), 1)
    vis_new = jnp.logical_and((r_n >> ns) == (c_n >> ts), (c_n & (t - 1)) <= (r_n & (t - 1)))
    sink = _sink_rows(sink_ref, t)
    nt = (((1,), (1,)), ((), ()))
    kn_all, vn_all = kn_ref[...], vn_ref[...]
    qs, s_c = [], []
    for b in range(bb):
        rows = slice(b * t, (b + 1) * t)
        ck, cv = ck_ref[b], cv_ref[b]
        ko_ref[b] = jnp.concatenate([ck[t:, :], kn_all[rows, :]], axis=0)
        vo_ref[b] = jnp.concatenate([cv[t:, :], vn_all[rows, :]], axis=0)
        qs.append(_stack_heads([q_ref[rows, j * LANES:(j + 1) * LANES] for j in range(GROUP)]).astype(BF16))
        s_c.append(jnp.where(vis_cache, lax.dot_general(qs[b], ck.astype(BF16), nt, preferred_element_type=F32),
                             -jnp.inf))
    s_n = jnp.where(vis_new, lax.dot_general(jnp.concatenate(qs, axis=0), kn_all.astype(BF16), nt,
                                             preferred_element_type=F32), -jnp.inf)
    p_c, p_n, rden = [], [], []
    for b in range(bb):
        s_nb = s_n[b * nrow:(b + 1) * nrow, :]
        m = jnp.maximum(jnp.maximum(jnp.max(s_c[b], axis=-1, keepdims=True), jnp.max(s_nb, axis=-1, keepdims=True)),
                        sink)
        p_c.append(jnp.exp(s_c[b] - m))
        p_n.append(jnp.exp(s_nb - m[:, :bb * t]))
        rden.append(1.0 / (jnp.sum(p_c[b], axis=-1, keepdims=True) + jnp.sum(p_n[b], axis=-1, keepdims=True)
                           + jnp.exp(sink - m)))
    o_n = jnp.dot(jnp.concatenate(p_n, axis=0).astype(BF16), vn_all.astype(BF16), preferred_element_type=F32)
    outs = [[] for _ in range(GROUP)]
    for b in range(bb):
        o = (jnp.dot(p_c[b].astype(BF16), cv_ref[b].astype(BF16), preferred_element_type=F32)
             + o_n[b * nrow:(b + 1) * nrow, :]) * rden[b]
        for j, tile in enumerate(_unstack_heads(o, t)):
            outs[j].append(tile)
    for j, parts in enumerate(outs):
        o_ref[:, j * LANES:(j + 1) * LANES] = jnp.concatenate(parts, axis=0).astype(o_ref.dtype)


def _attn_sample(sink, qa, ka, va, cache_k, cache_v, dec_seq):
    nb = cache_k.shape[0]
    bb = SAMPLE_ATTN_BATCH
    row = lambda n: pl.BlockSpec((bb * dec_seq, n), lambda i: (i, 0))
    cache = pl.BlockSpec((bb, WINDOW, KV_WIDTH), lambda i: (i, 0, 0))
    return pl.pallas_call(
        functools.partial(_attn_sample_kernel, dec_seq=dec_seq),
        grid=(nb // bb,),
        in_specs=[pl.BlockSpec(memory_space=pltpu.SMEM), row(ATTN_WIDTH), row(KV_WIDTH), row(KV_WIDTH), cache, cache],
        out_specs=[row(ATTN_WIDTH), cache, cache],
        out_shape=[jax.ShapeDtypeStruct((nb * dec_seq, ATTN_WIDTH), BF16),
                   jax.ShapeDtypeStruct(cache_k.shape, F32), jax.ShapeDtypeStruct(cache_v.shape, F32)],
        compiler_params=_params(("parallel",)),
        name="attn_sample",
    )(sink, qa, ka, va, cache_k, cache_v)


def _head_out(hh, mnorm_row, om):
    y = hh * lax.rsqrt(jnp.mean(hh * hh, axis=-1, keepdims=True) + EPS) * mnorm_row
    return (jax.nn.sigmoid(om) * y).astype(BF16)


def _interleave(first, second):
    first, second = list(first), list(second)
    n1, n2 = len(first), len(second)
    i2 = 0
    for i1, op in enumerate(first):
        op()
        while i2 < n2 and (i2 + 1) * n1 <= (i1 + 1) * n2:
            second[i2]()
            i2 += 1
    for op in second[i2:]:
        op()


def _mixer_prompt_kernel(*refs):
    step = pl.program_id(0)
    shared, slots = refs[:-2 * MIXER_SLOT_REFS], refs[-2 * MIXER_SLOT_REFS:]
    halves = (slots[:MIXER_SLOT_REFS], slots[MIXER_SLOT_REFS:])
    for parity in (0, 1):
        pl.when(step % 2 == parity)(functools.partial(_mixer_step, *shared, *halves[parity], *halves[1 - parity]))


def _mixer_step(sink_ref, bi_ref, bf_ref, x_ref, g_ref, win_ref, mnorm_ref, wg_ref, wu_ref, wd_ref, wo_ref,
                mix_ref, kw_ref, vw_ref, c_ref, m_ref, w_ref, wgb_ref, wub_ref, wdb_ref, wob_ref,
                qt_w, km_w, vt_w, gt_w, qa_w, ka_w, va_w, om_w, qt_r, km_r, vt_r, gt_r, qa_r, ka_r, va_r, om_r):
    step = pl.program_id(0)

    @pl.when(step == 0)
    def _():
        for ref in (qt_w, km_w, vt_w, gt_w, qa_w, ka_w, va_w, om_w, qt_r, km_r, vt_r, gt_r, qa_r, ka_r, va_r, om_r):
            ref[...] = jnp.zeros_like(ref)
        for j in range(GROUP):
            lo = win_ref[:, j * HEAD_DIM:(j + 1) * HEAD_DIM]
            hi = win_ref[:, (j + GROUP) * HEAD_DIM:(j + GROUP + 1) * HEAD_DIM]
            w_ref[:, j * LANES:(j + 1) * LANES] = jnp.concatenate([lo, hi], axis=1).astype(BF16)
        w_ref[:, ATTN_WIDTH:IN_MAIN] = win_ref[:, ATTN_WIDTH:IN_MAIN].astype(BF16)
        gw = jnp.concatenate([win_ref[:, IN_MAIN:IN_MAIN + 2 * M_HEADS],
                              jnp.zeros((D_MODEL, GATE_PAD - 2 * M_HEADS), F32)], axis=1)
        w_ref[:, IN_MAIN:] = gw.astype(BF16)

    @pl.when(step <= 1)
    def _():
        kw_ref[...] = jnp.zeros_like(kw_ref)
        vw_ref[...] = jnp.zeros_like(vw_ref)
        c_ref[...] = jnp.zeros_like(c_ref)
        m_ref[...] = jnp.zeros_like(m_ref)

    wgb_ref[...] = wg_ref[...].astype(BF16)
    wub_ref[...] = wu_ref[...].astype(BF16)
    wdb_ref[...] = wd_ref[...].astype(BF16)
    wob_ref[...] = wo_ref[...].astype(BF16)

    tm = x_ref.shape[0]
    ln = WINDOW
    groups = [slice(r0, r0 + tm // ROW_GROUPS) for r0 in range(0, tm, tm // ROW_GROUPS)]
    blocks = [slice(r0, r0 + ln) for r0 in range(0, tm, ln)]
    hcols = [slice(h * M_HEAD_DIM, (h + 1) * M_HEAD_DIM) for h in range(M_HEADS)]
    heads = range(M_HEADS)
    nt = (((1,), (1,)), ((), ()))

    def put_t(ref, n=None):
        def store(rs, val):
            val = val.T if n is None else val.T[:n, :]
            ref[:, rs] = val.astype(ref.dtype)
        return store

    def put(ref, scale=None):
        def store(rs, val):
            ref[rs, :] = (val if scale is None else val * scale).astype(ref.dtype)
        return store

    segments = ((OFF_QM, M_WIDTH, put_t(qt_w)), (OFF_KM, M_WIDTH, put(km_w, M_HEAD_DIM ** -0.5)),
                (OFF_VM, M_WIDTH, put_t(vt_w)), (IN_MAIN, GATE_PAD, put_t(gt_w, SUBLANES)),
                (OFF_QA, ATTN_WIDTH, put(qa_w, HEAD_DIM ** -0.5)), (OFF_KA, KV_WIDTH, put(ka_w)),
                (OFF_VA, KV_WIDTH, put(va_w)), (OFF_OM, M_WIDTH, put(om_w)))

    def projection():
        for rs in groups:
            h = _rms(x_ref[rs, :], g_ref[...]).astype(BF16)
            for off, n, store in segments:
                yield functools.partial(
                    lambda rs, h, off, n, store: store(rs, jnp.dot(h, w_ref[:, off:off + n],
                                                                   preferred_element_type=F32)), rs, h, off, n, store)

    r = lax.broadcasted_iota(jnp.int32, (ln, ln), 0)
    c = lax.broadcasted_iota(jnp.int32, (ln, ln), 1)
    causal_t = r <= c
    upper = causal_t.astype(F32)
    lane8 = lax.broadcasted_iota(jnp.int32, (SUBLANES, ln), 1)
    ones_rows = jnp.ones((AUG_ROWS - M_HEAD_DIM, ln), BF16)
    zrows = jnp.zeros((ln - SUBLANES, ln), F32)
    row8 = lax.broadcasted_iota(jnp.int32, (SUBLANES, ln), 0)
    bias = jnp.zeros((SUBLANES, ln), F32)
    for h in heads:
        bias = jnp.where(row8 == h, bi_ref[0, h], jnp.where(row8 == M_HEADS + h, bf_ref[0, h], bias))
    nrow = ATTN_HEADS * ln
    ra = lax.broadcasted_iota(jnp.int32, (nrow, ln), 0) & (ln - 1)
    ca = lax.broadcasted_iota(jnp.int32, (nrow, ln), 1)
    own = ca <= ra
    sink = _sink_rows(sink_ref, ln)
    no_prev = jnp.where(step <= 1, -jnp.inf, 0.0)
    ones = jnp.ones((2 * ln, LANES), BF16)
    pre, scores_m, scores_a, v_augs, scal, gated, probs, maxes = [], [], [], [], [], [], [], []
    mem = []
    parts = {}

    def mlstm_scores(rows):
        gi = gt_r[:, rows] + bias
        b = pltpu.roll(jnp.dot(jax.nn.log_sigmoid(gi), upper, precision=HIGHEST, preferred_element_type=F32),
                       M_HEADS, axis=0)
        g = gi - b
        cm0 = g
        sh = 1
        while sh < ln:
            cm0 = jnp.maximum(cm0, jnp.where(lane8 >= sh, pltpu.roll(cm0, sh, axis=1), -jnp.inf))
            sh *= 2
        b_last = jnp.broadcast_to(b[:, ln - 1:ln], b.shape)
        g_max = jnp.broadcast_to(cm0[:, ln - 1:ln], b.shape)
        g_cols = jnp.concatenate([g, zrows], axis=0).T
        pre.append((b, g_cols, cm0, b_last, g_max, (b_last - b) + gi))
        scores_m.append([jnp.dot(km_r[rows, hc], qt_r[hc, rows], preferred_element_type=F32)
                         for hc in hcols])

    def attn_scores(blk, rows):
        if blk == 0:
            k_prev, v_prev = kw_ref[...], vw_ref[...]
        else:
            k_prev, v_prev = ka_r[blocks[blk - 1], :], va_r[blocks[blk - 1], :]
        kcat = jnp.concatenate([ka_r[rows, :], k_prev], axis=0).astype(BF16)
        vcat = jnp.concatenate([va_r[rows, :], v_prev], axis=0).astype(BF16)
        v_augs.append(jnp.concatenate([vcat, ones], axis=1))
        qs = _stack_heads([qa_r[rows, j * LANES:(j + 1) * LANES] for j in range(GROUP)])
        scores_a.append(lax.dot_general(qs, kcat, nt, preferred_element_type=F32))

    def scalars():
        kw_ref[...] = ka_r[blocks[-1], :]
        vw_ref[...] = va_r[blocks[-1], :]
        m_prev = m_ref[...]
        for b, _, cm0, b_last, g_max, w_end_arg in pre:
            cm = jnp.maximum(cm0, m_prev)
            m_end = b_last + jnp.maximum(g_max, m_prev)
            scal.append((cm, jnp.exp(m_prev - cm), jnp.exp(-(b + cm)), jnp.exp(b_last + m_prev - m_end),
                         jnp.exp(w_end_arg - m_end)))
            m_prev = m_end
        m_ref[...] = m_prev
        mem.extend(c_ref[h] for h in heads)

    def prepare(i, rows):
        g_cols, (cm, _, _, _, w_end) = pre[i][1], scal[i]
        vts = [jnp.concatenate([vt_r[hc, rows], ones_rows], axis=0) for hc in hcols]
        sqks = [(jnp.exp(jnp.where(causal_t, g_cols[:, h:h + 1] - cm[h:h + 1, :], -jnp.inf))
                 * scores_m[i][h]).astype(BF16) for h in heads]
        kvws = [(vts[h].astype(F32) * w_end[h:h + 1, :]).astype(BF16) for h in heads]
        gated.append((vts, sqks, kvws))
        s = scores_a[i]
        s_prev = s[:, ln:]
        if i == 0:
            s_prev = s_prev + no_prev
        sc = jnp.where(own, s[:, :ln], s_prev)
        mx = jnp.maximum(jnp.max(sc, axis=-1, keepdims=True), sink)
        p = jnp.exp(sc - mx)
        zero = jnp.zeros_like(p)
        probs.append(jnp.concatenate([jnp.where(own, p, zero), jnp.where(own, zero, p)], axis=1).astype(BF16))
        maxes.append(mx)

    def update(i, rows):
        parts["upd"] = [jnp.dot(gated[i][2][h], km_r[rows, hcols[h]], preferred_element_type=F32)
                        for h in heads]

    def intra(i, rows):
        parts["intra"] = [jnp.dot(gated[i][0][h], gated[i][1][h], preferred_element_type=F32) for h in heads]

    def inter(i, rows):
        parts["inter"] = [jnp.dot(mem[h].astype(BF16), qt_r[hcols[h], rows], preferred_element_type=F32)
                          for h in heads]

    def outputs(i, rows):
        _, w_inter, e_negm, dec, _ = scal[i]
        o = jnp.dot(probs[i], v_augs[i], preferred_element_type=F32)
        o = o[:, :LANES] * (1.0 / (o[:, LANES:] + jnp.exp(sink - maxes[i])))
        for j, tile in enumerate(_unstack_heads(o, ln)):
            mix_ref[rows, j * LANES:(j + 1) * LANES] = tile.astype(mix_ref.dtype)
        for h in heads:
            num = parts["inter"][h] * w_inter[h:h + 1, :] + parts["intra"][h]
            mem[h] = dec[h:h + 1, :] * mem[h] + parts["upd"][h]
            den = jnp.maximum(jnp.abs(num[M_HEAD_DIM:M_HEAD_DIM + 1, :]), e_negm[h:h + 1, :])
            hh = num[:M_HEAD_DIM, :] * (1.0 / den)
            y = hh * lax.rsqrt(jnp.mean(hh * hh, axis=0, keepdims=True) + EPS)
            mcols = slice(ATTN_WIDTH + h * M_HEAD_DIM, ATTN_WIDTH + (h + 1) * M_HEAD_DIM)
            mix_ref[rows, mcols] = (jax.nn.sigmoid(om_r[rows, hcols[h]])
                                    * (y.T * mnorm_ref[:, hcols[h]])).astype(mix_ref.dtype)

    def mixers():
        for rows in blocks:
            yield functools.partial(mlstm_scores, rows)
        for blk, rows in enumerate(blocks):
            yield functools.partial(attn_scores, blk, rows)
        yield scalars
        for i, rows in enumerate(blocks):
            for stage in (prepare, update, intra, inter, outputs):
                yield functools.partial(stage, i, rows)

    _interleave(mixers(), projection())
    for h in heads:
        c_ref[h] = mem[h]


def _out_row_block(i):
    per_head = (ATTN_WIDTH // ATTN_HEADS) // WEIGHT_SLAB
    j, part = i // per_head, i % per_head
    head = (j % 2) * GROUP + j // 2
    return jnp.where(i < ATTN_HEADS * per_head, head * per_head + part, i)


def _mixer_prompt(sink, b_i, b_f, x, g_pre, w_in, mnorm, w_gate, w_up, w_down, w_out):
    m = x.shape[0]
    tm = TOKEN_TILE
    tiles = m // tm
    assert D_MODEL == tiles * WEIGHT_SLAB and D_FF % (tiles // 2) == 0 and (D_FF // (tiles // 2)) % BF16_SUBLANES == 0
    down_slab = D_FF // (tiles // 2)
    last = tiles - 1
    cur = lambda i: jnp.minimum(i, last)
    whole = lambda shape: pl.BlockSpec(shape, lambda i: (0,) * len(shape))
    smem = pl.BlockSpec(memory_space=pltpu.SMEM)
    slab = lambda n: pl.BlockSpec((WEIGHT_SLAB, n), lambda i: (cur(i), 0))
    down = pl.BlockSpec((down_slab, D_MODEL), lambda i: (cur(i) // 2, 0))
    c_shape = (M_HEADS, AUG_ROWS, M_HEAD_DIM)
    w_shape = (WINDOW, KV_WIDTH)
    s_shape = (SUBLANES, LANES)
    sds = jax.ShapeDtypeStruct
    tok = lambda n, dt: pltpu.VMEM((tm, n), dt)
    feat = lambda n, dt: pltpu.VMEM((n, tm), dt)
    slot = [feat(M_WIDTH, BF16), tok(M_WIDTH, BF16), feat(M_WIDTH, BF16), feat(SUBLANES, F32),
            tok(ATTN_WIDTH, BF16), tok(KV_WIDTH, F32), tok(KV_WIDTH, F32), tok(M_WIDTH, F32)]
    assert len(slot) == MIXER_SLOT_REFS
    return pl.pallas_call(
        _mixer_prompt_kernel,
        grid=(tiles + 1,),
        in_specs=[smem, smem, smem, pl.BlockSpec((tm, D_MODEL), lambda i: (cur(i), 0)), _const_spec((1, D_MODEL)),
                  _const_spec(w_in.shape), _const_spec((1, M_WIDTH)), slab(D_FF), slab(D_FF), down,
                  pl.BlockSpec((WEIGHT_SLAB, D_MODEL), lambda i: (_out_row_block(cur(i)), 0))],
        out_specs=[pl.BlockSpec((tm, ATTN_WIDTH + M_WIDTH), lambda i: (jnp.maximum(i - 1, 0), 0)),
                   whole(w_shape), whole(w_shape), whole(c_shape), whole(s_shape),
                   whole((D_MODEL, IN_PAD)), slab(D_FF), slab(D_FF), down, slab(D_MODEL)],
        out_shape=[sds((m, ATTN_WIDTH + M_WIDTH), BF16), sds(w_shape, F32), sds(w_shape, F32), sds(c_shape, F32),
                   sds(s_shape, F32), sds((D_MODEL, IN_PAD), BF16), sds(w_gate.shape, BF16), sds(w_up.shape, BF16),
                   sds(w_down.shape, BF16), sds(w_out.shape, BF16)],
        scratch_shapes=slot + slot,
        compiler_params=_params(("arbitrary",)),
        name="mixer_prompt",
    )(sink, b_i, b_f, x, g_pre, w_in, mnorm, w_gate, w_up, w_down, w_out)


def _gates(g_blk, bias_row, cum):
    pre = g_blk + bias_row
    lane = lax.broadcasted_iota(jnp.int32, pre.shape, 1)
    a = jnp.where(lane < M_HEADS, pre, jax.nn.log_sigmoid(pre))
    b = jnp.dot(cum, a, precision=HIGHEST, preferred_element_type=F32)
    return a, b, a.T, b.T


def _col(x, j):
    return jnp.broadcast_to(x[:, j:j + 1], x.shape)


def _mlstm_gated_scores(q, k, a, b, at, bt, h, mask, m_prev, last):
    bc, ic = _col(b, M_HEADS + h), _col(a, h)
    br, ir = bt[M_HEADS + h:M_HEADS + h + 1, :], at[h:h + 1, :]
    d = jnp.where(mask, (bc - br) + ir, -jnp.inf)
    inter = bc + m_prev
    m_t = jnp.maximum(inter, jnp.max(d, axis=-1, keepdims=True))
    w_inter = jnp.exp(inter - m_t)
    sqk = jnp.exp(d - m_t) * lax.dot_general(q, k, (((1,), (1,)), ((), ())), preferred_element_type=F32)
    m_end = last(m_t)
    bl = last(bc)
    dec = jnp.exp(bl + m_prev - m_end)
    w_end = jnp.exp((bl - bc) + ic - m_end)
    kw = k.astype(F32) * w_end
    return w_inter, sqk, m_t, m_end, dec, kw


def _mlstm_sample_kernel(q_ref, k_ref, v_ref, om_ref, g_ref, m0_ref, bias_ref, mnorm_ref, c_ref, n_ref,
                         o_ref, c_out, n_out, m_out, *, dec_seq):
    t = dec_seq
    bb = SAMPLE_MLSTM_BATCH
    ln = bb * t
    r = lax.broadcasted_iota(jnp.int32, (ln, ln), 0)
    c = lax.broadcasted_iota(jnp.int32, (ln, ln), 1)
    shift = t.bit_length() - 1
    same = (r >> shift) == (c >> shift)
    mask = jnp.logical_and(same, c <= r)
    cum = mask.astype(F32)
    expand = (c == (r >> shift)).astype(F32)
    gather = ((c >> shift) == r).astype(F32)
    is_last = (r & (t - 1)) == t - 1
    is_first = (r & (t - 1)) == 0

    def last(x):
        y = jnp.where(is_last, x, 0.0)
        step = 1
        while step < t:
            y = y + pltpu.roll(y, ln - step, axis=0)
            step *= 2
        return y

    a, b, at, bt = _gates(g_ref[...], bias_ref[...], cum)
    m0 = m0_ref[...]
    lane = lax.broadcasted_iota(jnp.int32, (ln, LANES), 1)
    zpad = jnp.zeros((ln - bb, M_HEAD_DIM), F32)
    heads = range(M_HEADS)
    hcols = [slice(h * M_HEAD_DIM, (h + 1) * M_HEAD_DIM) for h in heads]
    qfs = [q_ref[:, hc] for hc in hcols]
    vs = [v_ref[:, hc].astype(BF16) for hc in hcols]
    qcs = [jnp.concatenate(
        [jnp.dot(qfs[h][s * t:(s + 1) * t, :], c_ref[s * M_HEADS + h].astype(BF16).astype(F32),
                 preferred_element_type=F32) for s in range(bb)], axis=0) for h in heads]
    n_exps = [jnp.dot(expand, jnp.concatenate([n_ref[:, hc], zpad], axis=0), precision=HIGHEST,
                      preferred_element_type=F32) for hc in hcols]
    parts = [_mlstm_gated_scores(qfs[h].astype(BF16), k_ref[:, hcols[h]].astype(BF16), a, b, at, bt, h, mask,
                                 _col(m0, h), last) for h in heads]
    intras = [jnp.dot(parts[h][1].astype(BF16), vs[h], preferred_element_type=F32) for h in heads]
    m_cols = jnp.zeros((ln, LANES), F32)
    kwts = []
    for h in heads:
        w_inter, sqk, m_t, m_end, dec, kw = parts[h]
        num = w_inter * qcs[h] + intras[h]
        nq = w_inter * jnp.sum(qfs[h] * n_exps[h], axis=-1, keepdims=True) + jnp.sum(sqk, axis=-1, keepdims=True)
        hh = num / jnp.maximum(jnp.abs(nq), jnp.exp(-m_t))
        o_ref[:, hcols[h]] = _head_out(hh, mnorm_ref[:, hcols[h]], om_ref[:, hcols[h]])
        kwts.append(kw.T.astype(BF16))
        m_cols = jnp.where(lane == h, m_end, m_cols)
    m_out[...] = m_cols
    for h in heads:
        dec = parts[h][4]
        for s in range(bb):
            lhs = jnp.where((c >> shift) == s, kwts[h], jnp.zeros_like(kwts[h]))
            upd = jnp.dot(lhs, vs[h], preferred_element_type=F32)
            c_out[s * M_HEADS + h] = dec[s * t:s * t + 1, :] * c_ref[s * M_HEADS + h] + upd
    for h in heads:
        dec, kw = parts[h][4], parts[h][5]
        n_new = jnp.dot(gather, jnp.where(is_first, dec * n_exps[h], 0.0) + kw, precision=HIGHEST,
                        preferred_element_type=F32)
        n_out[:, hcols[h]] = n_new[:bb, :]


def _mlstm_sample(qm, km, vm, om, gt, m0_rows, bias_row, mnorm, c_in, n_in, dec_seq):
    m = qm.shape[0]
    bb = SAMPLE_MLSTM_BATCH
    tm = bb * dec_seq
    nb = m // dec_seq
    row = lambda n: pl.BlockSpec((tm, n), lambda i: (i, 0))
    whole = lambda shape: pl.BlockSpec(shape, lambda i: (0,) * len(shape))
    c_spec = pl.BlockSpec((bb * M_HEADS, M_HEAD_DIM, M_HEAD_DIM), lambda i: (i, 0, 0))
    n_spec = pl.BlockSpec((bb, M_WIDTH), lambda i: (i, 0))
    return pl.pallas_call(
        functools.partial(_mlstm_sample_kernel, dec_seq=dec_seq),
        grid=(nb // bb,),
        in_specs=[row(M_WIDTH), row(M_WIDTH), row(M_WIDTH), row(M_WIDTH), row(GATE_PAD), row(LANES),
                  whole((1, GATE_PAD)), whole((1, M_WIDTH)), c_spec, n_spec],
        out_specs=[row(M_WIDTH), c_spec, n_spec, row(LANES)],
        out_shape=[jax.ShapeDtypeStruct((m, M_WIDTH), BF16), jax.ShapeDtypeStruct(c_in.shape, F32),
                   jax.ShapeDtypeStruct(n_in.shape, F32), jax.ShapeDtypeStruct((m, LANES), F32)],
        compiler_params=_params(("parallel",)),
        name="mlstm_sample",
    )(qm, km, vm, om, gt, m0_rows, bias_row, mnorm, c_in, n_in)


def _out_ffn_kernel(x_ref, mix_ref, wo_ref, g1_ref, g2_ref, wg_ref, wu_ref, wd_ref, g3_ref, o_ref):
    tm = x_ref.shape[0]
    groups = [slice(r, r + tm // ROW_GROUPS) for r in range(0, tm, tm // ROW_GROUPS)]
    ys = [jnp.dot(mix_ref[rs, :], wo_ref[...], preferred_element_type=F32) for rs in groups]
    x1s = [x_ref[rs, :] + _rms(y, g1_ref[...]) for rs, y in zip(groups, ys)]
    fs = [_rms(x1, g2_ref[...]).astype(BF16) for x1 in x1s]
    accs = [None] * ROW_GROUPS
    for off in range(0, D_FF, FFN_CHUNK):
        acts = []
        for f in fs:
            g = jnp.dot(f, wg_ref[:, off:off + FFN_CHUNK], preferred_element_type=F32)
            u = jnp.dot(f, wu_ref[:, off:off + FFN_CHUNK], preferred_element_type=F32)
            acts.append((g * jax.nn.sigmoid(g) * u).astype(BF16))
        for i, act in enumerate(acts):
            part = jnp.dot(act, wd_ref[off:off + FFN_CHUNK, :], preferred_element_type=F32)
            accs[i] = part if accs[i] is None else accs[i] + part
    for rs, x1, acc in zip(groups, x1s, accs):
        o_ref[rs, :] = x1 + _rms(acc, g3_ref[...])


def _out_ffn(x, mix, w_out, g_post_mix, g_pre_ffn, w_gate, w_up, w_down, g_post_ffn):
    m = x.shape[0]
    tm = TOKEN_TILE
    row = lambda n: pl.BlockSpec((tm, n), lambda i: (i, 0))
    vec = _const_spec((1, D_MODEL))
    return pl.pallas_call(
        _out_ffn_kernel,
        grid=(m // tm,),
        in_specs=[row(D_MODEL), row(ATTN_WIDTH + M_WIDTH), _const_spec((D_MODEL, D_MODEL)), vec, vec,
                  _const_spec((D_MODEL, D_FF)), _const_spec((D_MODEL, D_FF)), _const_spec((D_FF, D_MODEL)), vec],
        out_specs=row(D_MODEL),
        out_shape=jax.ShapeDtypeStruct((m, D_MODEL), F32),
        compiler_params=_params(("parallel",)),
        name="out_ffn",
    )(x, mix, w_out, g_post_mix, g_pre_ffn, w_gate, w_up, w_down, g_post_ffn)


def _layer(xp, xs, cache_k, cache_v, state_c, state_n, state_m, w_in, b_i, b_f, attn_sink, m_norm, w_out,
           g_pre_mix, g_post_mix, g_pre_ffn, g_post_ffn, w_gate, w_up, w_down):
    bp, sp, _ = xp.shape
    bs, ts, _ = xs.shape
    assert bp == 1 and sp % TOKEN_TILE == 0 and TOKEN_TILE % WINDOW == 0
    assert ts & (ts - 1) == 0 and (bs * ts) % TOKEN_TILE == 0 and bs % SAMPLE_MLSTM_BATCH == 0

    row = lambda v: v.reshape(1, -1)
    bias_row = jnp.pad(jnp.concatenate([b_i, b_f]), (0, GATE_PAD - 2 * M_HEADS)).reshape(1, GATE_PAD)
    sink = row(attn_sink)

    x2 = xp.reshape(sp, D_MODEL)
    mix, k_w, v_w, c_aug, m_p, w_pad, wg, wu, wd, wo = _mixer_prompt(
        sink, row(b_i), row(b_f), x2, row(g_pre_mix), w_in, row(m_norm), w_gate, w_up, w_down, w_out)
    ffn = (wo, row(g_post_mix), row(g_pre_ffn), wg, wu, wd, row(g_post_ffn))
    yp = _out_ffn(x2, mix, *ffn).reshape(xp.shape)
    k_p = k_w.reshape(1, WINDOW, KV_HEADS, HEAD_DIM)
    v_p = v_w.reshape(1, WINDOW, KV_HEADS, HEAD_DIM)
    c_p = jnp.swapaxes(c_aug[:, :M_HEAD_DIM, :], 1, 2).reshape(1, M_HEADS, M_HEAD_DIM, M_HEAD_DIM)
    n_p = c_aug[:, M_HEAD_DIM, :].reshape(1, M_HEADS, M_HEAD_DIM)
    m_p = m_p[:M_HEADS, 0].reshape(1, M_HEADS)

    x2 = xs.reshape(bs * ts, D_MODEL)
    qa, ka, va, qm, km, vm, om, gt = _inproj(x2, row(g_pre_mix), w_pad)
    mix_a, k_s, v_s = _attn_sample(sink, qa, ka, va, cache_k.reshape(bs, WINDOW, KV_WIDTH),
                                   cache_v.reshape(bs, WINDOW, KV_WIDTH), ts)
    m0_rows = jnp.pad(jnp.repeat(state_m, ts, axis=0), ((0, 0), (0, LANES - M_HEADS)))
    mix_m, c_s, n_s, m_rows = _mlstm_sample(qm, km, vm, om, gt, m0_rows, bias_row, row(m_norm),
                                            state_c.reshape(bs * M_HEADS, M_HEAD_DIM, M_HEAD_DIM),
                                            state_n.reshape(bs, M_WIDTH), ts)
    ys = _out_ffn(x2, jnp.concatenate([mix_a, mix_m], axis=1), *ffn).reshape(xs.shape)
    k_s = k_s.reshape(bs, WINDOW, KV_HEADS, HEAD_DIM)
    v_s = v_s.reshape(bs, WINDOW, KV_HEADS, HEAD_DIM)
    c_s = c_s.reshape(bs, M_HEADS, M_HEAD_DIM, M_HEAD_DIM)
    n_s = n_s.reshape(bs, M_HEADS, M_HEAD_DIM)
    m_s = m_rows[ts - 1::ts, :M_HEADS]
    return yp, ys, (k_p, v_p, c_p, n_p, m_p), (k_s, v_s, c_s, n_s, m_s)


def kernel(x_prompt, x_sample, cache_k, cache_v, state_C, state_n, state_m, w_in, b_i, b_f, attn_sink, m_norm,
           w_out, g_pre_mix, g_post_mix, g_pre_ffn, g_post_ffn, w_gate, w_up, w_down):
    depth = w_in.shape[0]
    xp, xs = x_prompt, x_sample
    prompt_states, sample_states = [], []
    for l in range(depth):
        xp, xs, st_p, st_s = _layer(xp, xs, cache_k[l], cache_v[l], state_C[l], state_n[l], state_m[l],
                                    w_in[l], b_i[l], b_f[l], attn_sink[l], m_norm[l], w_out[l],
                                    g_pre_mix[l], g_post_mix[l], g_pre_ffn[l], g_post_ffn[l],
                                    w_gate[l], w_up[l], w_down[l])
        prompt_states.append(st_p)
        sample_states.append(st_s)
    stack = lambda states, i: jnp.stack([s[i] for s in states], axis=0)
    return (xp, xs) + tuple(stack(prompt_states, i) for i in range(5)) + tuple(stack(sample_states, i) for i in range(5))
```

```python
import functools

import jax
import jax.numpy as jnp
from jax import lax
from jax.experimental import pallas as pl
from jax.experimental.pallas import tpu as pltpu

F32 = jnp.float32
BF16 = jnp.bfloat16
HIGHEST = lax.Precision.HIGHEST

D_MODEL = 1024
HEAD_DIM = 64
ATTN_HEADS = 8
KV_HEADS = 2
GROUP = ATTN_HEADS // KV_HEADS
ATTN_WIDTH = ATTN_HEADS * HEAD_DIM
KV_WIDTH = KV_HEADS * HEAD_DIM
WINDOW = 128
M_HEADS = 4
M_HEAD_DIM = 128
M_WIDTH = M_HEADS * M_HEAD_DIM
M_PAIRS = M_HEADS // 2
D_FF = 2816
EPS = 1e-6

LANES = 128
SUBLANES = 8
GATE_PAD = LANES
BF16_SUBLANES = 16
AUG_ROWS = M_HEAD_DIM + BF16_SUBLANES
IN_MAIN = ATTN_WIDTH + 2 * KV_WIDTH + 4 * M_WIDTH
IN_PAD = IN_MAIN + GATE_PAD
VMEM_LIMIT = 56 * 1024 * 1024

HEAD_ORDER = tuple(h for j in range(GROUP) for h in (j, j + GROUP))

TOKEN_TILE = 512
FFN_CHUNK = 1408
ROW_GROUPS = 2
WEIGHT_SLAB = 32
SAMPLE_ATTN_BATCH = 16
SAMPLE_MLSTM_BATCH = 16


def _rms(x, g):
    return x * lax.rsqrt(jnp.mean(x * x, axis=-1, keepdims=True) + EPS) * g


def _const_spec(shape):
    nd = len(shape)
    return pl.BlockSpec(shape, lambda i: (0,) * nd, pipeline_mode=pl.Buffered(1))


def _params(semantics):
    return pltpu.CompilerParams(dimension_semantics=semantics, vmem_limit_bytes=VMEM_LIMIT)


OFF_QA, OFF_KA, OFF_VA = 0, ATTN_WIDTH, ATTN_WIDTH + KV_WIDTH
OFF_QM = ATTN_WIDTH + 2 * KV_WIDTH
OFF_KM, OFF_VM, OFF_OM = OFF_QM + M_WIDTH, OFF_QM + 2 * M_WIDTH, OFF_QM + 3 * M_WIDTH


def _inproj_kernel(x_ref, g_ref, w_ref, qa_ref, ka_ref, va_ref, qm_ref, km_ref, vm_ref, om_ref, gt_ref):
    tm = x_ref.shape[0]
    step = tm // ROW_GROUPS
    for r0 in range(0, tm, step):
        rs = slice(r0, r0 + step)
        h = _rms(x_ref[rs, :], g_ref[...]).astype(BF16)

        def proj(off, n):
            return jnp.dot(h, w_ref[:, off:off + n], preferred_element_type=F32)

        qa_ref[rs, :] = proj(OFF_QA, ATTN_WIDTH) * (HEAD_DIM ** -0.5)
        ka_ref[rs, :] = proj(OFF_KA, KV_WIDTH)
        va_ref[rs, :] = proj(OFF_VA, KV_WIDTH)
        qm_ref[rs, :] = proj(OFF_QM, M_WIDTH)
        km_ref[rs, :] = proj(OFF_KM, M_WIDTH) * (M_HEAD_DIM ** -0.5)
        vm_ref[rs, :] = proj(OFF_VM, M_WIDTH)
        om_ref[rs, :] = proj(OFF_OM, M_WIDTH)
        gt_ref[rs, :] = proj(IN_MAIN, GATE_PAD)


def _inproj(x, g_pre, w_pad):
    m = x.shape[0]
    tm = TOKEN_TILE
    row = lambda n: pl.BlockSpec((tm, n), lambda i: (i, 0))
    widths = (ATTN_WIDTH, KV_WIDTH, KV_WIDTH, M_WIDTH, M_WIDTH, M_WIDTH, M_WIDTH, GATE_PAD)
    return pl.pallas_call(
        _inproj_kernel,
        grid=(m // tm,),
        in_specs=[row(D_MODEL), _const_spec((1, D_MODEL)), _const_spec((D_MODEL, IN_PAD))],
        out_specs=[row(n) for n in widths],
        out_shape=[jax.ShapeDtypeStruct((m, n), F32) for n in widths],
        compiler_params=_params(("parallel",)),
        name="inproj",
    )(x, g_pre, w_pad)


def _stack_heads(q_tiles):
    lane = lax.broadcasted_iota(jnp.int32, q_tiles[0].shape, 1)
    lo = lane < HEAD_DIM
    zero = jnp.zeros_like(q_tiles[0])
    parts = []
    for qt in q_tiles:
        parts += [jnp.where(lo, qt, zero), jnp.where(lo, zero, qt)]
    return jnp.concatenate(parts, axis=0)


def _unstack_heads(o, rows):
    lane = lax.broadcasted_iota(jnp.int32, (rows, LANES), 1)
    lo = lane < HEAD_DIM
    return [jnp.where(lo, o[(2 * j) * rows:(2 * j + 1) * rows, :], o[(2 * j + 1) * rows:(2 * j + 2) * rows, :])
            for j in range(GROUP)]


def _sink_rows(sink_ref, rows):
    return jnp.concatenate([jnp.full((rows, LANES), sink_ref[0, h], F32) for h in HEAD_ORDER], axis=0)


def _attn_sample_kernel(sink_ref, q_ref, kn_ref, vn_ref, ck_ref, cv_ref, o_ref, ko_ref, vo_ref, *, dec_seq):
    t = dec_seq
    bb = SAMPLE_ATTN_BATCH
    nrow = ATTN_HEADS * t
    ts, ns = t.bit_length() - 1, nrow.bit_length() - 1
    r_c = lax.broadcasted_iota(jnp.int32, (nrow, WINDOW), 0) & (t - 1)
    c_c = lax.broadcasted_iota(jnp.int32, (nrow, WINDOW), 1)
    vis_cache = c_c > r_c
    r_n = lax.broadcasted_iota(jnp.int32, (bb * nrow, bb * t), 0)
    c_n = lax.broadcasted_iota(jnp.int32, (bb * nrow, bb * t), 1)
    vis_new = jnp.logical_and((r_n >> ns) == (c_n >> ts), (c_n & (t - 1)) <= (r_n & (t - 1)))
    sink = _sink_rows(sink_ref, t)
    nt = (((1,), (1,)), ((), ()))
    kn_all, vn_all = kn_ref[...], vn_ref[...]
    qs, s_c = [], []
    for b in range(bb):
        rows = slice(b * t, (b + 1) * t)
        ck, cv = ck_ref[b], cv_ref[b]
        ko_ref[b] = jnp.concatenate([ck[t:, :], kn_all[rows, :]], axis=0)
        vo_ref[b] = jnp.concatenate([cv[t:, :], vn_all[rows, :]], axis=0)
        qs.append(_stack_heads([q_ref[rows, j * LANES:(j + 1) * LANES] for j in range(GROUP)]).astype(BF16))
        s_c.append(jnp.where(vis_cache, lax.dot_general(qs[b], ck.astype(BF16), nt, preferred_element_type=F32),
                             -jnp.inf))
    s_n = jnp.where(vis_new, lax.dot_general(jnp.concatenate(qs, axis=0), kn_all.astype(BF16), nt,
                                             preferred_element_type=F32), -jnp.inf)
    p_c, p_n, rden = [], [], []
    for b in range(bb):
        s_nb = s_n[b * nrow:(b + 1) * nrow, :]
        m = jnp.maximum(jnp.maximum(jnp.max(s_c[b], axis=-1, keepdims=True), jnp.max(s_nb, axis=-1, keepdims=True)),
                        sink)
        p_c.append(jnp.exp(s_c[b] - m))
        p_n.append(jnp.exp(s_nb - m[:, :bb * t]))
        rden.append(1.0 / (jnp.sum(p_c[b], axis=-1, keepdims=True) + jnp.sum(p_n[b], axis=-1, keepdims=True)
                           + jnp.exp(sink - m)))
    o_n = jnp.dot(jnp.concatenate(p_n, axis=0).astype(BF16), vn_all.astype(BF16), preferred_element_type=F32)
    outs = [[] for _ in range(GROUP)]
    for b in range(bb):
        o = (jnp.dot(p_c[b].astype(BF16), cv_ref[b].astype(BF16), preferred_element_type=F32)
             + o_n[b * nrow:(b + 1) * nrow, :]) * rden[b]
        for j, tile in enumerate(_unstack_heads(o, t)):
            outs[j].append(tile)
    for j, parts in enumerate(outs):
        o_ref[:, j * LANES:(j + 1) * LANES] = jnp.concatenate(parts, axis=0).astype(o_ref.dtype)


def _attn_sample(sink, qa, ka, va, cache_k, cache_v, dec_seq):
    nb = cache_k.shape[0]
    bb = SAMPLE_ATTN_BATCH
    row = lambda n: pl.BlockSpec((bb * dec_seq, n), lambda i: (i, 0))
    cache = pl.BlockSpec((bb, WINDOW, KV_WIDTH), lambda i: (i, 0, 0))
    return pl.pallas_call(
        functools.partial(_attn_sample_kernel, dec_seq=dec_seq),
        grid=(nb // bb,),
        in_specs=[pl.BlockSpec(memory_space=pltpu.SMEM), row(ATTN_WIDTH), row(KV_WIDTH), row(KV_WIDTH), cache, cache],
        out_specs=[row(ATTN_WIDTH), cache, cache],
        out_shape=[jax.ShapeDtypeStruct((nb * dec_seq, ATTN_WIDTH), BF16),
                   jax.ShapeDtypeStruct(cache_k.shape, F32), jax.ShapeDtypeStruct(cache_v.shape, F32)],
        compiler_params=_params(("parallel",)),
        name="attn_sample",
    )(sink, qa, ka, va, cache_k, cache_v)


def _block_diag(a, b):
    za, zb = jnp.zeros_like(a), jnp.zeros_like(b)
    return jnp.concatenate([jnp.concatenate([a, zb], axis=1), jnp.concatenate([za, b], axis=1)], axis=0)


def _head_out(hh, mnorm_row, om):
    y = hh * lax.rsqrt(jnp.mean(hh * hh, axis=-1, keepdims=True) + EPS) * mnorm_row
    return (jax.nn.sigmoid(om) * y).astype(BF16)


def _mixer_prompt_kernel(sink_ref, bi_ref, bf_ref, x_ref, g_ref, win_ref, mnorm_ref, wg_ref, wu_ref, wd_ref, wo_ref,
                         mix_ref, kw_ref, vw_ref, c_ref, m_ref, w_ref, wgb_ref, wub_ref, wdb_ref, wob_ref):
    step = pl.program_id(0)

    @pl.when(step == 0)
    def _():
        kw_ref[...] = jnp.zeros_like(kw_ref)
        vw_ref[...] = jnp.zeros_like(vw_ref)
        c_ref[...] = jnp.zeros_like(c_ref)
        m_ref[...] = jnp.zeros_like(m_ref)
        for j in range(GROUP):
            lo = win_ref[:, j * HEAD_DIM:(j + 1) * HEAD_DIM]
            hi = win_ref[:, (j + GROUP) * HEAD_DIM:(j + GROUP + 1) * HEAD_DIM]
            w_ref[:, j * LANES:(j + 1) * LANES] = jnp.concatenate([lo, hi], axis=1).astype(BF16)
        w_ref[:, ATTN_WIDTH:IN_MAIN] = win_ref[:, ATTN_WIDTH:IN_MAIN].astype(BF16)
        gw = jnp.concatenate([win_ref[:, IN_MAIN:IN_MAIN + 2 * M_HEADS],
                              jnp.zeros((D_MODEL, GATE_PAD - 2 * M_HEADS), F32)], axis=1)
        w_ref[:, IN_MAIN:] = gw.astype(BF16)

    wgb_ref[...] = wg_ref[...].astype(BF16)
    wub_ref[...] = wu_ref[...].astype(BF16)
    wdb_ref[...] = wd_ref[...].astype(BF16)
    wob_ref[...] = wo_ref[...].astype(BF16)

    tm = x_ref.shape[0]
    ln = WINDOW
    groups = [slice(r0, r0 + tm // ROW_GROUPS) for r0 in range(0, tm, tm // ROW_GROUPS)]
    blocks = [slice(r0, r0 + ln) for r0 in range(0, tm, ln)]
    hcols = [slice(h * M_HEAD_DIM, (h + 1) * M_HEAD_DIM) for h in range(M_HEADS)]
    pcols = [slice(2 * p * M_HEAD_DIM, 2 * (p + 1) * M_HEAD_DIM) for p in range(M_PAIRS)]
    nt = (((1,), (1,)), ((), ()))
    hs = [_rms(x_ref[rs, :], g_ref[...]).astype(BF16) for rs in groups]

    def proj(off, n):
        return jnp.concatenate([jnp.dot(h, w_ref[:, off:off + n], preferred_element_type=F32) for h in hs], axis=0)

    qt = proj(OFF_QM, M_WIDTH).T.astype(BF16)
    km = (proj(OFF_KM, M_WIDTH) * (M_HEAD_DIM ** -0.5)).astype(BF16)
    vt = proj(OFF_VM, M_WIDTH).T.astype(BF16)
    gates = proj(IN_MAIN, GATE_PAD).T
    r = lax.broadcasted_iota(jnp.int32, (ln, ln), 0)
    c = lax.broadcasted_iota(jnp.int32, (ln, ln), 1)
    causal_t = r <= c
    upper = causal_t.astype(F32)
    lane8 = lax.broadcasted_iota(jnp.int32, (SUBLANES, ln), 1)
    ones_rows = jnp.ones((AUG_ROWS - M_HEAD_DIM, ln), BF16)
    zrows = jnp.zeros((ln - SUBLANES, ln), F32)
    row8 = lax.broadcasted_iota(jnp.int32, (SUBLANES, ln), 0)
    bias = jnp.zeros((SUBLANES, ln), F32)
    for h in range(M_HEADS):
        bias = jnp.where(row8 == h, bi_ref[0, h], jnp.where(row8 == M_HEADS + h, bf_ref[0, h], bias))
    pre, scores_m = [], []
    for rows in blocks:
        gi = gates[0:SUBLANES, rows] + bias
        b = pltpu.roll(jnp.dot(jax.nn.log_sigmoid(gi), upper, precision=HIGHEST, preferred_element_type=F32),
                       M_HEADS, axis=0)
        g = gi - b
        cm0 = g
        sh = 1
        while sh < ln:
            cm0 = jnp.maximum(cm0, jnp.where(lane8 >= sh, pltpu.roll(cm0, sh, axis=1), -jnp.inf))
            sh *= 2
        b_last = jnp.broadcast_to(b[:, ln - 1:ln], b.shape)
        g_max = jnp.broadcast_to(cm0[:, ln - 1:ln], b.shape)
        g_cols = jnp.concatenate([g, zrows], axis=0).T
        pre.append((b, g_cols, cm0, b_last, g_max, (b_last - b) + gi))
        scores_m.append([jnp.dot(km[rows, pc], _block_diag(qt[hcols[2 * p], rows], qt[hcols[2 * p + 1], rows]),
                                 preferred_element_type=F32) for p, pc in enumerate(pcols)])

    qa = (proj(OFF_QA, ATTN_WIDTH) * (HEAD_DIM ** -0.5)).astype(BF16)
    kv = proj(OFF_KA, 2 * KV_WIDTH)
    ka, va = kv[:, :KV_WIDTH], kv[:, KV_WIDTH:]
    nrow = ATTN_HEADS * ln
    ra = lax.broadcasted_iota(jnp.int32, (nrow, ln), 0) & (ln - 1)
    ca = lax.broadcasted_iota(jnp.int32, (nrow, ln), 1)
    own = ca <= ra
    sink = _sink_rows(sink_ref, ln)
    no_prev = jnp.where(step == 0, -jnp.inf, 0.0)
    ones = jnp.ones((2 * ln, LANES), BF16)
    scores_a, v_augs = [], []
    for blk, rows in enumerate(blocks):
        if blk == 0:
            k_prev, v_prev = kw_ref[...], vw_ref[...]
        else:
            k_prev, v_prev = ka[blocks[blk - 1], :], va[blocks[blk - 1], :]
        kcat = jnp.concatenate([ka[rows, :], k_prev], axis=0).astype(BF16)
        vcat = jnp.concatenate([va[rows, :], v_prev], axis=0).astype(BF16)
        v_augs.append(jnp.concatenate([vcat, ones], axis=1))
        qs = _stack_heads([qa[rows, j * LANES:(j + 1) * LANES] for j in range(GROUP)])
        scores_a.append(lax.dot_general(qs, kcat, nt, preferred_element_type=F32))
    kw_ref[...] = ka[blocks[-1], :]
    vw_ref[...] = va[blocks[-1], :]

    m_prev = m_ref[...]
    scal = []
    for b, _, cm0, b_last, g_max, w_end_arg in pre:
        cm = jnp.maximum(cm0, m_prev)
        m_end = b_last + jnp.maximum(g_max, m_prev)
        scal.append((cm, jnp.exp(m_prev - cm), jnp.exp(-(b + cm)), jnp.exp(b_last + m_prev - m_end),
                     jnp.exp(w_end_arg - m_end)))
        m_prev = m_end
    m_ref[...] = m_prev
    gated = []
    for rows, s_t, (_, g_cols, *_), (cm, _, _, _, w_end) in zip(blocks, scores_m, pre, scal):
        vts = [jnp.concatenate([vt[hc, rows], ones_rows], axis=0) for hc in hcols]
        sqks = [(jnp.exp(jnp.where(causal_t, g_cols[:, h:h + 1] - cm[h:h + 1, :], -jnp.inf))
                 * s_t[h // 2][:, (h % 2) * ln:(h % 2 + 1) * ln]).astype(BF16) for h in range(M_HEADS)]
        kvws = [(vts[h].astype(F32) * w_end[h:h + 1, :]).astype(BF16) for h in range(M_HEADS)]
        gated.append((vts, sqks, kvws))
    probs, maxes = [], []
    for blk, s in enumerate(scores_a):
        s_prev = s[:, ln:]
        if blk == 0:
            s_prev = s_prev + no_prev
        sc = jnp.where(own, s[:, :ln], s_prev)
        mx = jnp.maximum(jnp.max(sc, axis=-1, keepdims=True), sink)
        p = jnp.exp(sc - mx)
        zero = jnp.zeros_like(p)
        probs.append(jnp.concatenate([jnp.where(own, p, zero), jnp.where(own, zero, p)], axis=1).astype(BF16))
        maxes.append(mx)
    om = proj(OFF_OM, M_WIDTH)

    mem = [jnp.concatenate([c_ref[2 * p], c_ref[2 * p + 1]], axis=1) for p in range(M_PAIRS)]
    pair_row = lambda x, p: jnp.concatenate([x[2 * p:2 * p + 1, :], x[2 * p + 1:2 * p + 2, :]], axis=1)
    for rows, (vts, sqks, kvws), (_, w_inter, e_negm, dec, _), p2, v_aug, mx in zip(
            blocks, gated, scal, probs, v_augs, maxes):
        pairs = range(M_PAIRS)
        upd = [jnp.dot(jnp.concatenate([kvws[2 * p], kvws[2 * p + 1]], axis=1),
                       _block_diag(km[rows, hcols[2 * p]], km[rows, hcols[2 * p + 1]]),
                       preferred_element_type=F32) for p in pairs]
        intra = [jnp.dot(jnp.concatenate([vts[2 * p], vts[2 * p + 1]], axis=1),
                         _block_diag(sqks[2 * p], sqks[2 * p + 1]), preferred_element_type=F32) for p in pairs]
        inter = [jnp.dot(mem[p].astype(BF16), _block_diag(qt[hcols[2 * p], rows], qt[hcols[2 * p + 1], rows]),
                         preferred_element_type=F32) for p in pairs]
        o = jnp.dot(p2, v_aug, preferred_element_type=F32)
        o = o[:, :LANES] * (1.0 / (o[:, LANES:] + jnp.exp(sink - mx)))
        for j, tile in enumerate(_unstack_heads(o, ln)):
            mix_ref[rows, j * LANES:(j + 1) * LANES] = tile.astype(mix_ref.dtype)
        for p in pairs:
            num = inter[p] * pair_row(w_inter, p) + intra[p]
            mem[p] = pair_row(dec, p) * mem[p] + upd[p]
            den = jnp.maximum(jnp.abs(num[M_HEAD_DIM:M_HEAD_DIM + 1, :]), pair_row(e_negm, p))
            hh = num[:M_HEAD_DIM, :] * (1.0 / den)
            y = hh * lax.rsqrt(jnp.mean(hh * hh, axis=0, keepdims=True) + EPS)
            for h in (2 * p, 2 * p + 1):
                y_h = y[:, (h % 2) * ln:(h % 2 + 1) * ln]
                mcols = slice(ATTN_WIDTH + h * M_HEAD_DIM, ATTN_WIDTH + (h + 1) * M_HEAD_DIM)
                mix_ref[rows, mcols] = (jax.nn.sigmoid(om[rows, hcols[h]])
                                        * (y_h.T * mnorm_ref[:, hcols[h]])).astype(mix_ref.dtype)
    for p in range(M_PAIRS):
        c_ref[2 * p] = mem[p][:, :M_HEAD_DIM]
        c_ref[2 * p + 1] = mem[p][:, M_HEAD_DIM:]


def _out_row_block(i):
    per_head = (ATTN_WIDTH // ATTN_HEADS) // WEIGHT_SLAB
    j, part = i // per_head, i % per_head
    head = (j % 2) * GROUP + j // 2
    return jnp.where(i < ATTN_HEADS * per_head, head * per_head + part, i)


def _mixer_prompt(sink, b_i, b_f, x, g_pre, w_in, mnorm, w_gate, w_up, w_down, w_out):
    m = x.shape[0]
    tm = TOKEN_TILE
    steps = m // tm
    assert D_MODEL == steps * WEIGHT_SLAB and D_FF % (steps // 2) == 0 and (D_FF // (steps // 2)) % BF16_SUBLANES == 0
    down_slab = D_FF // (steps // 2)
    row = lambda n: pl.BlockSpec((tm, n), lambda i: (i, 0))
    whole = lambda shape: pl.BlockSpec(shape, lambda i: (0,) * len(shape))
    smem = pl.BlockSpec(memory_space=pltpu.SMEM)
    slab = lambda n: pl.BlockSpec((WEIGHT_SLAB, n), lambda i: (i, 0))
    down = pl.BlockSpec((down_slab, D_MODEL), lambda i: (i // 2, 0))
    c_shape = (M_HEADS, AUG_ROWS, M_HEAD_DIM)
    w_shape = (WINDOW, KV_WIDTH)
    s_shape = (SUBLANES, LANES)
    sds = jax.ShapeDtypeStruct
    return pl.pallas_call(
        _mixer_prompt_kernel,
        grid=(steps,),
        in_specs=[smem, smem, smem, row(D_MODEL), _const_spec((1, D_MODEL)), _const_spec(w_in.shape),
                  _const_spec((1, M_WIDTH)), slab(D_FF), slab(D_FF), down,
                  pl.BlockSpec((WEIGHT_SLAB, D_MODEL), lambda i: (_out_row_block(i), 0))],
        out_specs=[row(ATTN_WIDTH + M_WIDTH), whole(w_shape), whole(w_shape), whole(c_shape), whole(s_shape),
                   whole((D_MODEL, IN_PAD)), slab(D_FF), slab(D_FF), down, slab(D_MODEL)],
        out_shape=[sds((m, ATTN_WIDTH + M_WIDTH), BF16), sds(w_shape, F32), sds(w_shape, F32), sds(c_shape, F32),
                   sds(s_shape, F32), sds((D_MODEL, IN_PAD), BF16), sds(w_gate.shape, BF16), sds(w_up.shape, BF16),
                   sds(w_down.shape, BF16), sds(w_out.shape, BF16)],
        compiler_params=_params(("arbitrary",)),
        name="mixer_prompt",
    )(sink, b_i, b_f, x, g_pre, w_in, mnorm, w_gate, w_up, w_down, w_out)


def _gates(g_blk, bias_row, cum):
    pre = g_blk + bias_row
    lane = lax.broadcasted_iota(jnp.int32, pre.shape, 1)
    a = jnp.where(lane < M_HEADS, pre, jax.nn.log_sigmoid(pre))
    b = jnp.dot(cum, a, precision=HIGHEST, preferred_element_type=F32)
    return a, b, a.T, b.T


def _col(x, j):
    return jnp.broadcast_to(x[:, j:j + 1], x.shape)


def _mlstm_gated_scores(q, k, a, b, at, bt, h, mask, m_prev, last):
    bc, ic = _col(b, M_HEADS + h), _col(a, h)
    br, ir = bt[M_HEADS + h:M_HEADS + h + 1, :], at[h:h + 1, :]
    d = jnp.where(mask, (bc - br) + ir, -jnp.inf)
    inter = bc + m_prev
    m_t = jnp.maximum(inter, jnp.max(d, axis=-1, keepdims=True))
    w_inter = jnp.exp(inter - m_t)
    sqk = jnp.exp(d - m_t) * lax.dot_general(q, k, (((1,), (1,)), ((), ())), preferred_element_type=F32)
    m_end = last(m_t)
    bl = last(bc)
    dec = jnp.exp(bl + m_prev - m_end)
    w_end = jnp.exp((bl - bc) + ic - m_end)
    kw = k.astype(F32) * w_end
    return w_inter, sqk, m_t, m_end, dec, kw


def _mlstm_sample_kernel(q_ref, k_ref, v_ref, om_ref, g_ref, m0_ref, bias_ref, mnorm_ref, c_ref, n_ref,
                         o_ref, c_out, n_out, m_out, *, dec_seq):
    t = dec_seq
    bb = SAMPLE_MLSTM_BATCH
    ln = bb * t
    r = lax.broadcasted_iota(jnp.int32, (ln, ln), 0)
    c = lax.broadcasted_iota(jnp.int32, (ln, ln), 1)
    shift = t.bit_length() - 1
    same = (r >> shift) == (c >> shift)
    mask = jnp.logical_and(same, c <= r)
    cum = mask.astype(F32)
    expand = (c == (r >> shift)).astype(F32)
    gather = ((c >> shift) == r).astype(F32)
    is_last = (r & (t - 1)) == t - 1
    is_first = (r & (t - 1)) == 0

    def last(x):
        y = jnp.where(is_last, x, 0.0)
        step = 1
        while step < t:
            y = y + pltpu.roll(y, ln - step, axis=0)
            step *= 2
        return y

    a, b, at, bt = _gates(g_ref[...], bias_ref[...], cum)
    m0 = m0_ref[...]
    lane = lax.broadcasted_iota(jnp.int32, (ln, LANES), 1)
    zpad = jnp.zeros((ln - bb, M_HEAD_DIM), F32)
    heads = range(M_HEADS)
    hcols = [slice(h * M_HEAD_DIM, (h + 1) * M_HEAD_DIM) for h in heads]
    qfs = [q_ref[:, hc] for hc in hcols]
    vs = [v_ref[:, hc].astype(BF16) for hc in hcols]
    qcs = [jnp.concatenate(
        [jnp.dot(qfs[h][s * t:(s + 1) * t, :], c_ref[s * M_HEADS + h].astype(BF16).astype(F32),
                 preferred_element_type=F32) for s in range(bb)], axis=0) for h in heads]
    n_exps = [jnp.dot(expand, jnp.concatenate([n_ref[:, hc], zpad], axis=0), precision=HIGHEST,
                      preferred_element_type=F32) for hc in hcols]
    parts = [_mlstm_gated_scores(qfs[h].astype(BF16), k_ref[:, hcols[h]].astype(BF16), a, b, at, bt, h, mask,
                                 _col(m0, h), last) for h in heads]
    intras = [jnp.dot(parts[h][1].astype(BF16), vs[h], preferred_element_type=F32) for h in heads]
    m_cols = jnp.zeros((ln, LANES), F32)
    kwts = []
    for h in heads:
        w_inter, sqk, m_t, m_end, dec, kw = parts[h]
        num = w_inter * qcs[h] + intras[h]
        nq = w_inter * jnp.sum(qfs[h] * n_exps[h], axis=-1, keepdims=True) + jnp.sum(sqk, axis=-1, keepdims=True)
        hh = num / jnp.maximum(jnp.abs(nq), jnp.exp(-m_t))
        o_ref[:, hcols[h]] = _head_out(hh, mnorm_ref[:, hcols[h]], om_ref[:, hcols[h]])
        kwts.append(kw.T.astype(BF16))
        m_cols = jnp.where(lane == h, m_end, m_cols)
    m_out[...] = m_cols
    for h in heads:
        dec = parts[h][4]
        for s in range(bb):
            lhs = jnp.where((c >> shift) == s, kwts[h], jnp.zeros_like(kwts[h]))
            upd = jnp.dot(lhs, vs[h], preferred_element_type=F32)
            c_out[s * M_HEADS + h] = dec[s * t:s * t + 1, :] * c_ref[s * M_HEADS + h] + upd
    for h in heads:
        dec, kw = parts[h][4], parts[h][5]
        n_new = jnp.dot(gather, jnp.where(is_first, dec * n_exps[h], 0.0) + kw, precision=HIGHEST,
                        preferred_element_type=F32)
        n_out[:, hcols[h]] = n_new[:bb, :]


def _mlstm_sample(qm, km, vm, om, gt, m0_rows, bias_row, mnorm, c_in, n_in, dec_seq):
    m = qm.shape[0]
    bb = SAMPLE_MLSTM_BATCH
    tm = bb * dec_seq
    nb = m // dec_seq
    row = lambda n: pl.BlockSpec((tm, n), lambda i: (i, 0))
    whole = lambda shape: pl.BlockSpec(shape, lambda i: (0,) * len(shape))
    c_spec = pl.BlockSpec((bb * M_HEADS, M_HEAD_DIM, M_HEAD_DIM), lambda i: (i, 0, 0))
    n_spec = pl.BlockSpec((bb, M_WIDTH), lambda i: (i, 0))
    return pl.pallas_call(
        functools.partial(_mlstm_sample_kernel, dec_seq=dec_seq),
        grid=(nb // bb,),
        in_specs=[row(M_WIDTH), row(M_WIDTH), row(M_WIDTH), row(M_WIDTH), row(GATE_PAD), row(LANES),
                  whole((1, GATE_PAD)), whole((1, M_WIDTH)), c_spec, n_spec],
        out_specs=[row(M_WIDTH), c_spec, n_spec, row(LANES)],
        out_shape=[jax.ShapeDtypeStruct((m, M_WIDTH), BF16), jax.ShapeDtypeStruct(c_in.shape, F32),
                   jax.ShapeDtypeStruct(n_in.shape, F32), jax.ShapeDtypeStruct((m, LANES), F32)],
        compiler_params=_params(("parallel",)),
        name="mlstm_sample",
    )(qm, km, vm, om, gt, m0_rows, bias_row, mnorm, c_in, n_in)


def _out_ffn_kernel(x_ref, mix_ref, wo_ref, g1_ref, g2_ref, wg_ref, wu_ref, wd_ref, g3_ref, o_ref):
    tm = x_ref.shape[0]
    groups = [slice(r, r + tm // ROW_GROUPS) for r in range(0, tm, tm // ROW_GROUPS)]
    ys = [jnp.dot(mix_ref[rs, :], wo_ref[...], preferred_element_type=F32) for rs in groups]
    x1s = [x_ref[rs, :] + _rms(y, g1_ref[...]) for rs, y in zip(groups, ys)]
    fs = [_rms(x1, g2_ref[...]).astype(BF16) for x1 in x1s]
    accs = [None] * ROW_GROUPS
    for off in range(0, D_FF, FFN_CHUNK):
        acts = []
        for f in fs:
            g = jnp.dot(f, wg_ref[:, off:off + FFN_CHUNK], preferred_element_type=F32)
            u = jnp.dot(f, wu_ref[:, off:off + FFN_CHUNK], preferred_element_type=F32)
            acts.append((g * jax.nn.sigmoid(g) * u).astype(BF16))
        for i, act in enumerate(acts):
            part = jnp.dot(act, wd_ref[off:off + FFN_CHUNK, :], preferred_element_type=F32)
            accs[i] = part if accs[i] is None else accs[i] + part
    for rs, x1, acc in zip(groups, x1s, accs):
        o_ref[rs, :] = x1 + _rms(acc, g3_ref[...])


def _out_ffn(x, mix, w_out, g_post_mix, g_pre_ffn, w_gate, w_up, w_down, g_post_ffn):
    m = x.shape[0]
    tm = TOKEN_TILE
    row = lambda n: pl.BlockSpec((tm, n), lambda i: (i, 0))
    vec = _const_spec((1, D_MODEL))
    return pl.pallas_call(
        _out_ffn_kernel,
        grid=(m // tm,),
        in_specs=[row(D_MODEL), row(ATTN_WIDTH + M_WIDTH), _const_spec((D_MODEL, D_MODEL)), vec, vec,
                  _const_spec((D_MODEL, D_FF)), _const_spec((D_MODEL, D_FF)), _const_spec((D_FF, D_MODEL)), vec],
        out_specs=row(D_MODEL),
        out_shape=jax.ShapeDtypeStruct((m, D_MODEL), F32),
        compiler_params=_params(("parallel",)),
        name="out_ffn",
    )(x, mix, w_out, g_post_mix, g_pre_ffn, w_gate, w_up, w_down, g_post_ffn)


def _layer(xp, xs, cache_k, cache_v, state_c, state_n, state_m, w_in, b_i, b_f, attn_sink, m_norm, w_out,
           g_pre_mix, g_post_mix, g_pre_ffn, g_post_ffn, w_gate, w_up, w_down):
    bp, sp, _ = xp.shape
    bs, ts, _ = xs.shape
    assert bp == 1 and sp % TOKEN_TILE == 0 and TOKEN_TILE % WINDOW == 0
    assert ts & (ts - 1) == 0 and (bs * ts) % TOKEN_TILE == 0 and bs % SAMPLE_MLSTM_BATCH == 0

    row = lambda v: v.reshape(1, -1)
    bias_row = jnp.pad(jnp.concatenate([b_i, b_f]), (0, GATE_PAD - 2 * M_HEADS)).reshape(1, GATE_PAD)
    sink = row(attn_sink)

    x2 = xp.reshape(sp, D_MODEL)
    mix, k_w, v_w, c_aug, m_p, w_pad, wg, wu, wd, wo = _mixer_prompt(
        sink, row(b_i), row(b_f), x2, row(g_pre_mix), w_in, row(m_norm), w_gate, w_up, w_down, w_out)
    ffn = (wo, row(g_post_mix), row(g_pre_ffn), wg, wu, wd, row(g_post_ffn))
    yp = _out_ffn(x2, mix, *ffn).reshape(xp.shape)
    k_p = k_w.reshape(1, WINDOW, KV_HEADS, HEAD_DIM)
    v_p = v_w.reshape(1, WINDOW, KV_HEADS, HEAD_DIM)
    c_p = jnp.swapaxes(c_aug[:, :M_HEAD_DIM, :], 1, 2).reshape(1, M_HEADS, M_HEAD_DIM, M_HEAD_DIM)
    n_p = c_aug[:, M_HEAD_DIM, :].reshape(1, M_HEADS, M_HEAD_DIM)
    m_p = m_p[:M_HEADS, 0].reshape(1, M_HEADS)

    x2 = xs.reshape(bs * ts, D_MODEL)
    qa, ka, va, qm, km, vm, om, gt = _inproj(x2, row(g_pre_mix), w_pad)
    mix_a, k_s, v_s = _attn_sample(sink, qa, ka, va, cache_k.reshape(bs, WINDOW, KV_WIDTH),
                                   cache_v.reshape(bs, WINDOW, KV_WIDTH), ts)
    m0_rows = jnp.pad(jnp.repeat(state_m, ts, axis=0), ((0, 0), (0, LANES - M_HEADS)))
    mix_m, c_s, n_s, m_rows = _mlstm_sample(qm, km, vm, om, gt, m0_rows, bias_row, row(m_norm),
                                            state_c.reshape(bs * M_HEADS, M_HEAD_DIM, M_HEAD_DIM),
                                            state_n.reshape(bs, M_WIDTH), ts)
    ys = _out_ffn(x2, jnp.concatenate([mix_a, mix_m], axis=1), *ffn).reshape(xs.shape)
    k_s = k_s.reshape(bs, WINDOW, KV_HEADS, HEAD_DIM)
    v_s = v_s.reshape(bs, WINDOW, KV_HEADS, HEAD_DIM)
    c_s = c_s.reshape(bs, M_HEADS, M_HEAD_DIM, M_HEAD_DIM)
    n_s = n_s.reshape(bs, M_HEADS, M_HEAD_DIM)
    m_s = m_rows[ts - 1::ts, :M_HEADS]
    return yp, ys, (k_p, v_p, c_p, n_p, m_p), (k_s, v_s, c_s, n_s, m_s)


def kernel(x_prompt, x_sample, cache_k, cache_v, state_C, state_n, state_m, w_in, b_i, b_f, attn_sink, m_norm,
           w_out, g_pre_mix, g_post_mix, g_pre_ffn, g_post_ffn, w_gate, w_up, w_down):
    depth = w_in.shape[0]
    xp, xs = x_prompt, x_sample
    prompt_states, sample_states = [], []
    for l in range(depth):
        xp, xs, st_p, st_s = _layer(xp, xs, cache_k[l], cache_v[l], state_C[l], state_n[l], state_m[l],
                                    w_in[l], b_i[l], b_f[l], attn_sink[l], m_norm[l], w_out[l],
                                    g_pre_mix[l], g_post_mix[l], g_pre_ffn[l], g_post_ffn[l],
                                    w_gate[l], w_up[l], w_down[l])
        prompt_states.append(st_p)
        sample_states.append(st_s)
    stack = lambda states, i: jnp.stack([s[i] for s in states], axis=0)
    return (xp, xs) + tuple(stack(prompt_states, i) for i in range(5)) + tuple(stack(sample_states, i) for i in range(5))
```

```python
import functools

import jax
import jax.numpy as jnp
from jax import lax
from jax.experimental import pallas as pl
from jax.experimental.pallas import tpu as pltpu

F32 = jnp.float32
BF16 = jnp.bfloat16
HIGHEST = lax.Precision.HIGHEST

D_MODEL = 1024
HEAD_DIM = 64
ATTN_HEADS = 8
KV_HEADS = 2
GROUP = ATTN_HEADS // KV_HEADS
ATTN_WIDTH = ATTN_HEADS * HEAD_DIM
KV_WIDTH = KV_HEADS * HEAD_DIM
WINDOW = 128
M_HEADS = 4
M_HEAD_DIM = 128
M_WIDTH = M_HEADS * M_HEAD_DIM
M_PAIRS = M_HEADS // 2
D_FF = 2816
EPS = 1e-6

LANES = 128
SUBLANES = 8
GATE_PAD = LANES
BF16_SUBLANES = 16
AUG_ROWS = M_HEAD_DIM + BF16_SUBLANES
IN_MAIN = ATTN_WIDTH + 2 * KV_WIDTH + 4 * M_WIDTH
IN_PAD = IN_MAIN + GATE_PAD
VMEM_LIMIT = 56 * 1024 * 1024

HEAD_ORDER = tuple(h for j in range(GROUP) for h in (j, j + GROUP))

TOKEN_TILE = 512
FFN_CHUNK = 1408
ROW_GROUPS = 2
WEIGHT_SLAB = 32
SAMPLE_ATTN_BATCH = 16
SAMPLE_MLSTM_BATCH = 16


def _rms(x, g):
    return x * lax.rsqrt(jnp.mean(x * x, axis=-1, keepdims=True) + EPS) * g


def _const_spec(shape):
    nd = len(shape)
    return pl.BlockSpec(shape, lambda i: (0,) * nd, pipeline_mode=pl.Buffered(1))


def _params(semantics):
    return pltpu.CompilerParams(dimension_semantics=semantics, vmem_limit_bytes=VMEM_LIMIT)


OFF_QA, OFF_KA, OFF_VA = 0, ATTN_WIDTH, ATTN_WIDTH + KV_WIDTH
OFF_QM = ATTN_WIDTH + 2 * KV_WIDTH
OFF_KM, OFF_VM, OFF_OM = OFF_QM + M_WIDTH, OFF_QM + 2 * M_WIDTH, OFF_QM + 3 * M_WIDTH


def _inproj_kernel(x_ref, g_ref, w_ref, qa_ref, ka_ref, va_ref, qm_ref, km_ref, vm_ref, om_ref, gt_ref):
    tm = x_ref.shape[0]
    step = tm // ROW_GROUPS
    for r0 in range(0, tm, step):
        rs = slice(r0, r0 + step)
        h = _rms(x_ref[rs, :], g_ref[...]).astype(BF16)

        def proj(off, n):
            return jnp.dot(h, w_ref[:, off:off + n], preferred_element_type=F32)

        qa_ref[rs, :] = proj(OFF_QA, ATTN_WIDTH) * (HEAD_DIM ** -0.5)
        ka_ref[rs, :] = proj(OFF_KA, KV_WIDTH)
        va_ref[rs, :] = proj(OFF_VA, KV_WIDTH)
        qm_ref[rs, :] = proj(OFF_QM, M_WIDTH)
        km_ref[rs, :] = proj(OFF_KM, M_WIDTH) * (M_HEAD_DIM ** -0.5)
        vm_ref[rs, :] = proj(OFF_VM, M_WIDTH)
        om_ref[rs, :] = proj(OFF_OM, M_WIDTH)
        gt_ref[rs, :] = proj(IN_MAIN, GATE_PAD)


def _inproj(x, g_pre, w_pad):
    m = x.shape[0]
    tm = TOKEN_TILE
    row = lambda n: pl.BlockSpec((tm, n), lambda i: (i, 0))
    widths = (ATTN_WIDTH, KV_WIDTH, KV_WIDTH, M_WIDTH, M_WIDTH, M_WIDTH, M_WIDTH, GATE_PAD)
    return pl.pallas_call(
        _inproj_kernel,
        grid=(m // tm,),
        in_specs=[row(D_MODEL), _const_spec((1, D_MODEL)), _const_spec((D_MODEL, IN_PAD))],
        out_specs=[row(n) for n in widths],
        out_shape=[jax.ShapeDtypeStruct((m, n), F32) for n in widths],
        compiler_params=_params(("parallel",)),
        name="inproj",
    )(x, g_pre, w_pad)


def _stack_heads(q_tiles):
    lane = lax.broadcasted_iota(jnp.int32, q_tiles[0].shape, 1)
    lo = lane < HEAD_DIM
    zero = jnp.zeros_like(q_tiles[0])
    parts = []
    for qt in q_tiles:
        parts += [jnp.where(lo, qt, zero), jnp.where(lo, zero, qt)]
    return jnp.concatenate(parts, axis=0)


def _unstack_heads(o, rows):
    lane = lax.broadcasted_iota(jnp.int32, (rows, LANES), 1)
    lo = lane < HEAD_DIM
    return [jnp.where(lo, o[(2 * j) * rows:(2 * j + 1) * rows, :], o[(2 * j + 1) * rows:(2 * j + 2) * rows, :])
            for j in range(GROUP)]


def _sink_rows(sink_ref, rows):
    return jnp.concatenate([jnp.full((rows, LANES), sink_ref[0, h], F32) for h in HEAD_ORDER], axis=0)


def _attn_sample_kernel(sink_ref, q_ref, kn_ref, vn_ref, ck_ref, cv_ref, o_ref, ko_ref, vo_ref, *, dec_seq):
    t = dec_seq
    bb = SAMPLE_ATTN_BATCH
    nrow = ATTN_HEADS * t
    ts, ns = t.bit_length() - 1, nrow.bit_length() - 1
    r_c = lax.broadcasted_iota(jnp.int32, (nrow, WINDOW), 0) & (t - 1)
    c_c = lax.broadcasted_iota(jnp.int32, (nrow, WINDOW), 1)
    vis_cache = c_c > r_c
    r_n = lax.broadcasted_iota(jnp.int32, (bb * nrow, bb * t), 0)
    c_n = lax.broadcasted_iota(jnp.int32, (bb * nrow, bb * t), 1)
    vis_new = jnp.logical_and((r_n >> ns) == (c_n >> ts), (c_n & (t - 1)) <= (r_n & (t - 1)))
    sink = _sink_rows(sink_ref, t)
    nt = (((1,), (1,)), ((), ()))
    kn_all, vn_all = kn_ref[...], vn_ref[...]
    lane_w = lax.broadcasted_iota(jnp.int32, (KV_WIDTH, WINDOW), 1)
    is_new = lane_w >= WINDOW - t
    zero_rows = jnp.zeros((WINDOW - t, KV_WIDTH), F32)
    state_shape = (KV_HEADS, HEAD_DIM, WINDOW)

    def slide(old_t, new_rows):
        new_t = jnp.concatenate([zero_rows, new_rows], axis=0).T
        return jnp.where(is_new, new_t, pltpu.roll(old_t, WINDOW - t, axis=1)).reshape(state_shape)

    qs, s_c, cvs = [], [], []
    for b in range(bb):
        rows = slice(b * t, (b + 1) * t)
        ck = ck_ref[b].reshape(KV_WIDTH, WINDOW)
        cvs.append(cv_ref[b].reshape(KV_WIDTH, WINDOW))
        ko_ref[b] = slide(ck, kn_all[rows, :])
        vo_ref[b] = slide(cvs[b], vn_all[rows, :])
        qs.append(_stack_heads([q_ref[rows, j * LANES:(j + 1) * LANES] for j in range(GROUP)]).astype(BF16))
        s_c.append(jnp.where(vis_cache, jnp.dot(qs[b], ck.astype(BF16), preferred_element_type=F32), -jnp.inf))
    s_n = jnp.where(vis_new, lax.dot_general(jnp.concatenate(qs, axis=0), kn_all.astype(BF16), nt,
                                             preferred_element_type=F32), -jnp.inf)
    p_c, p_n, rden = [], [], []
    for b in range(bb):
        s_nb = s_n[b * nrow:(b + 1) * nrow, :]
        m = jnp.maximum(jnp.maximum(jnp.max(s_c[b], axis=-1, keepdims=True), jnp.max(s_nb, axis=-1, keepdims=True)),
                        sink)
        p_c.append(jnp.exp(s_c[b] - m))
        p_n.append(jnp.exp(s_nb - m[:, :bb * t]))
        rden.append(1.0 / (jnp.sum(p_c[b], axis=-1, keepdims=True) + jnp.sum(p_n[b], axis=-1, keepdims=True)
                           + jnp.exp(sink - m)))
    o_n = jnp.dot(jnp.concatenate(p_n, axis=0).astype(BF16), vn_all.astype(BF16), preferred_element_type=F32)
    outs = [[] for _ in range(GROUP)]
    for b in range(bb):
        o = (lax.dot_general(p_c[b].astype(BF16), cvs[b].astype(BF16), nt, preferred_element_type=F32)
             + o_n[b * nrow:(b + 1) * nrow, :]) * rden[b]
        for j, tile in enumerate(_unstack_heads(o, t)):
            outs[j].append(tile)
    for j, parts in enumerate(outs):
        o_ref[:, j * LANES:(j + 1) * LANES] = jnp.concatenate(parts, axis=0).astype(o_ref.dtype)


def _attn_sample(sink, qa, ka, va, cache_k, cache_v, dec_seq):
    nb = cache_k.shape[0]
    bb = SAMPLE_ATTN_BATCH
    row = lambda n: pl.BlockSpec((bb * dec_seq, n), lambda i: (i, 0))
    cache = pl.BlockSpec((bb, KV_HEADS, HEAD_DIM, WINDOW), lambda i: (i, 0, 0, 0))
    return pl.pallas_call(
        functools.partial(_attn_sample_kernel, dec_seq=dec_seq),
        grid=(nb // bb,),
        in_specs=[pl.BlockSpec(memory_space=pltpu.SMEM), row(ATTN_WIDTH), row(KV_WIDTH), row(KV_WIDTH), cache, cache],
        out_specs=[row(ATTN_WIDTH), cache, cache],
        out_shape=[jax.ShapeDtypeStruct((nb * dec_seq, ATTN_WIDTH), BF16),
                   jax.ShapeDtypeStruct(cache_k.shape, F32), jax.ShapeDtypeStruct(cache_v.shape, F32)],
        compiler_params=_params(("parallel",)),
        name="attn_sample",
    )(sink, qa, ka, va, cache_k, cache_v)


def _block_diag(a, b):
    za, zb = jnp.zeros_like(a), jnp.zeros_like(b)
    return jnp.concatenate([jnp.concatenate([a, zb], axis=1), jnp.concatenate([za, b], axis=1)], axis=0)


def _head_out(hh, mnorm_row, om):
    y = hh * lax.rsqrt(jnp.mean(hh * hh, axis=-1, keepdims=True) + EPS) * mnorm_row
    return (jax.nn.sigmoid(om) * y).astype(BF16)


def _mixer_prompt_kernel(sink_ref, bi_ref, bf_ref, x_ref, g_ref, win_ref, mnorm_ref, wg_ref, wu_ref, wd_ref, wo_ref,
                         mix_ref, kw_ref, vw_ref, c_ref, m_ref, w_ref, wgb_ref, wub_ref, wdb_ref, wob_ref):
    step = pl.program_id(0)

    @pl.when(step == 0)
    def _():
        kw_ref[...] = jnp.zeros_like(kw_ref)
        vw_ref[...] = jnp.zeros_like(vw_ref)
        c_ref[...] = jnp.zeros_like(c_ref)
        m_ref[...] = jnp.zeros_like(m_ref)
        for j in range(GROUP):
            lo = win_ref[j * HEAD_DIM:(j + 1) * HEAD_DIM, :]
            hi = win_ref[(j + GROUP) * HEAD_DIM:(j + GROUP + 1) * HEAD_DIM, :]
            w_ref[:, j * LANES:(j + 1) * LANES] = jnp.concatenate([lo, hi], axis=0).T.astype(BF16)
        for c0 in range(ATTN_WIDTH, IN_MAIN, M_WIDTH):
            c1 = min(c0 + M_WIDTH, IN_MAIN)
            w_ref[:, c0:c1] = win_ref[c0:c1, :].T.astype(BF16)
        gw = jnp.concatenate([win_ref[IN_MAIN:IN_MAIN + 2 * M_HEADS, :],
                              jnp.zeros((GATE_PAD - 2 * M_HEADS, D_MODEL), F32)], axis=0)
        w_ref[:, IN_MAIN:] = gw.T.astype(BF16)

    wgb_ref[...] = wg_ref[...].astype(BF16)
    wub_ref[...] = wu_ref[...].astype(BF16)
    wdb_ref[...] = wd_ref[...].astype(BF16)
    wob_ref[...] = wo_ref[...].astype(BF16)

    tm = x_ref.shape[0]
    ln = WINDOW
    groups = [slice(r0, r0 + tm // ROW_GROUPS) for r0 in range(0, tm, tm // ROW_GROUPS)]
    blocks = [slice(r0, r0 + ln) for r0 in range(0, tm, ln)]
    hcols = [slice(h * M_HEAD_DIM, (h + 1) * M_HEAD_DIM) for h in range(M_HEADS)]
    pcols = [slice(2 * p * M_HEAD_DIM, 2 * (p + 1) * M_HEAD_DIM) for p in range(M_PAIRS)]
    nt = (((1,), (1,)), ((), ()))
    hs = [_rms(x_ref[rs, :], g_ref[...]).astype(BF16) for rs in groups]

    def proj(off, n):
        return jnp.concatenate([jnp.dot(h, w_ref[:, off:off + n], preferred_element_type=F32) for h in hs], axis=0)

    qt = proj(OFF_QM, M_WIDTH).T.astype(BF16)
    km = (proj(OFF_KM, M_WIDTH) * (M_HEAD_DIM ** -0.5)).astype(BF16)
    vt = proj(OFF_VM, M_WIDTH).T.astype(BF16)
    gates = proj(IN_MAIN, GATE_PAD).T
    r = lax.broadcasted_iota(jnp.int32, (ln, ln), 0)
    c = lax.broadcasted_iota(jnp.int32, (ln, ln), 1)
    causal_t = r <= c
    upper = causal_t.astype(F32)
    lane8 = lax.broadcasted_iota(jnp.int32, (SUBLANES, ln), 1)
    ones_rows = jnp.ones((AUG_ROWS - M_HEAD_DIM, ln), BF16)
    zrows = jnp.zeros((ln - SUBLANES, ln), F32)
    row8 = lax.broadcasted_iota(jnp.int32, (SUBLANES, ln), 0)
    bias = jnp.zeros((SUBLANES, ln), F32)
    for h in range(M_HEADS):
        bias = jnp.where(row8 == h, bi_ref[0, h], jnp.where(row8 == M_HEADS + h, bf_ref[0, h], bias))
    pre, scores_m = [], []
    for rows in blocks:
        gi = gates[0:SUBLANES, rows] + bias
        b = pltpu.roll(jnp.dot(jax.nn.log_sigmoid(gi), upper, precision=HIGHEST, preferred_element_type=F32),
                       M_HEADS, axis=0)
        g = gi - b
        cm0 = g
        sh = 1
        while sh < ln:
            cm0 = jnp.maximum(cm0, jnp.where(lane8 >= sh, pltpu.roll(cm0, sh, axis=1), -jnp.inf))
            sh *= 2
        b_last = jnp.broadcast_to(b[:, ln - 1:ln], b.shape)
        g_max = jnp.broadcast_to(cm0[:, ln - 1:ln], b.shape)
        g_cols = jnp.concatenate([g, zrows], axis=0).T
        pre.append((b, g_cols, cm0, b_last, g_max, (b_last - b) + gi))
        scores_m.append([jnp.dot(km[rows, pc], _block_diag(qt[hcols[2 * p], rows], qt[hcols[2 * p + 1], rows]),
                                 preferred_element_type=F32) for p, pc in enumerate(pcols)])

    qa = (proj(OFF_QA, ATTN_WIDTH) * (HEAD_DIM ** -0.5)).astype(BF16)
    kv = proj(OFF_KA, 2 * KV_WIDTH)
    ka, va = kv[:, :KV_WIDTH], kv[:, KV_WIDTH:]
    nrow = ATTN_HEADS * ln
    ra = lax.broadcasted_iota(jnp.int32, (nrow, ln), 0) & (ln - 1)
    ca = lax.broadcasted_iota(jnp.int32, (nrow, ln), 1)
    own = ca <= ra
    sink = _sink_rows(sink_ref, ln)
    no_prev = jnp.where(step == 0, -jnp.inf, 0.0)
    ones = jnp.ones((2 * ln, LANES), BF16)
    scores_a, v_augs = [], []
    for blk, rows in enumerate(blocks):
        if blk == 0:
            k_prev, v_prev = kw_ref[...], vw_ref[...]
        else:
            k_prev, v_prev = ka[blocks[blk - 1], :], va[blocks[blk - 1], :]
        kcat = jnp.concatenate([ka[rows, :], k_prev], axis=0).astype(BF16)
        vcat = jnp.concatenate([va[rows, :], v_prev], axis=0).astype(BF16)
        v_augs.append(jnp.concatenate([vcat, ones], axis=1))
        qs = _stack_heads([qa[rows, j * LANES:(j + 1) * LANES] for j in range(GROUP)])
        scores_a.append(lax.dot_general(qs, kcat, nt, preferred_element_type=F32))
    kw_ref[...] = ka[blocks[-1], :]
    vw_ref[...] = va[blocks[-1], :]

    m_prev = m_ref[...]
    scal = []
    for b, _, cm0, b_last, g_max, w_end_arg in pre:
        cm = jnp.maximum(cm0, m_prev)
        m_end = b_last + jnp.maximum(g_max, m_prev)
        scal.append((cm, jnp.exp(m_prev - cm), jnp.exp(-(b + cm)), jnp.exp(b_last + m_prev - m_end),
                     jnp.exp(w_end_arg - m_end)))
        m_prev = m_end
    m_ref[...] = m_prev
    gated = []
    for rows, s_t, (_, g_cols, *_), (cm, _, _, _, w_end) in zip(blocks, scores_m, pre, scal):
        vts = [jnp.concatenate([vt[hc, rows], ones_rows], axis=0) for hc in hcols]
        sqks = [(jnp.exp(jnp.where(causal_t, g_cols[:, h:h + 1] - cm[h:h + 1, :], -jnp.inf))
                 * s_t[h // 2][:, (h % 2) * ln:(h % 2 + 1) * ln]).astype(BF16) for h in range(M_HEADS)]
        kvws = [(vts[h].astype(F32) * w_end[h:h + 1, :]).astype(BF16) for h in range(M_HEADS)]
        gated.append((vts, sqks, kvws))
    probs, maxes = [], []
    for blk, s in enumerate(scores_a):
        s_prev = s[:, ln:]
        if blk == 0:
            s_prev = s_prev + no_prev
        sc = jnp.where(own, s[:, :ln], s_prev)
        mx = jnp.maximum(jnp.max(sc, axis=-1, keepdims=True), sink)
        p = jnp.exp(sc - mx)
        zero = jnp.zeros_like(p)
        probs.append(jnp.concatenate([jnp.where(own, p, zero), jnp.where(own, zero, p)], axis=1).astype(BF16))
        maxes.append(mx)
    om = proj(OFF_OM, M_WIDTH)

    mem = [jnp.concatenate([c_ref[2 * p], c_ref[2 * p + 1]], axis=1) for p in range(M_PAIRS)]
    pair_row = lambda x, p: jnp.concatenate([x[2 * p:2 * p + 1, :], x[2 * p + 1:2 * p + 2, :]], axis=1)
    for rows, (vts, sqks, kvws), (_, w_inter, e_negm, dec, _), p2, v_aug, mx in zip(
            blocks, gated, scal, probs, v_augs, maxes):
        pairs = range(M_PAIRS)
        upd = [jnp.dot(jnp.concatenate([kvws[2 * p], kvws[2 * p + 1]], axis=1),
                       _block_diag(km[rows, hcols[2 * p]], km[rows, hcols[2 * p + 1]]),
                       preferred_element_type=F32) for p in pairs]
        intra = [jnp.dot(jnp.concatenate([vts[2 * p], vts[2 * p + 1]], axis=1),
                         _block_diag(sqks[2 * p], sqks[2 * p + 1]), preferred_element_type=F32) for p in pairs]
        inter = [jnp.dot(mem[p].astype(BF16), _block_diag(qt[hcols[2 * p], rows], qt[hcols[2 * p + 1], rows]),
                         preferred_element_type=F32) for p in pairs]
        o = jnp.dot(p2, v_aug, preferred_element_type=F32)
        o = o[:, :LANES] * (1.0 / (o[:, LANES:] + jnp.exp(sink - mx)))
        for j, tile in enumerate(_unstack_heads(o, ln)):
            mix_ref[rows, j * LANES:(j + 1) * LANES] = tile.astype(mix_ref.dtype)
        for p in pairs:
            num = inter[p] * pair_row(w_inter, p) + intra[p]
            mem[p] = pair_row(dec, p) * mem[p] + upd[p]
            den = jnp.maximum(jnp.abs(num[M_HEAD_DIM:M_HEAD_DIM + 1, :]), pair_row(e_negm, p))
            hh = num[:M_HEAD_DIM, :] * (1.0 / den)
            y = hh * lax.rsqrt(jnp.mean(hh * hh, axis=0, keepdims=True) + EPS)
            for h in (2 * p, 2 * p + 1):
                y_h = y[:, (h % 2) * ln:(h % 2 + 1) * ln]
                mcols = slice(ATTN_WIDTH + h * M_HEAD_DIM, ATTN_WIDTH + (h + 1) * M_HEAD_DIM)
                mix_ref[rows, mcols] = (jax.nn.sigmoid(om[rows, hcols[h]])
                                        * (y_h.T * mnorm_ref[:, hcols[h]])).astype(mix_ref.dtype)
    for p in range(M_PAIRS):
        c_ref[2 * p] = mem[p][:, :M_HEAD_DIM]
        c_ref[2 * p + 1] = mem[p][:, M_HEAD_DIM:]


def _out_row_block(i):
    per_head = (ATTN_WIDTH // ATTN_HEADS) // WEIGHT_SLAB
    j, part = i // per_head, i % per_head
    head = (j % 2) * GROUP + j // 2
    return jnp.where(i < ATTN_HEADS * per_head, head * per_head + part, i)


def _mixer_prompt(sink, b_i, b_f, x, g_pre, w_in_t, mnorm, w_gate, w_up, w_down, w_out):
    m = x.shape[0]
    tm = TOKEN_TILE
    steps = m // tm
    assert D_MODEL == steps * WEIGHT_SLAB and D_FF % (steps // 2) == 0 and (D_FF // (steps // 2)) % BF16_SUBLANES == 0
    down_slab = D_FF // (steps // 2)
    row = lambda n: pl.BlockSpec((tm, n), lambda i: (i, 0))
    whole = lambda shape: pl.BlockSpec(shape, lambda i: (0,) * len(shape))
    smem = pl.BlockSpec(memory_space=pltpu.SMEM)
    slab = lambda n: pl.BlockSpec((WEIGHT_SLAB, n), lambda i: (i, 0))
    down = pl.BlockSpec((down_slab, D_MODEL), lambda i: (i // 2, 0))
    c_shape = (M_HEADS, AUG_ROWS, M_HEAD_DIM)
    w_shape = (WINDOW, KV_WIDTH)
    s_shape = (SUBLANES, LANES)
    sds = jax.ShapeDtypeStruct
    return pl.pallas_call(
        _mixer_prompt_kernel,
        grid=(steps,),
        in_specs=[smem, smem, smem, row(D_MODEL), _const_spec((1, D_MODEL)), _const_spec(w_in_t.shape),
                  _const_spec((1, M_WIDTH)), slab(D_FF), slab(D_FF), down,
                  pl.BlockSpec((WEIGHT_SLAB, D_MODEL), lambda i: (_out_row_block(i), 0))],
        out_specs=[row(ATTN_WIDTH + M_WIDTH), whole(w_shape), whole(w_shape), whole(c_shape), whole(s_shape),
                   whole((D_MODEL, IN_PAD)), slab(D_FF), slab(D_FF), down, slab(D_MODEL)],
        out_shape=[sds((m, ATTN_WIDTH + M_WIDTH), BF16), sds(w_shape, F32), sds(w_shape, F32), sds(c_shape, F32),
                   sds(s_shape, F32), sds((D_MODEL, IN_PAD), BF16), sds(w_gate.shape, BF16), sds(w_up.shape, BF16),
                   sds(w_down.shape, BF16), sds(w_out.shape, BF16)],
        compiler_params=_params(("arbitrary",)),
        name="mixer_prompt",
    )(sink, b_i, b_f, x, g_pre, w_in_t, mnorm, w_gate, w_up, w_down, w_out)


def _gates(g_blk, bias_row, cum):
    pre = g_blk + bias_row
    lane = lax.broadcasted_iota(jnp.int32, pre.shape, 1)
    a = jnp.where(lane < M_HEADS, pre, jax.nn.log_sigmoid(pre))
    b = jnp.dot(cum, a, precision=HIGHEST, preferred_element_type=F32)
    return a, b, a.T, b.T


def _col(x, j):
    return jnp.broadcast_to(x[:, j:j + 1], x.shape)


def _mlstm_gated_scores(q, k, a, b, at, bt, h, mask, m_prev, last):
    bc, ic = _col(b, M_HEADS + h), _col(a, h)
    br, ir = bt[M_HEADS + h:M_HEADS + h + 1, :], at[h:h + 1, :]
    d = jnp.where(mask, (bc - br) + ir, -jnp.inf)
    inter = bc + m_prev
    m_t = jnp.maximum(inter, jnp.max(d, axis=-1, keepdims=True))
    w_inter = jnp.exp(inter - m_t)
    sqk = jnp.exp(d - m_t) * lax.dot_general(q, k, (((1,), (1,)), ((), ())), preferred_element_type=F32)
    m_end = last(m_t)
    bl = last(bc)
    dec = jnp.exp(bl + m_prev - m_end)
    w_end = jnp.exp((bl - bc) + ic - m_end)
    kw = k.astype(F32) * w_end
    return w_inter, sqk, m_t, m_end, dec, kw


def _mlstm_sample_kernel(q_ref, k_ref, v_ref, om_ref, g_ref, m0_ref, bias_ref, mnorm_ref, c_ref, n_ref,
                         o_ref, c_out, n_out, m_out, *, dec_seq):
    t = dec_seq
    bb = SAMPLE_MLSTM_BATCH
    ln = bb * t
    r = lax.broadcasted_iota(jnp.int32, (ln, ln), 0)
    c = lax.broadcasted_iota(jnp.int32, (ln, ln), 1)
    shift = t.bit_length() - 1
    same = (r >> shift) == (c >> shift)
    mask = jnp.logical_and(same, c <= r)
    cum = mask.astype(F32)
    expand = (c == (r >> shift)).astype(F32)
    gather = ((c >> shift) == r).astype(F32)
    is_last = (r & (t - 1)) == t - 1
    is_first = (r & (t - 1)) == 0

    def last(x):
        y = jnp.where(is_last, x, 0.0)
        step = 1
        while step < t:
            y = y + pltpu.roll(y, ln - step, axis=0)
            step *= 2
        return y

    a, b, at, bt = _gates(g_ref[...], bias_ref[...], cum)
    m0 = m0_ref[...]
    lane = lax.broadcasted_iota(jnp.int32, (ln, LANES), 1)
    zpad = jnp.zeros((ln - bb, M_HEAD_DIM), F32)
    heads = range(M_HEADS)
    hcols = [slice(h * M_HEAD_DIM, (h + 1) * M_HEAD_DIM) for h in heads]
    qfs = [q_ref[:, hc] for hc in hcols]
    vs = [v_ref[:, hc].astype(BF16) for hc in hcols]
    qcs = [jnp.concatenate(
        [jnp.dot(qfs[h][s * t:(s + 1) * t, :], c_ref[s * M_HEADS + h].astype(BF16).astype(F32),
                 preferred_element_type=F32) for s in range(bb)], axis=0) for h in heads]
    n_exps = [jnp.dot(expand, jnp.concatenate([n_ref[:, hc], zpad], axis=0), precision=HIGHEST,
                      preferred_element_type=F32) for hc in hcols]
    parts = [_mlstm_gated_scores(qfs[h].astype(BF16), k_ref[:, hcols[h]].astype(BF16), a, b, at, bt, h, mask,
                                 _col(m0, h), last) for h in heads]
    intras = [jnp.dot(parts[h][1].astype(BF16), vs[h], preferred_element_type=F32) for h in heads]
    m_cols = jnp.zeros((ln, LANES), F32)
    kwts = []
    for h in heads:
        w_inter, sqk, m_t, m_end, dec, kw = parts[h]
        num = w_inter * qcs[h] + intras[h]
        nq = w_inter * jnp.sum(qfs[h] * n_exps[h], axis=-1, keepdims=True) + jnp.sum(sqk, axis=-1, keepdims=True)
        hh = num / jnp.maximum(jnp.abs(nq), jnp.exp(-m_t))
        o_ref[:, hcols[h]] = _head_out(hh, mnorm_ref[:, hcols[h]], om_ref[:, hcols[h]])
        kwts.append(kw.T.astype(BF16))
        m_cols = jnp.where(lane == h, m_end, m_cols)
    m_out[...] = m_cols
    for h in heads:
        dec = parts[h][4]
        for s in range(bb):
            lhs = jnp.where((c >> shift) == s, kwts[h], jnp.zeros_like(kwts[h]))
            upd = jnp.dot(lhs, vs[h], preferred_element_type=F32)
            c_out[s * M_HEADS + h] = dec[s * t:s * t + 1, :] * c_ref[s * M_HEADS + h] + upd
    for h in heads:
        dec, kw = parts[h][4], parts[h][5]
        n_new = jnp.dot(gather, jnp.where(is_first, dec * n_exps[h], 0.0) + kw, precision=HIGHEST,
                        preferred_element_type=F32)
        n_out[:, hcols[h]] = n_new[:bb, :]


def _mlstm_sample(qm, km, vm, om, gt, m0_rows, bias_row, mnorm, c_in, n_in, dec_seq):
    m = qm.shape[0]
    bb = SAMPLE_MLSTM_BATCH
    tm = bb * dec_seq
    nb = m // dec_seq
    row = lambda n: pl.BlockSpec((tm, n), lambda i: (i, 0))
    whole = lambda shape: pl.BlockSpec(shape, lambda i: (0,) * len(shape))
    c_spec = pl.BlockSpec((bb * M_HEADS, M_HEAD_DIM, M_HEAD_DIM), lambda i: (i, 0, 0))
    n_spec = pl.BlockSpec((bb, M_WIDTH), lambda i: (i, 0))
    return pl.pallas_call(
        functools.partial(_mlstm_sample_kernel, dec_seq=dec_seq),
        grid=(nb // bb,),
        in_specs=[row(M_WIDTH), row(M_WIDTH), row(M_WIDTH), row(M_WIDTH), row(GATE_PAD), row(LANES),
                  whole((1, GATE_PAD)), whole((1, M_WIDTH)), c_spec, n_spec],
        out_specs=[row(M_WIDTH), c_spec, n_spec, row(LANES)],
        out_shape=[jax.ShapeDtypeStruct((m, M_WIDTH), BF16), jax.ShapeDtypeStruct(c_in.shape, F32),
                   jax.ShapeDtypeStruct(n_in.shape, F32), jax.ShapeDtypeStruct((m, LANES), F32)],
        compiler_params=_params(("parallel",)),
        name="mlstm_sample",
    )(qm, km, vm, om, gt, m0_rows, bias_row, mnorm, c_in, n_in)


def _out_ffn_kernel(x_ref, mix_ref, wo_ref, g1_ref, g2_ref, wg_ref, wu_ref, wd_ref, g3_ref, o_ref):
    tm = x_ref.shape[0]
    groups = [slice(r, r + tm // ROW_GROUPS) for r in range(0, tm, tm // ROW_GROUPS)]
    ys = [jnp.dot(mix_ref[rs, :], wo_ref[...], preferred_element_type=F32) for rs in groups]
    x1s = [x_ref[rs, :] + _rms(y, g1_ref[...]) for rs, y in zip(groups, ys)]
    fs = [_rms(x1, g2_ref[...]).astype(BF16) for x1 in x1s]
    accs = [None] * ROW_GROUPS
    for off in range(0, D_FF, FFN_CHUNK):
        acts = []
        for f in fs:
            g = jnp.dot(f, wg_ref[:, off:off + FFN_CHUNK], preferred_element_type=F32)
            u = jnp.dot(f, wu_ref[:, off:off + FFN_CHUNK], preferred_element_type=F32)
            acts.append((g * jax.nn.sigmoid(g) * u).astype(BF16))
        for i, act in enumerate(acts):
            part = jnp.dot(act, wd_ref[off:off + FFN_CHUNK, :], preferred_element_type=F32)
            accs[i] = part if accs[i] is None else accs[i] + part
    for rs, x1, acc in zip(groups, x1s, accs):
        o_ref[rs, :] = x1 + _rms(acc, g3_ref[...])


def _out_ffn(x, mix, w_out, g_post_mix, g_pre_ffn, w_gate, w_up, w_down, g_post_ffn):
    m = x.shape[0]
    tm = TOKEN_TILE
    row = lambda n: pl.BlockSpec((tm, n), lambda i: (i, 0))
    vec = _const_spec((1, D_MODEL))
    return pl.pallas_call(
        _out_ffn_kernel,
        grid=(m // tm,),
        in_specs=[row(D_MODEL), row(ATTN_WIDTH + M_WIDTH), _const_spec((D_MODEL, D_MODEL)), vec, vec,
                  _const_spec((D_MODEL, D_FF)), _const_spec((D_MODEL, D_FF)), _const_spec((D_FF, D_MODEL)), vec],
        out_specs=row(D_MODEL),
        out_shape=jax.ShapeDtypeStruct((m, D_MODEL), F32),
        compiler_params=_params(("parallel",)),
        name="out_ffn",
    )(x, mix, w_out, g_post_mix, g_pre_ffn, w_gate, w_up, w_down, g_post_ffn)


def _layer(xp, xs, cache_k, cache_v, state_c, state_n, state_m, w_in, b_i, b_f, attn_sink, m_norm, w_out,
           g_pre_mix, g_post_mix, g_pre_ffn, g_post_ffn, w_gate, w_up, w_down):
    bp, sp, _ = xp.shape
    bs, ts, _ = xs.shape
    assert bp == 1 and sp % TOKEN_TILE == 0 and TOKEN_TILE % WINDOW == 0
    assert ts & (ts - 1) == 0 and (bs * ts) % TOKEN_TILE == 0 and bs % SAMPLE_MLSTM_BATCH == 0

    row = lambda v: v.reshape(1, -1)
    bias_row = jnp.pad(jnp.concatenate([b_i, b_f]), (0, GATE_PAD - 2 * M_HEADS)).reshape(1, GATE_PAD)
    sink = row(attn_sink)

    x2 = xp.reshape(sp, D_MODEL)
    mix, k_w, v_w, c_aug, m_p, w_pad, wg, wu, wd, wo = _mixer_prompt(
        sink, row(b_i), row(b_f), x2, row(g_pre_mix), w_in.T, row(m_norm), w_gate, w_up, w_down, w_out)
    ffn = (wo, row(g_post_mix), row(g_pre_ffn), wg, wu, wd, row(g_post_ffn))
    yp = _out_ffn(x2, mix, *ffn).reshape(xp.shape)
    k_p = k_w.reshape(1, WINDOW, KV_HEADS, HEAD_DIM)
    v_p = v_w.reshape(1, WINDOW, KV_HEADS, HEAD_DIM)
    c_p = jnp.swapaxes(c_aug[:, :M_HEAD_DIM, :], 1, 2).reshape(1, M_HEADS, M_HEAD_DIM, M_HEAD_DIM)
    n_p = c_aug[:, M_HEAD_DIM, :].reshape(1, M_HEADS, M_HEAD_DIM)
    m_p = m_p[:M_HEADS, 0].reshape(1, M_HEADS)

    x2 = xs.reshape(bs * ts, D_MODEL)
    qa, ka, va, qm, km, vm, om, gt = _inproj(x2, row(g_pre_mix), w_pad)
    feature_major = lambda c: jnp.transpose(c, (0, 2, 3, 1))
    mix_a, k_s, v_s = _attn_sample(sink, qa, ka, va, feature_major(cache_k), feature_major(cache_v), ts)
    m0_rows = jnp.pad(jnp.repeat(state_m, ts, axis=0), ((0, 0), (0, LANES - M_HEADS)))
    mix_m, c_s, n_s, m_rows = _mlstm_sample(qm, km, vm, om, gt, m0_rows, bias_row, row(m_norm),
                                            state_c.reshape(bs * M_HEADS, M_HEAD_DIM, M_HEAD_DIM),
                                            state_n.reshape(bs, M_WIDTH), ts)
    ys = _out_ffn(x2, jnp.concatenate([mix_a, mix_m], axis=1), *ffn).reshape(xs.shape)
    k_s = jnp.transpose(k_s, (0, 3, 1, 2))
    v_s = jnp.transpose(v_s, (0, 3, 1, 2))
    c_s = c_s.reshape(bs, M_HEADS, M_HEAD_DIM, M_HEAD_DIM)
    n_s = n_s.reshape(bs, M_HEADS, M_HEAD_DIM)
    m_s = m_rows[ts - 1::ts, :M_HEADS]
    return yp, ys, (k_p, v_p, c_p, n_p, m_p), (k_s, v_s, c_s, n_s, m_s)


def kernel(x_prompt, x_sample, cache_k, cache_v, state_C, state_n, state_m, w_in, b_i, b_f, attn_sink, m_norm,
           w_out, g_pre_mix, g_post_mix, g_pre_ffn, g_post_ffn, w_gate, w_up, w_down):
    depth = w_in.shape[0]
    xp, xs = x_prompt, x_sample
    prompt_states, sample_states = [], []
    for l in range(depth):
        xp, xs, st_p, st_s = _layer(xp, xs, cache_k[l], cache_v[l], state_C[l], state_n[l], state_m[l],
                                    w_in[l], b_i[l], b_f[l], attn_sink[l], m_norm[l], w_out[l],
                                    g_pre_mix[l], g_post_mix[l], g_pre_ffn[l], g_post_ffn[l],
                                    w_gate[l], w_up[l], w_down[l])
        prompt_states.append(st_p)
        sample_states.append(st_s)
    stack = lambda states, i: jnp.stack([s[i] for s in states], axis=0)
    return (xp, xs) + tuple(stack(prompt_states, i) for i in range(5)) + tuple(stack(sample_states, i) for i in range(5))
```

```python
import functools

import jax
import jax.numpy as jnp
from jax import lax
from jax.experimental import pallas as pl
from jax.experimental.pallas import tpu as pltpu

F32 = jnp.float32
BF16 = jnp.bfloat16
HIGHEST = lax.Precision.HIGHEST

D_MODEL = 1024
HEAD_DIM = 64
ATTN_HEADS = 8
KV_HEADS = 2
GROUP = ATTN_HEADS // KV_HEADS
ATTN_WIDTH = ATTN_HEADS * HEAD_DIM
KV_WIDTH = KV_HEADS * HEAD_DIM
WINDOW = 128
M_HEADS = 4
M_HEAD_DIM = 128
M_WIDTH = M_HEADS * M_HEAD_DIM
M_PAIRS = M_HEADS // 2
D_FF = 2816
EPS = 1e-6

LANES = 128
SUBLANES = 8
GATE_PAD = LANES
BF16_SUBLANES = 16
AUG_ROWS = M_HEAD_DIM + BF16_SUBLANES
IN_MAIN = ATTN_WIDTH + 2 * KV_WIDTH + 4 * M_WIDTH
IN_PAD = IN_MAIN + GATE_PAD
VMEM_LIMIT = 56 * 1024 * 1024

HEAD_ORDER = tuple(h for j in range(GROUP) for h in (j, j + GROUP))

TOKEN_TILE = 512
FFN_CHUNK = 256
ROW_GROUPS = 2
WEIGHT_SLAB = 32
SAMPLE_ATTN_BATCH = 16
SAMPLE_MLSTM_BATCH = 16


def _rms(x, g):
    return x * lax.rsqrt(jnp.mean(x * x, axis=-1, keepdims=True) + EPS) * g


def _const_spec(shape):
    nd = len(shape)
    return pl.BlockSpec(shape, lambda i: (0,) * nd, pipeline_mode=pl.Buffered(1))


def _params(semantics):
    return pltpu.CompilerParams(dimension_semantics=semantics, vmem_limit_bytes=VMEM_LIMIT)


OFF_QA, OFF_KA, OFF_VA = 0, ATTN_WIDTH, ATTN_WIDTH + KV_WIDTH
OFF_QM = ATTN_WIDTH + 2 * KV_WIDTH
OFF_KM, OFF_VM, OFF_OM = OFF_QM + M_WIDTH, OFF_QM + 2 * M_WIDTH, OFF_QM + 3 * M_WIDTH


def _inproj_kernel(x_ref, g_ref, w_ref, qa_ref, ka_ref, va_ref, qm_ref, km_ref, vm_ref, om_ref, gt_ref):
    tm = x_ref.shape[0]
    step = tm // ROW_GROUPS
    for r0 in range(0, tm, step):
        rs = slice(r0, r0 + step)
        h = _rms(x_ref[rs, :], g_ref[...]).astype(BF16)

        def proj(off, n):
            return jnp.dot(h, w_ref[:, off:off + n], preferred_element_type=F32)

        qa_ref[rs, :] = proj(OFF_QA, ATTN_WIDTH) * (HEAD_DIM ** -0.5)
        ka_ref[rs, :] = proj(OFF_KA, KV_WIDTH)
        va_ref[rs, :] = proj(OFF_VA, KV_WIDTH)
        qm_ref[rs, :] = proj(OFF_QM, M_WIDTH)
        km_ref[rs, :] = proj(OFF_KM, M_WIDTH) * (M_HEAD_DIM ** -0.5)
        vm_ref[rs, :] = proj(OFF_VM, M_WIDTH)
        om_ref[rs, :] = proj(OFF_OM, M_WIDTH)
        gt_ref[rs, :] = proj(IN_MAIN, GATE_PAD)


def _inproj(x, g_pre, w_pad):
    m = x.shape[0]
    tm = TOKEN_TILE
    row = lambda n: pl.BlockSpec((tm, n), lambda i: (i, 0))
    widths = (ATTN_WIDTH, KV_WIDTH, KV_WIDTH, M_WIDTH, M_WIDTH, M_WIDTH, M_WIDTH, GATE_PAD)
    return pl.pallas_call(
        _inproj_kernel,
        grid=(m // tm,),
        in_specs=[row(D_MODEL), _const_spec((1, D_MODEL)), _const_spec((D_MODEL, IN_PAD))],
        out_specs=[row(n) for n in widths],
        out_shape=[jax.ShapeDtypeStruct((m, n), F32) for n in widths],
        compiler_params=_params(("parallel",)),
        name="inproj",
    )(x, g_pre, w_pad)


def _stack_heads(q_tiles):
    lane = lax.broadcasted_iota(jnp.int32, q_tiles[0].shape, 1)
    lo = lane < HEAD_DIM
    zero = jnp.zeros_like(q_tiles[0])
    parts = []
    for qt in q_tiles:
        parts += [jnp.where(lo, qt, zero), jnp.where(lo, zero, qt)]
    return jnp.concatenate(parts, axis=0)


def _unstack_heads(o, rows):
    lane = lax.broadcasted_iota(jnp.int32, (rows, LANES), 1)
    lo = lane < HEAD_DIM
    return [jnp.where(lo, o[(2 * j) * rows:(2 * j + 1) * rows, :], o[(2 * j + 1) * rows:(2 * j + 2) * rows, :])
            for j in range(GROUP)]


def _sink_rows(sink_ref, rows):
    return jnp.concatenate([jnp.full((rows, LANES), sink_ref[0, h], F32) for h in HEAD_ORDER], axis=0)


def _attn_sample_kernel(sink_ref, q_ref, kn_ref, vn_ref, ck_ref, cv_ref, o_ref, ko_ref, vo_ref, *, dec_seq):
    t = dec_seq
    bb = SAMPLE_ATTN_BATCH
    nrow = ATTN_HEADS * t
    ts, ns = t.bit_length() - 1, nrow.bit_length() - 1
    r_c = lax.broadcasted_iota(jnp.int32, (nrow, WINDOW), 0) & (t - 1)
    c_c = lax.broadcasted_iota(jnp.int32, (nrow, WINDOW), 1)
    vis_cache = c_c > r_c
    r_n = lax.broadcasted_iota(jnp.int32, (bb * nrow, bb * t), 0)
    c_n = lax.broadcasted_iota(jnp.int32, (bb * nrow, bb * t), 1)
    vis_new = jnp.logical_and((r_n >> ns) == (c_n >> ts), (c_n & (t - 1)) <= (r_n & (t - 1)))
    sink = _sink_rows(sink_ref, t)
    nt = (((1,), (1,)), ((), ()))
    kn_all, vn_all = kn_ref[...], vn_ref[...]
    lane_w = lax.broadcasted_iota(jnp.int32, (KV_WIDTH, WINDOW), 1)
    is_new = lane_w >= WINDOW - t
    zero_rows = jnp.zeros((WINDOW - t, KV_WIDTH), F32)
    state_shape = (KV_HEADS, HEAD_DIM, WINDOW)

    def slide(old_t, new_rows):
        new_t = jnp.concatenate([zero_rows, new_rows], axis=0).T
        return jnp.where(is_new, new_t, pltpu.roll(old_t, WINDOW - t, axis=1)).reshape(state_shape)

    qs, s_c, cvs = [], [], []
    for b in range(bb):
        rows = slice(b * t, (b + 1) * t)
        ck = ck_ref[b].reshape(KV_WIDTH, WINDOW)
        cvs.append(cv_ref[b].reshape(KV_WIDTH, WINDOW))
        ko_ref[b] = slide(ck, kn_all[rows, :])
        vo_ref[b] = slide(cvs[b], vn_all[rows, :])
        qs.append(_stack_heads([q_ref[rows, j * LANES:(j + 1) * LANES] for j in range(GROUP)]).astype(BF16))
        s_c.append(jnp.where(vis_cache, jnp.dot(qs[b], ck.astype(BF16), preferred_element_type=F32), -jnp.inf))
    s_n = jnp.where(vis_new, lax.dot_general(jnp.concatenate(qs, axis=0), kn_all.astype(BF16), nt,
                                             preferred_element_type=F32), -jnp.inf)
    p_c, p_n, rden = [], [], []
    for b in range(bb):
        s_nb = s_n[b * nrow:(b + 1) * nrow, :]
        m = jnp.maximum(jnp.maximum(jnp.max(s_c[b], axis=-1, keepdims=True), jnp.max(s_nb, axis=-1, keepdims=True)),
                        sink)
        p_c.append(jnp.exp(s_c[b] - m))
        p_n.append(jnp.exp(s_nb - m[:, :bb * t]))
        rden.append(1.0 / (jnp.sum(p_c[b], axis=-1, keepdims=True) + jnp.sum(p_n[b], axis=-1, keepdims=True)
                           + jnp.exp(sink - m)))
    o_n = jnp.dot(jnp.concatenate(p_n, axis=0).astype(BF16), vn_all.astype(BF16), preferred_element_type=F32)
    outs = [[] for _ in range(GROUP)]
    for b in range(bb):
        o = (lax.dot_general(p_c[b].astype(BF16), cvs[b].astype(BF16), nt, preferred_element_type=F32)
             + o_n[b * nrow:(b + 1) * nrow, :]) * rden[b]
        for j, tile in enumerate(_unstack_heads(o, t)):
            outs[j].append(tile)
    for j, parts in enumerate(outs):
        o_ref[:, j * LANES:(j + 1) * LANES] = jnp.concatenate(parts, axis=0).astype(o_ref.dtype)


def _attn_sample(sink, qa, ka, va, cache_k, cache_v, dec_seq):
    nb = cache_k.shape[0]
    bb = SAMPLE_ATTN_BATCH
    row = lambda n: pl.BlockSpec((bb * dec_seq, n), lambda i: (i, 0))
    cache = pl.BlockSpec((bb, KV_HEADS, HEAD_DIM, WINDOW), lambda i: (i, 0, 0, 0))
    return pl.pallas_call(
        functools.partial(_attn_sample_kernel, dec_seq=dec_seq),
        grid=(nb // bb,),
        in_specs=[pl.BlockSpec(memory_space=pltpu.SMEM), row(ATTN_WIDTH), row(KV_WIDTH), row(KV_WIDTH), cache, cache],
        out_specs=[row(ATTN_WIDTH), cache, cache],
        out_shape=[jax.ShapeDtypeStruct((nb * dec_seq, ATTN_WIDTH), BF16),
                   jax.ShapeDtypeStruct(cache_k.shape, F32), jax.ShapeDtypeStruct(cache_v.shape, F32)],
        compiler_params=_params(("parallel",)),
        name="attn_sample",
    )(sink, qa, ka, va, cache_k, cache_v)


def _block_diag(a, b):
    za, zb = jnp.zeros_like(a), jnp.zeros_like(b)
    return jnp.concatenate([jnp.concatenate([a, zb], axis=1), jnp.concatenate([za, b], axis=1)], axis=0)


def _head_out(hh, mnorm_row, om):
    y = hh * lax.rsqrt(jnp.mean(hh * hh, axis=-1, keepdims=True) + EPS) * mnorm_row
    return (jax.nn.sigmoid(om) * y).astype(BF16)


def _mixer_prompt_kernel(sink_ref, bi_ref, bf_ref, x_ref, g_ref, win_ref, mnorm_ref, wg_ref, wu_ref, wd_ref, wo_ref,
                         mix_ref, kw_ref, vw_ref, c_ref, m_ref, w_ref, wgb_ref, wub_ref, wdb_ref, wob_ref):
    step = pl.program_id(0)

    @pl.when(step == 0)
    def _():
        kw_ref[...] = jnp.zeros_like(kw_ref)
        vw_ref[...] = jnp.zeros_like(vw_ref)
        c_ref[...] = jnp.zeros_like(c_ref)
        m_ref[...] = jnp.zeros_like(m_ref)
        for j in range(GROUP):
            lo = win_ref[j * HEAD_DIM:(j + 1) * HEAD_DIM, :]
            hi = win_ref[(j + GROUP) * HEAD_DIM:(j + GROUP + 1) * HEAD_DIM, :]
            w_ref[:, j * LANES:(j + 1) * LANES] = jnp.concatenate([lo, hi], axis=0).T.astype(BF16)
        for c0 in range(ATTN_WIDTH, IN_MAIN, M_WIDTH):
            c1 = min(c0 + M_WIDTH, IN_MAIN)
            w_ref[:, c0:c1] = win_ref[c0:c1, :].T.astype(BF16)
        gw = jnp.concatenate([win_ref[IN_MAIN:IN_MAIN + 2 * M_HEADS, :],
                              jnp.zeros((GATE_PAD - 2 * M_HEADS, D_MODEL), F32)], axis=0)
        w_ref[:, IN_MAIN:] = gw.T.astype(BF16)

    wgb_ref[...] = wg_ref[...].astype(BF16)
    wub_ref[...] = wu_ref[...].astype(BF16)
    wdb_ref[...] = wd_ref[...].astype(BF16)
    wob_ref[...] = wo_ref[...].astype(BF16)

    tm = x_ref.shape[0]
    ln = WINDOW
    groups = [slice(r0, r0 + tm // ROW_GROUPS) for r0 in range(0, tm, tm // ROW_GROUPS)]
    blocks = [slice(r0, r0 + ln) for r0 in range(0, tm, ln)]
    hcols = [slice(h * M_HEAD_DIM, (h + 1) * M_HEAD_DIM) for h in range(M_HEADS)]
    pcols = [slice(2 * p * M_HEAD_DIM, 2 * (p + 1) * M_HEAD_DIM) for p in range(M_PAIRS)]
    nt = (((1,), (1,)), ((), ()))
    hs = [_rms(x_ref[rs, :], g_ref[...]).astype(BF16) for rs in groups]

    def proj(off, n):
        return jnp.concatenate([jnp.dot(h, w_ref[:, off:off + n], preferred_element_type=F32) for h in hs], axis=0)

    qt = proj(OFF_QM, M_WIDTH).T.astype(BF16)
    km = (proj(OFF_KM, M_WIDTH) * (M_HEAD_DIM ** -0.5)).astype(BF16)
    vt = proj(OFF_VM, M_WIDTH).T.astype(BF16)
    gates = proj(IN_MAIN, GATE_PAD).T
    r = lax.broadcasted_iota(jnp.int32, (ln, ln), 0)
    c = lax.broadcasted_iota(jnp.int32, (ln, ln), 1)
    causal_t = r <= c
    upper = causal_t.astype(F32)
    lane8 = lax.broadcasted_iota(jnp.int32, (SUBLANES, ln), 1)
    ones_rows = jnp.ones((AUG_ROWS - M_HEAD_DIM, ln), BF16)
    zrows = jnp.zeros((ln - SUBLANES, ln), F32)
    row8 = lax.broadcasted_iota(jnp.int32, (SUBLANES, ln), 0)
    bias = jnp.zeros((SUBLANES, ln), F32)
    for h in range(M_HEADS):
        bias = jnp.where(row8 == h, bi_ref[0, h], jnp.where(row8 == M_HEADS + h, bf_ref[0, h], bias))
    pre, scores_m = [], []
    for rows in blocks:
        gi = gates[0:SUBLANES, rows] + bias
        b = pltpu.roll(jnp.dot(jax.nn.log_sigmoid(gi), upper, precision=HIGHEST, preferred_element_type=F32),
                       M_HEADS, axis=0)
        g = gi - b
        cm0 = g
        sh = 1
        while sh < ln:
            cm0 = jnp.maximum(cm0, jnp.where(lane8 >= sh, pltpu.roll(cm0, sh, axis=1), -jnp.inf))
            sh *= 2
        b_last = jnp.broadcast_to(b[:, ln - 1:ln], b.shape)
        g_max = jnp.broadcast_to(cm0[:, ln - 1:ln], b.shape)
        g_cols = jnp.concatenate([g, zrows], axis=0).T
        pre.append((b, g_cols, cm0, b_last, g_max, (b_last - b) + gi))
        scores_m.append([jnp.dot(km[rows, pc], _block_diag(qt[hcols[2 * p], rows], qt[hcols[2 * p + 1], rows]),
                                 preferred_element_type=F32) for p, pc in enumerate(pcols)])

    qa = (proj(OFF_QA, ATTN_WIDTH) * (HEAD_DIM ** -0.5)).astype(BF16)
    kv = proj(OFF_KA, 2 * KV_WIDTH)
    ka, va = kv[:, :KV_WIDTH], kv[:, KV_WIDTH:]
    nrow = ATTN_HEADS * ln
    ra = lax.broadcasted_iota(jnp.int32, (nrow, ln), 0) & (ln - 1)
    ca = lax.broadcasted_iota(jnp.int32, (nrow, ln), 1)
    own = ca <= ra
    sink = _sink_rows(sink_ref, ln)
    no_prev = jnp.where(step == 0, -jnp.inf, 0.0)
    ones = jnp.ones((2 * ln, LANES), BF16)
    scores_a, v_augs = [], []
    for blk, rows in enumerate(blocks):
        if blk == 0:
            k_prev, v_prev = kw_ref[...], vw_ref[...]
        else:
            k_prev, v_prev = ka[blocks[blk - 1], :], va[blocks[blk - 1], :]
        kcat = jnp.concatenate([ka[rows, :], k_prev], axis=0).astype(BF16)
        vcat = jnp.concatenate([va[rows, :], v_prev], axis=0).astype(BF16)
        v_augs.append(jnp.concatenate([vcat, ones], axis=1))
        qs = _stack_heads([qa[rows, j * LANES:(j + 1) * LANES] for j in range(GROUP)])
        scores_a.append(lax.dot_general(qs, kcat, nt, preferred_element_type=F32))
    kw_ref[...] = ka[blocks[-1], :]
    vw_ref[...] = va[blocks[-1], :]

    m_prev = m_ref[...]
    scal = []
    for b, _, cm0, b_last, g_max, w_end_arg in pre:
        cm = jnp.maximum(cm0, m_prev)
        m_end = b_last + jnp.maximum(g_max, m_prev)
        scal.append((cm, jnp.exp(m_prev - cm), jnp.exp(-(b + cm)), jnp.exp(b_last + m_prev - m_end),
                     jnp.exp(w_end_arg - m_end)))
        m_prev = m_end
    m_ref[...] = m_prev
    gated = []
    for rows, s_t, (_, g_cols, *_), (cm, _, _, _, w_end) in zip(blocks, scores_m, pre, scal):
        vts = [jnp.concatenate([vt[hc, rows], ones_rows], axis=0) for hc in hcols]
        sqks = [(jnp.exp(jnp.where(causal_t, g_cols[:, h:h + 1] - cm[h:h + 1, :], -jnp.inf))
                 * s_t[h // 2][:, (h % 2) * ln:(h % 2 + 1) * ln]).astype(BF16) for h in range(M_HEADS)]
        kvws = [(vts[h].astype(F32) * w_end[h:h + 1, :]).astype(BF16) for h in range(M_HEADS)]
        gated.append((vts, sqks, kvws))
    probs, maxes = [], []
    for blk, s in enumerate(scores_a):
        s_prev = s[:, ln:]
        if blk == 0:
            s_prev = s_prev + no_prev
        sc = jnp.where(own, s[:, :ln], s_prev)
        mx = jnp.maximum(jnp.max(sc, axis=-1, keepdims=True), sink)
        p = jnp.exp(sc - mx)
        zero = jnp.zeros_like(p)
        probs.append(jnp.concatenate([jnp.where(own, p, zero), jnp.where(own, zero, p)], axis=1).astype(BF16))
        maxes.append(mx)
    om = proj(OFF_OM, M_WIDTH)

    mem = [jnp.concatenate([c_ref[2 * p], c_ref[2 * p + 1]], axis=1) for p in range(M_PAIRS)]
    pair_row = lambda x, p: jnp.concatenate([x[2 * p:2 * p + 1, :], x[2 * p + 1:2 * p + 2, :]], axis=1)
    for rows, (vts, sqks, kvws), (_, w_inter, e_negm, dec, _), p2, v_aug, mx in zip(
            blocks, gated, scal, probs, v_augs, maxes):
        pairs = range(M_PAIRS)
        upd = [jnp.dot(jnp.concatenate([kvws[2 * p], kvws[2 * p + 1]], axis=1),
                       _block_diag(km[rows, hcols[2 * p]], km[rows, hcols[2 * p + 1]]),
                       preferred_element_type=F32) for p in pairs]
        intra = [jnp.dot(jnp.concatenate([vts[2 * p], vts[2 * p + 1]], axis=1),
                         _block_diag(sqks[2 * p], sqks[2 * p + 1]), preferred_element_type=F32) for p in pairs]
        inter = [jnp.dot(mem[p].astype(BF16), _block_diag(qt[hcols[2 * p], rows], qt[hcols[2 * p + 1], rows]),
                         preferred_element_type=F32) for p in pairs]
        o = jnp.dot(p2, v_aug, preferred_element_type=F32)
        o = o[:, :LANES] * (1.0 / (o[:, LANES:] + jnp.exp(sink - mx)))
        for j, tile in enumerate(_unstack_heads(o, ln)):
            mix_ref[rows, j * LANES:(j + 1) * LANES] = tile.astype(mix_ref.dtype)
        for p in pairs:
            num = inter[p] * pair_row(w_inter, p) + intra[p]
            mem[p] = pair_row(dec, p) * mem[p] + upd[p]
            den = jnp.maximum(jnp.abs(num[M_HEAD_DIM:M_HEAD_DIM + 1, :]), pair_row(e_negm, p))
            hh = num[:M_HEAD_DIM, :] * (1.0 / den)
            y = hh * lax.rsqrt(jnp.mean(hh * hh, axis=0, keepdims=True) + EPS)
            for h in (2 * p, 2 * p + 1):
                y_h = y[:, (h % 2) * ln:(h % 2 + 1) * ln]
                mcols = slice(ATTN_WIDTH + h * M_HEAD_DIM, ATTN_WIDTH + (h + 1) * M_HEAD_DIM)
                mix_ref[rows, mcols] = (jax.nn.sigmoid(om[rows, hcols[h]])
                                        * (y_h.T * mnorm_ref[:, hcols[h]])).astype(mix_ref.dtype)
    for p in range(M_PAIRS):
        c_ref[2 * p] = mem[p][:, :M_HEAD_DIM]
        c_ref[2 * p + 1] = mem[p][:, M_HEAD_DIM:]


def _out_row_block(i):
    per_head = (ATTN_WIDTH // ATTN_HEADS) // WEIGHT_SLAB
    j, part = i // per_head, i % per_head
    head = (j % 2) * GROUP + j // 2
    return jnp.where(i < ATTN_HEADS * per_head, head * per_head + part, i)


def _mixer_prompt(sink, b_i, b_f, x, g_pre, w_in_t, mnorm, w_gate, w_up, w_down, w_out):
    m = x.shape[0]
    tm = TOKEN_TILE
    steps = m // tm
    assert D_MODEL == steps * WEIGHT_SLAB and D_FF % (steps // 2) == 0 and (D_FF // (steps // 2)) % BF16_SUBLANES == 0
    down_slab = D_FF // (steps // 2)
    row = lambda n: pl.BlockSpec((tm, n), lambda i: (i, 0))
    whole = lambda shape: pl.BlockSpec(shape, lambda i: (0,) * len(shape))
    smem = pl.BlockSpec(memory_space=pltpu.SMEM)
    slab = lambda n: pl.BlockSpec((WEIGHT_SLAB, n), lambda i: (i, 0))
    down = pl.BlockSpec((down_slab, D_MODEL), lambda i: (i // 2, 0))
    c_shape = (M_HEADS, AUG_ROWS, M_HEAD_DIM)
    w_shape = (WINDOW, KV_WIDTH)
    s_shape = (SUBLANES, LANES)
    sds = jax.ShapeDtypeStruct
    return pl.pallas_call(
        _mixer_prompt_kernel,
        grid=(steps,),
        in_specs=[smem, smem, smem, row(D_MODEL), _const_spec((1, D_MODEL)), _const_spec(w_in_t.shape),
                  _const_spec((1, M_WIDTH)), slab(D_FF), slab(D_FF), down,
                  pl.BlockSpec((WEIGHT_SLAB, D_MODEL), lambda i: (_out_row_block(i), 0))],
        out_specs=[row(ATTN_WIDTH + M_WIDTH), whole(w_shape), whole(w_shape), whole(c_shape), whole(s_shape),
                   whole((D_MODEL, IN_PAD)), slab(D_FF), slab(D_FF), down, slab(D_MODEL)],
        out_shape=[sds((m, ATTN_WIDTH + M_WIDTH), BF16), sds(w_shape, F32), sds(w_shape, F32), sds(c_shape, F32),
                   sds(s_shape, F32), sds((D_MODEL, IN_PAD), BF16), sds(w_gate.shape, BF16), sds(w_up.shape, BF16),
                   sds(w_down.shape, BF16), sds(w_out.shape, BF16)],
        compiler_params=_params(("arbitrary",)),
        name="mixer_prompt",
    )(sink, b_i, b_f, x, g_pre, w_in_t, mnorm, w_gate, w_up, w_down, w_out)


def _gates(g_blk, bias_row, cum):
    pre = g_blk + bias_row
    lane = lax.broadcasted_iota(jnp.int32, pre.shape, 1)
    a = jnp.where(lane < M_HEADS, pre, jax.nn.log_sigmoid(pre))
    b = jnp.dot(cum, a, precision=HIGHEST, preferred_element_type=F32)
    return a, b, a.T, b.T


def _col(x, j):
    return jnp.broadcast_to(x[:, j:j + 1], x.shape)


def _mlstm_gated_scores(q, k, a, b, at, bt, h, mask, m_prev, last):
    bc, ic = _col(b, M_HEADS + h), _col(a, h)
    br, ir = bt[M_HEADS + h:M_HEADS + h + 1, :], at[h:h + 1, :]
    d = jnp.where(mask, (bc - br) + ir, -jnp.inf)
    inter = bc + m_prev
    m_t = jnp.maximum(inter, jnp.max(d, axis=-1, keepdims=True))
    w_inter = jnp.exp(inter - m_t)
    sqk = jnp.exp(d - m_t) * lax.dot_general(q, k, (((1,), (1,)), ((), ())), preferred_element_type=F32)
    m_end = last(m_t)
    bl = last(bc)
    dec = jnp.exp(bl + m_prev - m_end)
    w_end = jnp.exp((bl - bc) + ic - m_end)
    kw = k.astype(F32) * w_end
    return w_inter, sqk, m_t, m_end, dec, kw


def _mlstm_sample_kernel(q_ref, k_ref, v_ref, om_ref, g_ref, m0_ref, bias_ref, mnorm_ref, c_ref, n_ref,
                         o_ref, c_out, n_out, m_out, *, dec_seq):
    t = dec_seq
    bb = SAMPLE_MLSTM_BATCH
    ln = bb * t
    r = lax.broadcasted_iota(jnp.int32, (ln, ln), 0)
    c = lax.broadcasted_iota(jnp.int32, (ln, ln), 1)
    shift = t.bit_length() - 1
    same = (r >> shift) == (c >> shift)
    mask = jnp.logical_and(same, c <= r)
    cum = mask.astype(F32)
    expand = (c == (r >> shift)).astype(F32)
    gather = ((c >> shift) == r).astype(F32)
    is_last = (r & (t - 1)) == t - 1
    is_first = (r & (t - 1)) == 0

    def last(x):
        y = jnp.where(is_last, x, 0.0)
        step = 1
        while step < t:
            y = y + pltpu.roll(y, ln - step, axis=0)
            step *= 2
        return y

    a, b, at, bt = _gates(g_ref[...], bias_ref[...], cum)
    m0 = m0_ref[...]
    lane = lax.broadcasted_iota(jnp.int32, (ln, LANES), 1)
    zpad = jnp.zeros((ln - bb, M_HEAD_DIM), F32)
    heads = range(M_HEADS)
    hcols = [slice(h * M_HEAD_DIM, (h + 1) * M_HEAD_DIM) for h in heads]
    qfs = [q_ref[:, hc] for hc in hcols]
    vs = [v_ref[:, hc].astype(BF16) for hc in hcols]
    qcs = [jnp.concatenate(
        [jnp.dot(qfs[h][s * t:(s + 1) * t, :], c_ref[s * M_HEADS + h].astype(BF16).astype(F32),
                 preferred_element_type=F32) for s in range(bb)], axis=0) for h in heads]
    n_exps = [jnp.dot(expand, jnp.concatenate([n_ref[:, hc], zpad], axis=0), precision=HIGHEST,
                      preferred_element_type=F32) for hc in hcols]
    parts = [_mlstm_gated_scores(qfs[h].astype(BF16), k_ref[:, hcols[h]].astype(BF16), a, b, at, bt, h, mask,
                                 _col(m0, h), last) for h in heads]
    intras = [jnp.dot(parts[h][1].astype(BF16), vs[h], preferred_element_type=F32) for h in heads]
    m_cols = jnp.zeros((ln, LANES), F32)
    kwts = []
    for h in heads:
        w_inter, sqk, m_t, m_end, dec, kw = parts[h]
        num = w_inter * qcs[h] + intras[h]
        nq = w_inter * jnp.sum(qfs[h] * n_exps[h], axis=-1, keepdims=True) + jnp.sum(sqk, axis=-1, keepdims=True)
        hh = num / jnp.maximum(jnp.abs(nq), jnp.exp(-m_t))
        o_ref[:, hcols[h]] = _head_out(hh, mnorm_ref[:, hcols[h]], om_ref[:, hcols[h]])
        kwts.append(kw.T.astype(BF16))
        m_cols = jnp.where(lane == h, m_end, m_cols)
    m_out[...] = m_cols
    for h in heads:
        dec = parts[h][4]
        for s in range(bb):
            lhs = jnp.where((c >> shift) == s, kwts[h], jnp.zeros_like(kwts[h]))
            upd = jnp.dot(lhs, vs[h], preferred_element_type=F32)
            c_out[s * M_HEADS + h] = dec[s * t:s * t + 1, :] * c_ref[s * M_HEADS + h] + upd
    for h in heads:
        dec, kw = parts[h][4], parts[h][5]
        n_new = jnp.dot(gather, jnp.where(is_first, dec * n_exps[h], 0.0) + kw, precision=HIGHEST,
                        preferred_element_type=F32)
        n_out[:, hcols[h]] = n_new[:bb, :]


def _mlstm_sample(qm, km, vm, om, gt, m0_rows, bias_row, mnorm, c_in, n_in, dec_seq):
    m = qm.shape[0]
    bb = SAMPLE_MLSTM_BATCH
    tm = bb * dec_seq
    nb = m // dec_seq
    row = lambda n: pl.BlockSpec((tm, n), lambda i: (i, 0))
    whole = lambda shape: pl.BlockSpec(shape, lambda i: (0,) * len(shape))
    c_spec = pl.BlockSpec((bb * M_HEADS, M_HEAD_DIM, M_HEAD_DIM), lambda i: (i, 0, 0))
    n_spec = pl.BlockSpec((bb, M_WIDTH), lambda i: (i, 0))
    return pl.pallas_call(
        functools.partial(_mlstm_sample_kernel, dec_seq=dec_seq),
        grid=(nb // bb,),
        in_specs=[row(M_WIDTH), row(M_WIDTH), row(M_WIDTH), row(M_WIDTH), row(GATE_PAD), row(LANES),
                  whole((1, GATE_PAD)), whole((1, M_WIDTH)), c_spec, n_spec],
        out_specs=[row(M_WIDTH), c_spec, n_spec, row(LANES)],
        out_shape=[jax.ShapeDtypeStruct((m, M_WIDTH), BF16), jax.ShapeDtypeStruct(c_in.shape, F32),
                   jax.ShapeDtypeStruct(n_in.shape, F32), jax.ShapeDtypeStruct((m, LANES), F32)],
        compiler_params=_params(("parallel",)),
        name="mlstm_sample",
    )(qm, km, vm, om, gt, m0_rows, bias_row, mnorm, c_in, n_in)


def _out_ffn_kernel(x_ref, mix_ref, wo_ref, g1_ref, g2_ref, wg_ref, wu_ref, wd_ref, g3_ref, o_ref):
    tm = x_ref.shape[0]
    groups = [slice(r, r + tm // ROW_GROUPS) for r in range(0, tm, tm // ROW_GROUPS)]
    ys = [jnp.dot(mix_ref[rs, :], wo_ref[...], preferred_element_type=F32) for rs in groups]
    x1s = [x_ref[rs, :] + _rms(y, g1_ref[...]) for rs, y in zip(groups, ys)]
    fs = [_rms(x1, g2_ref[...]).astype(BF16) for x1 in x1s]
    accs = [None] * ROW_GROUPS
    for off in range(0, D_FF, FFN_CHUNK):
        acts = []
        for f in fs:
            g = jnp.dot(f, wg_ref[:, off:off + FFN_CHUNK], preferred_element_type=F32)
            u = jnp.dot(f, wu_ref[:, off:off + FFN_CHUNK], preferred_element_type=F32)
            acts.append((g * jax.nn.sigmoid(g) * u).astype(BF16))
        for i, act in enumerate(acts):
            part = jnp.dot(act, wd_ref[off:off + FFN_CHUNK, :], preferred_element_type=F32)
            accs[i] = part if accs[i] is None else accs[i] + part
    for rs, x1, acc in zip(groups, x1s, accs):
        o_ref[rs, :] = x1 + _rms(acc, g3_ref[...])


def _out_ffn(x, mix, w_out, g_post_mix, g_pre_ffn, w_gate, w_up, w_down, g_post_ffn):
    m = x.shape[0]
    tm = TOKEN_TILE
    row = lambda n: pl.BlockSpec((tm, n), lambda i: (i, 0))
    vec = _const_spec((1, D_MODEL))
    return pl.pallas_call(
        _out_ffn_kernel,
        grid=(m // tm,),
        in_specs=[row(D_MODEL), row(ATTN_WIDTH + M_WIDTH), _const_spec((D_MODEL, D_MODEL)), vec, vec,
                  _const_spec((D_MODEL, D_FF)), _const_spec((D_MODEL, D_FF)), _const_spec((D_FF, D_MODEL)), vec],
        out_specs=row(D_MODEL),
        out_shape=jax.ShapeDtypeStruct((m, D_MODEL), F32),
        compiler_params=_params(("parallel",)),
        name="out_ffn",
    )(x, mix, w_out, g_post_mix, g_pre_ffn, w_gate, w_up, w_down, g_post_ffn)


def _layer(xp, xs, cache_k, cache_v, state_c, state_n, state_m, w_in, b_i, b_f, attn_sink, m_norm, w_out,
           g_pre_mix, g_post_mix, g_pre_ffn, g_post_ffn, w_gate, w_up, w_down):
    bp, sp, _ = xp.shape
    bs, ts, _ = xs.shape
    assert bp == 1 and sp % TOKEN_TILE == 0 and TOKEN_TILE % WINDOW == 0
    assert ts & (ts - 1) == 0 and (bs * ts) % TOKEN_TILE == 0 and bs % SAMPLE_MLSTM_BATCH == 0

    row = lambda v: v.reshape(1, -1)
    bias_row = jnp.pad(jnp.concatenate([b_i, b_f]), (0, GATE_PAD - 2 * M_HEADS)).reshape(1, GATE_PAD)
    sink = row(attn_sink)

    x2 = xp.reshape(sp, D_MODEL)
    mix, k_w, v_w, c_aug, m_p, w_pad, wg, wu, wd, wo = _mixer_prompt(
        sink, row(b_i), row(b_f), x2, row(g_pre_mix), w_in.T, row(m_norm), w_gate, w_up, w_down, w_out)
    ffn = (wo, row(g_post_mix), row(g_pre_ffn), wg, wu, wd, row(g_post_ffn))
    yp = _out_ffn(x2, mix, *ffn).reshape(xp.shape)
    k_p = k_w.reshape(1, WINDOW, KV_HEADS, HEAD_DIM)
    v_p = v_w.reshape(1, WINDOW, KV_HEADS, HEAD_DIM)
    c_p = jnp.swapaxes(c_aug[:, :M_HEAD_DIM, :], 1, 2).reshape(1, M_HEADS, M_HEAD_DIM, M_HEAD_DIM)
    n_p = c_aug[:, M_HEAD_DIM, :].reshape(1, M_HEADS, M_HEAD_DIM)
    m_p = m_p[:M_HEADS, 0].reshape(1, M_HEADS)

    x2 = xs.reshape(bs * ts, D_MODEL)
    qa, ka, va, qm, km, vm, om, gt = _inproj(x2, row(g_pre_mix), w_pad)
    feature_major = lambda c: jnp.transpose(c, (0, 2, 3, 1))
    mix_a, k_s, v_s = _attn_sample(sink, qa, ka, va, feature_major(cache_k), feature_major(cache_v), ts)
    m0_rows = jnp.pad(jnp.repeat(state_m, ts, axis=0), ((0, 0), (0, LANES - M_HEADS)))
    mix_m, c_s, n_s, m_rows = _mlstm_sample(qm, km, vm, om, gt, m0_rows, bias_row, row(m_norm),
                                            state_c.reshape(bs * M_HEADS, M_HEAD_DIM, M_HEAD_DIM),
                                            state_n.reshape(bs, M_WIDTH), ts)
    ys = _out_ffn(x2, jnp.concatenate([mix_a, mix_m], axis=1), *ffn).reshape(xs.shape)
    k_s = jnp.transpose(k_s, (0, 3, 1, 2))
    v_s = jnp.transpose(v_s, (0, 3, 1, 2))
    c_s = c_s.reshape(bs, M_HEADS, M_HEAD_DIM, M_HEAD_DIM)
    n_s = n_s.reshape(bs, M_HEADS, M_HEAD_DIM)
    m_s = m_rows[ts - 1::ts, :M_HEADS]
    return yp, ys, (k_p, v_p, c_p, n_p, m_p), (k_s, v_s, c_s, n_s, m_s)


def kernel(x_prompt, x_sample, cache_k, cache_v, state_C, state_n, state_m, w_in, b_i, b_f, attn_sink, m_norm,
           w_out, g_pre_mix, g_post_mix, g_pre_ffn, g_post_ffn, w_gate, w_up, w_down):
    depth = w_in.shape[0]
    xp, xs = x_prompt, x_sample
    prompt_states, sample_states = [], []
    for l in range(depth):
        xp, xs, st_p, st_s = _layer(xp, xs, cache_k[l], cache_v[l], state_C[l], state_n[l], state_m[l],
                                    w_in[l], b_i[l], b_f[l], attn_sink[l], m_norm[l], w_out[l],
                                    g_pre_mix[l], g_post_mix[l], g_pre_ffn[l], g_post_ffn[l],
                                    w_gate[l], w_up[l], w_down[l])
        prompt_states.append(st_p)
        sample_states.append(st_s)
    stack = lambda states, i: jnp.stack([s[i] for s in states], axis=0)
    return (xp, xs) + tuple(stack(prompt_states, i) for i in range(5)) + tuple(stack(sample_states, i) for i in range(5))
```

```python
import functools

import jax
import jax.numpy as jnp
from jax import lax
from jax.experimental import pallas as pl
from jax.experimental.pallas import tpu as pltpu

F32 = jnp.float32
BF16 = jnp.bfloat16
HIGHEST = lax.Precision.HIGHEST

D_MODEL = 1024
HEAD_DIM = 64
ATTN_HEADS = 8
KV_HEADS = 2
GROUP = ATTN_HEADS // KV_HEADS
ATTN_WIDTH = ATTN_HEADS * HEAD_DIM
KV_WIDTH = KV_HEADS * HEAD_DIM
WINDOW = 128
M_HEADS = 4
M_HEAD_DIM = 128
M_WIDTH = M_HEADS * M_HEAD_DIM
M_PAIRS = M_HEADS // 2
D_FF = 2816
EPS = 1e-6

LANES = 128
SUBLANES = 8
GATE_PAD = LANES
BF16_SUBLANES = 16
AUG_ROWS = M_HEAD_DIM + BF16_SUBLANES
IN_MAIN = ATTN_WIDTH + 2 * KV_WIDTH + 4 * M_WIDTH
IN_PAD = IN_MAIN + GATE_PAD
VMEM_LIMIT = 56 * 1024 * 1024

HEAD_ORDER = tuple(h for j in range(GROUP) for h in (j, j + GROUP))

TOKEN_TILE = 512
FFN_TOKEN_TILE = 1024
FFN_CHUNK = 256
ROW_GROUPS = 2
WEIGHT_SLAB = 32
SAMPLE_ATTN_BATCH = 16
SAMPLE_MLSTM_BATCH = 16


def _rms(x, g):
    return x * lax.rsqrt(jnp.mean(x * x, axis=-1, keepdims=True) + EPS) * g


def _const_spec(shape):
    nd = len(shape)
    return pl.BlockSpec(shape, lambda i: (0,) * nd, pipeline_mode=pl.Buffered(1))


def _params(semantics):
    return pltpu.CompilerParams(dimension_semantics=semantics, vmem_limit_bytes=VMEM_LIMIT)


OFF_QA, OFF_KA, OFF_VA = 0, ATTN_WIDTH, ATTN_WIDTH + KV_WIDTH
OFF_QM = ATTN_WIDTH + 2 * KV_WIDTH
OFF_KM, OFF_VM, OFF_OM = OFF_QM + M_WIDTH, OFF_QM + 2 * M_WIDTH, OFF_QM + 3 * M_WIDTH


def _inproj_kernel(x_ref, g_ref, w_ref, qa_ref, ka_ref, va_ref, qm_ref, km_ref, vm_ref, om_ref, gt_ref):
    tm = x_ref.shape[0]
    step = tm // ROW_GROUPS
    for r0 in range(0, tm, step):
        rs = slice(r0, r0 + step)
        h = _rms(x_ref[rs, :], g_ref[...]).astype(BF16)

        def proj(off, n):
            return jnp.dot(h, w_ref[:, off:off + n], preferred_element_type=F32)

        qa_ref[rs, :] = proj(OFF_QA, ATTN_WIDTH) * (HEAD_DIM ** -0.5)
        ka_ref[rs, :] = proj(OFF_KA, KV_WIDTH)
        va_ref[rs, :] = proj(OFF_VA, KV_WIDTH)
        qm_ref[rs, :] = proj(OFF_QM, M_WIDTH)
        km_ref[rs, :] = proj(OFF_KM, M_WIDTH) * (M_HEAD_DIM ** -0.5)
        vm_ref[rs, :] = proj(OFF_VM, M_WIDTH)
        om_ref[rs, :] = proj(OFF_OM, M_WIDTH)
        gt_ref[rs, :] = proj(IN_MAIN, GATE_PAD)


def _inproj(x, g_pre, w_pad):
    m = x.shape[0]
    tm = TOKEN_TILE
    row = lambda n: pl.BlockSpec((tm, n), lambda i: (i, 0))
    widths = (ATTN_WIDTH, KV_WIDTH, KV_WIDTH, M_WIDTH, M_WIDTH, M_WIDTH, M_WIDTH, GATE_PAD)
    return pl.pallas_call(
        _inproj_kernel,
        grid=(m // tm,),
        in_specs=[row(D_MODEL), _const_spec((1, D_MODEL)), _const_spec((D_MODEL, IN_PAD))],
        out_specs=[row(n) for n in widths],
        out_shape=[jax.ShapeDtypeStruct((m, n), F32) for n in widths],
        compiler_params=_params(("parallel",)),
        name="inproj",
    )(x, g_pre, w_pad)


def _stack_heads(q_tiles):
    lane = lax.broadcasted_iota(jnp.int32, q_tiles[0].shape, 1)
    lo = lane < HEAD_DIM
    zero = jnp.zeros_like(q_tiles[0])
    parts = []
    for qt in q_tiles:
        parts += [jnp.where(lo, qt, zero), jnp.where(lo, zero, qt)]
    return jnp.concatenate(parts, axis=0)


def _unstack_heads(o, rows):
    lane = lax.broadcasted_iota(jnp.int32, (rows, LANES), 1)
    lo = lane < HEAD_DIM
    return [jnp.where(lo, o[(2 * j) * rows:(2 * j + 1) * rows, :], o[(2 * j + 1) * rows:(2 * j + 2) * rows, :])
            for j in range(GROUP)]


def _sink_rows(sink_ref, rows):
    return jnp.concatenate([jnp.full((rows, LANES), sink_ref[0, h], F32) for h in HEAD_ORDER], axis=0)


def _attn_sample_kernel(sink_ref, q_ref, kn_ref, vn_ref, ck_ref, cv_ref, o_ref, ko_ref, vo_ref, *, dec_seq):
    t = dec_seq
    bb = SAMPLE_ATTN_BATCH
    nrow = ATTN_HEADS * t
    ts, ns = t.bit_length() - 1, nrow.bit_length() - 1
    r_c = lax.broadcasted_iota(jnp.int32, (nrow, WINDOW), 0) & (t - 1)
    c_c = lax.broadcasted_iota(jnp.int32, (nrow, WINDOW), 1)
    vis_cache = c_c > r_c
    r_n = lax.broadcasted_iota(jnp.int32, (bb * nrow, bb * t), 0)
    c_n = lax.broadcasted_iota(jnp.int32, (bb * nrow, bb * t), 1)
    vis_new = jnp.logical_and((r_n >> ns) == (c_n >> ts), (c_n & (t - 1)) <= (r_n & (t - 1)))
    sink = _sink_rows(sink_ref, t)
    nt = (((1,), (1,)), ((), ()))
    kn_all, vn_all = kn_ref[...], vn_ref[...]
    lane_w = lax.broadcasted_iota(jnp.int32, (KV_WIDTH, WINDOW), 1)
    is_new = lane_w >= WINDOW - t
    zero_rows = jnp.zeros((WINDOW - t, KV_WIDTH), F32)
    state_shape = (KV_HEADS, HEAD_DIM, WINDOW)

    def slide(old_t, new_rows):
        new_t = jnp.concatenate([zero_rows, new_rows], axis=0).T
        return jnp.where(is_new, new_t, pltpu.roll(old_t, WINDOW - t, axis=1)).reshape(state_shape)

    qs, s_c, cvs = [], [], []
    for b in range(bb):
        rows = slice(b * t, (b + 1) * t)
        ck = ck_ref[b].reshape(KV_WIDTH, WINDOW)
        cvs.append(cv_ref[b].reshape(KV_WIDTH, WINDOW))
        ko_ref[b] = slide(ck, kn_all[rows, :])
        vo_ref[b] = slide(cvs[b], vn_all[rows, :])
        qs.append(_stack_heads([q_ref[rows, j * LANES:(j + 1) * LANES] for j in range(GROUP)]).astype(BF16))
        s_c.append(jnp.where(vis_cache, jnp.dot(qs[b], ck.astype(BF16), preferred_element_type=F32), -jnp.inf))
    s_n = jnp.where(vis_new, lax.dot_general(jnp.concatenate(qs, axis=0), kn_all.astype(BF16), nt,
                                             preferred_element_type=F32), -jnp.inf)
    p_c, p_n, rden = [], [], []
    for b in range(bb):
        s_nb = s_n[b * nrow:(b + 1) * nrow, :]
        m = jnp.maximum(jnp.maximum(jnp.max(s_c[b], axis=-1, keepdims=True), jnp.max(s_nb, axis=-1, keepdims=True)),
                        sink)
        p_c.append(jnp.exp(s_c[b] - m))
        p_n.append(jnp.exp(s_nb - m[:, :bb * t]))
        rden.append(1.0 / (jnp.sum(p_c[b], axis=-1, keepdims=True) + jnp.sum(p_n[b], axis=-1, keepdims=True)
                           + jnp.exp(sink - m)))
    o_n = jnp.dot(jnp.concatenate(p_n, axis=0).astype(BF16), vn_all.astype(BF16), preferred_element_type=F32)
    outs = [[] for _ in range(GROUP)]
    for b in range(bb):
        o = (lax.dot_general(p_c[b].astype(BF16), cvs[b].astype(BF16), nt, preferred_element_type=F32)
             + o_n[b * nrow:(b + 1) * nrow, :]) * rden[b]
        for j, tile in enumerate(_unstack_heads(o, t)):
            outs[j].append(tile)
    for j, parts in enumerate(outs):
        o_ref[:, j * LANES:(j + 1) * LANES] = jnp.concatenate(parts, axis=0).astype(o_ref.dtype)


def _attn_sample(sink, qa, ka, va, cache_k, cache_v, dec_seq):
    nb = cache_k.shape[0]
    bb = SAMPLE_ATTN_BATCH
    row = lambda n: pl.BlockSpec((bb * dec_seq, n), lambda i: (i, 0))
    cache = pl.BlockSpec((bb, KV_HEADS, HEAD_DIM, WINDOW), lambda i: (i, 0, 0, 0))
    return pl.pallas_call(
        functools.partial(_attn_sample_kernel, dec_seq=dec_seq),
        grid=(nb // bb,),
        in_specs=[pl.BlockSpec(memory_space=pltpu.SMEM), row(ATTN_WIDTH), row(KV_WIDTH), row(KV_WIDTH), cache, cache],
        out_specs=[row(ATTN_WIDTH), cache, cache],
        out_shape=[jax.ShapeDtypeStruct((nb * dec_seq, ATTN_WIDTH), BF16),
                   jax.ShapeDtypeStruct(cache_k.shape, F32), jax.ShapeDtypeStruct(cache_v.shape, F32)],
        compiler_params=_params(("parallel",)),
        name="attn_sample",
    )(sink, qa, ka, va, cache_k, cache_v)


def _block_diag(a, b):
    za, zb = jnp.zeros_like(a), jnp.zeros_like(b)
    return jnp.concatenate([jnp.concatenate([a, zb], axis=1), jnp.concatenate([za, b], axis=1)], axis=0)


def _head_out(hh, mnorm_row, om):
    y = hh * lax.rsqrt(jnp.mean(hh * hh, axis=-1, keepdims=True) + EPS) * mnorm_row
    return (jax.nn.sigmoid(om) * y).astype(BF16)


def _mixer_prompt_kernel(sink_ref, bi_ref, bf_ref, x_ref, g_ref, win_ref, mnorm_ref, wg_ref, wu_ref, wd_ref, wo_ref,
                         mix_ref, kw_ref, vw_ref, c_ref, m_ref, w_ref, wgb_ref, wub_ref, wdb_ref, wob_ref):
    step = pl.program_id(0)

    @pl.when(step == 0)
    def _():
        kw_ref[...] = jnp.zeros_like(kw_ref)
        vw_ref[...] = jnp.zeros_like(vw_ref)
        c_ref[...] = jnp.zeros_like(c_ref)
        m_ref[...] = jnp.zeros_like(m_ref)
        for j in range(GROUP):
            lo = win_ref[j * HEAD_DIM:(j + 1) * HEAD_DIM, :]
            hi = win_ref[(j + GROUP) * HEAD_DIM:(j + GROUP + 1) * HEAD_DIM, :]
            w_ref[:, j * LANES:(j + 1) * LANES] = jnp.concatenate([lo, hi], axis=0).T.astype(BF16)
        for c0 in range(ATTN_WIDTH, IN_MAIN, M_WIDTH):
            c1 = min(c0 + M_WIDTH, IN_MAIN)
            w_ref[:, c0:c1] = win_ref[c0:c1, :].T.astype(BF16)
        gw = jnp.concatenate([win_ref[IN_MAIN:IN_MAIN + 2 * M_HEADS, :],
                              jnp.zeros((GATE_PAD - 2 * M_HEADS, D_MODEL), F32)], axis=0)
        w_ref[:, IN_MAIN:] = gw.T.astype(BF16)

    wgb_ref[...] = wg_ref[...].astype(BF16)
    wub_ref[...] = wu_ref[...].astype(BF16)
    wdb_ref[...] = wd_ref[...].astype(BF16)
    wob_ref[...] = wo_ref[...].astype(BF16)

    tm = x_ref.shape[0]
    ln = WINDOW
    groups = [slice(r0, r0 + tm // ROW_GROUPS) for r0 in range(0, tm, tm // ROW_GROUPS)]
    blocks = [slice(r0, r0 + ln) for r0 in range(0, tm, ln)]
    hcols = [slice(h * M_HEAD_DIM, (h + 1) * M_HEAD_DIM) for h in range(M_HEADS)]
    pcols = [slice(2 * p * M_HEAD_DIM, 2 * (p + 1) * M_HEAD_DIM) for p in range(M_PAIRS)]
    nt = (((1,), (1,)), ((), ()))
    hs = [_rms(x_ref[rs, :], g_ref[...]).astype(BF16) for rs in groups]

    def proj(off, n):
        return jnp.concatenate([jnp.dot(h, w_ref[:, off:off + n], preferred_element_type=F32) for h in hs], axis=0)

    qt = proj(OFF_QM, M_WIDTH).T.astype(BF16)
    km = (proj(OFF_KM, M_WIDTH) * (M_HEAD_DIM ** -0.5)).astype(BF16)
    vt = proj(OFF_VM, M_WIDTH).T.astype(BF16)
    gates = proj(IN_MAIN, GATE_PAD).T
    r = lax.broadcasted_iota(jnp.int32, (ln, ln), 0)
    c = lax.broadcasted_iota(jnp.int32, (ln, ln), 1)
    causal_t = r <= c
    upper = causal_t.astype(F32)
    lane8 = lax.broadcasted_iota(jnp.int32, (SUBLANES, ln), 1)
    ones_rows = jnp.ones((AUG_ROWS - M_HEAD_DIM, ln), BF16)
    zrows = jnp.zeros((ln - SUBLANES, ln), F32)
    row8 = lax.broadcasted_iota(jnp.int32, (SUBLANES, ln), 0)
    bias = jnp.zeros((SUBLANES, ln), F32)
    for h in range(M_HEADS):
        bias = jnp.where(row8 == h, bi_ref[0, h], jnp.where(row8 == M_HEADS + h, bf_ref[0, h], bias))
    pre, scores_m = [], []
    for rows in blocks:
        gi = gates[0:SUBLANES, rows] + bias
        b = pltpu.roll(jnp.dot(jax.nn.log_sigmoid(gi), upper, precision=HIGHEST, preferred_element_type=F32),
                       M_HEADS, axis=0)
        g = gi - b
        cm0 = g
        sh = 1
        while sh < ln:
            cm0 = jnp.maximum(cm0, jnp.where(lane8 >= sh, pltpu.roll(cm0, sh, axis=1), -jnp.inf))
            sh *= 2
        b_last = jnp.broadcast_to(b[:, ln - 1:ln], b.shape)
        g_max = jnp.broadcast_to(cm0[:, ln - 1:ln], b.shape)
        g_cols = jnp.concatenate([g, zrows], axis=0).T
        pre.append((b, g_cols, cm0, b_last, g_max, (b_last - b) + gi))
        scores_m.append([jnp.dot(km[rows, pc], _block_diag(qt[hcols[2 * p], rows], qt[hcols[2 * p + 1], rows]),
                                 preferred_element_type=F32) for p, pc in enumerate(pcols)])

    qa = (proj(OFF_QA, ATTN_WIDTH) * (HEAD_DIM ** -0.5)).astype(BF16)
    kv = proj(OFF_KA, 2 * KV_WIDTH)
    ka, va = kv[:, :KV_WIDTH], kv[:, KV_WIDTH:]
    nrow = ATTN_HEADS * ln
    ra = lax.broadcasted_iota(jnp.int32, (nrow, ln), 0) & (ln - 1)
    ca = lax.broadcasted_iota(jnp.int32, (nrow, ln), 1)
    own = ca <= ra
    sink = _sink_rows(sink_ref, ln)
    no_prev = jnp.where(step == 0, -jnp.inf, 0.0)
    ones = jnp.ones((2 * ln, LANES), BF16)
    scores_a, v_augs = [], []
    for blk, rows in enumerate(blocks):
        if blk == 0:
            k_prev, v_prev = kw_ref[...], vw_ref[...]
        else:
            k_prev, v_prev = ka[blocks[blk - 1], :], va[blocks[blk - 1], :]
        kcat = jnp.concatenate([ka[rows, :], k_prev], axis=0).astype(BF16)
        vcat = jnp.concatenate([va[rows, :], v_prev], axis=0).astype(BF16)
        v_augs.append(jnp.concatenate([vcat, ones], axis=1))
        qs = _stack_heads([qa[rows, j * LANES:(j + 1) * LANES] for j in range(GROUP)])
        scores_a.append(lax.dot_general(qs, kcat, nt, preferred_element_type=F32))
    kw_ref[...] = ka[blocks[-1], :]
    vw_ref[...] = va[blocks[-1], :]

    m_prev = m_ref[...]
    scal = []
    for b, _, cm0, b_last, g_max, w_end_arg in pre:
        cm = jnp.maximum(cm0, m_prev)
        m_end = b_last + jnp.maximum(g_max, m_prev)
        scal.append((cm, jnp.exp(m_prev - cm), jnp.exp(-(b + cm)), jnp.exp(b_last + m_prev - m_end),
                     jnp.exp(w_end_arg - m_end)))
        m_prev = m_end
    m_ref[...] = m_prev
    gated = []
    for rows, s_t, (_, g_cols, *_), (cm, _, _, _, w_end) in zip(blocks, scores_m, pre, scal):
        vts = [jnp.concatenate([vt[hc, rows], ones_rows], axis=0) for hc in hcols]
        sqks = [(jnp.exp(jnp.where(causal_t, g_cols[:, h:h + 1] - cm[h:h + 1, :], -jnp.inf))
                 * s_t[h // 2][:, (h % 2) * ln:(h % 2 + 1) * ln]).astype(BF16) for h in range(M_HEADS)]
        kvws = [(vts[h].astype(F32) * w_end[h:h + 1, :]).astype(BF16) for h in range(M_HEADS)]
        gated.append((vts, sqks, kvws))
    probs, maxes = [], []
    for blk, s in enumerate(scores_a):
        s_prev = s[:, ln:]
        if blk == 0:
            s_prev = s_prev + no_prev
        sc = jnp.where(own, s[:, :ln], s_prev)
        mx = jnp.maximum(jnp.max(sc, axis=-1, keepdims=True), sink)
        p = jnp.exp(sc - mx)
        zero = jnp.zeros_like(p)
        probs.append(jnp.concatenate([jnp.where(own, p, zero), jnp.where(own, zero, p)], axis=1).astype(BF16))
        maxes.append(mx)
    om = proj(OFF_OM, M_WIDTH)

    mem = [jnp.concatenate([c_ref[2 * p], c_ref[2 * p + 1]], axis=1) for p in range(M_PAIRS)]
    pair_row = lambda x, p: jnp.concatenate([x[2 * p:2 * p + 1, :], x[2 * p + 1:2 * p + 2, :]], axis=1)
    for rows, (vts, sqks, kvws), (_, w_inter, e_negm, dec, _), p2, v_aug, mx in zip(
            blocks, gated, scal, probs, v_augs, maxes):
        pairs = range(M_PAIRS)
        upd = [jnp.dot(jnp.concatenate([kvws[2 * p], kvws[2 * p + 1]], axis=1),
                       _block_diag(km[rows, hcols[2 * p]], km[rows, hcols[2 * p + 1]]),
                       preferred_element_type=F32) for p in pairs]
        intra = [jnp.dot(jnp.concatenate([vts[2 * p], vts[2 * p + 1]], axis=1),
                         _block_diag(sqks[2 * p], sqks[2 * p + 1]), preferred_element_type=F32) for p in pairs]
        inter = [jnp.dot(mem[p].astype(BF16), _block_diag(qt[hcols[2 * p], rows], qt[hcols[2 * p + 1], rows]),
                         preferred_element_type=F32) for p in pairs]
        o = jnp.dot(p2, v_aug, preferred_element_type=F32)
        o = o[:, :LANES] * (1.0 / (o[:, LANES:] + jnp.exp(sink - mx)))
        for j, tile in enumerate(_unstack_heads(o, ln)):
            mix_ref[rows, j * LANES:(j + 1) * LANES] = tile.astype(mix_ref.dtype)
        for p in pairs:
            num = inter[p] * pair_row(w_inter, p) + intra[p]
            mem[p] = pair_row(dec, p) * mem[p] + upd[p]
            den = jnp.maximum(jnp.abs(num[M_HEAD_DIM:M_HEAD_DIM + 1, :]), pair_row(e_negm, p))
            hh = num[:M_HEAD_DIM, :] * (1.0 / den)
            y = hh * lax.rsqrt(jnp.mean(hh * hh, axis=0, keepdims=True) + EPS)
            for h in (2 * p, 2 * p + 1):
                y_h = y[:, (h % 2) * ln:(h % 2 + 1) * ln]
                mcols = slice(ATTN_WIDTH + h * M_HEAD_DIM, ATTN_WIDTH + (h + 1) * M_HEAD_DIM)
                mix_ref[rows, mcols] = (jax.nn.sigmoid(om[rows, hcols[h]])
                                        * (y_h.T * mnorm_ref[:, hcols[h]])).astype(mix_ref.dtype)
    for p in range(M_PAIRS):
        c_ref[2 * p] = mem[p][:, :M_HEAD_DIM]
        c_ref[2 * p + 1] = mem[p][:, M_HEAD_DIM:]


def _out_row_block(i):
    per_head = (ATTN_WIDTH // ATTN_HEADS) // WEIGHT_SLAB
    j, part = i // per_head, i % per_head
    head = (j % 2) * GROUP + j // 2
    return jnp.where(i < ATTN_HEADS * per_head, head * per_head + part, i)


def _mixer_prompt(sink, b_i, b_f, x, g_pre, w_in_t, mnorm, w_gate, w_up, w_down, w_out):
    m = x.shape[0]
    tm = TOKEN_TILE
    steps = m // tm
    assert D_MODEL == steps * WEIGHT_SLAB and D_FF % (steps // 2) == 0 and (D_FF // (steps // 2)) % BF16_SUBLANES == 0
    down_slab = D_FF // (steps // 2)
    row = lambda n: pl.BlockSpec((tm, n), lambda i: (i, 0))
    whole = lambda shape: pl.BlockSpec(shape, lambda i: (0,) * len(shape))
    smem = pl.BlockSpec(memory_space=pltpu.SMEM)
    slab = lambda n: pl.BlockSpec((WEIGHT_SLAB, n), lambda i: (i, 0))
    down = pl.BlockSpec((down_slab, D_MODEL), lambda i: (i // 2, 0))
    c_shape = (M_HEADS, AUG_ROWS, M_HEAD_DIM)
    w_shape = (WINDOW, KV_WIDTH)
    s_shape = (SUBLANES, LANES)
    sds = jax.ShapeDtypeStruct
    return pl.pallas_call(
        _mixer_prompt_kernel,
        grid=(steps,),
        in_specs=[smem, smem, smem, row(D_MODEL), _const_spec((1, D_MODEL)), _const_spec(w_in_t.shape),
                  _const_spec((1, M_WIDTH)), slab(D_FF), slab(D_FF), down,
                  pl.BlockSpec((WEIGHT_SLAB, D_MODEL), lambda i: (_out_row_block(i), 0))],
        out_specs=[row(ATTN_WIDTH + M_WIDTH), whole(w_shape), whole(w_shape), whole(c_shape), whole(s_shape),
                   whole((D_MODEL, IN_PAD)), slab(D_FF), slab(D_FF), down, slab(D_MODEL)],
        out_shape=[sds((m, ATTN_WIDTH + M_WIDTH), BF16), sds(w_shape, F32), sds(w_shape, F32), sds(c_shape, F32),
                   sds(s_shape, F32), sds((D_MODEL, IN_PAD), BF16), sds(w_gate.shape, BF16), sds(w_up.shape, BF16),
                   sds(w_down.shape, BF16), sds(w_out.shape, BF16)],
        compiler_params=_params(("arbitrary",)),
        name="mixer_prompt",
    )(sink, b_i, b_f, x, g_pre, w_in_t, mnorm, w_gate, w_up, w_down, w_out)


def _gates(g_blk, bias_row, cum):
    pre = g_blk + bias_row
    lane = lax.broadcasted_iota(jnp.int32, pre.shape, 1)
    a = jnp.where(lane < M_HEADS, pre, jax.nn.log_sigmoid(pre))
    b = jnp.dot(cum, a, precision=HIGHEST, preferred_element_type=F32)
    return a, b, a.T, b.T


def _col(x, j):
    return jnp.broadcast_to(x[:, j:j + 1], x.shape)


def _mlstm_gated_scores(q, k, a, b, at, bt, h, mask, m_prev, last):
    bc, ic = _col(b, M_HEADS + h), _col(a, h)
    br, ir = bt[M_HEADS + h:M_HEADS + h + 1, :], at[h:h + 1, :]
    d = jnp.where(mask, (bc - br) + ir, -jnp.inf)
    inter = bc + m_prev
    m_t = jnp.maximum(inter, jnp.max(d, axis=-1, keepdims=True))
    w_inter = jnp.exp(inter - m_t)
    sqk = jnp.exp(d - m_t) * lax.dot_general(q, k, (((1,), (1,)), ((), ())), preferred_element_type=F32)
    m_end = last(m_t)
    bl = last(bc)
    dec = jnp.exp(bl + m_prev - m_end)
    w_end = jnp.exp((bl - bc) + ic - m_end)
    kw = k.astype(F32) * w_end
    return w_inter, sqk, m_t, m_end, dec, kw


def _mlstm_sample_kernel(q_ref, k_ref, v_ref, om_ref, g_ref, m0_ref, bias_ref, mnorm_ref, c_ref, n_ref,
                         o_ref, c_out, n_out, m_out, *, dec_seq):
    t = dec_seq
    bb = SAMPLE_MLSTM_BATCH
    ln = bb * t
    r = lax.broadcasted_iota(jnp.int32, (ln, ln), 0)
    c = lax.broadcasted_iota(jnp.int32, (ln, ln), 1)
    shift = t.bit_length() - 1
    same = (r >> shift) == (c >> shift)
    mask = jnp.logical_and(same, c <= r)
    cum = mask.astype(F32)
    expand = (c == (r >> shift)).astype(F32)
    gather = ((c >> shift) == r).astype(F32)
    is_last = (r & (t - 1)) == t - 1
    is_first = (r & (t - 1)) == 0

    def last(x):
        y = jnp.where(is_last, x, 0.0)
        step = 1
        while step < t:
            y = y + pltpu.roll(y, ln - step, axis=0)
            step *= 2
        return y

    a, b, at, bt = _gates(g_ref[...], bias_ref[...], cum)
    m0 = m0_ref[...]
    lane = lax.broadcasted_iota(jnp.int32, (ln, LANES), 1)
    zpad = jnp.zeros((ln - bb, M_HEAD_DIM), F32)
    heads = range(M_HEADS)
    hcols = [slice(h * M_HEAD_DIM, (h + 1) * M_HEAD_DIM) for h in heads]
    qfs = [q_ref[:, hc] for hc in hcols]
    vs = [v_ref[:, hc].astype(BF16) for hc in hcols]
    qcs = [jnp.concatenate(
        [jnp.dot(qfs[h][s * t:(s + 1) * t, :], c_ref[s * M_HEADS + h].astype(BF16).astype(F32),
                 preferred_element_type=F32) for s in range(bb)], axis=0) for h in heads]
    n_exps = [jnp.dot(expand, jnp.concatenate([n_ref[:, hc], zpad], axis=0), precision=HIGHEST,
                      preferred_element_type=F32) for hc in hcols]
    parts = [_mlstm_gated_scores(qfs[h].astype(BF16), k_ref[:, hcols[h]].astype(BF16), a, b, at, bt, h, mask,
                                 _col(m0, h), last) for h in heads]
    intras = [jnp.dot(parts[h][1].astype(BF16), vs[h], preferred_element_type=F32) for h in heads]
    m_cols = jnp.zeros((ln, LANES), F32)
    kwts = []
    for h in heads:
        w_inter, sqk, m_t, m_end, dec, kw = parts[h]
        num = w_inter * qcs[h] + intras[h]
        nq = w_inter * jnp.sum(qfs[h] * n_exps[h], axis=-1, keepdims=True) + jnp.sum(sqk, axis=-1, keepdims=True)
        hh = num / jnp.maximum(jnp.abs(nq), jnp.exp(-m_t))
        o_ref[:, hcols[h]] = _head_out(hh, mnorm_ref[:, hcols[h]], om_ref[:, hcols[h]])
        kwts.append(kw.T.astype(BF16))
        m_cols = jnp.where(lane == h, m_end, m_cols)
    m_out[...] = m_cols
    for h in heads:
        dec = parts[h][4]
        for s in range(bb):
            lhs = jnp.where((c >> shift) == s, kwts[h], jnp.zeros_like(kwts[h]))
            upd = jnp.dot(lhs, vs[h], preferred_element_type=F32)
            c_out[s * M_HEADS + h] = dec[s * t:s * t + 1, :] * c_ref[s * M_HEADS + h] + upd
    for h in heads:
        dec, kw = parts[h][4], parts[h][5]
        n_new = jnp.dot(gather, jnp.where(is_first, dec * n_exps[h], 0.0) + kw, precision=HIGHEST,
                        preferred_element_type=F32)
        n_out[:, hcols[h]] = n_new[:bb, :]


def _mlstm_sample(qm, km, vm, om, gt, m0_rows, bias_row, mnorm, c_in, n_in, dec_seq):
    m = qm.shape[0]
    bb = SAMPLE_MLSTM_BATCH
    tm = bb * dec_seq
    nb = m // dec_seq
    row = lambda n: pl.BlockSpec((tm, n), lambda i: (i, 0))
    whole = lambda shape: pl.BlockSpec(shape, lambda i: (0,) * len(shape))
    c_spec = pl.BlockSpec((bb * M_HEADS, M_HEAD_DIM, M_HEAD_DIM), lambda i: (i, 0, 0))
    n_spec = pl.BlockSpec((bb, M_WIDTH), lambda i: (i, 0))
    return pl.pallas_call(
        functools.partial(_mlstm_sample_kernel, dec_seq=dec_seq),
        grid=(nb // bb,),
        in_specs=[row(M_WIDTH), row(M_WIDTH), row(M_WIDTH), row(M_WIDTH), row(GATE_PAD), row(LANES),
                  whole((1, GATE_PAD)), whole((1, M_WIDTH)), c_spec, n_spec],
        out_specs=[row(M_WIDTH), c_spec, n_spec, row(LANES)],
        out_shape=[jax.ShapeDtypeStruct((m, M_WIDTH), BF16), jax.ShapeDtypeStruct(c_in.shape, F32),
                   jax.ShapeDtypeStruct(n_in.shape, F32), jax.ShapeDtypeStruct((m, LANES), F32)],
        compiler_params=_params(("parallel",)),
        name="mlstm_sample",
    )(qm, km, vm, om, gt, m0_rows, bias_row, mnorm, c_in, n_in)


def _out_ffn_kernel(x_ref, mix_ref, wo_ref, g1_ref, g2_ref, wg_ref, wu_ref, wd_ref, g3_ref, o_ref):
    tm = x_ref.shape[0]
    groups = [slice(r, r + tm // ROW_GROUPS) for r in range(0, tm, tm // ROW_GROUPS)]
    ys = [jnp.dot(mix_ref[rs, :], wo_ref[...], preferred_element_type=F32) for rs in groups]
    x1s = [x_ref[rs, :] + _rms(y, g1_ref[...]) for rs, y in zip(groups, ys)]
    fs = [_rms(x1, g2_ref[...]).astype(BF16) for x1 in x1s]
    accs = [None] * ROW_GROUPS
    for off in range(0, D_FF, FFN_CHUNK):
        acts = []
        for f in fs:
            g = jnp.dot(f, wg_ref[:, off:off + FFN_CHUNK], preferred_element_type=F32)
            u = jnp.dot(f, wu_ref[:, off:off + FFN_CHUNK], preferred_element_type=F32)
            acts.append((g * jax.nn.sigmoid(g) * u).astype(BF16))
        for i, act in enumerate(acts):
            part = jnp.dot(act, wd_ref[off:off + FFN_CHUNK, :], preferred_element_type=F32)
            accs[i] = part if accs[i] is None else accs[i] + part
    for rs, x1, acc in zip(groups, x1s, accs):
        o_ref[rs, :] = x1 + _rms(acc, g3_ref[...])


def _out_ffn(x, mix, w_out, g_post_mix, g_pre_ffn, w_gate, w_up, w_down, g_post_ffn):
    m = x.shape[0]
    tm = FFN_TOKEN_TILE if m >= 2 * FFN_TOKEN_TILE else TOKEN_TILE
    row = lambda n: pl.BlockSpec((tm, n), lambda i: (i, 0))
    vec = _const_spec((1, D_MODEL))
    return pl.pallas_call(
        _out_ffn_kernel,
        grid=(m // tm,),
        in_specs=[row(D_MODEL), row(ATTN_WIDTH + M_WIDTH), _const_spec((D_MODEL, D_MODEL)), vec, vec,
                  _const_spec((D_MODEL, D_FF)), _const_spec((D_MODEL, D_FF)), _const_spec((D_FF, D_MODEL)), vec],
        out_specs=row(D_MODEL),
        out_shape=jax.ShapeDtypeStruct((m, D_MODEL), F32),
        compiler_params=_params(("parallel",)),
        name="out_ffn",
    )(x, mix, w_out, g_post_mix, g_pre_ffn, w_gate, w_up, w_down, g_post_ffn)


def _layer(xp, xs, cache_k, cache_v, state_c, state_n, state_m, w_in, b_i, b_f, attn_sink, m_norm, w_out,
           g_pre_mix, g_post_mix, g_pre_ffn, g_post_ffn, w_gate, w_up, w_down):
    bp, sp, _ = xp.shape
    bs, ts, _ = xs.shape
    assert bp == 1 and sp % TOKEN_TILE == 0 and TOKEN_TILE % WINDOW == 0
    assert ts & (ts - 1) == 0 and (bs * ts) % TOKEN_TILE == 0 and bs % SAMPLE_MLSTM_BATCH == 0

    row = lambda v: v.reshape(1, -1)
    bias_row = jnp.pad(jnp.concatenate([b_i, b_f]), (0, GATE_PAD - 2 * M_HEADS)).reshape(1, GATE_PAD)
    sink = row(attn_sink)

    x2 = xp.reshape(sp, D_MODEL)
    mix, k_w, v_w, c_aug, m_p, w_pad, wg, wu, wd, wo = _mixer_prompt(
        sink, row(b_i), row(b_f), x2, row(g_pre_mix), w_in.T, row(m_norm), w_gate, w_up, w_down, w_out)
    ffn = (wo, row(g_post_mix), row(g_pre_ffn), wg, wu, wd, row(g_post_ffn))
    yp = _out_ffn(x2, mix, *ffn).reshape(xp.shape)
    k_p = k_w.reshape(1, WINDOW, KV_HEADS, HEAD_DIM)
    v_p = v_w.reshape(1, WINDOW, KV_HEADS, HEAD_DIM)
    c_p = jnp.swapaxes(c_aug[:, :M_HEAD_DIM, :], 1, 2).reshape(1, M_HEADS, M_HEAD_DIM, M_HEAD_DIM)
    n_p = c_aug[:, M_HEAD_DIM, :].reshape(1, M_HEADS, M_HEAD_DIM)
    m_p = m_p[:M_HEADS, 0].reshape(1, M_HEADS)

    x2 = xs.reshape(bs * ts, D_MODEL)
    qa, ka, va, qm, km, vm, om, gt = _inproj(x2, row(g_pre_mix), w_pad)
    feature_major = lambda c: jnp.transpose(c, (0, 2, 3, 1))
    mix_a, k_s, v_s = _attn_sample(sink, qa, ka, va, feature_major(cache_k), feature_major(cache_v), ts)
    m0_rows = jnp.pad(jnp.repeat(state_m, ts, axis=0), ((0, 0), (0, LANES - M_HEADS)))
    mix_m, c_s, n_s, m_rows = _mlstm_sample(qm, km, vm, om, gt, m0_rows, bias_row, row(m_norm),
                                            state_c.reshape(bs * M_HEADS, M_HEAD_DIM, M_HEAD_DIM),
                                            state_n.reshape(bs, M_WIDTH), ts)
    ys = _out_ffn(x2, jnp.concatenate([mix_a, mix_m], axis=1), *ffn).reshape(xs.shape)
    k_s = jnp.transpose(k_s, (0, 3, 1, 2))
    v_s = jnp.transpose(v_s, (0, 3, 1, 2))
    c_s = c_s.reshape(bs, M_HEADS, M_HEAD_DIM, M_HEAD_DIM)
    n_s = n_s.reshape(bs, M_HEADS, M_HEAD_DIM)
    m_s = m_rows[ts - 1::ts, :M_HEADS]
    return yp, ys, (k_p, v_p, c_p, n_p, m_p), (k_s, v_s, c_s, n_s, m_s)


def kernel(x_prompt, x_sample, cache_k, cache_v, state_C, state_n, state_m, w_in, b_i, b_f, attn_sink, m_norm,
           w_out, g_pre_mix, g_post_mix, g_pre_ffn, g_post_ffn, w_gate, w_up, w_down):
    depth = w_in.shape[0]
    xp, xs = x_prompt, x_sample
    prompt_states, sample_states = [], []
    for l in range(depth):
        xp, xs, st_p, st_s = _layer(xp, xs, cache_k[l], cache_v[l], state_C[l], state_n[l], state_m[l],
                                    w_in[l], b_i[l], b_f[l], attn_sink[l], m_norm[l], w_out[l],
                                    g_pre_mix[l], g_post_mix[l], g_pre_ffn[l], g_post_ffn[l],
                                    w_gate[l], w_up[l], w_down[l])
        prompt_states.append(st_p)
        sample_states.append(st_s)
    stack = lambda states, i: jnp.stack([s[i] for s in states], axis=0)
    return (xp, xs) + tuple(stack(prompt_states, i) for i in range(5)) + tuple(stack(sample_states, i) for i in range(5))
```

```python
import functools

import jax
import jax.numpy as jnp
from jax import lax
from jax.experimental import pallas as pl
from jax.experimental.pallas import tpu as pltpu

F32 = jnp.float32
BF16 = jnp.bfloat16
HIGHEST = lax.Precision.HIGHEST

D_MODEL = 1024
HEAD_DIM = 64
ATTN_HEADS = 8
KV_HEADS = 2
GROUP = ATTN_HEADS // KV_HEADS
ATTN_WIDTH = ATTN_HEADS * HEAD_DIM
KV_WIDTH = KV_HEADS * HEAD_DIM
WINDOW = 128
M_HEADS = 4
M_HEAD_DIM = 128
M_WIDTH = M_HEADS * M_HEAD_DIM
M_PAIRS = M_HEADS // 2
D_FF = 2816
EPS = 1e-6

LANES = 128
SUBLANES = 8
GATE_PAD = LANES
BF16_SUBLANES = 16
AUG_ROWS = M_HEAD_DIM + BF16_SUBLANES
IN_MAIN = ATTN_WIDTH + 2 * KV_WIDTH + 4 * M_WIDTH
IN_PAD = IN_MAIN + GATE_PAD
VMEM_LIMIT = 56 * 1024 * 1024

HEAD_ORDER = tuple(h for j in range(GROUP) for h in (j, j + GROUP))

TOKEN_TILE = 512
FFN_CHUNK = 256
ROW_GROUPS = 2
WEIGHT_SLAB = 32
SAMPLE_BATCH = 16


def _rms(x, g):
    return x * lax.rsqrt(jnp.mean(x * x, axis=-1, keepdims=True) + EPS) * g


def _const_spec(shape):
    nd = len(shape)
    return pl.BlockSpec(shape, lambda i: (0,) * nd, pipeline_mode=pl.Buffered(1))


def _params(semantics):
    return pltpu.CompilerParams(dimension_semantics=semantics, vmem_limit_bytes=VMEM_LIMIT)


OFF_QA, OFF_KA, OFF_VA = 0, ATTN_WIDTH, ATTN_WIDTH + KV_WIDTH
OFF_QM = ATTN_WIDTH + 2 * KV_WIDTH
OFF_KM, OFF_VM, OFF_OM = OFF_QM + M_WIDTH, OFF_QM + 2 * M_WIDTH, OFF_QM + 3 * M_WIDTH


def _inproj_kernel(x_ref, g_ref, w_ref, qa_ref, ka_ref, va_ref, qm_ref, km_ref, vm_ref, om_ref, gt_ref):
    tm = x_ref.shape[0]
    step = tm // ROW_GROUPS
    for r0 in range(0, tm, step):
        rs = slice(r0, r0 + step)
        h = _rms(x_ref[rs, :], g_ref[...]).astype(BF16)

        def proj(off, n):
            return jnp.dot(h, w_ref[:, off:off + n], preferred_element_type=F32)

        qa_ref[rs, :] = proj(OFF_QA, ATTN_WIDTH) * (HEAD_DIM ** -0.5)
        ka_ref[rs, :] = proj(OFF_KA, KV_WIDTH)
        va_ref[rs, :] = proj(OFF_VA, KV_WIDTH)
        qm_ref[rs, :] = proj(OFF_QM, M_WIDTH)
        km_ref[rs, :] = proj(OFF_KM, M_WIDTH) * (M_HEAD_DIM ** -0.5)
        vm_ref[rs, :] = proj(OFF_VM, M_WIDTH)
        om_ref[rs, :] = proj(OFF_OM, M_WIDTH)
        gt_ref[rs, :] = proj(IN_MAIN, GATE_PAD)


def _inproj(x, g_pre, w_pad):
    m = x.shape[0]
    tm = TOKEN_TILE
    row = lambda n: pl.BlockSpec((tm, n), lambda i: (i, 0))
    widths = (ATTN_WIDTH, KV_WIDTH, KV_WIDTH, M_WIDTH, M_WIDTH, M_WIDTH, M_WIDTH, GATE_PAD)
    return pl.pallas_call(
        _inproj_kernel,
        grid=(m // tm,),
        in_specs=[row(D_MODEL), _const_spec((1, D_MODEL)), _const_spec((D_MODEL, IN_PAD))],
        out_specs=[row(n) for n in widths],
        out_shape=[jax.ShapeDtypeStruct((m, n), F32) for n in widths],
        compiler_params=_params(("parallel",)),
        name="inproj",
    )(x, g_pre, w_pad)


def _stack_heads(q_tiles):
    lane = lax.broadcasted_iota(jnp.int32, q_tiles[0].shape, 1)
    lo = lane < HEAD_DIM
    zero = jnp.zeros_like(q_tiles[0])
    parts = []
    for qt in q_tiles:
        parts += [jnp.where(lo, qt, zero), jnp.where(lo, zero, qt)]
    return jnp.concatenate(parts, axis=0)


def _unstack_heads(o, rows):
    lane = lax.broadcasted_iota(jnp.int32, (rows, LANES), 1)
    lo = lane < HEAD_DIM
    return [jnp.where(lo, o[(2 * j) * rows:(2 * j + 1) * rows, :], o[(2 * j + 1) * rows:(2 * j + 2) * rows, :])
            for j in range(GROUP)]


def _sink_rows(sink_ref, rows):
    return jnp.concatenate([jnp.full((rows, LANES), sink_ref[0, h], F32) for h in HEAD_ORDER], axis=0)


def _attn_sample_kernel(sink_ref, q_ref, kn_ref, vn_ref, ck_ref, cv_ref, o_ref, ko_ref, vo_ref, *, dec_seq):
    t = dec_seq
    bb = SAMPLE_BATCH
    nrow = ATTN_HEADS * t
    ts, ns = t.bit_length() - 1, nrow.bit_length() - 1
    r_c = lax.broadcasted_iota(jnp.int32, (nrow, WINDOW), 0) & (t - 1)
    c_c = lax.broadcasted_iota(jnp.int32, (nrow, WINDOW), 1)
    vis_cache = c_c > r_c
    r_n = lax.broadcasted_iota(jnp.int32, (bb * nrow, bb * t), 0)
    c_n = lax.broadcasted_iota(jnp.int32, (bb * nrow, bb * t), 1)
    vis_new = jnp.logical_and((r_n >> ns) == (c_n >> ts), (c_n & (t - 1)) <= (r_n & (t - 1)))
    sink = _sink_rows(sink_ref, t)
    nt = (((1,), (1,)), ((), ()))
    kn_all, vn_all = kn_ref[...], vn_ref[...]
    lane_w = lax.broadcasted_iota(jnp.int32, (KV_WIDTH, WINDOW), 1)
    is_new = lane_w >= WINDOW - t
    zero_rows = jnp.zeros((WINDOW - t, KV_WIDTH), F32)
    state_shape = (KV_HEADS, HEAD_DIM, WINDOW)

    def slide(old_t, new_rows):
        new_t = jnp.concatenate([zero_rows, new_rows], axis=0).T
        return jnp.where(is_new, new_t, pltpu.roll(old_t, WINDOW - t, axis=1)).reshape(state_shape)

    qs, s_c, cvs = [], [], []
    for b in range(bb):
        rows = slice(b * t, (b + 1) * t)
        ck = ck_ref[b].reshape(KV_WIDTH, WINDOW)
        cvs.append(cv_ref[b].reshape(KV_WIDTH, WINDOW))
        ko_ref[b] = slide(ck, kn_all[rows, :])
        vo_ref[b] = slide(cvs[b], vn_all[rows, :])
        qs.append(_stack_heads([q_ref[rows, j * LANES:(j + 1) * LANES] for j in range(GROUP)]).astype(BF16))
        s_c.append(jnp.where(vis_cache, jnp.dot(qs[b], ck.astype(BF16), preferred_element_type=F32), -jnp.inf))
    s_n = jnp.where(vis_new, lax.dot_general(jnp.concatenate(qs, axis=0), kn_all.astype(BF16), nt,
                                             preferred_element_type=F32), -jnp.inf)
    p_c, p_n, rden = [], [], []
    for b in range(bb):
        s_nb = s_n[b * nrow:(b + 1) * nrow, :]
        m = jnp.maximum(jnp.maximum(jnp.max(s_c[b], axis=-1, keepdims=True), jnp.max(s_nb, axis=-1, keepdims=True)),
                        sink)
        p_c.append(jnp.exp(s_c[b] - m))
        p_n.append(jnp.exp(s_nb - m[:, :bb * t]))
        rden.append(1.0 / (jnp.sum(p_c[b], axis=-1, keepdims=True) + jnp.sum(p_n[b], axis=-1, keepdims=True)
                           + jnp.exp(sink - m)))
    o_n = jnp.dot(jnp.concatenate(p_n, axis=0).astype(BF16), vn_all.astype(BF16), preferred_element_type=F32)
    outs = [[] for _ in range(GROUP)]
    for b in range(bb):
        o = (lax.dot_general(p_c[b].astype(BF16), cvs[b].astype(BF16), nt, preferred_element_type=F32)
             + o_n[b * nrow:(b + 1) * nrow, :]) * rden[b]
        for j, tile in enumerate(_unstack_heads(o, t)):
            outs[j].append(tile)
    for j, parts in enumerate(outs):
        o_ref[:, j * LANES:(j + 1) * LANES] = jnp.concatenate(parts, axis=0).astype(o_ref.dtype)


def _block_diag(a, b):
    za, zb = jnp.zeros_like(a), jnp.zeros_like(b)
    return jnp.concatenate([jnp.concatenate([a, zb], axis=1), jnp.concatenate([za, b], axis=1)], axis=0)


def _head_out(hh, mnorm_row, om):
    y = hh * lax.rsqrt(jnp.mean(hh * hh, axis=-1, keepdims=True) + EPS) * mnorm_row
    return (jax.nn.sigmoid(om) * y).astype(BF16)


def _mixer_prompt_kernel(sink_ref, bi_ref, bf_ref, x_ref, g_ref, win_ref, mnorm_ref, wg_ref, wu_ref, wd_ref, wo_ref,
                         mix_ref, kw_ref, vw_ref, c_ref, m_ref, w_ref, wgb_ref, wub_ref, wdb_ref, wob_ref):
    step = pl.program_id(0)

    @pl.when(step == 0)
    def _():
        kw_ref[...] = jnp.zeros_like(kw_ref)
        vw_ref[...] = jnp.zeros_like(vw_ref)
        c_ref[...] = jnp.zeros_like(c_ref)
        m_ref[...] = jnp.zeros_like(m_ref)
        for j in range(GROUP):
            lo = win_ref[j * HEAD_DIM:(j + 1) * HEAD_DIM, :]
            hi = win_ref[(j + GROUP) * HEAD_DIM:(j + GROUP + 1) * HEAD_DIM, :]
            w_ref[:, j * LANES:(j + 1) * LANES] = jnp.concatenate([lo, hi], axis=0).T.astype(BF16)
        for c0 in range(ATTN_WIDTH, IN_MAIN, M_WIDTH):
            c1 = min(c0 + M_WIDTH, IN_MAIN)
            w_ref[:, c0:c1] = win_ref[c0:c1, :].T.astype(BF16)
        gw = jnp.concatenate([win_ref[IN_MAIN:IN_MAIN + 2 * M_HEADS, :],
                              jnp.zeros((GATE_PAD - 2 * M_HEADS, D_MODEL), F32)], axis=0)
        w_ref[:, IN_MAIN:] = gw.T.astype(BF16)

    wgb_ref[...] = wg_ref[...].astype(BF16)
    wub_ref[...] = wu_ref[...].astype(BF16)
    wdb_ref[...] = wd_ref[...].astype(BF16)
    wob_ref[...] = wo_ref[...].astype(BF16)

    tm = x_ref.shape[0]
    ln = WINDOW
    groups = [slice(r0, r0 + tm // ROW_GROUPS) for r0 in range(0, tm, tm // ROW_GROUPS)]
    blocks = [slice(r0, r0 + ln) for r0 in range(0, tm, ln)]
    hcols = [slice(h * M_HEAD_DIM, (h + 1) * M_HEAD_DIM) for h in range(M_HEADS)]
    pcols = [slice(2 * p * M_HEAD_DIM, 2 * (p + 1) * M_HEAD_DIM) for p in range(M_PAIRS)]
    nt = (((1,), (1,)), ((), ()))
    hs = [_rms(x_ref[rs, :], g_ref[...]).astype(BF16) for rs in groups]

    def proj(off, n):
        return jnp.concatenate([jnp.dot(h, w_ref[:, off:off + n], preferred_element_type=F32) for h in hs], axis=0)

    qt = proj(OFF_QM, M_WIDTH).T.astype(BF16)
    km = (proj(OFF_KM, M_WIDTH) * (M_HEAD_DIM ** -0.5)).astype(BF16)
    vt = proj(OFF_VM, M_WIDTH).T.astype(BF16)
    gates = proj(IN_MAIN, GATE_PAD).T
    r = lax.broadcasted_iota(jnp.int32, (ln, ln), 0)
    c = lax.broadcasted_iota(jnp.int32, (ln, ln), 1)
    causal_t = r <= c
    upper = causal_t.astype(F32)
    lane8 = lax.broadcasted_iota(jnp.int32, (SUBLANES, ln), 1)
    ones_rows = jnp.ones((AUG_ROWS - M_HEAD_DIM, ln), BF16)
    zrows = jnp.zeros((ln - SUBLANES, ln), F32)
    row8 = lax.broadcasted_iota(jnp.int32, (SUBLANES, ln), 0)
    bias = jnp.zeros((SUBLANES, ln), F32)
    for h in range(M_HEADS):
        bias = jnp.where(row8 == h, bi_ref[0, h], jnp.where(row8 == M_HEADS + h, bf_ref[0, h], bias))
    pre, scores_m = [], []
    for rows in blocks:
        gi = gates[0:SUBLANES, rows] + bias
        b = pltpu.roll(jnp.dot(jax.nn.log_sigmoid(gi), upper, precision=HIGHEST, preferred_element_type=F32),
                       M_HEADS, axis=0)
        g = gi - b
        cm0 = g
        sh = 1
        while sh < ln:
            cm0 = jnp.maximum(cm0, jnp.where(lane8 >= sh, pltpu.roll(cm0, sh, axis=1), -jnp.inf))
            sh *= 2
        b_last = jnp.broadcast_to(b[:, ln - 1:ln], b.shape)
        g_max = jnp.broadcast_to(cm0[:, ln - 1:ln], b.shape)
        g_cols = jnp.concatenate([g, zrows], axis=0).T
        pre.append((b, g_cols, cm0, b_last, g_max, (b_last - b) + gi))
        scores_m.append([jnp.dot(km[rows, pc], _block_diag(qt[hcols[2 * p], rows], qt[hcols[2 * p + 1], rows]),
                                 preferred_element_type=F32) for p, pc in enumerate(pcols)])

    qa = (proj(OFF_QA, ATTN_WIDTH) * (HEAD_DIM ** -0.5)).astype(BF16)
    kv = proj(OFF_KA, 2 * KV_WIDTH)
    ka, va = kv[:, :KV_WIDTH], kv[:, KV_WIDTH:]
    nrow = ATTN_HEADS * ln
    ra = lax.broadcasted_iota(jnp.int32, (nrow, ln), 0) & (ln - 1)
    ca = lax.broadcasted_iota(jnp.int32, (nrow, ln), 1)
    own = ca <= ra
    sink = _sink_rows(sink_ref, ln)
    no_prev = jnp.where(step == 0, -jnp.inf, 0.0)
    ones = jnp.ones((2 * ln, LANES), BF16)
    scores_a, v_augs = [], []
    for blk, rows in enumerate(blocks):
        if blk == 0:
            k_prev, v_prev = kw_ref[...], vw_ref[...]
        else:
            k_prev, v_prev = ka[blocks[blk - 1], :], va[blocks[blk - 1], :]
        kcat = jnp.concatenate([ka[rows, :], k_prev], axis=0).astype(BF16)
        vcat = jnp.concatenate([va[rows, :], v_prev], axis=0).astype(BF16)
        v_augs.append(jnp.concatenate([vcat, ones], axis=1))
        qs = _stack_heads([qa[rows, j * LANES:(j + 1) * LANES] for j in range(GROUP)])
        scores_a.append(lax.dot_general(qs, kcat, nt, preferred_element_type=F32))
    kw_ref[...] = ka[blocks[-1], :]
    vw_ref[...] = va[blocks[-1], :]

    m_prev = m_ref[...]
    scal = []
    for b, _, cm0, b_last, g_max, w_end_arg in pre:
        cm = jnp.maximum(cm0, m_prev)
        m_end = b_last + jnp.maximum(g_max, m_prev)
        scal.append((cm, jnp.exp(m_prev - cm), jnp.exp(-(b + cm)), jnp.exp(b_last + m_prev - m_end),
                     jnp.exp(w_end_arg - m_end)))
        m_prev = m_end
    m_ref[...] = m_prev
    gated = []
    for rows, s_t, (_, g_cols, *_), (cm, _, _, _, w_end) in zip(blocks, scores_m, pre, scal):
        vts = [jnp.concatenate([vt[hc, rows], ones_rows], axis=0) for hc in hcols]
        sqks = [(jnp.exp(jnp.where(causal_t, g_cols[:, h:h + 1] - cm[h:h + 1, :], -jnp.inf))
                 * s_t[h // 2][:, (h % 2) * ln:(h % 2 + 1) * ln]).astype(BF16) for h in range(M_HEADS)]
        kvws = [(vts[h].astype(F32) * w_end[h:h + 1, :]).astype(BF16) for h in range(M_HEADS)]
        gated.append((vts, sqks, kvws))
    probs, maxes = [], []
    for blk, s in enumerate(scores_a):
        s_prev = s[:, ln:]
        if blk == 0:
            s_prev = s_prev + no_prev
        sc = jnp.where(own, s[:, :ln], s_prev)
        mx = jnp.maximum(jnp.max(sc, axis=-1, keepdims=True), sink)
        p = jnp.exp(sc - mx)
        zero = jnp.zeros_like(p)
        probs.append(jnp.concatenate([jnp.where(own, p, zero), jnp.where(own, zero, p)], axis=1).astype(BF16))
        maxes.append(mx)
    om = proj(OFF_OM, M_WIDTH)

    mem = [jnp.concatenate([c_ref[2 * p], c_ref[2 * p + 1]], axis=1) for p in range(M_PAIRS)]
    pair_row = lambda x, p: jnp.concatenate([x[2 * p:2 * p + 1, :], x[2 * p + 1:2 * p + 2, :]], axis=1)
    for rows, (vts, sqks, kvws), (_, w_inter, e_negm, dec, _), p2, v_aug, mx in zip(
            blocks, gated, scal, probs, v_augs, maxes):
        pairs = range(M_PAIRS)
        upd = [jnp.dot(jnp.concatenate([kvws[2 * p], kvws[2 * p + 1]], axis=1),
                       _block_diag(km[rows, hcols[2 * p]], km[rows, hcols[2 * p + 1]]),
                       preferred_element_type=F32) for p in pairs]
        intra = [jnp.dot(jnp.concatenate([vts[2 * p], vts[2 * p + 1]], axis=1),
                         _block_diag(sqks[2 * p], sqks[2 * p + 1]), preferred_element_type=F32) for p in pairs]
        inter = [jnp.dot(mem[p].astype(BF16), _block_diag(qt[hcols[2 * p], rows], qt[hcols[2 * p + 1], rows]),
                         preferred_element_type=F32) for p in pairs]
        o = jnp.dot(p2, v_aug, preferred_element_type=F32)
        o = o[:, :LANES] * (1.0 / (o[:, LANES:] + jnp.exp(sink - mx)))
        for j, tile in enumerate(_unstack_heads(o, ln)):
            mix_ref[rows, j * LANES:(j + 1) * LANES] = tile.astype(mix_ref.dtype)
        for p in pairs:
            num = inter[p] * pair_row(w_inter, p) + intra[p]
            mem[p] = pair_row(dec, p) * mem[p] + upd[p]
            den = jnp.maximum(jnp.abs(num[M_HEAD_DIM:M_HEAD_DIM + 1, :]), pair_row(e_negm, p))
            hh = num[:M_HEAD_DIM, :] * (1.0 / den)
            y = hh * lax.rsqrt(jnp.mean(hh * hh, axis=0, keepdims=True) + EPS)
            for h in (2 * p, 2 * p + 1):
                y_h = y[:, (h % 2) * ln:(h % 2 + 1) * ln]
                mcols = slice(ATTN_WIDTH + h * M_HEAD_DIM, ATTN_WIDTH + (h + 1) * M_HEAD_DIM)
                mix_ref[rows, mcols] = (jax.nn.sigmoid(om[rows, hcols[h]])
                                        * (y_h.T * mnorm_ref[:, hcols[h]])).astype(mix_ref.dtype)
    for p in range(M_PAIRS):
        c_ref[2 * p] = mem[p][:, :M_HEAD_DIM]
        c_ref[2 * p + 1] = mem[p][:, M_HEAD_DIM:]


def _out_row_block(i):
    per_head = (ATTN_WIDTH // ATTN_HEADS) // WEIGHT_SLAB
    j, part = i // per_head, i % per_head
    head = (j % 2) * GROUP + j // 2
    return jnp.where(i < ATTN_HEADS * per_head, head * per_head + part, i)


def _mixer_prompt(sink, b_i, b_f, x, g_pre, w_in_t, mnorm, w_gate, w_up, w_down, w_out):
    m = x.shape[0]
    tm = TOKEN_TILE
    steps = m // tm
    assert D_MODEL == steps * WEIGHT_SLAB and D_FF % (steps // 2) == 0 and (D_FF // (steps // 2)) % BF16_SUBLANES == 0
    down_slab = D_FF // (steps // 2)
    row = lambda n: pl.BlockSpec((tm, n), lambda i: (i, 0))
    whole = lambda shape: pl.BlockSpec(shape, lambda i: (0,) * len(shape))
    smem = pl.BlockSpec(memory_space=pltpu.SMEM)
    slab = lambda n: pl.BlockSpec((WEIGHT_SLAB, n), lambda i: (i, 0))
    down = pl.BlockSpec((down_slab, D_MODEL), lambda i: (i // 2, 0))
    c_shape = (M_HEADS, AUG_ROWS, M_HEAD_DIM)
    w_shape = (WINDOW, KV_WIDTH)
    s_shape = (SUBLANES, LANES)
    sds = jax.ShapeDtypeStruct
    return pl.pallas_call(
        _mixer_prompt_kernel,
        grid=(steps,),
        in_specs=[smem, smem, smem, row(D_MODEL), _const_spec((1, D_MODEL)), _const_spec(w_in_t.shape),
                  _const_spec((1, M_WIDTH)), slab(D_FF), slab(D_FF), down,
                  pl.BlockSpec((WEIGHT_SLAB, D_MODEL), lambda i: (_out_row_block(i), 0))],
        out_specs=[row(ATTN_WIDTH + M_WIDTH), whole(w_shape), whole(w_shape), whole(c_shape), whole(s_shape),
                   whole((D_MODEL, IN_PAD)), slab(D_FF), slab(D_FF), down, slab(D_MODEL)],
        out_shape=[sds((m, ATTN_WIDTH + M_WIDTH), BF16), sds(w_shape, F32), sds(w_shape, F32), sds(c_shape, F32),
                   sds(s_shape, F32), sds((D_MODEL, IN_PAD), BF16), sds(w_gate.shape, BF16), sds(w_up.shape, BF16),
                   sds(w_down.shape, BF16), sds(w_out.shape, BF16)],
        compiler_params=_params(("arbitrary",)),
        name="mixer_prompt",
    )(sink, b_i, b_f, x, g_pre, w_in_t, mnorm, w_gate, w_up, w_down, w_out)


def _gates(g_blk, bias_row, cum):
    pre = g_blk + bias_row
    lane = lax.broadcasted_iota(jnp.int32, pre.shape, 1)
    a = jnp.where(lane < M_HEADS, pre, jax.nn.log_sigmoid(pre))
    b = jnp.dot(cum, a, precision=HIGHEST, preferred_element_type=F32)
    return a, b, a.T, b.T


def _col(x, j):
    return jnp.broadcast_to(x[:, j:j + 1], x.shape)


def _mlstm_gated_scores(q, k, a, b, at, bt, h, mask, m_prev, last):
    bc, ic = _col(b, M_HEADS + h), _col(a, h)
    br, ir = bt[M_HEADS + h:M_HEADS + h + 1, :], at[h:h + 1, :]
    d = jnp.where(mask, (bc - br) + ir, -jnp.inf)
    inter = bc + m_prev
    m_t = jnp.maximum(inter, jnp.max(d, axis=-1, keepdims=True))
    w_inter = jnp.exp(inter - m_t)
    sqk = jnp.exp(d - m_t) * lax.dot_general(q, k, (((1,), (1,)), ((), ())), preferred_element_type=F32)
    m_end = last(m_t)
    bl = last(bc)
    dec = jnp.exp(bl + m_prev - m_end)
    w_end = jnp.exp((bl - bc) + ic - m_end)
    kw = k.astype(F32) * w_end
    return w_inter, sqk, m_t, m_end, dec, kw


def _mlstm_sample_kernel(q_ref, k_ref, v_ref, om_ref, g_ref, m0_ref, bias_ref, mnorm_ref, c_ref, n_ref,
                         o_ref, c_out, n_out, m_out, *, dec_seq):
    t = dec_seq
    bb = SAMPLE_BATCH
    ln = bb * t
    r = lax.broadcasted_iota(jnp.int32, (ln, ln), 0)
    c = lax.broadcasted_iota(jnp.int32, (ln, ln), 1)
    shift = t.bit_length() - 1
    same = (r >> shift) == (c >> shift)
    mask = jnp.logical_and(same, c <= r)
    cum = mask.astype(F32)
    expand = (c == (r >> shift)).astype(F32)
    gather = ((c >> shift) == r).astype(F32)
    is_last = (r & (t - 1)) == t - 1
    is_first = (r & (t - 1)) == 0

    def last(x):
        y = jnp.where(is_last, x, 0.0)
        step = 1
        while step < t:
            y = y + pltpu.roll(y, ln - step, axis=0)
            step *= 2
        return y

    a, b, at, bt = _gates(g_ref[...], bias_ref[...], cum)
    m0 = m0_ref[...]
    lane = lax.broadcasted_iota(jnp.int32, (ln, LANES), 1)
    zpad = jnp.zeros((ln - bb, M_HEAD_DIM), F32)
    heads = range(M_HEADS)
    hcols = [slice(h * M_HEAD_DIM, (h + 1) * M_HEAD_DIM) for h in heads]
    qfs = [q_ref[:, hc] for hc in hcols]
    vs = [v_ref[:, hc].astype(BF16) for hc in hcols]
    qcs = [jnp.concatenate(
        [jnp.dot(qfs[h][s * t:(s + 1) * t, :], c_ref[s * M_HEADS + h].astype(BF16).astype(F32),
                 preferred_element_type=F32) for s in range(bb)], axis=0) for h in heads]
    n_exps = [jnp.dot(expand, jnp.concatenate([n_ref[:, hc], zpad], axis=0), precision=HIGHEST,
                      preferred_element_type=F32) for hc in hcols]
    parts = [_mlstm_gated_scores(qfs[h].astype(BF16), k_ref[:, hcols[h]].astype(BF16), a, b, at, bt, h, mask,
                                 _col(m0, h), last) for h in heads]
    intras = [jnp.dot(parts[h][1].astype(BF16), vs[h], preferred_element_type=F32) for h in heads]
    m_cols = jnp.zeros((ln, LANES), F32)
    kwts = []
    for h in heads:
        w_inter, sqk, m_t, m_end, dec, kw = parts[h]
        num = w_inter * qcs[h] + intras[h]
        nq = w_inter * jnp.sum(qfs[h] * n_exps[h], axis=-1, keepdims=True) + jnp.sum(sqk, axis=-1, keepdims=True)
        hh = num / jnp.maximum(jnp.abs(nq), jnp.exp(-m_t))
        o_ref[:, hcols[h]] = _head_out(hh, mnorm_ref[:, hcols[h]], om_ref[:, hcols[h]])
        kwts.append(kw.T.astype(BF16))
        m_cols = jnp.where(lane == h, m_end, m_cols)
    m_out[...] = m_cols
    for h in heads:
        dec = parts[h][4]
        for s in range(bb):
            lhs = jnp.where((c >> shift) == s, kwts[h], jnp.zeros_like(kwts[h]))
            upd = jnp.dot(lhs, vs[h], preferred_element_type=F32)
            c_out[s * M_HEADS + h] = dec[s * t:s * t + 1, :] * c_ref[s * M_HEADS + h] + upd
    for h in heads:
        dec, kw = parts[h][4], parts[h][5]
        n_new = jnp.dot(gather, jnp.where(is_first, dec * n_exps[h], 0.0) + kw, precision=HIGHEST,
                        preferred_element_type=F32)
        n_out[:, hcols[h]] = n_new[:bb, :]


def _mixer_sample_kernel(sink_ref, qa_ref, kn_ref, vn_ref, ck_ref, cv_ref, qm_ref, km_ref, vm_ref, om_ref, g_ref,
                         m0_ref, bias_ref, mnorm_ref, c_ref, n_ref, mix_ref, ko_ref, vo_ref, c_out, n_out, m_out,
                         *, dec_seq):
    _attn_sample_kernel(sink_ref, qa_ref, kn_ref, vn_ref, ck_ref, cv_ref, mix_ref.at[:, pl.ds(0, ATTN_WIDTH)],
                        ko_ref, vo_ref, dec_seq=dec_seq)
    _mlstm_sample_kernel(qm_ref, km_ref, vm_ref, om_ref, g_ref, m0_ref, bias_ref, mnorm_ref, c_ref, n_ref,
                         mix_ref.at[:, pl.ds(ATTN_WIDTH, M_WIDTH)], c_out, n_out, m_out, dec_seq=dec_seq)


def _mixer_sample(sink, qa, ka, va, cache_k, cache_v, qm, km, vm, om, gt, m0_rows, bias_row, mnorm, c_in, n_in,
                  dec_seq):
    m = qm.shape[0]
    bb = SAMPLE_BATCH
    tm = bb * dec_seq
    nb = m // dec_seq
    row = lambda n: pl.BlockSpec((tm, n), lambda i: (i, 0))
    whole = lambda shape: pl.BlockSpec(shape, lambda i: (0,) * len(shape))
    cache = pl.BlockSpec((bb, KV_HEADS, HEAD_DIM, WINDOW), lambda i: (i, 0, 0, 0))
    c_spec = pl.BlockSpec((bb * M_HEADS, M_HEAD_DIM, M_HEAD_DIM), lambda i: (i, 0, 0))
    n_spec = pl.BlockSpec((bb, M_WIDTH), lambda i: (i, 0))
    sds = jax.ShapeDtypeStruct
    return pl.pallas_call(
        functools.partial(_mixer_sample_kernel, dec_seq=dec_seq),
        grid=(nb // bb,),
        in_specs=[pl.BlockSpec(memory_space=pltpu.SMEM), row(ATTN_WIDTH), row(KV_WIDTH), row(KV_WIDTH), cache, cache,
                  row(M_WIDTH), row(M_WIDTH), row(M_WIDTH), row(M_WIDTH), row(GATE_PAD), row(LANES),
                  whole((1, GATE_PAD)), whole((1, M_WIDTH)), c_spec, n_spec],
        out_specs=[row(ATTN_WIDTH + M_WIDTH), cache, cache, c_spec, n_spec, row(LANES)],
        out_shape=[sds((m, ATTN_WIDTH + M_WIDTH), BF16), sds(cache_k.shape, F32), sds(cache_v.shape, F32),
                   sds(c_in.shape, F32), sds(n_in.shape, F32), sds((m, LANES), F32)],
        compiler_params=_params(("parallel",)),
        name="mixer_sample",
    )(sink, qa, ka, va, cache_k, cache_v, qm, km, vm, om, gt, m0_rows, bias_row, mnorm, c_in, n_in)


def _out_ffn_kernel(x_ref, mix_ref, wo_ref, g1_ref, g2_ref, wg_ref, wu_ref, wd_ref, g3_ref, o_ref):
    tm = x_ref.shape[0]
    groups = [slice(r, r + tm // ROW_GROUPS) for r in range(0, tm, tm // ROW_GROUPS)]
    ys = [jnp.dot(mix_ref[rs, :], wo_ref[...], preferred_element_type=F32) for rs in groups]
    x1s = [x_ref[rs, :] + _rms(y, g1_ref[...]) for rs, y in zip(groups, ys)]
    fs = [_rms(x1, g2_ref[...]).astype(BF16) for x1 in x1s]
    accs = [None] * ROW_GROUPS
    for off in range(0, D_FF, FFN_CHUNK):
        acts = []
        for f in fs:
            g = jnp.dot(f, wg_ref[:, off:off + FFN_CHUNK], preferred_element_type=F32)
            u = jnp.dot(f, wu_ref[:, off:off + FFN_CHUNK], preferred_element_type=F32)
            acts.append((g * jax.nn.sigmoid(g) * u).astype(BF16))
        for i, act in enumerate(acts):
            part = jnp.dot(act, wd_ref[off:off + FFN_CHUNK, :], preferred_element_type=F32)
            accs[i] = part if accs[i] is None else accs[i] + part
    for rs, x1, acc in zip(groups, x1s, accs):
        o_ref[rs, :] = x1 + _rms(acc, g3_ref[...])


def _out_ffn(x, mix, w_out, g_post_mix, g_pre_ffn, w_gate, w_up, w_down, g_post_ffn):
    m = x.shape[0]
    tm = TOKEN_TILE
    row = lambda n: pl.BlockSpec((tm, n), lambda i: (i, 0))
    vec = _const_spec((1, D_MODEL))
    return pl.pallas_call(
        _out_ffn_kernel,
        grid=(m // tm,),
        in_specs=[row(D_MODEL), row(ATTN_WIDTH + M_WIDTH), _const_spec((D_MODEL, D_MODEL)), vec, vec,
                  _const_spec((D_MODEL, D_FF)), _const_spec((D_MODEL, D_FF)), _const_spec((D_FF, D_MODEL)), vec],
        out_specs=row(D_MODEL),
        out_shape=jax.ShapeDtypeStruct((m, D_MODEL), F32),
        compiler_params=_params(("parallel",)),
        name="out_ffn",
    )(x, mix, w_out, g_post_mix, g_pre_ffn, w_gate, w_up, w_down, g_post_ffn)


def _layer(xp, xs, cache_k, cache_v, state_c, state_n, state_m, w_in, b_i, b_f, attn_sink, m_norm, w_out,
           g_pre_mix, g_post_mix, g_pre_ffn, g_post_ffn, w_gate, w_up, w_down):
    bp, sp, _ = xp.shape
    bs, ts, _ = xs.shape
    assert bp == 1 and sp % TOKEN_TILE == 0 and TOKEN_TILE % WINDOW == 0
    assert ts & (ts - 1) == 0 and (bs * ts) % TOKEN_TILE == 0 and bs % SAMPLE_BATCH == 0

    row = lambda v: v.reshape(1, -1)
    bias_row = jnp.pad(jnp.concatenate([b_i, b_f]), (0, GATE_PAD - 2 * M_HEADS)).reshape(1, GATE_PAD)
    sink = row(attn_sink)

    x2 = xp.reshape(sp, D_MODEL)
    mix, k_w, v_w, c_aug, m_p, w_pad, wg, wu, wd, wo = _mixer_prompt(
        sink, row(b_i), row(b_f), x2, row(g_pre_mix), w_in.T, row(m_norm), w_gate, w_up, w_down, w_out)
    ffn = (wo, row(g_post_mix), row(g_pre_ffn), wg, wu, wd, row(g_post_ffn))
    yp = _out_ffn(x2, mix, *ffn).reshape(xp.shape)
    k_p = k_w.reshape(1, WINDOW, KV_HEADS, HEAD_DIM)
    v_p = v_w.reshape(1, WINDOW, KV_HEADS, HEAD_DIM)
    c_p = jnp.swapaxes(c_aug[:, :M_HEAD_DIM, :], 1, 2).reshape(1, M_HEADS, M_HEAD_DIM, M_HEAD_DIM)
    n_p = c_aug[:, M_HEAD_DIM, :].reshape(1, M_HEADS, M_HEAD_DIM)
    m_p = m_p[:M_HEADS, 0].reshape(1, M_HEADS)

    x2 = xs.reshape(bs * ts, D_MODEL)
    qa, ka, va, qm, km, vm, om, gt = _inproj(x2, row(g_pre_mix), w_pad)
    feature_major = lambda c: jnp.transpose(c, (0, 2, 3, 1))
    m0_rows = jnp.pad(jnp.repeat(state_m, ts, axis=0), ((0, 0), (0, LANES - M_HEADS)))
    mix, k_s, v_s, c_s, n_s, m_rows = _mixer_sample(
        sink, qa, ka, va, feature_major(cache_k), feature_major(cache_v), qm, km, vm, om, gt, m0_rows, bias_row,
        row(m_norm), state_c.reshape(bs * M_HEADS, M_HEAD_DIM, M_HEAD_DIM), state_n.reshape(bs, M_WIDTH), ts)
    ys = _out_ffn(x2, mix, *ffn).reshape(xs.shape)
    k_s = jnp.transpose(k_s, (0, 3, 1, 2))
    v_s = jnp.transpose(v_s, (0, 3, 1, 2))
    c_s = c_s.reshape(bs, M_HEADS, M_HEAD_DIM, M_HEAD_DIM)
    n_s = n_s.reshape(bs, M_HEADS, M_HEAD_DIM)
    m_s = m_rows[ts - 1::ts, :M_HEADS]
    return yp, ys, (k_p, v_p, c_p, n_p, m_p), (k_s, v_s, c_s, n_s, m_s)


def kernel(x_prompt, x_sample, cache_k, cache_v, state_C, state_n, state_m, w_in, b_i, b_f, attn_sink, m_norm,
           w_out, g_pre_mix, g_post_mix, g_pre_ffn, g_post_ffn, w_gate, w_up, w_down):
    depth = w_in.shape[0]
    xp, xs = x_prompt, x_sample
    prompt_states, sample_states = [], []
    for l in range(depth):
        xp, xs, st_p, st_s = _layer(xp, xs, cache_k[l], cache_v[l], state_C[l], state_n[l], state_m[l],
                                    w_in[l], b_i[l], b_f[l], attn_sink[l], m_norm[l], w_out[l],
                                    g_pre_mix[l], g_post_mix[l], g_pre_ffn[l], g_post_ffn[l],
                                    w_gate[l], w_up[l], w_down[l])
        prompt_states.append(st_p)
        sample_states.append(st_s)
    stack = lambda states, i: jnp.stack([s[i] for s in states], axis=0)
    return (xp, xs) + tuple(stack(prompt_states, i) for i in range(5)) + tuple(stack(sample_states, i) for i in range(5))
```

```python
import functools

import jax
import jax.numpy as jnp
from jax import lax
from jax.experimental import pallas as pl
from jax.experimental.pallas import tpu as pltpu

F32 = jnp.float32
BF16 = jnp.bfloat16
HIGHEST = lax.Precision.HIGHEST

D_MODEL = 1024
HEAD_DIM = 64
ATTN_HEADS = 8
KV_HEADS = 2
GROUP = ATTN_HEADS // KV_HEADS
ATTN_WIDTH = ATTN_HEADS * HEAD_DIM
KV_WIDTH = KV_HEADS * HEAD_DIM
WINDOW = 128
M_HEADS = 4
M_HEAD_DIM = 128
M_WIDTH = M_HEADS * M_HEAD_DIM
M_PAIRS = M_HEADS // 2
D_FF = 2816
EPS = 1e-6

LANES = 128
SUBLANES = 8
GATE_PAD = LANES
BF16_SUBLANES = 16
AUG_ROWS = M_HEAD_DIM + BF16_SUBLANES
IN_MAIN = ATTN_WIDTH + 2 * KV_WIDTH + 4 * M_WIDTH
IN_PAD = IN_MAIN + GATE_PAD
VMEM_LIMIT = 56 * 1024 * 1024

HEAD_ORDER = tuple(h for j in range(GROUP) for h in (j, j + GROUP))

TOKEN_TILE = 512
FFN_CHUNK = 256
ROW_GROUPS = 2
WEIGHT_SLAB = 32
SAMPLE_BATCH = 16


def _rms(x, g):
    return x * lax.rsqrt(jnp.mean(x * x, axis=-1, keepdims=True) + EPS) * g


def _const_spec(shape):
    nd = len(shape)
    return pl.BlockSpec(shape, lambda i: (0,) * nd, pipeline_mode=pl.Buffered(1))


def _params(semantics):
    return pltpu.CompilerParams(dimension_semantics=semantics, vmem_limit_bytes=VMEM_LIMIT)


OFF_QA, OFF_KA, OFF_VA = 0, ATTN_WIDTH, ATTN_WIDTH + KV_WIDTH
OFF_QM = ATTN_WIDTH + 2 * KV_WIDTH
OFF_KM, OFF_VM, OFF_OM = OFF_QM + M_WIDTH, OFF_QM + 2 * M_WIDTH, OFF_QM + 3 * M_WIDTH


def _inproj_kernel(x_ref, g_ref, w_ref, qa_ref, ka_ref, va_ref, qm_ref, km_ref, vm_ref, om_ref, gt_ref):
    tm = x_ref.shape[0]
    step = tm // ROW_GROUPS
    for r0 in range(0, tm, step):
        rs = slice(r0, r0 + step)
        h = _rms(x_ref[rs, :], g_ref[...]).astype(BF16)

        def proj(off, n):
            return jnp.dot(h, w_ref[:, off:off + n], preferred_element_type=F32)

        qa_ref[rs, :] = proj(OFF_QA, ATTN_WIDTH) * (HEAD_DIM ** -0.5)
        ka_ref[rs, :] = proj(OFF_KA, KV_WIDTH)
        va_ref[rs, :] = proj(OFF_VA, KV_WIDTH)
        qm_ref[rs, :] = proj(OFF_QM, M_WIDTH)
        km_ref[rs, :] = proj(OFF_KM, M_WIDTH) * (M_HEAD_DIM ** -0.5)
        vm_ref[rs, :] = proj(OFF_VM, M_WIDTH)
        om_ref[rs, :] = proj(OFF_OM, M_WIDTH)
        gt_ref[rs, :] = proj(IN_MAIN, GATE_PAD)


def _inproj(x, g_pre, w_pad):
    m = x.shape[0]
    tm = TOKEN_TILE
    row = lambda n: pl.BlockSpec((tm, n), lambda i: (i, 0))
    widths = (ATTN_WIDTH, KV_WIDTH, KV_WIDTH, M_WIDTH, M_WIDTH, M_WIDTH, M_WIDTH, GATE_PAD)
    return pl.pallas_call(
        _inproj_kernel,
        grid=(m // tm,),
        in_specs=[row(D_MODEL), _const_spec((1, D_MODEL)), _const_spec((D_MODEL, IN_PAD))],
        out_specs=[row(n) for n in widths],
        out_shape=[jax.ShapeDtypeStruct((m, n), F32) for n in widths],
        compiler_params=_params(("parallel",)),
        name="inproj",
    )(x, g_pre, w_pad)


def _stack_heads(q_tiles):
    lane = lax.broadcasted_iota(jnp.int32, q_tiles[0].shape, 1)
    lo = lane < HEAD_DIM
    zero = jnp.zeros_like(q_tiles[0])
    parts = []
    for qt in q_tiles:
        parts += [jnp.where(lo, qt, zero), jnp.where(lo, zero, qt)]
    return jnp.concatenate(parts, axis=0)


def _unstack_heads(o, rows):
    lane = lax.broadcasted_iota(jnp.int32, (rows, LANES), 1)
    lo = lane < HEAD_DIM
    return [jnp.where(lo, o[(2 * j) * rows:(2 * j + 1) * rows, :], o[(2 * j + 1) * rows:(2 * j + 2) * rows, :])
            for j in range(GROUP)]


def _sink_rows(sink_ref, rows):
    return jnp.concatenate([jnp.full((rows, LANES), sink_ref[0, h], F32) for h in HEAD_ORDER], axis=0)


def _attn_sample_kernel(sink_ref, q_ref, kn_ref, vn_ref, ck_ref, cv_ref, o_ref, ko_ref, vo_ref, *, dec_seq):
    t = dec_seq
    bb = SAMPLE_BATCH
    nrow = ATTN_HEADS * t
    ts, ns = t.bit_length() - 1, nrow.bit_length() - 1
    r_c = lax.broadcasted_iota(jnp.int32, (nrow, WINDOW), 0) & (t - 1)
    c_c = lax.broadcasted_iota(jnp.int32, (nrow, WINDOW), 1)
    vis_cache = c_c > r_c
    r_n = lax.broadcasted_iota(jnp.int32, (bb * nrow, bb * t), 0)
    c_n = lax.broadcasted_iota(jnp.int32, (bb * nrow, bb * t), 1)
    vis_new = jnp.logical_and((r_n >> ns) == (c_n >> ts), (c_n & (t - 1)) <= (r_n & (t - 1)))
    sink = _sink_rows(sink_ref, t)
    nt = (((1,), (1,)), ((), ()))
    kn_all, vn_all = kn_ref[...], vn_ref[...]
    lane_w = lax.broadcasted_iota(jnp.int32, (KV_WIDTH, WINDOW), 1)
    is_new = lane_w >= WINDOW - t
    zero_rows = jnp.zeros((WINDOW - t, KV_WIDTH), F32)
    state_shape = (KV_HEADS, HEAD_DIM, WINDOW)

    def slide(old_t, new_rows):
        new_t = jnp.concatenate([zero_rows, new_rows], axis=0).T
        return jnp.where(is_new, new_t, pltpu.roll(old_t, WINDOW - t, axis=1)).reshape(state_shape)

    qs, s_c, cvs = [], [], []
    for b in range(bb):
        rows = slice(b * t, (b + 1) * t)
        ck = ck_ref[b].reshape(KV_WIDTH, WINDOW)
        cvs.append(cv_ref[b].reshape(KV_WIDTH, WINDOW))
        ko_ref[b] = slide(ck, kn_all[rows, :])
        vo_ref[b] = slide(cvs[b], vn_all[rows, :])
        qs.append(_stack_heads([q_ref[rows, j * LANES:(j + 1) * LANES] for j in range(GROUP)]).astype(BF16))
        s_c.append(jnp.where(vis_cache, jnp.dot(qs[b], ck.astype(BF16), preferred_element_type=F32), -jnp.inf))
    s_n = jnp.where(vis_new, lax.dot_general(jnp.concatenate(qs, axis=0), kn_all.astype(BF16), nt,
                                             preferred_element_type=F32), -jnp.inf)
    p_c, p_n, rden = [], [], []
    for b in range(bb):
        s_nb = s_n[b * nrow:(b + 1) * nrow, :]
        m = jnp.maximum(jnp.maximum(jnp.max(s_c[b], axis=-1, keepdims=True), jnp.max(s_nb, axis=-1, keepdims=True)),
                        sink)
        p_c.append(jnp.exp(s_c[b] - m))
        p_n.append(jnp.exp(s_nb - m[:, :bb * t]))
        rden.append(1.0 / (jnp.sum(p_c[b], axis=-1, keepdims=True) + jnp.sum(p_n[b], axis=-1, keepdims=True)
                           + jnp.exp(sink - m)))
    o_n = jnp.dot(jnp.concatenate(p_n, axis=0).astype(BF16), vn_all.astype(BF16), preferred_element_type=F32)
    outs = [[] for _ in range(GROUP)]
    for b in range(bb):
        o = (lax.dot_general(p_c[b].astype(BF16), cvs[b].astype(BF16), nt, preferred_element_type=F32)
             + o_n[b * nrow:(b + 1) * nrow, :]) * rden[b]
        for j, tile in enumerate(_unstack_heads(o, t)):
            outs[j].append(tile)
    for j, parts in enumerate(outs):
        o_ref[:, j * LANES:(j + 1) * LANES] = jnp.concatenate(parts, axis=0).astype(o_ref.dtype)


def _block_diag(a, b):
    za, zb = jnp.zeros_like(a), jnp.zeros_like(b)
    return jnp.concatenate([jnp.concatenate([a, zb], axis=1), jnp.concatenate([za, b], axis=1)], axis=0)


def _head_out(hh, mnorm_row, om):
    y = hh * lax.rsqrt(jnp.mean(hh * hh, axis=-1, keepdims=True) + EPS) * mnorm_row
    return (jax.nn.sigmoid(om) * y).astype(BF16)


def _mixer_prompt_kernel(sink_ref, bi_ref, bf_ref, x_ref, g_ref, win_ref, mnorm_ref, wg_ref, wu_ref, wd_ref, wo_ref,
                         mix_ref, kw_ref, vw_ref, c_ref, m_ref, w_ref, wgb_ref, wub_ref, wdb_ref, wob_ref):
    step = pl.program_id(0)

    @pl.when(step == 0)
    def _():
        kw_ref[...] = jnp.zeros_like(kw_ref)
        vw_ref[...] = jnp.zeros_like(vw_ref)
        c_ref[...] = jnp.zeros_like(c_ref)
        m_ref[...] = jnp.zeros_like(m_ref)
        for j in range(GROUP):
            lo = win_ref[j * HEAD_DIM:(j + 1) * HEAD_DIM, :]
            hi = win_ref[(j + GROUP) * HEAD_DIM:(j + GROUP + 1) * HEAD_DIM, :]
            w_ref[:, j * LANES:(j + 1) * LANES] = jnp.concatenate([lo, hi], axis=0).T.astype(BF16)
        for c0 in range(ATTN_WIDTH, IN_MAIN, M_WIDTH):
            c1 = min(c0 + M_WIDTH, IN_MAIN)
            w_ref[:, c0:c1] = win_ref[c0:c1, :].T.astype(BF16)
        gw = jnp.concatenate([win_ref[IN_MAIN:IN_MAIN + 2 * M_HEADS, :],
                              jnp.zeros((GATE_PAD - 2 * M_HEADS, D_MODEL), F32)], axis=0)
        w_ref[:, IN_MAIN:] = gw.T.astype(BF16)

    wgb_ref[...] = wg_ref[...].astype(BF16)
    wub_ref[...] = wu_ref[...].astype(BF16)
    wdb_ref[...] = wd_ref[...].astype(BF16)
    wob_ref[...] = wo_ref[...].astype(BF16)

    tm = x_ref.shape[0]
    ln = WINDOW
    groups = [slice(r0, r0 + tm // ROW_GROUPS) for r0 in range(0, tm, tm // ROW_GROUPS)]
    blocks = [slice(r0, r0 + ln) for r0 in range(0, tm, ln)]
    hcols = [slice(h * M_HEAD_DIM, (h + 1) * M_HEAD_DIM) for h in range(M_HEADS)]
    pcols = [slice(2 * p * M_HEAD_DIM, 2 * (p + 1) * M_HEAD_DIM) for p in range(M_PAIRS)]
    nt = (((1,), (1,)), ((), ()))
    hs = [_rms(x_ref[rs, :], g_ref[...]).astype(BF16) for rs in groups]

    def proj(off, n):
        return jnp.concatenate([jnp.dot(h, w_ref[:, off:off + n], preferred_element_type=F32) for h in hs], axis=0)

    qt = proj(OFF_QM, M_WIDTH).astype(BF16).T
    gates = proj(IN_MAIN, GATE_PAD).T
    km = (proj(OFF_KM, M_WIDTH) * (M_HEAD_DIM ** -0.5)).astype(BF16)
    r = lax.broadcasted_iota(jnp.int32, (ln, ln), 0)
    c = lax.broadcasted_iota(jnp.int32, (ln, ln), 1)
    causal_t = r <= c
    upper = causal_t.astype(F32)
    lane8 = lax.broadcasted_iota(jnp.int32, (SUBLANES, ln), 1)
    ones_rows = jnp.ones((AUG_ROWS - M_HEAD_DIM, ln), BF16)
    zrows = jnp.zeros((ln - SUBLANES, ln), F32)
    row8 = lax.broadcasted_iota(jnp.int32, (SUBLANES, ln), 0)
    bias = jnp.zeros((SUBLANES, ln), F32)
    for h in range(M_HEADS):
        bias = jnp.where(row8 == h, bi_ref[0, h], jnp.where(row8 == M_HEADS + h, bf_ref[0, h], bias))
    gis = [gates[0:SUBLANES, rows] + bias for rows in blocks]
    prefix = jnp.dot(jnp.concatenate([jax.nn.log_sigmoid(gi) for gi in gis], axis=0), upper, precision=HIGHEST,
                     preferred_element_type=F32)
    pre = []
    for blk, gi in enumerate(gis):
        b = pltpu.roll(prefix[blk * SUBLANES:(blk + 1) * SUBLANES, :], M_HEADS, axis=0)
        g = gi - b
        cm0 = g
        sh = 1
        while sh < ln:
            cm0 = jnp.maximum(cm0, jnp.where(lane8 >= sh, pltpu.roll(cm0, sh, axis=1), -jnp.inf))
            sh *= 2
        b_last = jnp.broadcast_to(b[:, ln - 1:ln], b.shape)
        g_max = jnp.broadcast_to(cm0[:, ln - 1:ln], b.shape)
        g_cols = jnp.concatenate([g, zrows], axis=0).T
        pre.append((b, g_cols, cm0, b_last, g_max, (b_last - b) + gi))
    scores_m = [[jnp.dot(km[rows, pc], _block_diag(qt[hcols[2 * p], rows], qt[hcols[2 * p + 1], rows]),
                         preferred_element_type=F32) for p, pc in enumerate(pcols)] for rows in blocks]
    vt = proj(OFF_VM, M_WIDTH).astype(BF16).T

    qa = (proj(OFF_QA, ATTN_WIDTH) * (HEAD_DIM ** -0.5)).astype(BF16)
    kv = proj(OFF_KA, 2 * KV_WIDTH)
    ka, va = kv[:, :KV_WIDTH], kv[:, KV_WIDTH:]
    nrow = ATTN_HEADS * ln
    ra = lax.broadcasted_iota(jnp.int32, (nrow, ln), 0) & (ln - 1)
    ca = lax.broadcasted_iota(jnp.int32, (nrow, ln), 1)
    own = ca <= ra
    sink = _sink_rows(sink_ref, ln)
    no_prev = jnp.where(step == 0, -jnp.inf, 0.0)
    ones = jnp.ones((2 * ln, LANES), BF16)
    kcats, v_augs = [], []
    for blk, rows in enumerate(blocks):
        if blk == 0:
            k_prev, v_prev = kw_ref[...], vw_ref[...]
        else:
            k_prev, v_prev = ka[blocks[blk - 1], :], va[blocks[blk - 1], :]
        kcats.append(jnp.concatenate([ka[rows, :], k_prev], axis=0).astype(BF16))
        vcat = jnp.concatenate([va[rows, :], v_prev], axis=0).astype(BF16)
        v_augs.append(jnp.concatenate([vcat, ones], axis=1))
    kw_ref[...] = ka[blocks[-1], :]
    vw_ref[...] = va[blocks[-1], :]

    m_prev = m_ref[...]
    scal = []
    for b, _, cm0, b_last, g_max, w_end_arg in pre:
        cm = jnp.maximum(cm0, m_prev)
        m_end = b_last + jnp.maximum(g_max, m_prev)
        scal.append((cm, jnp.exp(m_prev - cm), jnp.exp(-(b + cm)), jnp.exp(b_last + m_prev - m_end),
                     jnp.exp(w_end_arg - m_end)))
        m_prev = m_end
    m_ref[...] = m_prev
    gated = []
    for rows, s_t, (_, g_cols, *_), (cm, _, _, _, w_end) in zip(blocks, scores_m, pre, scal):
        vts = [jnp.concatenate([vt[hc, rows], ones_rows], axis=0) for hc in hcols]
        sqks = [(jnp.exp(jnp.where(causal_t, g_cols[:, h:h + 1] - cm[h:h + 1, :], -jnp.inf))
                 * s_t[h // 2][:, (h % 2) * ln:(h % 2 + 1) * ln]).astype(BF16) for h in range(M_HEADS)]
        kvws = [(vts[h].astype(F32) * w_end[h:h + 1, :]).astype(BF16) for h in range(M_HEADS)]
        gated.append((vts, sqks, kvws))
    pairs = range(M_PAIRS)
    upds = [[jnp.dot(jnp.concatenate([kvws[2 * p], kvws[2 * p + 1]], axis=1),
                     _block_diag(km[rows, hcols[2 * p]], km[rows, hcols[2 * p + 1]]),
                     preferred_element_type=F32) for p in pairs] for rows, (_, _, kvws) in zip(blocks, gated)]
    intras = [[jnp.dot(jnp.concatenate([vts[2 * p], vts[2 * p + 1]], axis=1),
                       _block_diag(sqks[2 * p], sqks[2 * p + 1]), preferred_element_type=F32) for p in pairs]
              for vts, sqks, _ in gated]

    scores_a = [lax.dot_general(_stack_heads([qa[rows, j * LANES:(j + 1) * LANES] for j in range(GROUP)]), kcat, nt,
                                preferred_element_type=F32) for rows, kcat in zip(blocks, kcats)]
    probs, maxes = [], []
    for blk, s in enumerate(scores_a):
        s_prev = s[:, ln:]
        if blk == 0:
            s_prev = s_prev + no_prev
        sc = jnp.where(own, s[:, :ln], s_prev)
        mx = jnp.maximum(jnp.max(sc, axis=-1, keepdims=True), sink)
        p = jnp.exp(sc - mx)
        zero = jnp.zeros_like(p)
        probs.append(jnp.concatenate([jnp.where(own, p, zero), jnp.where(own, zero, p)], axis=1).astype(BF16))
        maxes.append(mx)
    om = proj(OFF_OM, M_WIDTH)

    mem = [jnp.concatenate([c_ref[2 * p], c_ref[2 * p + 1]], axis=1) for p in range(M_PAIRS)]
    pair_row = lambda x, p: jnp.concatenate([x[2 * p:2 * p + 1, :], x[2 * p + 1:2 * p + 2, :]], axis=1)
    for rows, upd, intra, (_, w_inter, e_negm, dec, _), p2, v_aug, mx in zip(
            blocks, upds, intras, scal, probs, v_augs, maxes):
        inter = [jnp.dot(mem[p].astype(BF16), _block_diag(qt[hcols[2 * p], rows], qt[hcols[2 * p + 1], rows]),
                         preferred_element_type=F32) for p in pairs]
        mem = [pair_row(dec, p) * mem[p] + upd[p] for p in pairs]
        o = jnp.dot(p2, v_aug, preferred_element_type=F32)
        o = o[:, :LANES] * (1.0 / (o[:, LANES:] + jnp.exp(sink - mx)))
        for j, tile in enumerate(_unstack_heads(o, ln)):
            mix_ref[rows, j * LANES:(j + 1) * LANES] = tile.astype(mix_ref.dtype)
        for p in pairs:
            num = inter[p] * pair_row(w_inter, p) + intra[p]
            den = jnp.maximum(jnp.abs(num[M_HEAD_DIM:M_HEAD_DIM + 1, :]), pair_row(e_negm, p))
            hh = num[:M_HEAD_DIM, :] * (1.0 / den)
            y = hh * lax.rsqrt(jnp.mean(hh * hh, axis=0, keepdims=True) + EPS)
            for h in (2 * p, 2 * p + 1):
                y_h = y[:, (h % 2) * ln:(h % 2 + 1) * ln]
                mcols = slice(ATTN_WIDTH + h * M_HEAD_DIM, ATTN_WIDTH + (h + 1) * M_HEAD_DIM)
                mix_ref[rows, mcols] = (jax.nn.sigmoid(om[rows, hcols[h]])
                                        * (y_h.T * mnorm_ref[:, hcols[h]])).astype(mix_ref.dtype)
    for p in range(M_PAIRS):
        c_ref[2 * p] = mem[p][:, :M_HEAD_DIM]
        c_ref[2 * p + 1] = mem[p][:, M_HEAD_DIM:]


def _out_row_block(i):
    per_head = (ATTN_WIDTH // ATTN_HEADS) // WEIGHT_SLAB
    j, part = i // per_head, i % per_head
    head = (j % 2) * GROUP + j // 2
    return jnp.where(i < ATTN_HEADS * per_head, head * per_head + part, i)


def _mixer_prompt(sink, b_i, b_f, x, g_pre, w_in_t, mnorm, w_gate, w_up, w_down, w_out):
    m = x.shape[0]
    tm = TOKEN_TILE
    steps = m // tm
    assert D_MODEL == steps * WEIGHT_SLAB and D_FF % (steps // 2) == 0 and (D_FF // (steps // 2)) % BF16_SUBLANES == 0
    down_slab = D_FF // (steps // 2)
    row = lambda n: pl.BlockSpec((tm, n), lambda i: (i, 0))
    whole = lambda shape: pl.BlockSpec(shape, lambda i: (0,) * len(shape))
    smem = pl.BlockSpec(memory_space=pltpu.SMEM)
    slab = lambda n: pl.BlockSpec((WEIGHT_SLAB, n), lambda i: (i, 0))
    down = pl.BlockSpec((down_slab, D_MODEL), lambda i: (i // 2, 0))
    c_shape = (M_HEADS, AUG_ROWS, M_HEAD_DIM)
    w_shape = (WINDOW, KV_WIDTH)
    s_shape = (SUBLANES, LANES)
    sds = jax.ShapeDtypeStruct
    return pl.pallas_call(
        _mixer_prompt_kernel,
        grid=(steps,),
        in_specs=[smem, smem, smem, row(D_MODEL), _const_spec((1, D_MODEL)), _const_spec(w_in_t.shape),
                  _const_spec((1, M_WIDTH)), slab(D_FF), slab(D_FF), down,
                  pl.BlockSpec((WEIGHT_SLAB, D_MODEL), lambda i: (_out_row_block(i), 0))],
        out_specs=[row(ATTN_WIDTH + M_WIDTH), whole(w_shape), whole(w_shape), whole(c_shape), whole(s_shape),
                   whole((D_MODEL, IN_PAD)), slab(D_FF), slab(D_FF), down, slab(D_MODEL)],
        out_shape=[sds((m, ATTN_WIDTH + M_WIDTH), BF16), sds(w_shape, F32), sds(w_shape, F32), sds(c_shape, F32),
                   sds(s_shape, F32), sds((D_MODEL, IN_PAD), BF16), sds(w_gate.shape, BF16), sds(w_up.shape, BF16),
                   sds(w_down.shape, BF16), sds(w_out.shape, BF16)],
        compiler_params=_params(("arbitrary",)),
        name="mixer_prompt",
    )(sink, b_i, b_f, x, g_pre, w_in_t, mnorm, w_gate, w_up, w_down, w_out)


def _gates(g_blk, bias_row, cum):
    pre = g_blk + bias_row
    lane = lax.broadcasted_iota(jnp.int32, pre.shape, 1)
    a = jnp.where(lane < M_HEADS, pre, jax.nn.log_sigmoid(pre))
    b = jnp.dot(cum, a, precision=HIGHEST, preferred_element_type=F32)
    return a, b, a.T, b.T


def _col(x, j):
    return jnp.broadcast_to(x[:, j:j + 1], x.shape)


def _mlstm_gated_scores(q, k, a, b, at, bt, h, mask, m_prev, last):
    bc, ic = _col(b, M_HEADS + h), _col(a, h)
    br, ir = bt[M_HEADS + h:M_HEADS + h + 1, :], at[h:h + 1, :]
    d = jnp.where(mask, (bc - br) + ir, -jnp.inf)
    inter = bc + m_prev
    m_t = jnp.maximum(inter, jnp.max(d, axis=-1, keepdims=True))
    w_inter = jnp.exp(inter - m_t)
    sqk = jnp.exp(d - m_t) * lax.dot_general(q, k, (((1,), (1,)), ((), ())), preferred_element_type=F32)
    m_end = last(m_t)
    bl = last(bc)
    dec = jnp.exp(bl + m_prev - m_end)
    w_end = jnp.exp((bl - bc) + ic - m_end)
    kw = k.astype(F32) * w_end
    return w_inter, sqk, m_t, m_end, dec, kw


def _mlstm_sample_kernel(q_ref, k_ref, v_ref, om_ref, g_ref, m0_ref, bias_ref, mnorm_ref, c_ref, n_ref,
                         o_ref, c_out, n_out, m_out, *, dec_seq):
    t = dec_seq
    bb = SAMPLE_BATCH
    ln = bb * t
    r = lax.broadcasted_iota(jnp.int32, (ln, ln), 0)
    c = lax.broadcasted_iota(jnp.int32, (ln, ln), 1)
    shift = t.bit_length() - 1
    same = (r >> shift) == (c >> shift)
    mask = jnp.logical_and(same, c <= r)
    cum = mask.astype(F32)
    expand = (c == (r >> shift)).astype(F32)
    gather = ((c >> shift) == r).astype(F32)
    is_last = (r & (t - 1)) == t - 1
    is_first = (r & (t - 1)) == 0

    def last(x):
        y = jnp.where(is_last, x, 0.0)
        step = 1
        while step < t:
            y = y + pltpu.roll(y, ln - step, axis=0)
            step *= 2
        return y

    a, b, at, bt = _gates(g_ref[...], bias_ref[...], cum)
    m0 = m0_ref[...]
    lane = lax.broadcasted_iota(jnp.int32, (ln, LANES), 1)
    zpad = jnp.zeros((ln - bb, M_HEAD_DIM), F32)
    heads = range(M_HEADS)
    hcols = [slice(h * M_HEAD_DIM, (h + 1) * M_HEAD_DIM) for h in heads]
    qfs = [q_ref[:, hc] for hc in hcols]
    vs = [v_ref[:, hc].astype(BF16) for hc in hcols]
    qcs = [jnp.concatenate(
        [jnp.dot(qfs[h][s * t:(s + 1) * t, :], c_ref[s * M_HEADS + h].astype(BF16).astype(F32),
                 preferred_element_type=F32) for s in range(bb)], axis=0) for h in heads]
    n_exps = [jnp.dot(expand, jnp.concatenate([n_ref[:, hc], zpad], axis=0), precision=HIGHEST,
                      preferred_element_type=F32) for hc in hcols]
    parts = [_mlstm_gated_scores(qfs[h].astype(BF16), k_ref[:, hcols[h]].astype(BF16), a, b, at, bt, h, mask,
                                 _col(m0, h), last) for h in heads]
    intras = [jnp.dot(parts[h][1].astype(BF16), vs[h], preferred_element_type=F32) for h in heads]
    m_cols = jnp.zeros((ln, LANES), F32)
    kwts = []
    for h in heads:
        w_inter, sqk, m_t, m_end, dec, kw = parts[h]
        num = w_inter * qcs[h] + intras[h]
        nq = w_inter * jnp.sum(qfs[h] * n_exps[h], axis=-1, keepdims=True) + jnp.sum(sqk, axis=-1, keepdims=True)
        hh = num / jnp.maximum(jnp.abs(nq), jnp.exp(-m_t))
        o_ref[:, hcols[h]] = _head_out(hh, mnorm_ref[:, hcols[h]], om_ref[:, hcols[h]])
        kwts.append(kw.T.astype(BF16))
        m_cols = jnp.where(lane == h, m_end, m_cols)
    m_out[...] = m_cols
    for h in heads:
        dec = parts[h][4]
        for s in range(bb):
            lhs = jnp.where((c >> shift) == s, kwts[h], jnp.zeros_like(kwts[h]))
            upd = jnp.dot(lhs, vs[h], preferred_element_type=F32)
            c_out[s * M_HEADS + h] = dec[s * t:s * t + 1, :] * c_ref[s * M_HEADS + h] + upd
    for h in heads:
        dec, kw = parts[h][4], parts[h][5]
        n_new = jnp.dot(gather, jnp.where(is_first, dec * n_exps[h], 0.0) + kw, precision=HIGHEST,
                        preferred_element_type=F32)
        n_out[:, hcols[h]] = n_new[:bb, :]


def _mixer_sample_kernel(sink_ref, qa_ref, kn_ref, vn_ref, ck_ref, cv_ref, qm_ref, km_ref, vm_ref, om_ref, g_ref,
                         m0_ref, bias_ref, mnorm_ref, c_ref, n_ref, mix_ref, ko_ref, vo_ref, c_out, n_out, m_out,
                         *, dec_seq):
    _attn_sample_kernel(sink_ref, qa_ref, kn_ref, vn_ref, ck_ref, cv_ref, mix_ref.at[:, pl.ds(0, ATTN_WIDTH)],
                        ko_ref, vo_ref, dec_seq=dec_seq)
    _mlstm_sample_kernel(qm_ref, km_ref, vm_ref, om_ref, g_ref, m0_ref, bias_ref, mnorm_ref, c_ref, n_ref,
                         mix_ref.at[:, pl.ds(ATTN_WIDTH, M_WIDTH)], c_out, n_out, m_out, dec_seq=dec_seq)


def _mixer_sample(sink, qa, ka, va, cache_k, cache_v, qm, km, vm, om, gt, m0_rows, bias_row, mnorm, c_in, n_in,
                  dec_seq):
    m = qm.shape[0]
    bb = SAMPLE_BATCH
    tm = bb * dec_seq
    nb = m // dec_seq
    row = lambda n: pl.BlockSpec((tm, n), lambda i: (i, 0))
    whole = lambda shape: pl.BlockSpec(shape, lambda i: (0,) * len(shape))
    cache = pl.BlockSpec((bb, KV_HEADS, HEAD_DIM, WINDOW), lambda i: (i, 0, 0, 0))
    c_spec = pl.BlockSpec((bb * M_HEADS, M_HEAD_DIM, M_HEAD_DIM), lambda i: (i, 0, 0))
    n_spec = pl.BlockSpec((bb, M_WIDTH), lambda i: (i, 0))
    sds = jax.ShapeDtypeStruct
    return pl.pallas_call(
        functools.partial(_mixer_sample_kernel, dec_seq=dec_seq),
        grid=(nb // bb,),
        in_specs=[pl.BlockSpec(memory_space=pltpu.SMEM), row(ATTN_WIDTH), row(KV_WIDTH), row(KV_WIDTH), cache, cache,
                  row(M_WIDTH), row(M_WIDTH), row(M_WIDTH), row(M_WIDTH), row(GATE_PAD), row(LANES),
                  whole((1, GATE_PAD)), whole((1, M_WIDTH)), c_spec, n_spec],
        out_specs=[row(ATTN_WIDTH + M_WIDTH), cache, cache, c_spec, n_spec, row(LANES)],
        out_shape=[sds((m, ATTN_WIDTH + M_WIDTH), BF16), sds(cache_k.shape, F32), sds(cache_v.shape, F32),
                   sds(c_in.shape, F32), sds(n_in.shape, F32), sds((m, LANES), F32)],
        compiler_params=_params(("parallel",)),
        name="mixer_sample",
    )(sink, qa, ka, va, cache_k, cache_v, qm, km, vm, om, gt, m0_rows, bias_row, mnorm, c_in, n_in)


def _out_ffn_kernel(x_ref, mix_ref, wo_ref, g1_ref, g2_ref, wg_ref, wu_ref, wd_ref, g3_ref, o_ref):
    tm = x_ref.shape[0]
    groups = [slice(r, r + tm // ROW_GROUPS) for r in range(0, tm, tm // ROW_GROUPS)]
    ys = [jnp.dot(mix_ref[rs, :], wo_ref[...], preferred_element_type=F32) for rs in groups]
    x1s = [x_ref[rs, :] + _rms(y, g1_ref[...]) for rs, y in zip(groups, ys)]
    fs = [_rms(x1, g2_ref[...]).astype(BF16) for x1 in x1s]
    accs = [None] * ROW_GROUPS
    for off in range(0, D_FF, FFN_CHUNK):
        acts = []
        for f in fs:
            g = jnp.dot(f, wg_ref[:, off:off + FFN_CHUNK], preferred_element_type=F32)
            u = jnp.dot(f, wu_ref[:, off:off + FFN_CHUNK], preferred_element_type=F32)
            acts.append((g * jax.nn.sigmoid(g) * u).astype(BF16))
        for i, act in enumerate(acts):
            part = jnp.dot(act, wd_ref[off:off + FFN_CHUNK, :], preferred_element_type=F32)
            accs[i] = part if accs[i] is None else accs[i] + part
    for rs, x1, acc in zip(groups, x1s, accs):
        o_ref[rs, :] = x1 + _rms(acc, g3_ref[...])


def _out_ffn(x, mix, w_out, g_post_mix, g_pre_ffn, w_gate, w_up, w_down, g_post_ffn):
    m = x.shape[0]
    tm = TOKEN_TILE
    row = lambda n: pl.BlockSpec((tm, n), lambda i: (i, 0))
    vec = _const_spec((1, D_MODEL))
    return pl.pallas_call(
        _out_ffn_kernel,
        grid=(m // tm,),
        in_specs=[row(D_MODEL), row(ATTN_WIDTH + M_WIDTH), _const_spec((D_MODEL, D_MODEL)), vec, vec,
                  _const_spec((D_MODEL, D_FF)), _const_spec((D_MODEL, D_FF)), _const_spec((D_FF, D_MODEL)), vec],
        out_specs=row(D_MODEL),
        out_shape=jax.ShapeDtypeStruct((m, D_MODEL), F32),
        compiler_params=_params(("parallel",)),
        name="out_ffn",
    )(x, mix, w_out, g_post_mix, g_pre_ffn, w_gate, w_up, w_down, g_post_ffn)


def _layer(xp, xs, cache_k, cache_v, state_c, state_n, state_m, w_in, b_i, b_f, attn_sink, m_norm, w_out,
           g_pre_mix, g_post_mix, g_pre_ffn, g_post_ffn, w_gate, w_up, w_down):
    bp, sp, _ = xp.shape
    bs, ts, _ = xs.shape
    assert bp == 1 and sp % TOKEN_TILE == 0 and TOKEN_TILE % WINDOW == 0
    assert ts & (ts - 1) == 0 and (bs * ts) % TOKEN_TILE == 0 and bs % SAMPLE_BATCH == 0

    row = lambda v: v.reshape(1, -1)
    bias_row = jnp.pad(jnp.concatenate([b_i, b_f]), (0, GATE_PAD - 2 * M_HEADS)).reshape(1, GATE_PAD)
    sink = row(attn_sink)

    x2 = xp.reshape(sp, D_MODEL)
    mix, k_w, v_w, c_aug, m_p, w_pad, wg, wu, wd, wo = _mixer_prompt(
        sink, row(b_i), row(b_f), x2, row(g_pre_mix), w_in.T, row(m_norm), w_gate, w_up, w_down, w_out)
    ffn = (wo, row(g_post_mix), row(g_pre_ffn), wg, wu, wd, row(g_post_ffn))
    yp = _out_ffn(x2, mix, *ffn).reshape(xp.shape)
    k_p = k_w.reshape(1, WINDOW, KV_HEADS, HEAD_DIM)
    v_p = v_w.reshape(1, WINDOW, KV_HEADS, HEAD_DIM)
    c_p = jnp.swapaxes(c_aug[:, :M_HEAD_DIM, :], 1, 2).reshape(1, M_HEADS, M_HEAD_DIM, M_HEAD_DIM)
    n_p = c_aug[:, M_HEAD_DIM, :].reshape(1, M_HEADS, M_HEAD_DIM)
    m_p = m_p[:M_HEADS, 0].reshape(1, M_HEADS)

    x2 = xs.reshape(bs * ts, D_MODEL)
    qa, ka, va, qm, km, vm, om, gt = _inproj(x2, row(g_pre_mix), w_pad)
    feature_major = lambda c: jnp.transpose(c, (0, 2, 3, 1))
    m0_rows = jnp.pad(jnp.repeat(state_m, ts, axis=0), ((0, 0), (0, LANES - M_HEADS)))
    mix, k_s, v_s, c_s, n_s, m_rows = _mixer_sample(
        sink, qa, ka, va, feature_major(cache_k), feature_major(cache_v), qm, km, vm, om, gt, m0_rows, bias_row,
        row(m_norm), state_c.reshape(bs * M_HEADS, M_HEAD_DIM, M_HEAD_DIM), state_n.reshape(bs, M_WIDTH), ts)
    ys = _out_ffn(x2, mix, *ffn).reshape(xs.shape)
    k_s = jnp.transpose(k_s, (0, 3, 1, 2))
    v_s = jnp.transpose(v_s, (0, 3, 1, 2))
    c_s = c_s.reshape(bs, M_HEADS, M_HEAD_DIM, M_HEAD_DIM)
    n_s = n_s.reshape(bs, M_HEADS, M_HEAD_DIM)
    m_s = m_rows[ts - 1::ts, :M_HEADS]
    return yp, ys, (k_p, v_p, c_p, n_p, m_p), (k_s, v_s, c_s, n_s, m_s)


def kernel(x_prompt, x_sample, cache_k, cache_v, state_C, state_n, state_m, w_in, b_i, b_f, attn_sink, m_norm,
           w_out, g_pre_mix, g_post_mix, g_pre_ffn, g_post_ffn, w_gate, w_up, w_down):
    depth = w_in.shape[0]
    xp, xs = x_prompt, x_sample
    prompt_states, sample_states = [], []
    for l in range(depth):
        xp, xs, st_p, st_s = _layer(xp, xs, cache_k[l], cache_v[l], state_C[l], state_n[l], state_m[l],
                                    w_in[l], b_i[l], b_f[l], attn_sink[l], m_norm[l], w_out[l],
                                    g_pre_mix[l], g_post_mix[l], g_pre_ffn[l], g_post_ffn[l],
                                    w_gate[l], w_up[l], w_down[l])
        prompt_states.append(st_p)
        sample_states.append(st_s)
    stack = lambda states, i: jnp.stack([s[i] for s in states], axis=0)
    return (xp, xs) + tuple(stack(prompt_states, i) for i in range(5)) + tuple(stack(sample_states, i) for i in range(5))
```

```python
import functools

import jax
import jax.numpy as jnp
from jax import lax
from jax.experimental import pallas as pl
from jax.experimental.pallas import tpu as pltpu

F32 = jnp.float32
BF16 = jnp.bfloat16
HIGHEST = lax.Precision.HIGHEST

D_MODEL = 1024
HEAD_DIM = 64
ATTN_HEADS = 8
KV_HEADS = 2
GROUP = ATTN_HEADS // KV_HEADS
ATTN_WIDTH = ATTN_HEADS * HEAD_DIM
KV_WIDTH = KV_HEADS * HEAD_DIM
WINDOW = 128
M_HEADS = 4
M_HEAD_DIM = 128
M_WIDTH = M_HEADS * M_HEAD_DIM
M_PAIRS = M_HEADS // 2
D_FF = 2816
EPS = 1e-6

LANES = 128
SUBLANES = 8
GATE_PAD = LANES
BF16_SUBLANES = 16
AUG_ROWS = M_HEAD_DIM + BF16_SUBLANES
IN_MAIN = ATTN_WIDTH + 2 * KV_WIDTH + 4 * M_WIDTH
IN_PAD = IN_MAIN + GATE_PAD
VMEM_LIMIT = 56 * 1024 * 1024

HEAD_ORDER = tuple(h for j in range(GROUP) for h in (j, j + GROUP))

TOKEN_TILE = 512
FFN_CHUNK = 256
ROW_GROUPS = 2
FFN_GROUP_ROWS = (256, 256)
FFN_GROUP_LAG = (0, 1)
WEIGHT_SLAB = 32
SAMPLE_BATCH = 16


def _rms(x, g):
    return x * lax.rsqrt(jnp.mean(x * x, axis=-1, keepdims=True) + EPS) * g


def _const_spec(shape):
    nd = len(shape)
    return pl.BlockSpec(shape, lambda i: (0,) * nd, pipeline_mode=pl.Buffered(1))


def _params(semantics):
    return pltpu.CompilerParams(dimension_semantics=semantics, vmem_limit_bytes=VMEM_LIMIT)


OFF_QA, OFF_KA, OFF_VA = 0, ATTN_WIDTH, ATTN_WIDTH + KV_WIDTH
OFF_QM = ATTN_WIDTH + 2 * KV_WIDTH
OFF_KM, OFF_VM, OFF_OM = OFF_QM + M_WIDTH, OFF_QM + 2 * M_WIDTH, OFF_QM + 3 * M_WIDTH


def _inproj_kernel(x_ref, g_ref, w_ref, qa_ref, ka_ref, va_ref, qm_ref, km_ref, vm_ref, om_ref, gt_ref):
    tm = x_ref.shape[0]
    step = tm // ROW_GROUPS
    for r0 in range(0, tm, step):
        rs = slice(r0, r0 + step)
        h = _rms(x_ref[rs, :], g_ref[...]).astype(BF16)

        def proj(off, n):
            return jnp.dot(h, w_ref[:, off:off + n], preferred_element_type=F32)

        qa_ref[rs, :] = proj(OFF_QA, ATTN_WIDTH) * (HEAD_DIM ** -0.5)
        ka_ref[rs, :] = proj(OFF_KA, KV_WIDTH)
        va_ref[rs, :] = proj(OFF_VA, KV_WIDTH)
        qm_ref[rs, :] = proj(OFF_QM, M_WIDTH)
        km_ref[rs, :] = proj(OFF_KM, M_WIDTH) * (M_HEAD_DIM ** -0.5)
        vm_ref[rs, :] = proj(OFF_VM, M_WIDTH)
        om_ref[rs, :] = proj(OFF_OM, M_WIDTH)
        gt_ref[rs, :] = proj(IN_MAIN, GATE_PAD)


def _inproj(x, g_pre, w_pad):
    m = x.shape[0]
    tm = TOKEN_TILE
    row = lambda n: pl.BlockSpec((tm, n), lambda i: (i, 0))
    widths = (ATTN_WIDTH, KV_WIDTH, KV_WIDTH, M_WIDTH, M_WIDTH, M_WIDTH, M_WIDTH, GATE_PAD)
    return pl.pallas_call(
        _inproj_kernel,
        grid=(m // tm,),
        in_specs=[row(D_MODEL), _const_spec((1, D_MODEL)), _const_spec((D_MODEL, IN_PAD))],
        out_specs=[row(n) for n in widths],
        out_shape=[jax.ShapeDtypeStruct((m, n), F32) for n in widths],
        compiler_params=_params(("parallel",)),
        name="inproj",
    )(x, g_pre, w_pad)


def _stack_heads(q_tiles):
    lane = lax.broadcasted_iota(jnp.int32, q_tiles[0].shape, 1)
    lo = lane < HEAD_DIM
    zero = jnp.zeros_like(q_tiles[0])
    parts = []
    for qt in q_tiles:
        parts += [jnp.where(lo, qt, zero), jnp.where(lo, zero, qt)]
    return jnp.concatenate(parts, axis=0)


def _unstack_heads(o, rows):
    lane = lax.broadcasted_iota(jnp.int32, (rows, LANES), 1)
    lo = lane < HEAD_DIM
    return [jnp.where(lo, o[(2 * j) * rows:(2 * j + 1) * rows, :], o[(2 * j + 1) * rows:(2 * j + 2) * rows, :])
            for j in range(GROUP)]


def _sink_rows(sink_ref, rows):
    return jnp.concatenate([jnp.full((rows, LANES), sink_ref[0, h], F32) for h in HEAD_ORDER], axis=0)


def _attn_sample_kernel(sink_ref, q_ref, kn_ref, vn_ref, ck_ref, cv_ref, o_ref, ko_ref, vo_ref, *, dec_seq):
    t = dec_seq
    bb = SAMPLE_BATCH
    nrow = ATTN_HEADS * t
    ts, ns = t.bit_length() - 1, nrow.bit_length() - 1
    r_c = lax.broadcasted_iota(jnp.int32, (nrow, WINDOW), 0) & (t - 1)
    c_c = lax.broadcasted_iota(jnp.int32, (nrow, WINDOW), 1)
    vis_cache = c_c > r_c
    r_n = lax.broadcasted_iota(jnp.int32, (bb * nrow, bb * t), 0)
    c_n = lax.broadcasted_iota(jnp.int32, (bb * nrow, bb * t), 1)
    vis_new = jnp.logical_and((r_n >> ns) == (c_n >> ts), (c_n & (t - 1)) <= (r_n & (t - 1)))
    sink = _sink_rows(sink_ref, t)
    nt = (((1,), (1,)), ((), ()))
    kn_all, vn_all = kn_ref[...], vn_ref[...]
    lane_w = lax.broadcasted_iota(jnp.int32, (KV_WIDTH, WINDOW), 1)
    is_new = lane_w >= WINDOW - t
    zero_rows = jnp.zeros((WINDOW - t, KV_WIDTH), F32)
    state_shape = (KV_HEADS, HEAD_DIM, WINDOW)

    def slide(old_t, new_rows):
        new_t = jnp.concatenate([zero_rows, new_rows], axis=0).T
        return jnp.where(is_new, new_t, pltpu.roll(old_t, WINDOW - t, axis=1)).reshape(state_shape)

    qs, s_c, cvs = [], [], []
    for b in range(bb):
        rows = slice(b * t, (b + 1) * t)
        ck = ck_ref[b].reshape(KV_WIDTH, WINDOW)
        cvs.append(cv_ref[b].reshape(KV_WIDTH, WINDOW))
        ko_ref[b] = slide(ck, kn_all[rows, :])
        vo_ref[b] = slide(cvs[b], vn_all[rows, :])
        qs.append(_stack_heads([q_ref[rows, j * LANES:(j + 1) * LANES] for j in range(GROUP)]).astype(BF16))
        s_c.append(jnp.where(vis_cache, jnp.dot(qs[b], ck.astype(BF16), preferred_element_type=F32), -jnp.inf))
    s_n = jnp.where(vis_new, lax.dot_general(jnp.concatenate(qs, axis=0), kn_all.astype(BF16), nt,
                                             preferred_element_type=F32), -jnp.inf)
    p_c, p_n, rden = [], [], []
    for b in range(bb):
        s_nb = s_n[b * nrow:(b + 1) * nrow, :]
        m = jnp.maximum(jnp.maximum(jnp.max(s_c[b], axis=-1, keepdims=True), jnp.max(s_nb, axis=-1, keepdims=True)),
                        sink)
        p_c.append(jnp.exp(s_c[b] - m))
        p_n.append(jnp.exp(s_nb - m[:, :bb * t]))
        rden.append(1.0 / (jnp.sum(p_c[b], axis=-1, keepdims=True) + jnp.sum(p_n[b], axis=-1, keepdims=True)
                           + jnp.exp(sink - m)))
    o_n = jnp.dot(jnp.concatenate(p_n, axis=0).astype(BF16), vn_all.astype(BF16), preferred_element_type=F32)
    outs = [[] for _ in range(GROUP)]
    for b in range(bb):
        o = (lax.dot_general(p_c[b].astype(BF16), cvs[b].astype(BF16), nt, preferred_element_type=F32)
             + o_n[b * nrow:(b + 1) * nrow, :]) * rden[b]
        for j, tile in enumerate(_unstack_heads(o, t)):
            outs[j].append(tile)
    for j, parts in enumerate(outs):
        o_ref[:, j * LANES:(j + 1) * LANES] = jnp.concatenate(parts, axis=0).astype(o_ref.dtype)


def _block_diag(a, b):
    za, zb = jnp.zeros_like(a), jnp.zeros_like(b)
    return jnp.concatenate([jnp.concatenate([a, zb], axis=1), jnp.concatenate([za, b], axis=1)], axis=0)


def _head_out(hh, mnorm_row, om):
    y = hh * lax.rsqrt(jnp.mean(hh * hh, axis=-1, keepdims=True) + EPS) * mnorm_row
    return (jax.nn.sigmoid(om) * y).astype(BF16)


def _mixer_prompt_kernel(sink_ref, bi_ref, bf_ref, x_ref, g_ref, win_ref, mnorm_ref, wg_ref, wu_ref, wd_ref, wo_ref,
                         mix_ref, kw_ref, vw_ref, c_ref, m_ref, w_ref, wgb_ref, wub_ref, wdb_ref, wob_ref):
    step = pl.program_id(0)

    @pl.when(step == 0)
    def _():
        kw_ref[...] = jnp.zeros_like(kw_ref)
        vw_ref[...] = jnp.zeros_like(vw_ref)
        c_ref[...] = jnp.zeros_like(c_ref)
        m_ref[...] = jnp.zeros_like(m_ref)
        for j in range(GROUP):
            lo = win_ref[j * HEAD_DIM:(j + 1) * HEAD_DIM, :]
            hi = win_ref[(j + GROUP) * HEAD_DIM:(j + GROUP + 1) * HEAD_DIM, :]
            w_ref[:, j * LANES:(j + 1) * LANES] = jnp.concatenate([lo, hi], axis=0).T.astype(BF16)
        for c0 in range(ATTN_WIDTH, IN_MAIN, M_WIDTH):
            c1 = min(c0 + M_WIDTH, IN_MAIN)
            w_ref[:, c0:c1] = win_ref[c0:c1, :].T.astype(BF16)
        gw = jnp.concatenate([win_ref[IN_MAIN:IN_MAIN + 2 * M_HEADS, :],
                              jnp.zeros((GATE_PAD - 2 * M_HEADS, D_MODEL), F32)], axis=0)
        w_ref[:, IN_MAIN:] = gw.T.astype(BF16)

    wgb_ref[...] = wg_ref[...].astype(BF16)
    wub_ref[...] = wu_ref[...].astype(BF16)
    wdb_ref[...] = wd_ref[...].astype(BF16)
    wob_ref[...] = wo_ref[...].astype(BF16)

    tm = x_ref.shape[0]
    ln = WINDOW
    groups = [slice(r0, r0 + tm // ROW_GROUPS) for r0 in range(0, tm, tm // ROW_GROUPS)]
    blocks = [slice(r0, r0 + ln) for r0 in range(0, tm, ln)]
    hcols = [slice(h * M_HEAD_DIM, (h + 1) * M_HEAD_DIM) for h in range(M_HEADS)]
    pcols = [slice(2 * p * M_HEAD_DIM, 2 * (p + 1) * M_HEAD_DIM) for p in range(M_PAIRS)]
    nt = (((1,), (1,)), ((), ()))
    hs = [_rms(x_ref[rs, :], g_ref[...]).astype(BF16) for rs in groups]

    def proj(off, n):
        return jnp.concatenate([jnp.dot(h, w_ref[:, off:off + n], preferred_element_type=F32) for h in hs], axis=0)

    qt = proj(OFF_QM, M_WIDTH).astype(BF16).T
    gates = proj(IN_MAIN, GATE_PAD).T
    km = (proj(OFF_KM, M_WIDTH) * (M_HEAD_DIM ** -0.5)).astype(BF16)
    r = lax.broadcasted_iota(jnp.int32, (ln, ln), 0)
    c = lax.broadcasted_iota(jnp.int32, (ln, ln), 1)
    causal_t = r <= c
    upper = causal_t.astype(F32)
    lane8 = lax.broadcasted_iota(jnp.int32, (SUBLANES, ln), 1)
    ones_rows = jnp.ones((AUG_ROWS - M_HEAD_DIM, ln), BF16)
    zrows = jnp.zeros((ln - SUBLANES, ln), F32)
    row8 = lax.broadcasted_iota(jnp.int32, (SUBLANES, ln), 0)
    bias = jnp.zeros((SUBLANES, ln), F32)
    for h in range(M_HEADS):
        bias = jnp.where(row8 == h, bi_ref[0, h], jnp.where(row8 == M_HEADS + h, bf_ref[0, h], bias))
    gis = [gates[0:SUBLANES, rows] + bias for rows in blocks]
    prefix = jnp.dot(jnp.concatenate([jax.nn.log_sigmoid(gi) for gi in gis], axis=0), upper, precision=HIGHEST,
                     preferred_element_type=F32)
    pre = []
    for blk, gi in enumerate(gis):
        b = pltpu.roll(prefix[blk * SUBLANES:(blk + 1) * SUBLANES, :], M_HEADS, axis=0)
        g = gi - b
        cm0 = g
        sh = 1
        while sh < ln:
            cm0 = jnp.maximum(cm0, jnp.where(lane8 >= sh, pltpu.roll(cm0, sh, axis=1), -jnp.inf))
            sh *= 2
        b_last = jnp.broadcast_to(b[:, ln - 1:ln], b.shape)
        g_max = jnp.broadcast_to(cm0[:, ln - 1:ln], b.shape)
        g_cols = jnp.concatenate([g, zrows], axis=0).T
        pre.append((b, g_cols, cm0, b_last, g_max, (b_last - b) + gi))
    scores_m = [[jnp.dot(km[rows, pc], _block_diag(qt[hcols[2 * p], rows], qt[hcols[2 * p + 1], rows]),
                         preferred_element_type=F32) for p, pc in enumerate(pcols)] for rows in blocks]
    vt = proj(OFF_VM, M_WIDTH).astype(BF16).T

    qa = (proj(OFF_QA, ATTN_WIDTH) * (HEAD_DIM ** -0.5)).astype(BF16)
    kv = proj(OFF_KA, 2 * KV_WIDTH)
    ka, va = kv[:, :KV_WIDTH], kv[:, KV_WIDTH:]
    nrow = ATTN_HEADS * ln
    ra = lax.broadcasted_iota(jnp.int32, (nrow, ln), 0) & (ln - 1)
    ca = lax.broadcasted_iota(jnp.int32, (nrow, ln), 1)
    own = ca <= ra
    sink = _sink_rows(sink_ref, ln)
    no_prev = jnp.where(step == 0, -jnp.inf, 0.0)
    ones = jnp.ones((2 * ln, LANES), BF16)
    kcats, v_augs = [], []
    for blk, rows in enumerate(blocks):
        if blk == 0:
            k_prev, v_prev = kw_ref[...], vw_ref[...]
        else:
            k_prev, v_prev = ka[blocks[blk - 1], :], va[blocks[blk - 1], :]
        kcats.append(jnp.concatenate([ka[rows, :], k_prev], axis=0).astype(BF16))
        vcat = jnp.concatenate([va[rows, :], v_prev], axis=0).astype(BF16)
        v_augs.append(jnp.concatenate([vcat, ones], axis=1))
    kw_ref[...] = ka[blocks[-1], :]
    vw_ref[...] = va[blocks[-1], :]

    m_prev = m_ref[...]
    scal = []
    for b, _, cm0, b_last, g_max, w_end_arg in pre:
        cm = jnp.maximum(cm0, m_prev)
        m_end = b_last + jnp.maximum(g_max, m_prev)
        scal.append((cm, jnp.exp(m_prev - cm), jnp.exp(-(b + cm)), jnp.exp(b_last + m_prev - m_end),
                     jnp.exp(w_end_arg - m_end)))
        m_prev = m_end
    m_ref[...] = m_prev
    gated = []
    for rows, s_t, (_, g_cols, *_), (cm, _, _, _, w_end) in zip(blocks, scores_m, pre, scal):
        vts = [jnp.concatenate([vt[hc, rows], ones_rows], axis=0) for hc in hcols]
        sqks = [(jnp.exp(jnp.where(causal_t, g_cols[:, h:h + 1] - cm[h:h + 1, :], -jnp.inf))
                 * s_t[h // 2][:, (h % 2) * ln:(h % 2 + 1) * ln]).astype(BF16) for h in range(M_HEADS)]
        kvws = [(vts[h].astype(F32) * w_end[h:h + 1, :]).astype(BF16) for h in range(M_HEADS)]
        gated.append((vts, sqks, kvws))
    pairs = range(M_PAIRS)
    upds = [[jnp.dot(jnp.concatenate([kvws[2 * p], kvws[2 * p + 1]], axis=1),
                     _block_diag(km[rows, hcols[2 * p]], km[rows, hcols[2 * p + 1]]),
                     preferred_element_type=F32) for p in pairs] for rows, (_, _, kvws) in zip(blocks, gated)]
    intras = [[jnp.dot(jnp.concatenate([vts[2 * p], vts[2 * p + 1]], axis=1),
                       _block_diag(sqks[2 * p], sqks[2 * p + 1]), preferred_element_type=F32) for p in pairs]
              for vts, sqks, _ in gated]

    scores_a = [lax.dot_general(_stack_heads([qa[rows, j * LANES:(j + 1) * LANES] for j in range(GROUP)]), kcat, nt,
                                preferred_element_type=F32) for rows, kcat in zip(blocks, kcats)]
    probs, maxes = [], []
    for blk, s in enumerate(scores_a):
        s_prev = s[:, ln:]
        if blk == 0:
            s_prev = s_prev + no_prev
        sc = jnp.where(own, s[:, :ln], s_prev)
        mx = jnp.maximum(jnp.max(sc, axis=-1, keepdims=True), sink)
        p = jnp.exp(sc - mx)
        zero = jnp.zeros_like(p)
        probs.append(jnp.concatenate([jnp.where(own, p, zero), jnp.where(own, zero, p)], axis=1).astype(BF16))
        maxes.append(mx)
    om = proj(OFF_OM, M_WIDTH)

    mem = [jnp.concatenate([c_ref[2 * p], c_ref[2 * p + 1]], axis=1) for p in range(M_PAIRS)]
    pair_row = lambda x, p: jnp.concatenate([x[2 * p:2 * p + 1, :], x[2 * p + 1:2 * p + 2, :]], axis=1)
    for rows, upd, intra, (_, w_inter, e_negm, dec, _), p2, v_aug, mx in zip(
            blocks, upds, intras, scal, probs, v_augs, maxes):
        inter = [jnp.dot(mem[p].astype(BF16), _block_diag(qt[hcols[2 * p], rows], qt[hcols[2 * p + 1], rows]),
                         preferred_element_type=F32) for p in pairs]
        mem = [pair_row(dec, p) * mem[p] + upd[p] for p in pairs]
        o = jnp.dot(p2, v_aug, preferred_element_type=F32)
        o = o[:, :LANES] * (1.0 / (o[:, LANES:] + jnp.exp(sink - mx)))
        for j, tile in enumerate(_unstack_heads(o, ln)):
            mix_ref[rows, j * LANES:(j + 1) * LANES] = tile.astype(mix_ref.dtype)
        for p in pairs:
            num = inter[p] * pair_row(w_inter, p) + intra[p]
            den = jnp.maximum(jnp.abs(num[M_HEAD_DIM:M_HEAD_DIM + 1, :]), pair_row(e_negm, p))
            hh = num[:M_HEAD_DIM, :] * (1.0 / den)
            y = hh * lax.rsqrt(jnp.mean(hh * hh, axis=0, keepdims=True) + EPS)
            for h in (2 * p, 2 * p + 1):
                y_h = y[:, (h % 2) * ln:(h % 2 + 1) * ln]
                mcols = slice(ATTN_WIDTH + h * M_HEAD_DIM, ATTN_WIDTH + (h + 1) * M_HEAD_DIM)
                mix_ref[rows, mcols] = (jax.nn.sigmoid(om[rows, hcols[h]])
                                        * (y_h.T * mnorm_ref[:, hcols[h]])).astype(mix_ref.dtype)
    for p in range(M_PAIRS):
        c_ref[2 * p] = mem[p][:, :M_HEAD_DIM]
        c_ref[2 * p + 1] = mem[p][:, M_HEAD_DIM:]


def _out_row_block(i):
    per_head = (ATTN_WIDTH // ATTN_HEADS) // WEIGHT_SLAB
    j, part = i // per_head, i % per_head
    head = (j % 2) * GROUP + j // 2
    return jnp.where(i < ATTN_HEADS * per_head, head * per_head + part, i)


def _mixer_prompt(sink, b_i, b_f, x, g_pre, w_in_t, mnorm, w_gate, w_up, w_down, w_out):
    m = x.shape[0]
    tm = TOKEN_TILE
    steps = m // tm
    assert D_MODEL == steps * WEIGHT_SLAB and D_FF % (steps // 2) == 0 and (D_FF // (steps // 2)) % BF16_SUBLANES == 0
    down_slab = D_FF // (steps // 2)
    row = lambda n: pl.BlockSpec((tm, n), lambda i: (i, 0))
    whole = lambda shape: pl.BlockSpec(shape, lambda i: (0,) * len(shape))
    smem = pl.BlockSpec(memory_space=pltpu.SMEM)
    slab = lambda n: pl.BlockSpec((WEIGHT_SLAB, n), lambda i: (i, 0))
    down = pl.BlockSpec((down_slab, D_MODEL), lambda i: (i // 2, 0))
    c_shape = (M_HEADS, AUG_ROWS, M_HEAD_DIM)
    w_shape = (WINDOW, KV_WIDTH)
    s_shape = (SUBLANES, LANES)
    sds = jax.ShapeDtypeStruct
    return pl.pallas_call(
        _mixer_prompt_kernel,
        grid=(steps,),
        in_specs=[smem, smem, smem, row(D_MODEL), _const_spec((1, D_MODEL)), _const_spec(w_in_t.shape),
                  _const_spec((1, M_WIDTH)), slab(D_FF), slab(D_FF), down,
                  pl.BlockSpec((WEIGHT_SLAB, D_MODEL), lambda i: (_out_row_block(i), 0))],
        out_specs=[row(ATTN_WIDTH + M_WIDTH), whole(w_shape), whole(w_shape), whole(c_shape), whole(s_shape),
                   whole((D_MODEL, IN_PAD)), slab(D_FF), slab(D_FF), down, slab(D_MODEL)],
        out_shape=[sds((m, ATTN_WIDTH + M_WIDTH), BF16), sds(w_shape, F32), sds(w_shape, F32), sds(c_shape, F32),
                   sds(s_shape, F32), sds((D_MODEL, IN_PAD), BF16), sds(w_gate.shape, BF16), sds(w_up.shape, BF16),
                   sds(w_down.shape, BF16), sds(w_out.shape, BF16)],
        compiler_params=_params(("arbitrary",)),
        name="mixer_prompt",
    )(sink, b_i, b_f, x, g_pre, w_in_t, mnorm, w_gate, w_up, w_down, w_out)


def _gates(g_blk, bias_row, cum):
    pre = g_blk + bias_row
    lane = lax.broadcasted_iota(jnp.int32, pre.shape, 1)
    a = jnp.where(lane < M_HEADS, pre, jax.nn.log_sigmoid(pre))
    b = jnp.dot(cum, a, precision=HIGHEST, preferred_element_type=F32)
    return a, b, a.T, b.T


def _col(x, j):
    return jnp.broadcast_to(x[:, j:j + 1], x.shape)


def _mlstm_gated_scores(q, k, a, b, at, bt, h, mask, m_prev, last):
    bc, ic = _col(b, M_HEADS + h), _col(a, h)
    br, ir = bt[M_HEADS + h:M_HEADS + h + 1, :], at[h:h + 1, :]
    d = jnp.where(mask, (bc - br) + ir, -jnp.inf)
    inter = bc + m_prev
    m_t = jnp.maximum(inter, jnp.max(d, axis=-1, keepdims=True))
    w_inter = jnp.exp(inter - m_t)
    sqk = jnp.exp(d - m_t) * lax.dot_general(q, k, (((1,), (1,)), ((), ())), preferred_element_type=F32)
    m_end = last(m_t)
    bl = last(bc)
    dec = jnp.exp(bl + m_prev - m_end)
    w_end = jnp.exp((bl - bc) + ic - m_end)
    kw = k.astype(F32) * w_end
    return w_inter, sqk, m_t, m_end, dec, kw


def _mlstm_sample_kernel(q_ref, k_ref, v_ref, om_ref, g_ref, m0_ref, bias_ref, mnorm_ref, c_ref, n_ref,
                         o_ref, c_out, n_out, m_out, *, dec_seq):
    t = dec_seq
    bb = SAMPLE_BATCH
    ln = bb * t
    r = lax.broadcasted_iota(jnp.int32, (ln, ln), 0)
    c = lax.broadcasted_iota(jnp.int32, (ln, ln), 1)
    shift = t.bit_length() - 1
    same = (r >> shift) == (c >> shift)
    mask = jnp.logical_and(same, c <= r)
    cum = mask.astype(F32)
    expand = (c == (r >> shift)).astype(F32)
    gather = ((c >> shift) == r).astype(F32)
    is_last = (r & (t - 1)) == t - 1
    is_first = (r & (t - 1)) == 0

    def last(x):
        y = jnp.where(is_last, x, 0.0)
        step = 1
        while step < t:
            y = y + pltpu.roll(y, ln - step, axis=0)
            step *= 2
        return y

    a, b, at, bt = _gates(g_ref[...], bias_ref[...], cum)
    m0 = m0_ref[...]
    lane = lax.broadcasted_iota(jnp.int32, (ln, LANES), 1)
    zpad = jnp.zeros((ln - bb, M_HEAD_DIM), F32)
    heads = range(M_HEADS)
    hcols = [slice(h * M_HEAD_DIM, (h + 1) * M_HEAD_DIM) for h in heads]
    qfs = [q_ref[:, hc] for hc in hcols]
    vs = [v_ref[:, hc].astype(BF16) for hc in hcols]
    qcs = [jnp.concatenate(
        [jnp.dot(qfs[h][s * t:(s + 1) * t, :], c_ref[s * M_HEADS + h].astype(BF16).astype(F32),
                 preferred_element_type=F32) for s in range(bb)], axis=0) for h in heads]
    n_exps = [jnp.dot(expand, jnp.concatenate([n_ref[:, hc], zpad], axis=0), precision=HIGHEST,
                      preferred_element_type=F32) for hc in hcols]
    parts = [_mlstm_gated_scores(qfs[h].astype(BF16), k_ref[:, hcols[h]].astype(BF16), a, b, at, bt, h, mask,
                                 _col(m0, h), last) for h in heads]
    intras = [jnp.dot(parts[h][1].astype(BF16), vs[h], preferred_element_type=F32) for h in heads]
    m_cols = jnp.zeros((ln, LANES), F32)
    kwts = []
    for h in heads:
        w_inter, sqk, m_t, m_end, dec, kw = parts[h]
        num = w_inter * qcs[h] + intras[h]
        nq = w_inter * jnp.sum(qfs[h] * n_exps[h], axis=-1, keepdims=True) + jnp.sum(sqk, axis=-1, keepdims=True)
        hh = num / jnp.maximum(jnp.abs(nq), jnp.exp(-m_t))
        o_ref[:, hcols[h]] = _head_out(hh, mnorm_ref[:, hcols[h]], om_ref[:, hcols[h]])
        kwts.append(kw.T.astype(BF16))
        m_cols = jnp.where(lane == h, m_end, m_cols)
    m_out[...] = m_cols
    for h in heads:
        dec = parts[h][4]
        for s in range(bb):
            lhs = jnp.where((c >> shift) == s, kwts[h], jnp.zeros_like(kwts[h]))
            upd = jnp.dot(lhs, vs[h], preferred_element_type=F32)
            c_out[s * M_HEADS + h] = dec[s * t:s * t + 1, :] * c_ref[s * M_HEADS + h] + upd
    for h in heads:
        dec, kw = parts[h][4], parts[h][5]
        n_new = jnp.dot(gather, jnp.where(is_first, dec * n_exps[h], 0.0) + kw, precision=HIGHEST,
                        preferred_element_type=F32)
        n_out[:, hcols[h]] = n_new[:bb, :]


def _mixer_sample_kernel(sink_ref, qa_ref, kn_ref, vn_ref, ck_ref, cv_ref, qm_ref, km_ref, vm_ref, om_ref, g_ref,
                         m0_ref, bias_ref, mnorm_ref, c_ref, n_ref, mix_ref, ko_ref, vo_ref, c_out, n_out, m_out,
                         *, dec_seq):
    _attn_sample_kernel(sink_ref, qa_ref, kn_ref, vn_ref, ck_ref, cv_ref, mix_ref.at[:, pl.ds(0, ATTN_WIDTH)],
                        ko_ref, vo_ref, dec_seq=dec_seq)
    _mlstm_sample_kernel(qm_ref, km_ref, vm_ref, om_ref, g_ref, m0_ref, bias_ref, mnorm_ref, c_ref, n_ref,
                         mix_ref.at[:, pl.ds(ATTN_WIDTH, M_WIDTH)], c_out, n_out, m_out, dec_seq=dec_seq)


def _mixer_sample(sink, qa, ka, va, cache_k, cache_v, qm, km, vm, om, gt, m0_rows, bias_row, mnorm, c_in, n_in,
                  dec_seq):
    m = qm.shape[0]
    bb = SAMPLE_BATCH
    tm = bb * dec_seq
    nb = m // dec_seq
    row = lambda n: pl.BlockSpec((tm, n), lambda i: (i, 0))
    whole = lambda shape: pl.BlockSpec(shape, lambda i: (0,) * len(shape))
    cache = pl.BlockSpec((bb, KV_HEADS, HEAD_DIM, WINDOW), lambda i: (i, 0, 0, 0))
    c_spec = pl.BlockSpec((bb * M_HEADS, M_HEAD_DIM, M_HEAD_DIM), lambda i: (i, 0, 0))
    n_spec = pl.BlockSpec((bb, M_WIDTH), lambda i: (i, 0))
    sds = jax.ShapeDtypeStruct
    return pl.pallas_call(
        functools.partial(_mixer_sample_kernel, dec_seq=dec_seq),
        grid=(nb // bb,),
        in_specs=[pl.BlockSpec(memory_space=pltpu.SMEM), row(ATTN_WIDTH), row(KV_WIDTH), row(KV_WIDTH), cache, cache,
                  row(M_WIDTH), row(M_WIDTH), row(M_WIDTH), row(M_WIDTH), row(GATE_PAD), row(LANES),
                  whole((1, GATE_PAD)), whole((1, M_WIDTH)), c_spec, n_spec],
        out_specs=[row(ATTN_WIDTH + M_WIDTH), cache, cache, c_spec, n_spec, row(LANES)],
        out_shape=[sds((m, ATTN_WIDTH + M_WIDTH), BF16), sds(cache_k.shape, F32), sds(cache_v.shape, F32),
                   sds(c_in.shape, F32), sds(n_in.shape, F32), sds((m, LANES), F32)],
        compiler_params=_params(("parallel",)),
        name="mixer_sample",
    )(sink, qa, ka, va, cache_k, cache_v, qm, km, vm, om, gt, m0_rows, bias_row, mnorm, c_in, n_in)


def _out_ffn_kernel(x_ref, mix_ref, wo_ref, g1_ref, g2_ref, wg_ref, wu_ref, wd_ref, g3_ref, o_ref):
    assert sum(FFN_GROUP_ROWS) == x_ref.shape[0]
    starts = [sum(FFN_GROUP_ROWS[:i]) for i in range(len(FFN_GROUP_ROWS))]
    groups = [slice(r, r + n) for r, n in zip(starts, FFN_GROUP_ROWS)]
    ys = [jnp.dot(mix_ref[rs, :], wo_ref[...], preferred_element_type=F32) for rs in groups]
    x1s = [x_ref[rs, :] + _rms(y, g1_ref[...]) for rs, y in zip(groups, ys)]
    fs = [_rms(x1, g2_ref[...]).astype(BF16) for x1 in x1s]
    accs = [None] * len(groups)
    acts = {}
    chunks = D_FF // FFN_CHUNK
    for rnd in range(chunks + max(FFN_GROUP_LAG) + 1):
        for i, f in enumerate(fs):
            c = rnd - FFN_GROUP_LAG[i]
            if 0 <= c < chunks:
                cols = slice(c * FFN_CHUNK, (c + 1) * FFN_CHUNK)
                g = jnp.dot(f, wg_ref[:, cols], preferred_element_type=F32)
                u = jnp.dot(f, wu_ref[:, cols], preferred_element_type=F32)
                acts[i, c] = (g * jax.nn.sigmoid(g) * u).astype(BF16)
        for i in range(len(groups)):
            c = rnd - FFN_GROUP_LAG[i] - 1
            if 0 <= c < chunks:
                part = jnp.dot(acts.pop((i, c)), wd_ref[c * FFN_CHUNK:(c + 1) * FFN_CHUNK, :],
                               preferred_element_type=F32)
                accs[i] = part if accs[i] is None else accs[i] + part
    for rs, x1, acc in zip(groups, x1s, accs):
        o_ref[rs, :] = x1 + _rms(acc, g3_ref[...])


def _out_ffn(x, mix, w_out, g_post_mix, g_pre_ffn, w_gate, w_up, w_down, g_post_ffn):
    m = x.shape[0]
    tm = TOKEN_TILE
    row = lambda n: pl.BlockSpec((tm, n), lambda i: (i, 0))
    vec = _const_spec((1, D_MODEL))
    return pl.pallas_call(
        _out_ffn_kernel,
        grid=(m // tm,),
        in_specs=[row(D_MODEL), row(ATTN_WIDTH + M_WIDTH), _const_spec((D_MODEL, D_MODEL)), vec, vec,
                  _const_spec((D_MODEL, D_FF)), _const_spec((D_MODEL, D_FF)), _const_spec((D_FF, D_MODEL)), vec],
        out_specs=row(D_MODEL),
        out_shape=jax.ShapeDtypeStruct((m, D_MODEL), F32),
        compiler_params=_params(("parallel",)),
        name="out_ffn",
    )(x, mix, w_out, g_post_mix, g_pre_ffn, w_gate, w_up, w_down, g_post_ffn)


def _layer(xp, xs, cache_k, cache_v, state_c, state_n, state_m, w_in, b_i, b_f, attn_sink, m_norm, w_out,
           g_pre_mix, g_post_mix, g_pre_ffn, g_post_ffn, w_gate, w_up, w_down):
    bp, sp, _ = xp.shape
    bs, ts, _ = xs.shape
    assert bp == 1 and sp % TOKEN_TILE == 0 and TOKEN_TILE % WINDOW == 0
    assert ts & (ts - 1) == 0 and (bs * ts) % TOKEN_TILE == 0 and bs % SAMPLE_BATCH == 0

    row = lambda v: v.reshape(1, -1)
    bias_row = jnp.pad(jnp.concatenate([b_i, b_f]), (0, GATE_PAD - 2 * M_HEADS)).reshape(1, GATE_PAD)
    sink = row(attn_sink)

    x2 = xp.reshape(sp, D_MODEL)
    mix, k_w, v_w, c_aug, m_p, w_pad, wg, wu, wd, wo = _mixer_prompt(
        sink, row(b_i), row(b_f), x2, row(g_pre_mix), w_in.T, row(m_norm), w_gate, w_up, w_down, w_out)
    ffn = (wo, row(g_post_mix), row(g_pre_ffn), wg, wu, wd, row(g_post_ffn))
    yp = _out_ffn(x2, mix, *ffn).reshape(xp.shape)
    k_p = k_w.reshape(1, WINDOW, KV_HEADS, HEAD_DIM)
    v_p = v_w.reshape(1, WINDOW, KV_HEADS, HEAD_DIM)
    c_p = jnp.swapaxes(c_aug[:, :M_HEAD_DIM, :], 1, 2).reshape(1, M_HEADS, M_HEAD_DIM, M_HEAD_DIM)
    n_p = c_aug[:, M_HEAD_DIM, :].reshape(1, M_HEADS, M_HEAD_DIM)
    m_p = m_p[:M_HEADS, 0].reshape(1, M_HEADS)

    x2 = xs.reshape(bs * ts, D_MODEL)
    qa, ka, va, qm, km, vm, om, gt = _inproj(x2, row(g_pre_mix), w_pad)
    feature_major = lambda c: jnp.transpose(c, (0, 2, 3, 1))
    m0_rows = jnp.pad(jnp.repeat(state_m, ts, axis=0), ((0, 0), (0, LANES - M_HEADS)))
    mix, k_s, v_s, c_s, n_s, m_rows = _mixer_sample(
        sink, qa, ka, va, feature_major(cache_k), feature_major(cache_v), qm, km, vm, om, gt, m0_rows, bias_row,
        row(m_norm), state_c.reshape(bs * M_HEADS, M_HEAD_DIM, M_HEAD_DIM), state_n.reshape(bs, M_WIDTH), ts)
    ys = _out_ffn(x2, mix, *ffn).reshape(xs.shape)
    k_s = jnp.transpose(k_s, (0, 3, 1, 2))
    v_s = jnp.transpose(v_s, (0, 3, 1, 2))
    c_s = c_s.reshape(bs, M_HEADS, M_HEAD_DIM, M_HEAD_DIM)
    n_s = n_s.reshape(bs, M_HEADS, M_HEAD_DIM)
    m_s = m_rows[ts - 1::ts, :M_HEADS]
    return yp, ys, (k_p, v_p, c_p, n_p, m_p), (k_s, v_s, c_s, n_s, m_s)


def kernel(x_prompt, x_sample, cache_k, cache_v, state_C, state_n, state_m, w_in, b_i, b_f, attn_sink, m_norm,
           w_out, g_pre_mix, g_post_mix, g_pre_ffn, g_post_ffn, w_gate, w_up, w_down):
    depth = w_in.shape[0]
    xp, xs = x_prompt, x_sample
    prompt_states, sample_states = [], []
    for l in range(depth):
        xp, xs, st_p, st_s = _layer(xp, xs, cache_k[l], cache_v[l], state_C[l], state_n[l], state_m[l],
                                    w_in[l], b_i[l], b_f[l], attn_sink[l], m_norm[l], w_out[l],
                                    g_pre_mix[l], g_post_mix[l], g_pre_ffn[l], g_post_ffn[l],
                                    w_gate[l], w_up[l], w_down[l])
        prompt_states.append(st_p)
        sample_states.append(st_s)
    stack = lambda states, i: jnp.stack([s[i] for s in states], axis=0)
    return (xp, xs) + tuple(stack(prompt_states, i) for i in range(5)) + tuple(stack(sample_states, i) for i in range(5))
```

```python
import functools

import jax
import jax.numpy as jnp
from jax import lax
from jax.experimental import pallas as pl
from jax.experimental.pallas import tpu as pltpu

F32 = jnp.float32
BF16 = jnp.bfloat16
HIGHEST = lax.Precision.HIGHEST

D_MODEL = 1024
HEAD_DIM = 64
ATTN_HEADS = 8
KV_HEADS = 2
GROUP = ATTN_HEADS // KV_HEADS
ATTN_WIDTH = ATTN_HEADS * HEAD_DIM
KV_WIDTH = KV_HEADS * HEAD_DIM
WINDOW = 128
M_HEADS = 4
M_HEAD_DIM = 128
M_WIDTH = M_HEADS * M_HEAD_DIM
M_PAIRS = M_HEADS // 2
D_FF = 2816
EPS = 1e-6

LANES = 128
SUBLANES = 8
GATE_PAD = LANES
BF16_SUBLANES = 16
AUG_ROWS = M_HEAD_DIM + BF16_SUBLANES
IN_MAIN = ATTN_WIDTH + 2 * KV_WIDTH + 4 * M_WIDTH
IN_PAD = IN_MAIN + GATE_PAD
VMEM_LIMIT = 56 * 1024 * 1024

HEAD_ORDER = tuple(h for j in range(GROUP) for h in (j, j + GROUP))

TOKEN_TILE = 512
FFN_CHUNK = 256
ROW_GROUPS = 2
FFN_GROUP_ROWS = (256, 256)
FFN_GROUP_LAG = (0, 1)
WEIGHT_SLAB = 32
SAMPLE_BATCH = 16


def _rms(x, g):
    return x * lax.rsqrt(jnp.mean(x * x, axis=-1, keepdims=True) + EPS) * g


def _const_spec(shape):
    nd = len(shape)
    return pl.BlockSpec(shape, lambda i: (0,) * nd, pipeline_mode=pl.Buffered(1))


def _params(semantics):
    return pltpu.CompilerParams(dimension_semantics=semantics, vmem_limit_bytes=VMEM_LIMIT)


OFF_QA, OFF_KA, OFF_VA = 0, ATTN_WIDTH, ATTN_WIDTH + KV_WIDTH
OFF_QM = ATTN_WIDTH + 2 * KV_WIDTH
OFF_KM, OFF_VM, OFF_OM = OFF_QM + M_WIDTH, OFF_QM + 2 * M_WIDTH, OFF_QM + 3 * M_WIDTH


def _inproj_kernel(x_ref, g_ref, w_ref, qa_ref, ka_ref, va_ref, qm_ref, km_ref, vm_ref, om_ref, gt_ref):
    tm = x_ref.shape[0]
    step = tm // ROW_GROUPS
    for r0 in range(0, tm, step):
        rs = slice(r0, r0 + step)
        h = _rms(x_ref[rs, :], g_ref[...]).astype(BF16)

        def proj(off, n):
            return jnp.dot(h, w_ref[:, off:off + n], preferred_element_type=F32)

        qa_ref[rs, :] = proj(OFF_QA, ATTN_WIDTH) * (HEAD_DIM ** -0.5)
        ka_ref[rs, :] = proj(OFF_KA, KV_WIDTH)
        va_ref[rs, :] = proj(OFF_VA, KV_WIDTH)
        qm_ref[rs, :] = proj(OFF_QM, M_WIDTH)
        km_ref[rs, :] = proj(OFF_KM, M_WIDTH) * (M_HEAD_DIM ** -0.5)
        vm_ref[rs, :] = proj(OFF_VM, M_WIDTH)
        om_ref[rs, :] = proj(OFF_OM, M_WIDTH)
        gt_ref[rs, :] = proj(IN_MAIN, GATE_PAD)


def _inproj(x, g_pre, w_pad):
    m = x.shape[0]
    tm = TOKEN_TILE
    row = lambda n: pl.BlockSpec((tm, n), lambda i: (i, 0))
    widths = (ATTN_WIDTH, KV_WIDTH, KV_WIDTH, M_WIDTH, M_WIDTH, M_WIDTH, M_WIDTH, GATE_PAD)
    return pl.pallas_call(
        _inproj_kernel,
        grid=(m // tm,),
        in_specs=[row(D_MODEL), _const_spec((1, D_MODEL)), _const_spec((D_MODEL, IN_PAD))],
        out_specs=[row(n) for n in widths],
        out_shape=[jax.ShapeDtypeStruct((m, n), F32) for n in widths],
        compiler_params=_params(("parallel",)),
        name="inproj",
    )(x, g_pre, w_pad)


def _stack_heads(q_tiles):
    lane = lax.broadcasted_iota(jnp.int32, q_tiles[0].shape, 1)
    lo = lane < HEAD_DIM
    zero = jnp.zeros_like(q_tiles[0])
    parts = []
    for qt in q_tiles:
        parts += [jnp.where(lo, qt, zero), jnp.where(lo, zero, qt)]
    return jnp.concatenate(parts, axis=0)


def _unstack_heads(o, rows):
    lane = lax.broadcasted_iota(jnp.int32, (rows, LANES), 1)
    lo = lane < HEAD_DIM
    return [jnp.where(lo, o[(2 * j) * rows:(2 * j + 1) * rows, :], o[(2 * j + 1) * rows:(2 * j + 2) * rows, :])
            for j in range(GROUP)]


def _sink_rows(sink_ref, rows):
    return jnp.concatenate([jnp.full((rows, LANES), sink_ref[0, h], F32) for h in HEAD_ORDER], axis=0)


def _attn_sample_kernel(sink_ref, q_ref, kn_ref, vn_ref, ck_ref, cv_ref, o_ref, ko_ref, vo_ref, *, dec_seq):
    t = dec_seq
    bb = SAMPLE_BATCH
    nrow = ATTN_HEADS * t
    ts, ns = t.bit_length() - 1, nrow.bit_length() - 1
    r_c = lax.broadcasted_iota(jnp.int32, (nrow, WINDOW), 0) & (t - 1)
    c_c = lax.broadcasted_iota(jnp.int32, (nrow, WINDOW), 1)
    vis_cache = c_c > r_c
    r_n = lax.broadcasted_iota(jnp.int32, (bb * nrow, bb * t), 0)
    c_n = lax.broadcasted_iota(jnp.int32, (bb * nrow, bb * t), 1)
    vis_new = jnp.logical_and((r_n >> ns) == (c_n >> ts), (c_n & (t - 1)) <= (r_n & (t - 1)))
    sink = _sink_rows(sink_ref, t)
    nt = (((1,), (1,)), ((), ()))
    kn_all, vn_all = kn_ref[...], vn_ref[...]
    lane_w = lax.broadcasted_iota(jnp.int32, (KV_WIDTH, WINDOW), 1)
    is_new = lane_w >= WINDOW - t
    zero_rows = jnp.zeros((WINDOW - t, KV_WIDTH), F32)
    state_shape = (KV_HEADS, HEAD_DIM, WINDOW)

    def slide(old_t, new_rows):
        new_t = jnp.concatenate([zero_rows, new_rows], axis=0).T
        return jnp.where(is_new, new_t, pltpu.roll(old_t, WINDOW - t, axis=1)).reshape(state_shape)

    qs, s_c, cvs = [], [], []
    for b in range(bb):
        rows = slice(b * t, (b + 1) * t)
        ck = ck_ref[b].reshape(KV_WIDTH, WINDOW)
        cvs.append(cv_ref[b].reshape(KV_WIDTH, WINDOW))
        ko_ref[b] = slide(ck, kn_all[rows, :])
        vo_ref[b] = slide(cvs[b], vn_all[rows, :])
        qs.append(_stack_heads([q_ref[rows, j * LANES:(j + 1) * LANES] for j in range(GROUP)]).astype(BF16))
        s_c.append(jnp.where(vis_cache, jnp.dot(qs[b], ck.astype(BF16), preferred_element_type=F32), -jnp.inf))
    s_n = jnp.where(vis_new, lax.dot_general(jnp.concatenate(qs, axis=0), kn_all.astype(BF16), nt,
                                             preferred_element_type=F32), -jnp.inf)
    p_c, p_n, rden = [], [], []
    for b in range(bb):
        s_nb = s_n[b * nrow:(b + 1) * nrow, :]
        m = jnp.maximum(jnp.maximum(jnp.max(s_c[b], axis=-1, keepdims=True), jnp.max(s_nb, axis=-1, keepdims=True)),
                        sink)
        p_c.append(jnp.exp(s_c[b] - m))
        p_n.append(jnp.exp(s_nb - m[:, :bb * t]))
        rden.append(1.0 / (jnp.sum(p_c[b], axis=-1, keepdims=True) + jnp.sum(p_n[b], axis=-1, keepdims=True)
                           + jnp.exp(sink - m)))
    o_n = jnp.dot(jnp.concatenate(p_n, axis=0).astype(BF16), vn_all.astype(BF16), preferred_element_type=F32)
    outs = [[] for _ in range(GROUP)]
    for b in range(bb):
        o = (lax.dot_general(p_c[b].astype(BF16), cvs[b].astype(BF16), nt, preferred_element_type=F32)
             + o_n[b * nrow:(b + 1) * nrow, :]) * rden[b]
        for j, tile in enumerate(_unstack_heads(o, t)):
            outs[j].append(tile)
    for j, parts in enumerate(outs):
        o_ref[:, j * LANES:(j + 1) * LANES] = jnp.concatenate(parts, axis=0).astype(o_ref.dtype)


def _block_diag(a, b):
    za, zb = jnp.zeros_like(a), jnp.zeros_like(b)
    return jnp.concatenate([jnp.concatenate([a, zb], axis=1), jnp.concatenate([za, b], axis=1)], axis=0)


def _head_out(hh, mnorm_row, om):
    y = hh * lax.rsqrt(jnp.mean(hh * hh, axis=-1, keepdims=True) + EPS) * mnorm_row
    return (jax.nn.sigmoid(om) * y).astype(BF16)


def _mixer_prompt_kernel(sink_ref, bi_ref, bf_ref, x_ref, g_ref, win_ref, mnorm_ref, wg_ref, wu_ref, wd_ref, wo_ref,
                         mix_ref, kw_ref, vw_ref, c_ref, m_ref, w_ref, wgb_ref, wub_ref, wdb_ref, wob_ref,
                         kwt_ref, vwt_ref, ct_ref):
    step = pl.program_id(0)

    @pl.when(step == 0)
    def _():
        kw_ref[...] = jnp.zeros_like(kw_ref)
        vw_ref[...] = jnp.zeros_like(vw_ref)
        c_ref[...] = jnp.zeros_like(c_ref)
        m_ref[...] = jnp.zeros_like(m_ref)
        for j in range(GROUP):
            lo = win_ref[j * HEAD_DIM:(j + 1) * HEAD_DIM, :]
            hi = win_ref[(j + GROUP) * HEAD_DIM:(j + GROUP + 1) * HEAD_DIM, :]
            w_ref[:, j * LANES:(j + 1) * LANES] = jnp.concatenate([lo, hi], axis=0).T.astype(BF16)
        for c0 in range(ATTN_WIDTH, IN_MAIN, M_WIDTH):
            c1 = min(c0 + M_WIDTH, IN_MAIN)
            w_ref[:, c0:c1] = win_ref[c0:c1, :].T.astype(BF16)
        gw = jnp.concatenate([win_ref[IN_MAIN:IN_MAIN + 2 * M_HEADS, :],
                              jnp.zeros((GATE_PAD - 2 * M_HEADS, D_MODEL), F32)], axis=0)
        w_ref[:, IN_MAIN:] = gw.T.astype(BF16)

    wgb_ref[...] = wg_ref[...].astype(BF16)
    wub_ref[...] = wu_ref[...].astype(BF16)
    wdb_ref[...] = wd_ref[...].astype(BF16)
    wob_ref[...] = wo_ref[...].astype(BF16)

    tm = x_ref.shape[0]
    ln = WINDOW
    groups = [slice(r0, r0 + tm // ROW_GROUPS) for r0 in range(0, tm, tm // ROW_GROUPS)]
    blocks = [slice(r0, r0 + ln) for r0 in range(0, tm, ln)]
    hcols = [slice(h * M_HEAD_DIM, (h + 1) * M_HEAD_DIM) for h in range(M_HEADS)]
    pcols = [slice(2 * p * M_HEAD_DIM, 2 * (p + 1) * M_HEAD_DIM) for p in range(M_PAIRS)]
    nt = (((1,), (1,)), ((), ()))
    hs = [_rms(x_ref[rs, :], g_ref[...]).astype(BF16) for rs in groups]

    def proj(off, n):
        return jnp.concatenate([jnp.dot(h, w_ref[:, off:off + n], preferred_element_type=F32) for h in hs], axis=0)

    qt = proj(OFF_QM, M_WIDTH).astype(BF16).T
    gates = proj(IN_MAIN, GATE_PAD).T
    km = (proj(OFF_KM, M_WIDTH) * (M_HEAD_DIM ** -0.5)).astype(BF16)
    r = lax.broadcasted_iota(jnp.int32, (ln, ln), 0)
    c = lax.broadcasted_iota(jnp.int32, (ln, ln), 1)
    causal_t = r <= c
    upper = causal_t.astype(F32)
    lane8 = lax.broadcasted_iota(jnp.int32, (SUBLANES, ln), 1)
    ones_rows = jnp.ones((AUG_ROWS - M_HEAD_DIM, ln), BF16)
    zrows = jnp.zeros((ln - SUBLANES, ln), F32)
    row8 = lax.broadcasted_iota(jnp.int32, (SUBLANES, ln), 0)
    bias = jnp.zeros((SUBLANES, ln), F32)
    for h in range(M_HEADS):
        bias = jnp.where(row8 == h, bi_ref[0, h], jnp.where(row8 == M_HEADS + h, bf_ref[0, h], bias))
    gis = [gates[0:SUBLANES, rows] + bias for rows in blocks]
    prefix = jnp.dot(jnp.concatenate([jax.nn.log_sigmoid(gi) for gi in gis], axis=0), upper, precision=HIGHEST,
                     preferred_element_type=F32)
    pre = []
    for blk, gi in enumerate(gis):
        b = pltpu.roll(prefix[blk * SUBLANES:(blk + 1) * SUBLANES, :], M_HEADS, axis=0)
        g = gi - b
        cm0 = g
        sh = 1
        while sh < ln:
            cm0 = jnp.maximum(cm0, jnp.where(lane8 >= sh, pltpu.roll(cm0, sh, axis=1), -jnp.inf))
            sh *= 2
        b_last = jnp.broadcast_to(b[:, ln - 1:ln], b.shape)
        g_max = jnp.broadcast_to(cm0[:, ln - 1:ln], b.shape)
        g_cols = jnp.concatenate([g, zrows], axis=0).T
        pre.append((b, g_cols, cm0, b_last, g_max, (b_last - b) + gi))
    scores_m = [[jnp.dot(km[rows, pc], _block_diag(qt[hcols[2 * p], rows], qt[hcols[2 * p + 1], rows]),
                         preferred_element_type=F32) for p, pc in enumerate(pcols)] for rows in blocks]
    vt = proj(OFF_VM, M_WIDTH).astype(BF16).T

    qa = (proj(OFF_QA, ATTN_WIDTH) * (HEAD_DIM ** -0.5)).astype(BF16)
    kv = proj(OFF_KA, 2 * KV_WIDTH)
    ka, va = kv[:, :KV_WIDTH], kv[:, KV_WIDTH:]
    nrow = ATTN_HEADS * ln
    ra = lax.broadcasted_iota(jnp.int32, (nrow, ln), 0) & (ln - 1)
    ca = lax.broadcasted_iota(jnp.int32, (nrow, ln), 1)
    own = ca <= ra
    sink = _sink_rows(sink_ref, ln)
    no_prev = jnp.where(step == 0, -jnp.inf, 0.0)
    ones = jnp.ones((2 * ln, LANES), BF16)
    kcats, v_augs = [], []
    for blk, rows in enumerate(blocks):
        if blk == 0:
            k_prev, v_prev = kw_ref[...], vw_ref[...]
        else:
            k_prev, v_prev = ka[blocks[blk - 1], :], va[blocks[blk - 1], :]
        kcats.append(jnp.concatenate([ka[rows, :], k_prev], axis=0).astype(BF16))
        vcat = jnp.concatenate([va[rows, :], v_prev], axis=0).astype(BF16)
        v_augs.append(jnp.concatenate([vcat, ones], axis=1))
    kw_ref[...] = ka[blocks[-1], :]
    vw_ref[...] = va[blocks[-1], :]

    m_prev = m_ref[...]
    scal = []
    for b, _, cm0, b_last, g_max, w_end_arg in pre:
        cm = jnp.maximum(cm0, m_prev)
        m_end = b_last + jnp.maximum(g_max, m_prev)
        scal.append((cm, jnp.exp(m_prev - cm), jnp.exp(-(b + cm)), jnp.exp(b_last + m_prev - m_end),
                     jnp.exp(w_end_arg - m_end)))
        m_prev = m_end
    m_ref[...] = m_prev
    gated = []
    for rows, s_t, (_, g_cols, *_), (cm, _, _, _, w_end) in zip(blocks, scores_m, pre, scal):
        vts = [jnp.concatenate([vt[hc, rows], ones_rows], axis=0) for hc in hcols]
        sqks = [(jnp.exp(jnp.where(causal_t, g_cols[:, h:h + 1] - cm[h:h + 1, :], -jnp.inf))
                 * s_t[h // 2][:, (h % 2) * ln:(h % 2 + 1) * ln]).astype(BF16) for h in range(M_HEADS)]
        kvws = [(vts[h].astype(F32) * w_end[h:h + 1, :]).astype(BF16) for h in range(M_HEADS)]
        gated.append((vts, sqks, kvws))
    pairs = range(M_PAIRS)
    upds = [[jnp.dot(jnp.concatenate([kvws[2 * p], kvws[2 * p + 1]], axis=1),
                     _block_diag(km[rows, hcols[2 * p]], km[rows, hcols[2 * p + 1]]),
                     preferred_element_type=F32) for p in pairs] for rows, (_, _, kvws) in zip(blocks, gated)]
    intras = [[jnp.dot(jnp.concatenate([vts[2 * p], vts[2 * p + 1]], axis=1),
                       _block_diag(sqks[2 * p], sqks[2 * p + 1]), preferred_element_type=F32) for p in pairs]
              for vts, sqks, _ in gated]

    scores_a = [lax.dot_general(_stack_heads([qa[rows, j * LANES:(j + 1) * LANES] for j in range(GROUP)]), kcat, nt,
                                preferred_element_type=F32) for rows, kcat in zip(blocks, kcats)]
    probs, maxes = [], []
    for blk, s in enumerate(scores_a):
        s_prev = s[:, ln:]
        if blk == 0:
            s_prev = s_prev + no_prev
        sc = jnp.where(own, s[:, :ln], s_prev)
        mx = jnp.maximum(jnp.max(sc, axis=-1, keepdims=True), sink)
        p = jnp.exp(sc - mx)
        zero = jnp.zeros_like(p)
        probs.append(jnp.concatenate([jnp.where(own, p, zero), jnp.where(own, zero, p)], axis=1).astype(BF16))
        maxes.append(mx)
    om = proj(OFF_OM, M_WIDTH)

    mem = [jnp.concatenate([c_ref[2 * p], c_ref[2 * p + 1]], axis=1) for p in range(M_PAIRS)]
    pair_row = lambda x, p: jnp.concatenate([x[2 * p:2 * p + 1, :], x[2 * p + 1:2 * p + 2, :]], axis=1)
    for rows, upd, intra, (_, w_inter, e_negm, dec, _), p2, v_aug, mx in zip(
            blocks, upds, intras, scal, probs, v_augs, maxes):
        inter = [jnp.dot(mem[p].astype(BF16), _block_diag(qt[hcols[2 * p], rows], qt[hcols[2 * p + 1], rows]),
                         preferred_element_type=F32) for p in pairs]
        mem = [pair_row(dec, p) * mem[p] + upd[p] for p in pairs]
        o = jnp.dot(p2, v_aug, preferred_element_type=F32)
        o = o[:, :LANES] * (1.0 / (o[:, LANES:] + jnp.exp(sink - mx)))
        for j, tile in enumerate(_unstack_heads(o, ln)):
            mix_ref[rows, j * LANES:(j + 1) * LANES] = tile.astype(mix_ref.dtype)
        for p in pairs:
            num = inter[p] * pair_row(w_inter, p) + intra[p]
            den = jnp.maximum(jnp.abs(num[M_HEAD_DIM:M_HEAD_DIM + 1, :]), pair_row(e_negm, p))
            hh = num[:M_HEAD_DIM, :] * (1.0 / den)
            y = hh * lax.rsqrt(jnp.mean(hh * hh, axis=0, keepdims=True) + EPS)
            for h in (2 * p, 2 * p + 1):
                y_h = y[:, (h % 2) * ln:(h % 2 + 1) * ln]
                mcols = slice(ATTN_WIDTH + h * M_HEAD_DIM, ATTN_WIDTH + (h + 1) * M_HEAD_DIM)
                mix_ref[rows, mcols] = (jax.nn.sigmoid(om[rows, hcols[h]])
                                        * (y_h.T * mnorm_ref[:, hcols[h]])).astype(mix_ref.dtype)
    for p in range(M_PAIRS):
        c_ref[2 * p] = mem[p][:, :M_HEAD_DIM]
        c_ref[2 * p + 1] = mem[p][:, M_HEAD_DIM:]

    @pl.when(step == pl.num_programs(0) - 1)
    def _():
        kwt_ref[...] = kw_ref[...].T
        vwt_ref[...] = vw_ref[...].T
        for h in range(M_HEADS):
            ct_ref[h] = c_ref[h][:M_HEAD_DIM, :].T


def _out_row_block(i):
    per_head = (ATTN_WIDTH // ATTN_HEADS) // WEIGHT_SLAB
    j, part = i // per_head, i % per_head
    head = (j % 2) * GROUP + j // 2
    return jnp.where(i < ATTN_HEADS * per_head, head * per_head + part, i)


def _mixer_prompt(sink, b_i, b_f, x, g_pre, w_in_t, mnorm, w_gate, w_up, w_down, w_out):
    m = x.shape[0]
    tm = TOKEN_TILE
    steps = m // tm
    assert D_MODEL == steps * WEIGHT_SLAB and D_FF % (steps // 2) == 0 and (D_FF // (steps // 2)) % BF16_SUBLANES == 0
    down_slab = D_FF // (steps // 2)
    row = lambda n: pl.BlockSpec((tm, n), lambda i: (i, 0))
    whole = lambda shape: pl.BlockSpec(shape, lambda i: (0,) * len(shape))
    smem = pl.BlockSpec(memory_space=pltpu.SMEM)
    slab = lambda n: pl.BlockSpec((WEIGHT_SLAB, n), lambda i: (i, 0))
    down = pl.BlockSpec((down_slab, D_MODEL), lambda i: (i // 2, 0))
    c_shape = (M_HEADS, AUG_ROWS, M_HEAD_DIM)
    w_shape = (WINDOW, KV_WIDTH)
    wt_shape = (KV_WIDTH, WINDOW)
    ct_shape = (M_HEADS, M_HEAD_DIM, M_HEAD_DIM)
    s_shape = (SUBLANES, LANES)
    sds = jax.ShapeDtypeStruct
    return pl.pallas_call(
        _mixer_prompt_kernel,
        grid=(steps,),
        in_specs=[smem, smem, smem, row(D_MODEL), _const_spec((1, D_MODEL)), _const_spec(w_in_t.shape),
                  _const_spec((1, M_WIDTH)), slab(D_FF), slab(D_FF), down,
                  pl.BlockSpec((WEIGHT_SLAB, D_MODEL), lambda i: (_out_row_block(i), 0))],
        out_specs=[row(ATTN_WIDTH + M_WIDTH), whole(w_shape), whole(w_shape), whole(c_shape), whole(s_shape),
                   whole((D_MODEL, IN_PAD)), slab(D_FF), slab(D_FF), down, slab(D_MODEL),
                   whole(wt_shape), whole(wt_shape), whole(ct_shape)],
        out_shape=[sds((m, ATTN_WIDTH + M_WIDTH), BF16), sds(w_shape, F32), sds(w_shape, F32), sds(c_shape, F32),
                   sds(s_shape, F32), sds((D_MODEL, IN_PAD), BF16), sds(w_gate.shape, BF16), sds(w_up.shape, BF16),
                   sds(w_down.shape, BF16), sds(w_out.shape, BF16),
                   sds(wt_shape, F32), sds(wt_shape, F32), sds(ct_shape, F32)],
        compiler_params=_params(("arbitrary",)),
        name="mixer_prompt",
    )(sink, b_i, b_f, x, g_pre, w_in_t, mnorm, w_gate, w_up, w_down, w_out)


def _gates(g_blk, bias_row, cum):
    pre = g_blk + bias_row
    lane = lax.broadcasted_iota(jnp.int32, pre.shape, 1)
    a = jnp.where(lane < M_HEADS, pre, jax.nn.log_sigmoid(pre))
    b = jnp.dot(cum, a, precision=HIGHEST, preferred_element_type=F32)
    return a, b, a.T, b.T


def _col(x, j):
    return jnp.broadcast_to(x[:, j:j + 1], x.shape)


def _mlstm_gated_scores(q, k, a, b, at, bt, h, mask, m_prev, last):
    bc, ic = _col(b, M_HEADS + h), _col(a, h)
    br, ir = bt[M_HEADS + h:M_HEADS + h + 1, :], at[h:h + 1, :]
    d = jnp.where(mask, (bc - br) + ir, -jnp.inf)
    inter = bc + m_prev
    m_t = jnp.maximum(inter, jnp.max(d, axis=-1, keepdims=True))
    w_inter = jnp.exp(inter - m_t)
    sqk = jnp.exp(d - m_t) * lax.dot_general(q, k, (((1,), (1,)), ((), ())), preferred_element_type=F32)
    m_end = last(m_t)
    bl = last(bc)
    dec = jnp.exp(bl + m_prev - m_end)
    w_end = jnp.exp((bl - bc) + ic - m_end)
    kw = k.astype(F32) * w_end
    return w_inter, sqk, m_t, m_end, dec, kw


def _mlstm_sample_kernel(q_ref, k_ref, v_ref, om_ref, g_ref, m0_ref, bias_ref, mnorm_ref, c_ref, n_ref,
                         o_ref, c_out, n_out, m_out, *, dec_seq):
    t = dec_seq
    bb = SAMPLE_BATCH
    ln = bb * t
    r = lax.broadcasted_iota(jnp.int32, (ln, ln), 0)
    c = lax.broadcasted_iota(jnp.int32, (ln, ln), 1)
    shift = t.bit_length() - 1
    same = (r >> shift) == (c >> shift)
    mask = jnp.logical_and(same, c <= r)
    cum = mask.astype(F32)
    expand = (c == (r >> shift)).astype(F32)
    gather = ((c >> shift) == r).astype(F32)
    is_last = (r & (t - 1)) == t - 1
    is_first = (r & (t - 1)) == 0

    def last(x):
        y = jnp.where(is_last, x, 0.0)
        step = 1
        while step < t:
            y = y + pltpu.roll(y, ln - step, axis=0)
            step *= 2
        return y

    a, b, at, bt = _gates(g_ref[...], bias_ref[...], cum)
    m0 = m0_ref[...]
    lane = lax.broadcasted_iota(jnp.int32, (ln, LANES), 1)
    zpad = jnp.zeros((ln - bb, M_HEAD_DIM), F32)
    heads = range(M_HEADS)
    hcols = [slice(h * M_HEAD_DIM, (h + 1) * M_HEAD_DIM) for h in heads]
    qfs = [q_ref[:, hc] for hc in hcols]
    vs = [v_ref[:, hc].astype(BF16) for hc in hcols]
    qcs = [jnp.concatenate(
        [jnp.dot(qfs[h][s * t:(s + 1) * t, :], c_ref[s * M_HEADS + h].astype(BF16).astype(F32),
                 preferred_element_type=F32) for s in range(bb)], axis=0) for h in heads]
    n_exps = [jnp.dot(expand, jnp.concatenate([n_ref[:, hc], zpad], axis=0), precision=HIGHEST,
                      preferred_element_type=F32) for hc in hcols]
    parts = [_mlstm_gated_scores(qfs[h].astype(BF16), k_ref[:, hcols[h]].astype(BF16), a, b, at, bt, h, mask,
                                 _col(m0, h), last) for h in heads]
    intras = [jnp.dot(parts[h][1].astype(BF16), vs[h], preferred_element_type=F32) for h in heads]
    m_cols = jnp.zeros((ln, LANES), F32)
    kwts = []
    for h in heads:
        w_inter, sqk, m_t, m_end, dec, kw = parts[h]
        num = w_inter * qcs[h] + intras[h]
        nq = w_inter * jnp.sum(qfs[h] * n_exps[h], axis=-1, keepdims=True) + jnp.sum(sqk, axis=-1, keepdims=True)
        hh = num / jnp.maximum(jnp.abs(nq), jnp.exp(-m_t))
        o_ref[:, hcols[h]] = _head_out(hh, mnorm_ref[:, hcols[h]], om_ref[:, hcols[h]])
        kwts.append(kw.T.astype(BF16))
        m_cols = jnp.where(lane == h, m_end, m_cols)
    m_out[...] = m_cols
    for h in heads:
        dec = parts[h][4]
        for s in range(bb):
            lhs = jnp.where((c >> shift) == s, kwts[h], jnp.zeros_like(kwts[h]))
            upd = jnp.dot(lhs, vs[h], preferred_element_type=F32)
            c_out[s * M_HEADS + h] = dec[s * t:s * t + 1, :] * c_ref[s * M_HEADS + h] + upd
    for h in heads:
        dec, kw = parts[h][4], parts[h][5]
        n_new = jnp.dot(gather, jnp.where(is_first, dec * n_exps[h], 0.0) + kw, precision=HIGHEST,
                        preferred_element_type=F32)
        n_out[:, hcols[h]] = n_new[:bb, :]


def _mixer_sample_kernel(sink_ref, qa_ref, kn_ref, vn_ref, ck_ref, cv_ref, qm_ref, km_ref, vm_ref, om_ref, g_ref,
                         m0_ref, bias_ref, mnorm_ref, c_ref, n_ref, mix_ref, ko_ref, vo_ref, c_out, n_out, m_out,
                         *, dec_seq):
    _attn_sample_kernel(sink_ref, qa_ref, kn_ref, vn_ref, ck_ref, cv_ref, mix_ref.at[:, pl.ds(0, ATTN_WIDTH)],
                        ko_ref, vo_ref, dec_seq=dec_seq)
    _mlstm_sample_kernel(qm_ref, km_ref, vm_ref, om_ref, g_ref, m0_ref, bias_ref, mnorm_ref, c_ref, n_ref,
                         mix_ref.at[:, pl.ds(ATTN_WIDTH, M_WIDTH)], c_out, n_out, m_out, dec_seq=dec_seq)


def _mixer_sample(sink, qa, ka, va, cache_k, cache_v, qm, km, vm, om, gt, m0_rows, bias_row, mnorm, c_in, n_in,
                  dec_seq):
    m = qm.shape[0]
    bb = SAMPLE_BATCH
    tm = bb * dec_seq
    nb = m // dec_seq
    row = lambda n: pl.BlockSpec((tm, n), lambda i: (i, 0))
    whole = lambda shape: pl.BlockSpec(shape, lambda i: (0,) * len(shape))
    cache = pl.BlockSpec((bb, KV_HEADS, HEAD_DIM, WINDOW), lambda i: (i, 0, 0, 0))
    c_spec = pl.BlockSpec((bb * M_HEADS, M_HEAD_DIM, M_HEAD_DIM), lambda i: (i, 0, 0))
    n_spec = pl.BlockSpec((bb, M_WIDTH), lambda i: (i, 0))
    sds = jax.ShapeDtypeStruct
    return pl.pallas_call(
        functools.partial(_mixer_sample_kernel, dec_seq=dec_seq),
        grid=(nb // bb,),
        in_specs=[pl.BlockSpec(memory_space=pltpu.SMEM), row(ATTN_WIDTH), row(KV_WIDTH), row(KV_WIDTH), cache, cache,
                  row(M_WIDTH), row(M_WIDTH), row(M_WIDTH), row(M_WIDTH), row(GATE_PAD), row(LANES),
                  whole((1, GATE_PAD)), whole((1, M_WIDTH)), c_spec, n_spec],
        out_specs=[row(ATTN_WIDTH + M_WIDTH), cache, cache, c_spec, n_spec, row(LANES)],
        out_shape=[sds((m, ATTN_WIDTH + M_WIDTH), BF16), sds(cache_k.shape, F32), sds(cache_v.shape, F32),
                   sds(c_in.shape, F32), sds(n_in.shape, F32), sds((m, LANES), F32)],
        compiler_params=_params(("parallel",)),
        name="mixer_sample",
    )(sink, qa, ka, va, cache_k, cache_v, qm, km, vm, om, gt, m0_rows, bias_row, mnorm, c_in, n_in)


def _out_ffn_kernel(x_ref, mix_ref, wo_ref, g1_ref, g2_ref, wg_ref, wu_ref, wd_ref, g3_ref, o_ref):
    assert sum(FFN_GROUP_ROWS) == x_ref.shape[0]
    starts = [sum(FFN_GROUP_ROWS[:i]) for i in range(len(FFN_GROUP_ROWS))]
    groups = [slice(r, r + n) for r, n in zip(starts, FFN_GROUP_ROWS)]
    ys = [jnp.dot(mix_ref[rs, :], wo_ref[...], preferred_element_type=F32) for rs in groups]
    x1s = [x_ref[rs, :] + _rms(y, g1_ref[...]) for rs, y in zip(groups, ys)]
    fs = [_rms(x1, g2_ref[...]).astype(BF16) for x1 in x1s]
    accs = [None] * len(groups)
    acts = {}
    chunks = D_FF // FFN_CHUNK
    for rnd in range(chunks + max(FFN_GROUP_LAG) + 1):
        for i, f in enumerate(fs):
            c = rnd - FFN_GROUP_LAG[i]
            if 0 <= c < chunks:
                cols = slice(c * FFN_CHUNK, (c + 1) * FFN_CHUNK)
                g = jnp.dot(f, wg_ref[:, cols], preferred_element_type=F32)
                u = jnp.dot(f, wu_ref[:, cols], preferred_element_type=F32)
                acts[i, c] = (g * jax.nn.sigmoid(g) * u).astype(BF16)
        for i in range(len(groups)):
            c = rnd - FFN_GROUP_LAG[i] - 1
            if 0 <= c < chunks:
                part = jnp.dot(acts.pop((i, c)), wd_ref[c * FFN_CHUNK:(c + 1) * FFN_CHUNK, :],
                               preferred_element_type=F32)
                accs[i] = part if accs[i] is None else accs[i] + part
    for rs, x1, acc in zip(groups, x1s, accs):
        o_ref[rs, :] = x1 + _rms(acc, g3_ref[...])


def _out_ffn(x, mix, w_out, g_post_mix, g_pre_ffn, w_gate, w_up, w_down, g_post_ffn):
    m = x.shape[0]
    tm = TOKEN_TILE
    row = lambda n: pl.BlockSpec((tm, n), lambda i: (i, 0))
    vec = _const_spec((1, D_MODEL))
    return pl.pallas_call(
        _out_ffn_kernel,
        grid=(m // tm,),
        in_specs=[row(D_MODEL), row(ATTN_WIDTH + M_WIDTH), _const_spec((D_MODEL, D_MODEL)), vec, vec,
                  _const_spec((D_MODEL, D_FF)), _const_spec((D_MODEL, D_FF)), _const_spec((D_FF, D_MODEL)), vec],
        out_specs=row(D_MODEL),
        out_shape=jax.ShapeDtypeStruct((m, D_MODEL), F32),
        compiler_params=_params(("parallel",)),
        name="out_ffn",
    )(x, mix, w_out, g_post_mix, g_pre_ffn, w_gate, w_up, w_down, g_post_ffn)


def _layer(xp, xs, cache_k, cache_v, state_c, state_n, state_m, w_in, b_i, b_f, attn_sink, m_norm, w_out,
           g_pre_mix, g_post_mix, g_pre_ffn, g_post_ffn, w_gate, w_up, w_down):
    bp, sp, _ = xp.shape
    bs, ts, _ = xs.shape
    assert bp == 1 and sp % TOKEN_TILE == 0 and TOKEN_TILE % WINDOW == 0
    assert ts & (ts - 1) == 0 and (bs * ts) % TOKEN_TILE == 0 and bs % SAMPLE_BATCH == 0

    row = lambda v: v.reshape(1, -1)
    bias_row = jnp.pad(jnp.concatenate([b_i, b_f]), (0, GATE_PAD - 2 * M_HEADS)).reshape(1, GATE_PAD)
    sink = row(attn_sink)

    x2 = xp.reshape(sp, D_MODEL)
    mix, _, _, c_aug, m_p, w_pad, wg, wu, wd, wo, k_wt, v_wt, c_p = _mixer_prompt(
        sink, row(b_i), row(b_f), x2, row(g_pre_mix), w_in.T, row(m_norm), w_gate, w_up, w_down, w_out)
    ffn = (wo, row(g_post_mix), row(g_pre_ffn), wg, wu, wd, row(g_post_ffn))
    yp = _out_ffn(x2, mix, *ffn).reshape(xp.shape)
    window_major = lambda c: jnp.transpose(c, (0, 3, 1, 2))
    k_p = window_major(k_wt.reshape(1, KV_HEADS, HEAD_DIM, WINDOW))
    v_p = window_major(v_wt.reshape(1, KV_HEADS, HEAD_DIM, WINDOW))
    c_p = c_p.reshape(1, M_HEADS, M_HEAD_DIM, M_HEAD_DIM)
    n_p = c_aug[:, M_HEAD_DIM, :].reshape(1, M_HEADS, M_HEAD_DIM)
    m_p = m_p[:M_HEADS, 0].reshape(1, M_HEADS)

    x2 = xs.reshape(bs * ts, D_MODEL)
    qa, ka, va, qm, km, vm, om, gt = _inproj(x2, row(g_pre_mix), w_pad)
    feature_major = lambda c: jnp.transpose(c, (0, 2, 3, 1))
    m0_rows = jnp.pad(jnp.repeat(state_m, ts, axis=0), ((0, 0), (0, LANES - M_HEADS)))
    mix, k_s, v_s, c_s, n_s, m_rows = _mixer_sample(
        sink, qa, ka, va, feature_major(cache_k), feature_major(cache_v), qm, km, vm, om, gt, m0_rows, bias_row,
        row(m_norm), state_c.reshape(bs * M_HEADS, M_HEAD_DIM, M_HEAD_DIM), state_n.reshape(bs, M_WIDTH), ts)
    ys = _out_ffn(x2, mix, *ffn).reshape(xs.shape)
    k_s, v_s = window_major(k_s), window_major(v_s)
    c_s = c_s.reshape(bs, M_HEADS, M_HEAD_DIM, M_HEAD_DIM)
    n_s = n_s.reshape(bs, M_HEADS, M_HEAD_DIM)
    m_s = m_rows[ts - 1::ts, :M_HEADS]
    return yp, ys, (k_p, v_p, c_p, n_p, m_p), (k_s, v_s, c_s, n_s, m_s)


def kernel(x_prompt, x_sample, cache_k, cache_v, state_C, state_n, state_m, w_in, b_i, b_f, attn_sink, m_norm,
           w_out, g_pre_mix, g_post_mix, g_pre_ffn, g_post_ffn, w_gate, w_up, w_down):
    depth = w_in.shape[0]
    xp, xs = x_prompt, x_sample
    prompt_states, sample_states = [], []
    for l in range(depth):
        xp, xs, st_p, st_s = _layer(xp, xs, cache_k[l], cache_v[l], state_C[l], state_n[l], state_m[l],
                                    w_in[l], b_i[l], b_f[l], attn_sink[l], m_norm[l], w_out[l],
                                    g_pre_mix[l], g_post_mix[l], g_pre_ffn[l], g_post_ffn[l],
                                    w_gate[l], w_up[l], w_down[l])
        prompt_states.append(st_p)
        sample_states.append(st_s)
    stack = lambda states, i: jnp.stack([s[i] for s in states], axis=0)
    return (xp, xs) + tuple(stack(prompt_states, i) for i in range(5)) + tuple(stack(sample_states, i) for i in range(5))
```

```python
import functools

import jax
import jax.numpy as jnp
from jax import lax
from jax.experimental import pallas as pl
from jax.experimental.pallas import tpu as pltpu

F32 = jnp.float32
BF16 = jnp.bfloat16
HIGHEST = lax.Precision.HIGHEST

D_MODEL = 1024
HEAD_DIM = 64
ATTN_HEADS = 8
KV_HEADS = 2
GROUP = ATTN_HEADS // KV_HEADS
ATTN_WIDTH = ATTN_HEADS * HEAD_DIM
KV_WIDTH = KV_HEADS * HEAD_DIM
WINDOW = 128
M_HEADS = 4
M_HEAD_DIM = 128
M_WIDTH = M_HEADS * M_HEAD_DIM
M_PAIRS = M_HEADS // 2
D_FF = 2816
EPS = 1e-6

LANES = 128
SUBLANES = 8
GATE_PAD = LANES
BF16_SUBLANES = 16
AUG_ROWS = M_HEAD_DIM + BF16_SUBLANES
IN_MAIN = ATTN_WIDTH + 2 * KV_WIDTH + 4 * M_WIDTH
IN_PAD = IN_MAIN + GATE_PAD
VMEM_LIMIT = 56 * 1024 * 1024

HEAD_ORDER = tuple(h for j in range(GROUP) for h in (j, j + GROUP))

TOKEN_TILE = 512
FFN_CHUNK = 256
ROW_GROUPS = 2
FFN_GROUP_ROWS = (256, 256)
FFN_GROUP_LAG = (0, 1)
WEIGHT_SLAB = 32
SAMPLE_BATCH = 16


def _rms(x, g):
    return x * lax.rsqrt(jnp.mean(x * x, axis=-1, keepdims=True) + EPS) * g


def _const_spec(shape):
    nd = len(shape)
    return pl.BlockSpec(shape, lambda i: (0,) * nd, pipeline_mode=pl.Buffered(1))


def _params(semantics):
    return pltpu.CompilerParams(dimension_semantics=semantics, vmem_limit_bytes=VMEM_LIMIT)


OFF_QA, OFF_KA, OFF_VA = 0, ATTN_WIDTH, ATTN_WIDTH + KV_WIDTH
OFF_QM = ATTN_WIDTH + 2 * KV_WIDTH
OFF_KM, OFF_VM, OFF_OM = OFF_QM + M_WIDTH, OFF_QM + 2 * M_WIDTH, OFF_QM + 3 * M_WIDTH


def _inproj_kernel(x_ref, g_ref, w_ref, qa_ref, ka_ref, va_ref, qm_ref, km_ref, vm_ref, om_ref, gt_ref):
    tm = x_ref.shape[0]
    step = tm // ROW_GROUPS
    for r0 in range(0, tm, step):
        rs = slice(r0, r0 + step)
        h = _rms(x_ref[rs, :], g_ref[...]).astype(BF16)

        def proj(off, n):
            return jnp.dot(h, w_ref[:, off:off + n], preferred_element_type=F32)

        qa_ref[rs, :] = proj(OFF_QA, ATTN_WIDTH) * (HEAD_DIM ** -0.5)
        ka_ref[rs, :] = proj(OFF_KA, KV_WIDTH)
        va_ref[rs, :] = proj(OFF_VA, KV_WIDTH)
        qm_ref[rs, :] = proj(OFF_QM, M_WIDTH)
        km_ref[rs, :] = (proj(OFF_KM, M_WIDTH) * (M_HEAD_DIM ** -0.5)).astype(km_ref.dtype)
        vm_ref[rs, :] = proj(OFF_VM, M_WIDTH).astype(vm_ref.dtype)
        om_ref[rs, :] = proj(OFF_OM, M_WIDTH)
        gt_ref[rs, :] = proj(IN_MAIN, GATE_PAD)


def _inproj(x, g_pre, w_pad):
    m = x.shape[0]
    tm = TOKEN_TILE
    row = lambda n: pl.BlockSpec((tm, n), lambda i: (i, 0))
    widths = (ATTN_WIDTH, KV_WIDTH, KV_WIDTH, M_WIDTH, M_WIDTH, M_WIDTH, M_WIDTH, GATE_PAD)
    dtypes = (F32, F32, F32, F32, BF16, BF16, F32, F32)
    return pl.pallas_call(
        _inproj_kernel,
        grid=(m // tm,),
        in_specs=[row(D_MODEL), _const_spec((1, D_MODEL)), _const_spec((D_MODEL, IN_PAD))],
        out_specs=[row(n) for n in widths],
        out_shape=[jax.ShapeDtypeStruct((m, n), dt) for n, dt in zip(widths, dtypes)],
        compiler_params=_params(("parallel",)),
        name="inproj",
    )(x, g_pre, w_pad)


def _stack_heads(q_tiles):
    lane = lax.broadcasted_iota(jnp.int32, q_tiles[0].shape, 1)
    lo = lane < HEAD_DIM
    zero = jnp.zeros_like(q_tiles[0])
    parts = []
    for qt in q_tiles:
        parts += [jnp.where(lo, qt, zero), jnp.where(lo, zero, qt)]
    return jnp.concatenate(parts, axis=0)


def _unstack_heads(o, rows):
    lane = lax.broadcasted_iota(jnp.int32, (rows, LANES), 1)
    lo = lane < HEAD_DIM
    return [jnp.where(lo, o[(2 * j) * rows:(2 * j + 1) * rows, :], o[(2 * j + 1) * rows:(2 * j + 2) * rows, :])
            for j in range(GROUP)]


def _sink_rows(sink_ref, rows):
    return jnp.concatenate([jnp.full((rows, LANES), sink_ref[0, h], F32) for h in HEAD_ORDER], axis=0)


def _attn_sample_kernel(sink_ref, q_ref, kn_ref, vn_ref, ck_ref, cv_ref, o_ref, ko_ref, vo_ref, *, dec_seq):
    t = dec_seq
    bb = SAMPLE_BATCH
    nrow = ATTN_HEADS * t
    ts, ns = t.bit_length() - 1, nrow.bit_length() - 1
    r_c = lax.broadcasted_iota(jnp.int32, (nrow, WINDOW), 0) & (t - 1)
    c_c = lax.broadcasted_iota(jnp.int32, (nrow, WINDOW), 1)
    vis_cache = c_c > r_c
    r_n = lax.broadcasted_iota(jnp.int32, (bb * nrow, bb * t), 0)
    c_n = lax.broadcasted_iota(jnp.int32, (bb * nrow, bb * t), 1)
    vis_new = jnp.logical_and((r_n >> ns) == (c_n >> ts), (c_n & (t - 1)) <= (r_n & (t - 1)))
    sink = _sink_rows(sink_ref, t)
    nt = (((1,), (1,)), ((), ()))
    kn_all, vn_all = kn_ref[...], vn_ref[...]
    lane_w = lax.broadcasted_iota(jnp.int32, (KV_WIDTH, WINDOW), 1)
    is_new = lane_w >= WINDOW - t
    zero_rows = jnp.zeros((WINDOW - t, KV_WIDTH), F32)
    state_shape = (KV_HEADS, HEAD_DIM, WINDOW)

    def slide(old_t, new_rows):
        new_t = jnp.concatenate([zero_rows, new_rows], axis=0).T
        return jnp.where(is_new, new_t, pltpu.roll(old_t, WINDOW - t, axis=1)).reshape(state_shape)

    qs, s_c, cvs = [], [], []
    for b in range(bb):
        rows = slice(b * t, (b + 1) * t)
        ck = ck_ref[b].reshape(KV_WIDTH, WINDOW)
        cvs.append(cv_ref[b].reshape(KV_WIDTH, WINDOW))
        ko_ref[b] = slide(ck, kn_all[rows, :])
        vo_ref[b] = slide(cvs[b], vn_all[rows, :])
        qs.append(_stack_heads([q_ref[rows, j * LANES:(j + 1) * LANES] for j in range(GROUP)]).astype(BF16))
        s_c.append(jnp.where(vis_cache, jnp.dot(qs[b], ck.astype(BF16), preferred_element_type=F32), -jnp.inf))
    s_n = jnp.where(vis_new, lax.dot_general(jnp.concatenate(qs, axis=0), kn_all.astype(BF16), nt,
                                             preferred_element_type=F32), -jnp.inf)
    p_c, p_n, rden = [], [], []
    for b in range(bb):
        s_nb = s_n[b * nrow:(b + 1) * nrow, :]
        m = jnp.maximum(jnp.maximum(jnp.max(s_c[b], axis=-1, keepdims=True), jnp.max(s_nb, axis=-1, keepdims=True)),
                        sink)
        p_c.append(jnp.exp(s_c[b] - m))
        p_n.append(jnp.exp(s_nb - m[:, :bb * t]))
        rden.append(1.0 / (jnp.sum(p_c[b], axis=-1, keepdims=True) + jnp.sum(p_n[b], axis=-1, keepdims=True)
                           + jnp.exp(sink - m)))
    o_n = jnp.dot(jnp.concatenate(p_n, axis=0).astype(BF16), vn_all.astype(BF16), preferred_element_type=F32)
    outs = [[] for _ in range(GROUP)]
    for b in range(bb):
        o = (lax.dot_general(p_c[b].astype(BF16), cvs[b].astype(BF16), nt, preferred_element_type=F32)
             + o_n[b * nrow:(b + 1) * nrow, :]) * rden[b]
        for j, tile in enumerate(_unstack_heads(o, t)):
            outs[j].append(tile)
    for j, parts in enumerate(outs):
        o_ref[:, j * LANES:(j + 1) * LANES] = jnp.concatenate(parts, axis=0).astype(o_ref.dtype)


def _block_diag(a, b):
    za, zb = jnp.zeros_like(a), jnp.zeros_like(b)
    return jnp.concatenate([jnp.concatenate([a, zb], axis=1), jnp.concatenate([za, b], axis=1)], axis=0)


def _head_out(hh, mnorm_row, om):
    y = hh * lax.rsqrt(jnp.mean(hh * hh, axis=-1, keepdims=True) + EPS) * mnorm_row
    return (jax.nn.sigmoid(om) * y).astype(BF16)


def _mixer_prompt_kernel(sink_ref, bi_ref, bf_ref, x_ref, g_ref, win_ref, mnorm_ref, wg_ref, wu_ref, wd_ref, wo_ref,
                         mix_ref, kw_ref, vw_ref, c_ref, m_ref, w_ref, wgb_ref, wub_ref, wdb_ref, wob_ref,
                         kwt_ref, vwt_ref, ct_ref, nt_ref, mt_ref, bias_ref):
    step = pl.program_id(0)

    @pl.when(step == 0)
    def _():
        kw_ref[...] = jnp.zeros_like(kw_ref)
        vw_ref[...] = jnp.zeros_like(vw_ref)
        c_ref[...] = jnp.zeros_like(c_ref)
        m_ref[...] = jnp.zeros_like(m_ref)
        for j in range(GROUP):
            lo = win_ref[j * HEAD_DIM:(j + 1) * HEAD_DIM, :]
            hi = win_ref[(j + GROUP) * HEAD_DIM:(j + GROUP + 1) * HEAD_DIM, :]
            w_ref[:, j * LANES:(j + 1) * LANES] = jnp.concatenate([lo, hi], axis=0).T.astype(BF16)
        for c0 in range(ATTN_WIDTH, IN_MAIN, M_WIDTH):
            c1 = min(c0 + M_WIDTH, IN_MAIN)
            w_ref[:, c0:c1] = win_ref[c0:c1, :].T.astype(BF16)
        gw = jnp.concatenate([win_ref[IN_MAIN:IN_MAIN + 2 * M_HEADS, :],
                              jnp.zeros((GATE_PAD - 2 * M_HEADS, D_MODEL), F32)], axis=0)
        w_ref[:, IN_MAIN:] = gw.T.astype(BF16)

    wgb_ref[...] = wg_ref[...].astype(BF16)
    wub_ref[...] = wu_ref[...].astype(BF16)
    wdb_ref[...] = wd_ref[...].astype(BF16)
    wob_ref[...] = wo_ref[...].astype(BF16)

    tm = x_ref.shape[0]
    ln = WINDOW
    groups = [slice(r0, r0 + tm // ROW_GROUPS) for r0 in range(0, tm, tm // ROW_GROUPS)]
    blocks = [slice(r0, r0 + ln) for r0 in range(0, tm, ln)]
    hcols = [slice(h * M_HEAD_DIM, (h + 1) * M_HEAD_DIM) for h in range(M_HEADS)]
    pcols = [slice(2 * p * M_HEAD_DIM, 2 * (p + 1) * M_HEAD_DIM) for p in range(M_PAIRS)]
    nt = (((1,), (1,)), ((), ()))
    hs = [_rms(x_ref[rs, :], g_ref[...]).astype(BF16) for rs in groups]

    def proj(off, n):
        return jnp.concatenate([jnp.dot(h, w_ref[:, off:off + n], preferred_element_type=F32) for h in hs], axis=0)

    qt = proj(OFF_QM, M_WIDTH).astype(BF16).T
    gates = proj(IN_MAIN, GATE_PAD).T
    km = (proj(OFF_KM, M_WIDTH) * (M_HEAD_DIM ** -0.5)).astype(BF16)
    r = lax.broadcasted_iota(jnp.int32, (ln, ln), 0)
    c = lax.broadcasted_iota(jnp.int32, (ln, ln), 1)
    causal_t = r <= c
    upper = causal_t.astype(F32)
    lane8 = lax.broadcasted_iota(jnp.int32, (SUBLANES, ln), 1)
    ones_rows = jnp.ones((AUG_ROWS - M_HEAD_DIM, ln), BF16)
    zrows = jnp.zeros((ln - SUBLANES, ln), F32)
    row8 = lax.broadcasted_iota(jnp.int32, (SUBLANES, ln), 0)
    bias = jnp.zeros((SUBLANES, ln), F32)
    for h in range(M_HEADS):
        bias = jnp.where(row8 == h, bi_ref[0, h], jnp.where(row8 == M_HEADS + h, bf_ref[0, h], bias))
    gis = [gates[0:SUBLANES, rows] + bias for rows in blocks]
    prefix = jnp.dot(jnp.concatenate([jax.nn.log_sigmoid(gi) for gi in gis], axis=0), upper, precision=HIGHEST,
                     preferred_element_type=F32)
    pre = []
    for blk, gi in enumerate(gis):
        b = pltpu.roll(prefix[blk * SUBLANES:(blk + 1) * SUBLANES, :], M_HEADS, axis=0)
        g = gi - b
        cm0 = g
        sh = 1
        while sh < ln:
            cm0 = jnp.maximum(cm0, jnp.where(lane8 >= sh, pltpu.roll(cm0, sh, axis=1), -jnp.inf))
            sh *= 2
        b_last = jnp.broadcast_to(b[:, ln - 1:ln], b.shape)
        g_max = jnp.broadcast_to(cm0[:, ln - 1:ln], b.shape)
        g_cols = jnp.concatenate([g, zrows], axis=0).T
        pre.append((b, g_cols, cm0, b_last, g_max, (b_last - b) + gi))
    scores_m = [[jnp.dot(km[rows, pc], _block_diag(qt[hcols[2 * p], rows], qt[hcols[2 * p + 1], rows]),
                         preferred_element_type=F32) for p, pc in enumerate(pcols)] for rows in blocks]
    vt = proj(OFF_VM, M_WIDTH).astype(BF16).T

    qa = (proj(OFF_QA, ATTN_WIDTH) * (HEAD_DIM ** -0.5)).astype(BF16)
    kv = proj(OFF_KA, 2 * KV_WIDTH)
    ka, va = kv[:, :KV_WIDTH], kv[:, KV_WIDTH:]
    nrow = ATTN_HEADS * ln
    ra = lax.broadcasted_iota(jnp.int32, (nrow, ln), 0) & (ln - 1)
    ca = lax.broadcasted_iota(jnp.int32, (nrow, ln), 1)
    own = ca <= ra
    sink = _sink_rows(sink_ref, ln)
    no_prev = jnp.where(step == 0, -jnp.inf, 0.0)
    ones = jnp.ones((2 * ln, LANES), BF16)
    kcats, v_augs = [], []
    for blk, rows in enumerate(blocks):
        if blk == 0:
            k_prev, v_prev = kw_ref[...], vw_ref[...]
        else:
            k_prev, v_prev = ka[blocks[blk - 1], :], va[blocks[blk - 1], :]
        kcats.append(jnp.concatenate([ka[rows, :], k_prev], axis=0).astype(BF16))
        vcat = jnp.concatenate([va[rows, :], v_prev], axis=0).astype(BF16)
        v_augs.append(jnp.concatenate([vcat, ones], axis=1))
    kw_ref[...] = ka[blocks[-1], :]
    vw_ref[...] = va[blocks[-1], :]

    m_prev = m_ref[...]
    scal = []
    for b, _, cm0, b_last, g_max, w_end_arg in pre:
        cm = jnp.maximum(cm0, m_prev)
        m_end = b_last + jnp.maximum(g_max, m_prev)
        scal.append((cm, jnp.exp(m_prev - cm), jnp.exp(-(b + cm)), jnp.exp(b_last + m_prev - m_end),
                     jnp.exp(w_end_arg - m_end)))
        m_prev = m_end
    m_ref[...] = m_prev
    gated = []
    for rows, s_t, (_, g_cols, *_), (cm, _, _, _, w_end) in zip(blocks, scores_m, pre, scal):
        vts = [jnp.concatenate([vt[hc, rows], ones_rows], axis=0) for hc in hcols]
        sqks = [(jnp.exp(jnp.where(causal_t, g_cols[:, h:h + 1] - cm[h:h + 1, :], -jnp.inf))
                 * s_t[h // 2][:, (h % 2) * ln:(h % 2 + 1) * ln]).astype(BF16) for h in range(M_HEADS)]
        kvws = [(vts[h].astype(F32) * w_end[h:h + 1, :]).astype(BF16) for h in range(M_HEADS)]
        gated.append((vts, sqks, kvws))
    pairs = range(M_PAIRS)
    upds = [[jnp.dot(jnp.concatenate([kvws[2 * p], kvws[2 * p + 1]], axis=1),
                     _block_diag(km[rows, hcols[2 * p]], km[rows, hcols[2 * p + 1]]),
                     preferred_element_type=F32) for p in pairs] for rows, (_, _, kvws) in zip(blocks, gated)]
    intras = [[jnp.dot(jnp.concatenate([vts[2 * p], vts[2 * p + 1]], axis=1),
                       _block_diag(sqks[2 * p], sqks[2 * p + 1]), preferred_element_type=F32) for p in pairs]
              for vts, sqks, _ in gated]

    scores_a = [lax.dot_general(_stack_heads([qa[rows, j * LANES:(j + 1) * LANES] for j in range(GROUP)]), kcat, nt,
                                preferred_element_type=F32) for rows, kcat in zip(blocks, kcats)]
    probs, maxes = [], []
    for blk, s in enumerate(scores_a):
        s_prev = s[:, ln:]
        if blk == 0:
            s_prev = s_prev + no_prev
        sc = jnp.where(own, s[:, :ln], s_prev)
        mx = jnp.maximum(jnp.max(sc, axis=-1, keepdims=True), sink)
        p = jnp.exp(sc - mx)
        zero = jnp.zeros_like(p)
        probs.append(jnp.concatenate([jnp.where(own, p, zero), jnp.where(own, zero, p)], axis=1).astype(BF16))
        maxes.append(mx)
    om = proj(OFF_OM, M_WIDTH)

    mem = [jnp.concatenate([c_ref[2 * p], c_ref[2 * p + 1]], axis=1) for p in range(M_PAIRS)]
    pair_row = lambda x, p: jnp.concatenate([x[2 * p:2 * p + 1, :], x[2 * p + 1:2 * p + 2, :]], axis=1)
    for rows, upd, intra, (_, w_inter, e_negm, dec, _), p2, v_aug, mx in zip(
            blocks, upds, intras, scal, probs, v_augs, maxes):
        inter = [jnp.dot(mem[p].astype(BF16), _block_diag(qt[hcols[2 * p], rows], qt[hcols[2 * p + 1], rows]),
                         preferred_element_type=F32) for p in pairs]
        mem = [pair_row(dec, p) * mem[p] + upd[p] for p in pairs]
        o = jnp.dot(p2, v_aug, preferred_element_type=F32)
        o = o[:, :LANES] * (1.0 / (o[:, LANES:] + jnp.exp(sink - mx)))
        for j, tile in enumerate(_unstack_heads(o, ln)):
            mix_ref[rows, j * LANES:(j + 1) * LANES] = tile.astype(mix_ref.dtype)
        for p in pairs:
            num = inter[p] * pair_row(w_inter, p) + intra[p]
            den = jnp.maximum(jnp.abs(num[M_HEAD_DIM:M_HEAD_DIM + 1, :]), pair_row(e_negm, p))
            hh = num[:M_HEAD_DIM, :] * (1.0 / den)
            y = hh * lax.rsqrt(jnp.mean(hh * hh, axis=0, keepdims=True) + EPS)
            for h in (2 * p, 2 * p + 1):
                y_h = y[:, (h % 2) * ln:(h % 2 + 1) * ln]
                mcols = slice(ATTN_WIDTH + h * M_HEAD_DIM, ATTN_WIDTH + (h + 1) * M_HEAD_DIM)
                mix_ref[rows, mcols] = (jax.nn.sigmoid(om[rows, hcols[h]])
                                        * (y_h.T * mnorm_ref[:, hcols[h]])).astype(mix_ref.dtype)
    for p in range(M_PAIRS):
        c_ref[2 * p] = mem[p][:, :M_HEAD_DIM]
        c_ref[2 * p + 1] = mem[p][:, M_HEAD_DIM:]

    @pl.when(step == pl.num_programs(0) - 1)
    def _():
        kwt_ref[...] = kw_ref[...].T
        vwt_ref[...] = vw_ref[...].T
        for h in range(M_HEADS):
            ct_ref[h] = c_ref[h][:M_HEAD_DIM, :].T
            nt_ref[h:h + 1, :] = c_ref[h][M_HEAD_DIM:M_HEAD_DIM + 1, :]
        diag = jnp.sum(jnp.where(row8 == lane8, m_ref[...], 0.0), axis=0, keepdims=True)
        mt_ref[...] = diag[:, :M_HEADS]
        lane = lax.broadcasted_iota(jnp.int32, bias_ref.shape, 1)
        gate_bias = jnp.zeros(bias_ref.shape, F32)
        for h in range(M_HEADS):
            gate_bias = jnp.where(lane == h, bi_ref[0, h], jnp.where(lane == M_HEADS + h, bf_ref[0, h], gate_bias))
        bias_ref[...] = gate_bias


def _out_row_block(i):
    per_head = (ATTN_WIDTH // ATTN_HEADS) // WEIGHT_SLAB
    j, part = i // per_head, i % per_head
    head = (j % 2) * GROUP + j // 2
    return jnp.where(i < ATTN_HEADS * per_head, head * per_head + part, i)


def _mixer_prompt(sink, b_i, b_f, x, g_pre, w_in_t, mnorm, w_gate, w_up, w_down, w_out):
    m = x.shape[0]
    tm = TOKEN_TILE
    steps = m // tm
    assert D_MODEL == steps * WEIGHT_SLAB and D_FF % (steps // 2) == 0 and (D_FF // (steps // 2)) % BF16_SUBLANES == 0
    down_slab = D_FF // (steps // 2)
    row = lambda n: pl.BlockSpec((tm, n), lambda i: (i, 0))
    whole = lambda shape: pl.BlockSpec(shape, lambda i: (0,) * len(shape))
    smem = pl.BlockSpec(memory_space=pltpu.SMEM)
    slab = lambda n: pl.BlockSpec((WEIGHT_SLAB, n), lambda i: (i, 0))
    down = pl.BlockSpec((down_slab, D_MODEL), lambda i: (i // 2, 0))
    c_shape = (M_HEADS, AUG_ROWS, M_HEAD_DIM)
    w_shape = (WINDOW, KV_WIDTH)
    wt_shape = (KV_WIDTH, WINDOW)
    ct_shape = (M_HEADS, M_HEAD_DIM, M_HEAD_DIM)
    n_shape = (M_HEADS, M_HEAD_DIM)
    b_shape = (1, GATE_PAD)
    s_shape = (SUBLANES, LANES)
    sds = jax.ShapeDtypeStruct
    return pl.pallas_call(
        _mixer_prompt_kernel,
        grid=(steps,),
        in_specs=[smem, smem, smem, row(D_MODEL), _const_spec((1, D_MODEL)), _const_spec(w_in_t.shape),
                  _const_spec((1, M_WIDTH)), slab(D_FF), slab(D_FF), down,
                  pl.BlockSpec((WEIGHT_SLAB, D_MODEL), lambda i: (_out_row_block(i), 0))],
        out_specs=[row(ATTN_WIDTH + M_WIDTH), whole(w_shape), whole(w_shape), whole(c_shape), whole(s_shape),
                   whole((D_MODEL, IN_PAD)), slab(D_FF), slab(D_FF), down, slab(D_MODEL),
                   whole(wt_shape), whole(wt_shape), whole(ct_shape), whole(n_shape),
                   whole((1, M_HEADS)), whole(b_shape)],
        out_shape=[sds((m, ATTN_WIDTH + M_WIDTH), BF16), sds(w_shape, F32), sds(w_shape, F32), sds(c_shape, F32),
                   sds(s_shape, F32), sds((D_MODEL, IN_PAD), BF16), sds(w_gate.shape, BF16), sds(w_up.shape, BF16),
                   sds(w_down.shape, BF16), sds(w_out.shape, BF16),
                   sds(wt_shape, F32), sds(wt_shape, F32), sds(ct_shape, F32), sds(n_shape, F32),
                   sds((1, M_HEADS), F32), sds(b_shape, F32)],
        compiler_params=_params(("arbitrary",)),
        name="mixer_prompt",
    )(sink, b_i, b_f, x, g_pre, w_in_t, mnorm, w_gate, w_up, w_down, w_out)


def _gates(g_blk, bias_row, cum):
    pre = g_blk + bias_row
    lane = lax.broadcasted_iota(jnp.int32, pre.shape, 1)
    a = jnp.where(lane < M_HEADS, pre, jax.nn.log_sigmoid(pre))
    b = jnp.dot(cum, a, precision=HIGHEST, preferred_element_type=F32)
    return a, b, a.T, b.T


def _col(x, j):
    return jnp.broadcast_to(x[:, j:j + 1], x.shape)


def _mlstm_gated_scores(q, k, a, b, at, bt, h, mask, m_prev, last):
    bc, ic = _col(b, M_HEADS + h), _col(a, h)
    br, ir = bt[M_HEADS + h:M_HEADS + h + 1, :], at[h:h + 1, :]
    d = jnp.where(mask, (bc - br) + ir, -jnp.inf)
    inter = bc + m_prev
    m_t = jnp.maximum(inter, jnp.max(d, axis=-1, keepdims=True))
    w_inter = jnp.exp(inter - m_t)
    sqk = jnp.exp(d - m_t) * lax.dot_general(q, k, (((1,), (1,)), ((), ())), preferred_element_type=F32)
    m_end = last(m_t)
    bl = last(bc)
    dec = jnp.exp(bl + m_prev - m_end)
    w_end = jnp.exp((bl - bc) + ic - m_end)
    kw = k.astype(F32) * w_end
    return w_inter, sqk, m_t, m_end, dec, kw


def _mlstm_sample_kernel(q_ref, k_ref, v_ref, om_ref, g_ref, m0_ref, bias_ref, mnorm_ref, c_ref, n_ref,
                         o_ref, c_out, n_out, m_out, *, dec_seq):
    t = dec_seq
    bb = SAMPLE_BATCH
    ln = bb * t
    r = lax.broadcasted_iota(jnp.int32, (ln, ln), 0)
    c = lax.broadcasted_iota(jnp.int32, (ln, ln), 1)
    shift = t.bit_length() - 1
    same = (r >> shift) == (c >> shift)
    mask = jnp.logical_and(same, c <= r)
    cum = mask.astype(F32)
    expand = (c == (r >> shift)).astype(F32)
    gather = ((c >> shift) == r).astype(F32)
    is_last = (r & (t - 1)) == t - 1
    is_first = (r & (t - 1)) == 0

    def last(x):
        y = jnp.where(is_last, x, 0.0)
        step = 1
        while step < t:
            y = y + pltpu.roll(y, ln - step, axis=0)
            step *= 2
        return y

    a, b, at, bt = _gates(g_ref[...], bias_ref[...], cum)
    m0 = m0_ref[...]
    lane = lax.broadcasted_iota(jnp.int32, (ln, LANES), 1)
    zpad = jnp.zeros((ln - bb, M_HEAD_DIM), F32)
    heads = range(M_HEADS)
    hcols = [slice(h * M_HEAD_DIM, (h + 1) * M_HEAD_DIM) for h in heads]
    qfs = [q_ref[:, hc] for hc in hcols]
    vs = [v_ref[:, hc] for hc in hcols]
    qcs = [jnp.concatenate(
        [jnp.dot(qfs[h][s * t:(s + 1) * t, :], c_ref[s * M_HEADS + h].astype(BF16).astype(F32),
                 preferred_element_type=F32) for s in range(bb)], axis=0) for h in heads]
    n_exps = [jnp.dot(expand, jnp.concatenate([n_ref[:, hc], zpad], axis=0), precision=HIGHEST,
                      preferred_element_type=F32) for hc in hcols]
    parts = [_mlstm_gated_scores(qfs[h].astype(BF16), k_ref[:, hcols[h]], a, b, at, bt, h, mask,
                                 _col(m0, h), last) for h in heads]
    intras = [jnp.dot(parts[h][1].astype(BF16), vs[h], preferred_element_type=F32) for h in heads]
    m_cols = jnp.zeros((ln, LANES), F32)
    kwts = []
    for h in heads:
        w_inter, sqk, m_t, m_end, dec, kw = parts[h]
        num = w_inter * qcs[h] + intras[h]
        nq = w_inter * jnp.sum(qfs[h] * n_exps[h], axis=-1, keepdims=True) + jnp.sum(sqk, axis=-1, keepdims=True)
        hh = num / jnp.maximum(jnp.abs(nq), jnp.exp(-m_t))
        o_ref[:, hcols[h]] = _head_out(hh, mnorm_ref[:, hcols[h]], om_ref[:, hcols[h]])
        kwts.append(kw.T.astype(BF16))
        m_cols = jnp.where(lane == h, m_end, m_cols)
    m_out[...] = m_cols
    for h in heads:
        dec = parts[h][4]
        for s in range(bb):
            lhs = jnp.where((c >> shift) == s, kwts[h], jnp.zeros_like(kwts[h]))
            upd = jnp.dot(lhs, vs[h], preferred_element_type=F32)
            c_out[s * M_HEADS + h] = dec[s * t:s * t + 1, :] * c_ref[s * M_HEADS + h] + upd
    for h in heads:
        dec, kw = parts[h][4], parts[h][5]
        n_new = jnp.dot(gather, jnp.where(is_first, dec * n_exps[h], 0.0) + kw, precision=HIGHEST,
                        preferred_element_type=F32)
        n_out[:, hcols[h]] = n_new[:bb, :]


def _mixer_sample_kernel(sink_ref, qa_ref, kn_ref, vn_ref, ck_ref, cv_ref, qm_ref, km_ref, vm_ref, om_ref, g_ref,
                         m0_ref, bias_ref, mnorm_ref, c_ref, n_ref, mix_ref, ko_ref, vo_ref, c_out, n_out, m_out,
                         *, dec_seq):
    _attn_sample_kernel(sink_ref, qa_ref, kn_ref, vn_ref, ck_ref, cv_ref, mix_ref.at[:, pl.ds(0, ATTN_WIDTH)],
                        ko_ref, vo_ref, dec_seq=dec_seq)
    _mlstm_sample_kernel(qm_ref, km_ref, vm_ref, om_ref, g_ref, m0_ref, bias_ref, mnorm_ref, c_ref, n_ref,
                         mix_ref.at[:, pl.ds(ATTN_WIDTH, M_WIDTH)], c_out, n_out, m_out, dec_seq=dec_seq)


def _mixer_sample(sink, qa, ka, va, cache_k, cache_v, qm, km, vm, om, gt, m0_rows, bias_row, mnorm, c_in, n_in,
                  dec_seq):
    m = qm.shape[0]
    bb = SAMPLE_BATCH
    tm = bb * dec_seq
    nb = m // dec_seq
    row = lambda n: pl.BlockSpec((tm, n), lambda i: (i, 0))
    whole = lambda shape: pl.BlockSpec(shape, lambda i: (0,) * len(shape))
    cache = pl.BlockSpec((bb, KV_HEADS, HEAD_DIM, WINDOW), lambda i: (i, 0, 0, 0))
    c_spec = pl.BlockSpec((bb * M_HEADS, M_HEAD_DIM, M_HEAD_DIM), lambda i: (i, 0, 0))
    n_spec = pl.BlockSpec((bb, M_WIDTH), lambda i: (i, 0))
    sds = jax.ShapeDtypeStruct
    return pl.pallas_call(
        functools.partial(_mixer_sample_kernel, dec_seq=dec_seq),
        grid=(nb // bb,),
        in_specs=[pl.BlockSpec(memory_space=pltpu.SMEM), row(ATTN_WIDTH), row(KV_WIDTH), row(KV_WIDTH), cache, cache,
                  row(M_WIDTH), row(M_WIDTH), row(M_WIDTH), row(M_WIDTH), row(GATE_PAD), row(LANES),
                  whole((1, GATE_PAD)), whole((1, M_WIDTH)), c_spec, n_spec],
        out_specs=[row(ATTN_WIDTH + M_WIDTH), cache, cache, c_spec, n_spec, row(LANES)],
        out_shape=[sds((m, ATTN_WIDTH + M_WIDTH), BF16), sds(cache_k.shape, F32), sds(cache_v.shape, F32),
                   sds(c_in.shape, F32), sds(n_in.shape, F32), sds((m, LANES), F32)],
        compiler_params=_params(("parallel",)),
        name="mixer_sample",
    )(sink, qa, ka, va, cache_k, cache_v, qm, km, vm, om, gt, m0_rows, bias_row, mnorm, c_in, n_in)


def _out_ffn_kernel(x_ref, mix_ref, wo_ref, g1_ref, g2_ref, wg_ref, wu_ref, wd_ref, g3_ref, o_ref):
    assert sum(FFN_GROUP_ROWS) == x_ref.shape[0]
    starts = [sum(FFN_GROUP_ROWS[:i]) for i in range(len(FFN_GROUP_ROWS))]
    groups = [slice(r, r + n) for r, n in zip(starts, FFN_GROUP_ROWS)]
    ys = [jnp.dot(mix_ref[rs, :], wo_ref[...], preferred_element_type=F32) for rs in groups]
    x1s = [x_ref[rs, :] + _rms(y, g1_ref[...]) for rs, y in zip(groups, ys)]
    fs = [_rms(x1, g2_ref[...]).astype(BF16) for x1 in x1s]
    accs = [None] * len(groups)
    acts = {}
    chunks = D_FF // FFN_CHUNK
    for rnd in range(chunks + max(FFN_GROUP_LAG) + 1):
        for i, f in enumerate(fs):
            c = rnd - FFN_GROUP_LAG[i]
            if 0 <= c < chunks:
                cols = slice(c * FFN_CHUNK, (c + 1) * FFN_CHUNK)
                g = jnp.dot(f, wg_ref[:, cols], preferred_element_type=F32)
                u = jnp.dot(f, wu_ref[:, cols], preferred_element_type=F32)
                acts[i, c] = (g * jax.nn.sigmoid(g) * u).astype(BF16)
        for i in range(len(groups)):
            c = rnd - FFN_GROUP_LAG[i] - 1
            if 0 <= c < chunks:
                part = jnp.dot(acts.pop((i, c)), wd_ref[c * FFN_CHUNK:(c + 1) * FFN_CHUNK, :],
                               preferred_element_type=F32)
                accs[i] = part if accs[i] is None else accs[i] + part
    for rs, x1, acc in zip(groups, x1s, accs):
        o_ref[rs, :] = x1 + _rms(acc, g3_ref[...])


def _out_ffn(x, mix, w_out, g_post_mix, g_pre_ffn, w_gate, w_up, w_down, g_post_ffn):
    m = x.shape[0]
    tm = TOKEN_TILE
    row = lambda n: pl.BlockSpec((tm, n), lambda i: (i, 0))
    vec = _const_spec((1, D_MODEL))
    return pl.pallas_call(
        _out_ffn_kernel,
        grid=(m // tm,),
        in_specs=[row(D_MODEL), row(ATTN_WIDTH + M_WIDTH), _const_spec((D_MODEL, D_MODEL)), vec, vec,
                  _const_spec((D_MODEL, D_FF)), _const_spec((D_MODEL, D_FF)), _const_spec((D_FF, D_MODEL)), vec],
        out_specs=row(D_MODEL),
        out_shape=jax.ShapeDtypeStruct((m, D_MODEL), F32),
        compiler_params=_params(("parallel",)),
        name="out_ffn",
    )(x, mix, w_out, g_post_mix, g_pre_ffn, w_gate, w_up, w_down, g_post_ffn)


def _layer(xp, xs, cache_k, cache_v, state_c, state_n, state_m, w_in, b_i, b_f, attn_sink, m_norm, w_out,
           g_pre_mix, g_post_mix, g_pre_ffn, g_post_ffn, w_gate, w_up, w_down):
    bp, sp, _ = xp.shape
    bs, ts, _ = xs.shape
    assert bp == 1 and sp % TOKEN_TILE == 0 and TOKEN_TILE % WINDOW == 0
    assert ts & (ts - 1) == 0 and (bs * ts) % TOKEN_TILE == 0 and bs % SAMPLE_BATCH == 0

    row = lambda v: v.reshape(1, -1)
    sink = row(attn_sink)

    x2 = xp.reshape(sp, D_MODEL)
    mix, _, _, _, _, w_pad, wg, wu, wd, wo, k_wt, v_wt, c_p, n_p, m_p, bias_row = _mixer_prompt(
        sink, row(b_i), row(b_f), x2, row(g_pre_mix), w_in.T, row(m_norm), w_gate, w_up, w_down, w_out)
    ffn = (wo, row(g_post_mix), row(g_pre_ffn), wg, wu, wd, row(g_post_ffn))
    yp = _out_ffn(x2, mix, *ffn).reshape(xp.shape)
    window_major = lambda c: jnp.transpose(c, (0, 3, 1, 2))
    k_p = window_major(k_wt.reshape(1, KV_HEADS, HEAD_DIM, WINDOW))
    v_p = window_major(v_wt.reshape(1, KV_HEADS, HEAD_DIM, WINDOW))
    c_p = c_p.reshape(1, M_HEADS, M_HEAD_DIM, M_HEAD_DIM)
    n_p = n_p.reshape(1, M_HEADS, M_HEAD_DIM)

    x2 = xs.reshape(bs * ts, D_MODEL)
    qa, ka, va, qm, km, vm, om, gt = _inproj(x2, row(g_pre_mix), w_pad)
    feature_major = lambda c: jnp.transpose(c, (0, 2, 3, 1))
    m0_rows = jnp.pad(jnp.repeat(state_m, ts, axis=0), ((0, 0), (0, LANES - M_HEADS)))
    mix, k_s, v_s, c_s, n_s, m_rows = _mixer_sample(
        sink, qa, ka, va, feature_major(cache_k), feature_major(cache_v), qm, km, vm, om, gt, m0_rows, bias_row,
        row(m_norm), state_c.reshape(bs * M_HEADS, M_HEAD_DIM, M_HEAD_DIM), state_n.reshape(bs, M_WIDTH), ts)
    ys = _out_ffn(x2, mix, *ffn).reshape(xs.shape)
    k_s, v_s = window_major(k_s), window_major(v_s)
    c_s = c_s.reshape(bs, M_HEADS, M_HEAD_DIM, M_HEAD_DIM)
    n_s = n_s.reshape(bs, M_HEADS, M_HEAD_DIM)
    m_s = m_rows[ts - 1::ts, :M_HEADS]
    return yp, ys, (k_p, v_p, c_p, n_p, m_p), (k_s, v_s, c_s, n_s, m_s)


def kernel(x_prompt, x_sample, cache_k, cache_v, state_C, state_n, state_m, w_in, b_i, b_f, attn_sink, m_norm,
           w_out, g_pre_mix, g_post_mix, g_pre_ffn, g_post_ffn, w_gate, w_up, w_down):
    depth = w_in.shape[0]
    xp, xs = x_prompt, x_sample
    prompt_states, sample_states = [], []
    for l in range(depth):
        xp, xs, st_p, st_s = _layer(xp, xs, cache_k[l], cache_v[l], state_C[l], state_n[l], state_m[l],
                                    w_in[l], b_i[l], b_f[l], attn_sink[l], m_norm[l], w_out[l],
                                    g_pre_mix[l], g_post_mix[l], g_pre_ffn[l], g_post_ffn[l],
                                    w_gate[l], w_up[l], w_down[l])
        prompt_states.append(st_p)
        sample_states.append(st_s)
    stack = lambda states, i: jnp.stack([s[i] for s in states], axis=0)
    return (xp, xs) + tuple(stack(prompt_states, i) for i in range(5)) + tuple(stack(sample_states, i) for i in range(5))
```

```python
import functools

import jax
import jax.numpy as jnp
from jax import lax
from jax.experimental import pallas as pl
from jax.experimental.pallas import tpu as pltpu

F32 = jnp.float32
BF16 = jnp.bfloat16
HIGHEST = lax.Precision.HIGHEST

D_MODEL = 1024
HEAD_DIM = 64
ATTN_HEADS = 8
KV_HEADS = 2
GROUP = ATTN_HEADS // KV_HEADS
ATTN_WIDTH = ATTN_HEADS * HEAD_DIM
KV_WIDTH = KV_HEADS * HEAD_DIM
WINDOW = 128
M_HEADS = 4
M_HEAD_DIM = 128
M_WIDTH = M_HEADS * M_HEAD_DIM
M_PAIRS = M_HEADS // 2
D_FF = 2816
EPS = 1e-6

LANES = 128
SUBLANES = 8
GATE_PAD = LANES
BF16_SUBLANES = 16
AUG_ROWS = M_HEAD_DIM + BF16_SUBLANES
IN_MAIN = ATTN_WIDTH + 2 * KV_WIDTH + 4 * M_WIDTH
IN_PAD = IN_MAIN + GATE_PAD
VMEM_LIMIT = 56 * 1024 * 1024

HEAD_ORDER = tuple(h for j in range(GROUP) for h in (j, j + GROUP))

TOKEN_TILE = 512
FFN_CHUNK = 256
ROW_GROUPS = 2
FFN_GROUP_ROWS = (256, 256)
FFN_GROUP_LAG = (0, 1)
WEIGHT_SLAB = 32
SAMPLE_BATCH = 16


def _rms(x, g):
    return x * lax.rsqrt(jnp.mean(x * x, axis=-1, keepdims=True) + EPS) * g


def _const_spec(shape):
    nd = len(shape)
    return pl.BlockSpec(shape, lambda i: (0,) * nd, pipeline_mode=pl.Buffered(1))


def _params(semantics):
    return pltpu.CompilerParams(dimension_semantics=semantics, vmem_limit_bytes=VMEM_LIMIT)


OFF_QA, OFF_KA, OFF_VA = 0, ATTN_WIDTH, ATTN_WIDTH + KV_WIDTH
OFF_QM = ATTN_WIDTH + 2 * KV_WIDTH
OFF_KM, OFF_VM, OFF_OM = OFF_QM + M_WIDTH, OFF_QM + 2 * M_WIDTH, OFF_QM + 3 * M_WIDTH


def _inproj_kernel(x_ref, g_ref, w_ref, qa_ref, ka_ref, va_ref, qm_ref, km_ref, vm_ref, om_ref, gt_ref):
    tm = x_ref.shape[0]
    step = tm // ROW_GROUPS
    for r0 in range(0, tm, step):
        rs = slice(r0, r0 + step)
        h = _rms(x_ref[rs, :], g_ref[...]).astype(BF16)

        def proj(off, n):
            return jnp.dot(h, w_ref[:, off:off + n], preferred_element_type=F32)

        qa_ref[rs, :] = proj(OFF_QA, ATTN_WIDTH) * (HEAD_DIM ** -0.5)
        ka_ref[rs, :] = proj(OFF_KA, KV_WIDTH)
        va_ref[rs, :] = proj(OFF_VA, KV_WIDTH)
        qm_ref[rs, :] = proj(OFF_QM, M_WIDTH)
        km_ref[rs, :] = (proj(OFF_KM, M_WIDTH) * (M_HEAD_DIM ** -0.5)).astype(km_ref.dtype)
        vm_ref[rs, :] = proj(OFF_VM, M_WIDTH).astype(vm_ref.dtype)
        om_ref[rs, :] = proj(OFF_OM, M_WIDTH)
        gt_ref[rs, :] = proj(IN_MAIN, GATE_PAD)


PROJ_WIDTHS = (ATTN_WIDTH, KV_WIDTH, KV_WIDTH, M_WIDTH, M_WIDTH, M_WIDTH, M_WIDTH, GATE_PAD)
PROJ_DTYPES = (F32, F32, F32, F32, BF16, BF16, F32, F32)


def _stack_heads(q_tiles):
    lane = lax.broadcasted_iota(jnp.int32, q_tiles[0].shape, 1)
    lo = lane < HEAD_DIM
    zero = jnp.zeros_like(q_tiles[0])
    parts = []
    for qt in q_tiles:
        parts += [jnp.where(lo, qt, zero), jnp.where(lo, zero, qt)]
    return jnp.concatenate(parts, axis=0)


def _unstack_heads(o, rows):
    lane = lax.broadcasted_iota(jnp.int32, (rows, LANES), 1)
    lo = lane < HEAD_DIM
    return [jnp.where(lo, o[(2 * j) * rows:(2 * j + 1) * rows, :], o[(2 * j + 1) * rows:(2 * j + 2) * rows, :])
            for j in range(GROUP)]


def _sink_rows(sink_ref, rows):
    return jnp.concatenate([jnp.full((rows, LANES), sink_ref[0, h], F32) for h in HEAD_ORDER], axis=0)


def _attn_sample_kernel(sink_ref, q_ref, kn_ref, vn_ref, ck_ref, cv_ref, o_ref, ko_ref, vo_ref, *, dec_seq):
    t = dec_seq
    bb = SAMPLE_BATCH
    nrow = ATTN_HEADS * t
    ts, ns = t.bit_length() - 1, nrow.bit_length() - 1
    r_c = lax.broadcasted_iota(jnp.int32, (nrow, WINDOW), 0) & (t - 1)
    c_c = lax.broadcasted_iota(jnp.int32, (nrow, WINDOW), 1)
    vis_cache = c_c > r_c
    r_n = lax.broadcasted_iota(jnp.int32, (bb * nrow, bb * t), 0)
    c_n = lax.broadcasted_iota(jnp.int32, (bb * nrow, bb * t), 1)
    vis_new = jnp.logical_and((r_n >> ns) == (c_n >> ts), (c_n & (t - 1)) <= (r_n & (t - 1)))
    sink = _sink_rows(sink_ref, t)
    nt = (((1,), (1,)), ((), ()))
    kn_all, vn_all = kn_ref[...], vn_ref[...]
    lane_w = lax.broadcasted_iota(jnp.int32, (KV_WIDTH, WINDOW), 1)
    is_new = lane_w >= WINDOW - t
    zero_rows = jnp.zeros((WINDOW - t, KV_WIDTH), F32)
    state_shape = (KV_HEADS, HEAD_DIM, WINDOW)

    def slide(old_t, new_rows):
        new_t = jnp.concatenate([zero_rows, new_rows], axis=0).T
        return jnp.where(is_new, new_t, pltpu.roll(old_t, WINDOW - t, axis=1)).reshape(state_shape)

    qs, s_c, cvs = [], [], []
    for b in range(bb):
        rows = slice(b * t, (b + 1) * t)
        ck = ck_ref[b].reshape(KV_WIDTH, WINDOW)
        cvs.append(cv_ref[b].reshape(KV_WIDTH, WINDOW))
        ko_ref[b] = slide(ck, kn_all[rows, :])
        vo_ref[b] = slide(cvs[b], vn_all[rows, :])
        qs.append(_stack_heads([q_ref[rows, j * LANES:(j + 1) * LANES] for j in range(GROUP)]).astype(BF16))
        s_c.append(jnp.where(vis_cache, jnp.dot(qs[b], ck.astype(BF16), preferred_element_type=F32), -jnp.inf))
    s_n = jnp.where(vis_new, lax.dot_general(jnp.concatenate(qs, axis=0), kn_all.astype(BF16), nt,
                                             preferred_element_type=F32), -jnp.inf)
    p_c, p_n, rden = [], [], []
    for b in range(bb):
        s_nb = s_n[b * nrow:(b + 1) * nrow, :]
        m = jnp.maximum(jnp.maximum(jnp.max(s_c[b], axis=-1, keepdims=True), jnp.max(s_nb, axis=-1, keepdims=True)),
                        sink)
        p_c.append(jnp.exp(s_c[b] - m))
        p_n.append(jnp.exp(s_nb - m[:, :bb * t]))
        rden.append(1.0 / (jnp.sum(p_c[b], axis=-1, keepdims=True) + jnp.sum(p_n[b], axis=-1, keepdims=True)
                           + jnp.exp(sink - m)))
    o_n = jnp.dot(jnp.concatenate(p_n, axis=0).astype(BF16), vn_all.astype(BF16), preferred_element_type=F32)
    outs = [[] for _ in range(GROUP)]
    for b in range(bb):
        o = (lax.dot_general(p_c[b].astype(BF16), cvs[b].astype(BF16), nt, preferred_element_type=F32)
             + o_n[b * nrow:(b + 1) * nrow, :]) * rden[b]
        for j, tile in enumerate(_unstack_heads(o, t)):
            outs[j].append(tile)
    for j, parts in enumerate(outs):
        o_ref[:, j * LANES:(j + 1) * LANES] = jnp.concatenate(parts, axis=0).astype(o_ref.dtype)


def _block_diag(a, b):
    za, zb = jnp.zeros_like(a), jnp.zeros_like(b)
    return jnp.concatenate([jnp.concatenate([a, zb], axis=1), jnp.concatenate([za, b], axis=1)], axis=0)


def _head_out(hh, mnorm_row, om):
    y = hh * lax.rsqrt(jnp.mean(hh * hh, axis=-1, keepdims=True) + EPS) * mnorm_row
    return (jax.nn.sigmoid(om) * y).astype(BF16)


def _mixer_prompt_kernel(sink_ref, bi_ref, bf_ref, x_ref, g_ref, win_ref, mnorm_ref, wg_ref, wu_ref, wd_ref, wo_ref,
                         mix_ref, kw_ref, vw_ref, c_ref, m_ref, w_ref, wgb_ref, wub_ref, wdb_ref, wob_ref,
                         kwt_ref, vwt_ref, ct_ref, nt_ref, mt_ref, bias_ref):
    step = pl.program_id(0)

    @pl.when(step == 0)
    def _():
        kw_ref[...] = jnp.zeros_like(kw_ref)
        vw_ref[...] = jnp.zeros_like(vw_ref)
        c_ref[...] = jnp.zeros_like(c_ref)
        m_ref[...] = jnp.zeros_like(m_ref)
        for j in range(GROUP):
            lo = win_ref[j * HEAD_DIM:(j + 1) * HEAD_DIM, :]
            hi = win_ref[(j + GROUP) * HEAD_DIM:(j + GROUP + 1) * HEAD_DIM, :]
            w_ref[:, j * LANES:(j + 1) * LANES] = jnp.concatenate([lo, hi], axis=0).T.astype(BF16)
        for c0 in range(ATTN_WIDTH, IN_MAIN, M_WIDTH):
            c1 = min(c0 + M_WIDTH, IN_MAIN)
            w_ref[:, c0:c1] = win_ref[c0:c1, :].T.astype(BF16)
        gw = jnp.concatenate([win_ref[IN_MAIN:IN_MAIN + 2 * M_HEADS, :],
                              jnp.zeros((GATE_PAD - 2 * M_HEADS, D_MODEL), F32)], axis=0)
        w_ref[:, IN_MAIN:] = gw.T.astype(BF16)

    wgb_ref[...] = wg_ref[...].astype(BF16)
    wub_ref[...] = wu_ref[...].astype(BF16)
    wdb_ref[...] = wd_ref[...].astype(BF16)
    wob_ref[...] = wo_ref[...].astype(BF16)

    tm = x_ref.shape[0]
    ln = WINDOW
    groups = [slice(r0, r0 + tm // ROW_GROUPS) for r0 in range(0, tm, tm // ROW_GROUPS)]
    blocks = [slice(r0, r0 + ln) for r0 in range(0, tm, ln)]
    hcols = [slice(h * M_HEAD_DIM, (h + 1) * M_HEAD_DIM) for h in range(M_HEADS)]
    pcols = [slice(2 * p * M_HEAD_DIM, 2 * (p + 1) * M_HEAD_DIM) for p in range(M_PAIRS)]
    nt = (((1,), (1,)), ((), ()))
    hs = [_rms(x_ref[rs, :], g_ref[...]).astype(BF16) for rs in groups]

    def proj(off, n):
        return jnp.concatenate([jnp.dot(h, w_ref[:, off:off + n], preferred_element_type=F32) for h in hs], axis=0)

    qt = proj(OFF_QM, M_WIDTH).astype(BF16).T
    gates = proj(IN_MAIN, GATE_PAD).T
    km = (proj(OFF_KM, M_WIDTH) * (M_HEAD_DIM ** -0.5)).astype(BF16)
    r = lax.broadcasted_iota(jnp.int32, (ln, ln), 0)
    c = lax.broadcasted_iota(jnp.int32, (ln, ln), 1)
    causal_t = r <= c
    upper = causal_t.astype(F32)
    lane8 = lax.broadcasted_iota(jnp.int32, (SUBLANES, ln), 1)
    ones_rows = jnp.ones((AUG_ROWS - M_HEAD_DIM, ln), BF16)
    zrows = jnp.zeros((ln - SUBLANES, ln), F32)
    row8 = lax.broadcasted_iota(jnp.int32, (SUBLANES, ln), 0)
    bias = jnp.zeros((SUBLANES, ln), F32)
    for h in range(M_HEADS):
        bias = jnp.where(row8 == h, bi_ref[0, h], jnp.where(row8 == M_HEADS + h, bf_ref[0, h], bias))
    gis = [gates[0:SUBLANES, rows] + bias for rows in blocks]
    prefix = jnp.dot(jnp.concatenate([jax.nn.log_sigmoid(gi) for gi in gis], axis=0), upper, precision=HIGHEST,
                     preferred_element_type=F32)
    pre = []
    for blk, gi in enumerate(gis):
        b = pltpu.roll(prefix[blk * SUBLANES:(blk + 1) * SUBLANES, :], M_HEADS, axis=0)
        g = gi - b
        cm0 = g
        sh = 1
        while sh < ln:
            cm0 = jnp.maximum(cm0, jnp.where(lane8 >= sh, pltpu.roll(cm0, sh, axis=1), -jnp.inf))
            sh *= 2
        b_last = jnp.broadcast_to(b[:, ln - 1:ln], b.shape)
        g_max = jnp.broadcast_to(cm0[:, ln - 1:ln], b.shape)
        g_cols = jnp.concatenate([g, zrows], axis=0).T
        pre.append((b, g_cols, cm0, b_last, g_max, (b_last - b) + gi))
    scores_m = [[jnp.dot(km[rows, pc], _block_diag(qt[hcols[2 * p], rows], qt[hcols[2 * p + 1], rows]),
                         preferred_element_type=F32) for p, pc in enumerate(pcols)] for rows in blocks]
    vt = proj(OFF_VM, M_WIDTH).astype(BF16).T

    qa = (proj(OFF_QA, ATTN_WIDTH) * (HEAD_DIM ** -0.5)).astype(BF16)
    kv = proj(OFF_KA, 2 * KV_WIDTH)
    ka, va = kv[:, :KV_WIDTH], kv[:, KV_WIDTH:]
    nrow = ATTN_HEADS * ln
    ra = lax.broadcasted_iota(jnp.int32, (nrow, ln), 0) & (ln - 1)
    ca = lax.broadcasted_iota(jnp.int32, (nrow, ln), 1)
    own = ca <= ra
    sink = _sink_rows(sink_ref, ln)
    no_prev = jnp.where(step == 0, -jnp.inf, 0.0)
    ones = jnp.ones((2 * ln, LANES), BF16)
    kcats, v_augs = [], []
    for blk, rows in enumerate(blocks):
        if blk == 0:
            k_prev, v_prev = kw_ref[...], vw_ref[...]
        else:
            k_prev, v_prev = ka[blocks[blk - 1], :], va[blocks[blk - 1], :]
        kcats.append(jnp.concatenate([ka[rows, :], k_prev], axis=0).astype(BF16))
        vcat = jnp.concatenate([va[rows, :], v_prev], axis=0).astype(BF16)
        v_augs.append(jnp.concatenate([vcat, ones], axis=1))
    kw_ref[...] = ka[blocks[-1], :]
    vw_ref[...] = va[blocks[-1], :]

    m_prev = m_ref[...]
    scal = []
    for b, _, cm0, b_last, g_max, w_end_arg in pre:
        cm = jnp.maximum(cm0, m_prev)
        m_end = b_last + jnp.maximum(g_max, m_prev)
        scal.append((cm, jnp.exp(m_prev - cm), jnp.exp(-(b + cm)), jnp.exp(b_last + m_prev - m_end),
                     jnp.exp(w_end_arg - m_end)))
        m_prev = m_end
    m_ref[...] = m_prev
    gated = []
    for rows, s_t, (_, g_cols, *_), (cm, _, _, _, w_end) in zip(blocks, scores_m, pre, scal):
        vts = [jnp.concatenate([vt[hc, rows], ones_rows], axis=0) for hc in hcols]
        sqks = [(jnp.exp(jnp.where(causal_t, g_cols[:, h:h + 1] - cm[h:h + 1, :], -jnp.inf))
                 * s_t[h // 2][:, (h % 2) * ln:(h % 2 + 1) * ln]).astype(BF16) for h in range(M_HEADS)]
        kvws = [(vts[h].astype(F32) * w_end[h:h + 1, :]).astype(BF16) for h in range(M_HEADS)]
        gated.append((vts, sqks, kvws))
    pairs = range(M_PAIRS)
    upds = [[jnp.dot(jnp.concatenate([kvws[2 * p], kvws[2 * p + 1]], axis=1),
                     _block_diag(km[rows, hcols[2 * p]], km[rows, hcols[2 * p + 1]]),
                     preferred_element_type=F32) for p in pairs] for rows, (_, _, kvws) in zip(blocks, gated)]
    intras = [[jnp.dot(jnp.concatenate([vts[2 * p], vts[2 * p + 1]], axis=1),
                       _block_diag(sqks[2 * p], sqks[2 * p + 1]), preferred_element_type=F32) for p in pairs]
              for vts, sqks, _ in gated]

    scores_a = [lax.dot_general(_stack_heads([qa[rows, j * LANES:(j + 1) * LANES] for j in range(GROUP)]), kcat, nt,
                                preferred_element_type=F32) for rows, kcat in zip(blocks, kcats)]
    probs, maxes = [], []
    for blk, s in enumerate(scores_a):
        s_prev = s[:, ln:]
        if blk == 0:
            s_prev = s_prev + no_prev
        sc = jnp.where(own, s[:, :ln], s_prev)
        mx = jnp.maximum(jnp.max(sc, axis=-1, keepdims=True), sink)
        p = jnp.exp(sc - mx)
        zero = jnp.zeros_like(p)
        probs.append(jnp.concatenate([jnp.where(own, p, zero), jnp.where(own, zero, p)], axis=1).astype(BF16))
        maxes.append(mx)
    om = proj(OFF_OM, M_WIDTH)

    mem = [jnp.concatenate([c_ref[2 * p], c_ref[2 * p + 1]], axis=1) for p in range(M_PAIRS)]
    pair_row = lambda x, p: jnp.concatenate([x[2 * p:2 * p + 1, :], x[2 * p + 1:2 * p + 2, :]], axis=1)
    for rows, upd, intra, (_, w_inter, e_negm, dec, _), p2, v_aug, mx in zip(
            blocks, upds, intras, scal, probs, v_augs, maxes):
        inter = [jnp.dot(mem[p].astype(BF16), _block_diag(qt[hcols[2 * p], rows], qt[hcols[2 * p + 1], rows]),
                         preferred_element_type=F32) for p in pairs]
        mem = [pair_row(dec, p) * mem[p] + upd[p] for p in pairs]
        o = jnp.dot(p2, v_aug, preferred_element_type=F32)
        o = o[:, :LANES] * (1.0 / (o[:, LANES:] + jnp.exp(sink - mx)))
        for j, tile in enumerate(_unstack_heads(o, ln)):
            mix_ref[rows, j * LANES:(j + 1) * LANES] = tile.astype(mix_ref.dtype)
        for p in pairs:
            num = inter[p] * pair_row(w_inter, p) + intra[p]
            den = jnp.maximum(jnp.abs(num[M_HEAD_DIM:M_HEAD_DIM + 1, :]), pair_row(e_negm, p))
            hh = num[:M_HEAD_DIM, :] * (1.0 / den)
            y = hh * lax.rsqrt(jnp.mean(hh * hh, axis=0, keepdims=True) + EPS)
            for h in (2 * p, 2 * p + 1):
                y_h = y[:, (h % 2) * ln:(h % 2 + 1) * ln]
                mcols = slice(ATTN_WIDTH + h * M_HEAD_DIM, ATTN_WIDTH + (h + 1) * M_HEAD_DIM)
                mix_ref[rows, mcols] = (jax.nn.sigmoid(om[rows, hcols[h]])
                                        * (y_h.T * mnorm_ref[:, hcols[h]])).astype(mix_ref.dtype)
    for p in range(M_PAIRS):
        c_ref[2 * p] = mem[p][:, :M_HEAD_DIM]
        c_ref[2 * p + 1] = mem[p][:, M_HEAD_DIM:]

    @pl.when(step == pl.num_programs(0) - 1)
    def _():
        kwt_ref[...] = kw_ref[...].T
        vwt_ref[...] = vw_ref[...].T
        for h in range(M_HEADS):
            ct_ref[h] = c_ref[h][:M_HEAD_DIM, :].T
            nt_ref[h:h + 1, :] = c_ref[h][M_HEAD_DIM:M_HEAD_DIM + 1, :]
        diag = jnp.sum(jnp.where(row8 == lane8, m_ref[...], 0.0), axis=0, keepdims=True)
        mt_ref[...] = diag[:, :M_HEADS]
        lane = lax.broadcasted_iota(jnp.int32, bias_ref.shape, 1)
        gate_bias = jnp.zeros(bias_ref.shape, F32)
        for h in range(M_HEADS):
            gate_bias = jnp.where(lane == h, bi_ref[0, h], jnp.where(lane == M_HEADS + h, bf_ref[0, h], gate_bias))
        bias_ref[...] = gate_bias


def _out_row_block(i):
    per_head = (ATTN_WIDTH // ATTN_HEADS) // WEIGHT_SLAB
    j, part = i // per_head, i % per_head
    head = (j % 2) * GROUP + j // 2
    return jnp.where(i < ATTN_HEADS * per_head, head * per_head + part, i)


def _mixer_prompt(sink, b_i, b_f, x, g_pre, w_in_t, mnorm, w_gate, w_up, w_down, w_out):
    m = x.shape[0]
    tm = TOKEN_TILE
    steps = m // tm
    assert D_MODEL == steps * WEIGHT_SLAB and D_FF % (steps // 2) == 0 and (D_FF // (steps // 2)) % BF16_SUBLANES == 0
    down_slab = D_FF // (steps // 2)
    row = lambda n: pl.BlockSpec((tm, n), lambda i: (i, 0))
    whole = lambda shape: pl.BlockSpec(shape, lambda i: (0,) * len(shape))
    smem = pl.BlockSpec(memory_space=pltpu.SMEM)
    slab = lambda n: pl.BlockSpec((WEIGHT_SLAB, n), lambda i: (i, 0))
    down = pl.BlockSpec((down_slab, D_MODEL), lambda i: (i // 2, 0))
    c_shape = (M_HEADS, AUG_ROWS, M_HEAD_DIM)
    w_shape = (WINDOW, KV_WIDTH)
    wt_shape = (KV_WIDTH, WINDOW)
    ct_shape = (M_HEADS, M_HEAD_DIM, M_HEAD_DIM)
    n_shape = (M_HEADS, M_HEAD_DIM)
    b_shape = (1, GATE_PAD)
    s_shape = (SUBLANES, LANES)
    sds = jax.ShapeDtypeStruct
    return pl.pallas_call(
        _mixer_prompt_kernel,
        grid=(steps,),
        in_specs=[smem, smem, smem, row(D_MODEL), _const_spec((1, D_MODEL)), _const_spec(w_in_t.shape),
                  _const_spec((1, M_WIDTH)), slab(D_FF), slab(D_FF), down,
                  pl.BlockSpec((WEIGHT_SLAB, D_MODEL), lambda i: (_out_row_block(i), 0))],
        out_specs=[row(ATTN_WIDTH + M_WIDTH), whole(w_shape), whole(w_shape), whole(c_shape), whole(s_shape),
                   whole((D_MODEL, IN_PAD)), slab(D_FF), slab(D_FF), down, slab(D_MODEL),
                   whole(wt_shape), whole(wt_shape), whole(ct_shape), whole(n_shape),
                   whole((1, M_HEADS)), whole(b_shape)],
        out_shape=[sds((m, ATTN_WIDTH + M_WIDTH), BF16), sds(w_shape, F32), sds(w_shape, F32), sds(c_shape, F32),
                   sds(s_shape, F32), sds((D_MODEL, IN_PAD), BF16), sds(w_gate.shape, BF16), sds(w_up.shape, BF16),
                   sds(w_down.shape, BF16), sds(w_out.shape, BF16),
                   sds(wt_shape, F32), sds(wt_shape, F32), sds(ct_shape, F32), sds(n_shape, F32),
                   sds((1, M_HEADS), F32), sds(b_shape, F32)],
        compiler_params=_params(("arbitrary",)),
        name="mixer_prompt",
    )(sink, b_i, b_f, x, g_pre, w_in_t, mnorm, w_gate, w_up, w_down, w_out)


def _gates(g_blk, bias_row, cum):
    pre = g_blk + bias_row
    lane = lax.broadcasted_iota(jnp.int32, pre.shape, 1)
    a = jnp.where(lane < M_HEADS, pre, jax.nn.log_sigmoid(pre))
    b = jnp.dot(cum, a, precision=HIGHEST, preferred_element_type=F32)
    return a, b, a.T, b.T


def _col(x, j):
    return jnp.broadcast_to(x[:, j:j + 1], x.shape)


def _mlstm_gated_scores(q, k, a, b, at, bt, h, mask, m_prev, last):
    bc, ic = _col(b, M_HEADS + h), _col(a, h)
    br, ir = bt[M_HEADS + h:M_HEADS + h + 1, :], at[h:h + 1, :]
    d = jnp.where(mask, (bc - br) + ir, -jnp.inf)
    inter = bc + m_prev
    m_t = jnp.maximum(inter, jnp.max(d, axis=-1, keepdims=True))
    w_inter = jnp.exp(inter - m_t)
    sqk = jnp.exp(d - m_t) * lax.dot_general(q, k, (((1,), (1,)), ((), ())), preferred_element_type=F32)
    m_end = last(m_t)
    bl = last(bc)
    dec = jnp.exp(bl + m_prev - m_end)
    w_end = jnp.exp((bl - bc) + ic - m_end)
    kw = k.astype(F32) * w_end
    return w_inter, sqk, m_t, m_end, dec, kw


def _mlstm_sample_kernel(q_ref, k_ref, v_ref, om_ref, g_ref, m0_ref, bias_ref, mnorm_ref, c_ref, n_ref,
                         o_ref, c_out, n_out, m_out, *, dec_seq):
    t = dec_seq
    bb = SAMPLE_BATCH
    ln = bb * t
    r = lax.broadcasted_iota(jnp.int32, (ln, ln), 0)
    c = lax.broadcasted_iota(jnp.int32, (ln, ln), 1)
    shift = t.bit_length() - 1
    same = (r >> shift) == (c >> shift)
    mask = jnp.logical_and(same, c <= r)
    cum = mask.astype(F32)
    expand = (c == (r >> shift)).astype(F32)
    gather = ((c >> shift) == r).astype(F32)
    is_last = (r & (t - 1)) == t - 1
    is_first = (r & (t - 1)) == 0

    def last(x):
        y = jnp.where(is_last, x, 0.0)
        step = 1
        while step < t:
            y = y + pltpu.roll(y, ln - step, axis=0)
            step *= 2
        return y

    a, b, at, bt = _gates(g_ref[...], bias_ref[...], cum)
    m0 = m0_ref[...]
    lane = lax.broadcasted_iota(jnp.int32, (ln, LANES), 1)
    zpad = jnp.zeros((ln - bb, M_HEAD_DIM), F32)
    heads = range(M_HEADS)
    hcols = [slice(h * M_HEAD_DIM, (h + 1) * M_HEAD_DIM) for h in heads]
    qfs = [q_ref[:, hc] for hc in hcols]
    vs = [v_ref[:, hc] for hc in hcols]
    qcs = [jnp.concatenate(
        [jnp.dot(qfs[h][s * t:(s + 1) * t, :], c_ref[s * M_HEADS + h].astype(BF16).astype(F32),
                 preferred_element_type=F32) for s in range(bb)], axis=0) for h in heads]
    n_exps = [jnp.dot(expand, jnp.concatenate([n_ref[:, hc], zpad], axis=0), precision=HIGHEST,
                      preferred_element_type=F32) for hc in hcols]
    parts = [_mlstm_gated_scores(qfs[h].astype(BF16), k_ref[:, hcols[h]], a, b, at, bt, h, mask,
                                 _col(m0, h), last) for h in heads]
    intras = [jnp.dot(parts[h][1].astype(BF16), vs[h], preferred_element_type=F32) for h in heads]
    m_cols = jnp.zeros((ln, LANES), F32)
    kwts = []
    for h in heads:
        w_inter, sqk, m_t, m_end, dec, kw = parts[h]
        num = w_inter * qcs[h] + intras[h]
        nq = w_inter * jnp.sum(qfs[h] * n_exps[h], axis=-1, keepdims=True) + jnp.sum(sqk, axis=-1, keepdims=True)
        hh = num / jnp.maximum(jnp.abs(nq), jnp.exp(-m_t))
        o_ref[:, hcols[h]] = _head_out(hh, mnorm_ref[:, hcols[h]], om_ref[:, hcols[h]])
        kwts.append(kw.T.astype(BF16))
        m_cols = jnp.where(lane == h, m_end, m_cols)
    m_out[...] = m_cols
    for h in heads:
        dec = parts[h][4]
        for s in range(bb):
            lhs = jnp.where((c >> shift) == s, kwts[h], jnp.zeros_like(kwts[h]))
            upd = jnp.dot(lhs, vs[h], preferred_element_type=F32)
            c_out[s * M_HEADS + h] = dec[s * t:s * t + 1, :] * c_ref[s * M_HEADS + h] + upd
    for h in heads:
        dec, kw = parts[h][4], parts[h][5]
        n_new = jnp.dot(gather, jnp.where(is_first, dec * n_exps[h], 0.0) + kw, precision=HIGHEST,
                        preferred_element_type=F32)
        n_out[:, hcols[h]] = n_new[:bb, :]


def _mixer_sample_kernel(sink_ref, x_ref, g_ref, w_ref, ck_ref, cv_ref, m0_ref, bias_ref, mnorm_ref, c_ref, n_ref,
                         mix_ref, ko_ref, vo_ref, c_out, n_out, m_out, *proj_refs, dec_seq):
    step = pl.program_id(0)

    @pl.when(step == 0)
    def _():
        _inproj_kernel(x_ref, g_ref, w_ref, *proj_refs)

    tm = mix_ref.shape[0]
    rows = pl.ds(pl.multiple_of(step * tm, tm), tm)
    qa_ref, kn_ref, vn_ref, qm_ref, km_ref, vm_ref, om_ref, gt_ref = [r.at[rows, :] for r in proj_refs]
    _attn_sample_kernel(sink_ref, qa_ref, kn_ref, vn_ref, ck_ref, cv_ref, mix_ref.at[:, pl.ds(0, ATTN_WIDTH)],
                        ko_ref, vo_ref, dec_seq=dec_seq)
    _mlstm_sample_kernel(qm_ref, km_ref, vm_ref, om_ref, gt_ref, m0_ref, bias_ref, mnorm_ref, c_ref, n_ref,
                         mix_ref.at[:, pl.ds(ATTN_WIDTH, M_WIDTH)], c_out, n_out, m_out, dec_seq=dec_seq)


def _mixer_sample(sink, x, g_pre, w_pad, cache_k, cache_v, m0_rows, bias_row, mnorm, c_in, n_in, dec_seq):
    m = x.shape[0]
    bb = SAMPLE_BATCH
    tm = bb * dec_seq
    nb = m // dec_seq
    row = lambda n: pl.BlockSpec((tm, n), lambda i: (i, 0))
    whole = lambda shape: pl.BlockSpec(shape, lambda i: (0,) * len(shape))
    cache = pl.BlockSpec((bb, KV_HEADS, HEAD_DIM, WINDOW), lambda i: (i, 0, 0, 0))
    c_spec = pl.BlockSpec((bb * M_HEADS, M_HEAD_DIM, M_HEAD_DIM), lambda i: (i, 0, 0))
    n_spec = pl.BlockSpec((bb, M_WIDTH), lambda i: (i, 0))
    sds = jax.ShapeDtypeStruct
    return pl.pallas_call(
        functools.partial(_mixer_sample_kernel, dec_seq=dec_seq),
        grid=(nb // bb,),
        in_specs=[pl.BlockSpec(memory_space=pltpu.SMEM), _const_spec(x.shape), _const_spec((1, D_MODEL)),
                  _const_spec(w_pad.shape), cache, cache, row(LANES),
                  whole((1, GATE_PAD)), whole((1, M_WIDTH)), c_spec, n_spec],
        out_specs=[row(ATTN_WIDTH + M_WIDTH), cache, cache, c_spec, n_spec, row(LANES)],
        out_shape=[sds((m, ATTN_WIDTH + M_WIDTH), BF16), sds(cache_k.shape, F32), sds(cache_v.shape, F32),
                   sds(c_in.shape, F32), sds(n_in.shape, F32), sds((m, LANES), F32)],
        scratch_shapes=[pltpu.VMEM((m, n), dt) for n, dt in zip(PROJ_WIDTHS, PROJ_DTYPES)],
        compiler_params=_params(("arbitrary",)),
        name="mixer_sample",
    )(sink, x, g_pre, w_pad, cache_k, cache_v, m0_rows, bias_row, mnorm, c_in, n_in)


def _out_ffn_kernel(x_ref, mix_ref, wo_ref, g1_ref, g2_ref, wg_ref, wu_ref, wd_ref, g3_ref, o_ref):
    assert sum(FFN_GROUP_ROWS) == x_ref.shape[0]
    starts = [sum(FFN_GROUP_ROWS[:i]) for i in range(len(FFN_GROUP_ROWS))]
    groups = [slice(r, r + n) for r, n in zip(starts, FFN_GROUP_ROWS)]
    ys = [jnp.dot(mix_ref[rs, :], wo_ref[...], preferred_element_type=F32) for rs in groups]
    x1s = [x_ref[rs, :] + _rms(y, g1_ref[...]) for rs, y in zip(groups, ys)]
    fs = [_rms(x1, g2_ref[...]).astype(BF16) for x1 in x1s]
    accs = [None] * len(groups)
    acts = {}
    chunks = D_FF // FFN_CHUNK
    for rnd in range(chunks + max(FFN_GROUP_LAG) + 1):
        for i, f in enumerate(fs):
            c = rnd - FFN_GROUP_LAG[i]
            if 0 <= c < chunks:
                cols = slice(c * FFN_CHUNK, (c + 1) * FFN_CHUNK)
                g = jnp.dot(f, wg_ref[:, cols], preferred_element_type=F32)
                u = jnp.dot(f, wu_ref[:, cols], preferred_element_type=F32)
                acts[i, c] = (g * jax.nn.sigmoid(g) * u).astype(BF16)
        for i in range(len(groups)):
            c = rnd - FFN_GROUP_LAG[i] - 1
            if 0 <= c < chunks:
                part = jnp.dot(acts.pop((i, c)), wd_ref[c * FFN_CHUNK:(c + 1) * FFN_CHUNK, :],
                               preferred_element_type=F32)
                accs[i] = part if accs[i] is None else accs[i] + part
    for rs, x1, acc in zip(groups, x1s, accs):
        o_ref[rs, :] = x1 + _rms(acc, g3_ref[...])


def _out_ffn(x, mix, w_out, g_post_mix, g_pre_ffn, w_gate, w_up, w_down, g_post_ffn):
    m = x.shape[0]
    tm = TOKEN_TILE
    row = lambda n: pl.BlockSpec((tm, n), lambda i: (i, 0))
    vec = _const_spec((1, D_MODEL))
    return pl.pallas_call(
        _out_ffn_kernel,
        grid=(m // tm,),
        in_specs=[row(D_MODEL), row(ATTN_WIDTH + M_WIDTH), _const_spec((D_MODEL, D_MODEL)), vec, vec,
                  _const_spec((D_MODEL, D_FF)), _const_spec((D_MODEL, D_FF)), _const_spec((D_FF, D_MODEL)), vec],
        out_specs=row(D_MODEL),
        out_shape=jax.ShapeDtypeStruct((m, D_MODEL), F32),
        compiler_params=_params(("parallel",)),
        name="out_ffn",
    )(x, mix, w_out, g_post_mix, g_pre_ffn, w_gate, w_up, w_down, g_post_ffn)


def _layer(xp, xs, cache_k, cache_v, state_c, state_n, state_m, w_in, b_i, b_f, attn_sink, m_norm, w_out,
           g_pre_mix, g_post_mix, g_pre_ffn, g_post_ffn, w_gate, w_up, w_down):
    bp, sp, _ = xp.shape
    bs, ts, _ = xs.shape
    assert bp == 1 and sp % TOKEN_TILE == 0 and TOKEN_TILE % WINDOW == 0
    assert ts & (ts - 1) == 0 and (bs * ts) % TOKEN_TILE == 0 and bs % SAMPLE_BATCH == 0

    row = lambda v: v.reshape(1, -1)
    sink = row(attn_sink)

    x2 = xp.reshape(sp, D_MODEL)
    mix, _, _, _, _, w_pad, wg, wu, wd, wo, k_wt, v_wt, c_p, n_p, m_p, bias_row = _mixer_prompt(
        sink, row(b_i), row(b_f), x2, row(g_pre_mix), w_in.T, row(m_norm), w_gate, w_up, w_down, w_out)
    ffn = (wo, row(g_post_mix), row(g_pre_ffn), wg, wu, wd, row(g_post_ffn))
    yp = _out_ffn(x2, mix, *ffn).reshape(xp.shape)
    window_major = lambda c: jnp.transpose(c, (0, 3, 1, 2))
    k_p = window_major(k_wt.reshape(1, KV_HEADS, HEAD_DIM, WINDOW))
    v_p = window_major(v_wt.reshape(1, KV_HEADS, HEAD_DIM, WINDOW))
    c_p = c_p.reshape(1, M_HEADS, M_HEAD_DIM, M_HEAD_DIM)
    n_p = n_p.reshape(1, M_HEADS, M_HEAD_DIM)

    x2 = xs.reshape(bs * ts, D_MODEL)
    feature_major = lambda c: jnp.transpose(c, (0, 2, 3, 1))
    m0_rows = jnp.pad(jnp.repeat(state_m, ts, axis=0), ((0, 0), (0, LANES - M_HEADS)))
    mix, k_s, v_s, c_s, n_s, m_rows = _mixer_sample(
        sink, x2, row(g_pre_mix), w_pad, feature_major(cache_k), feature_major(cache_v), m0_rows, bias_row,
        row(m_norm), state_c.reshape(bs * M_HEADS, M_HEAD_DIM, M_HEAD_DIM), state_n.reshape(bs, M_WIDTH), ts)
    ys = _out_ffn(x2, mix, *ffn).reshape(xs.shape)
    k_s, v_s = window_major(k_s), window_major(v_s)
    c_s = c_s.reshape(bs, M_HEADS, M_HEAD_DIM, M_HEAD_DIM)
    n_s = n_s.reshape(bs, M_HEADS, M_HEAD_DIM)
    m_s = m_rows[ts - 1::ts, :M_HEADS]
    return yp, ys, (k_p, v_p, c_p, n_p, m_p), (k_s, v_s, c_s, n_s, m_s)


def kernel(x_prompt, x_sample, cache_k, cache_v, state_C, state_n, state_m, w_in, b_i, b_f, attn_sink, m_norm,
           w_out, g_pre_mix, g_post_mix, g_pre_ffn, g_post_ffn, w_gate, w_up, w_down):
    depth = w_in.shape[0]
    xp, xs = x_prompt, x_sample
    prompt_states, sample_states = [], []
    for l in range(depth):
        xp, xs, st_p, st_s = _layer(xp, xs, cache_k[l], cache_v[l], state_C[l], state_n[l], state_m[l],
                                    w_in[l], b_i[l], b_f[l], attn_sink[l], m_norm[l], w_out[l],
                                    g_pre_mix[l], g_post_mix[l], g_pre_ffn[l], g_post_ffn[l],
                                    w_gate[l], w_up[l], w_down[l])
        prompt_states.append(st_p)
        sample_states.append(st_s)
    stack = lambda states, i: jnp.stack([s[i] for s in states], axis=0)
    return (xp, xs) + tuple(stack(prompt_states, i) for i in range(5)) + tuple(stack(sample_states, i) for i in range(5))
```

```python
import functools

import jax
import jax.numpy as jnp
from jax import lax
from jax.experimental import pallas as pl
from jax.experimental.pallas import tpu as pltpu

F32 = jnp.float32
BF16 = jnp.bfloat16
HIGHEST = lax.Precision.HIGHEST

D_MODEL = 1024
HEAD_DIM = 64
ATTN_HEADS = 8
KV_HEADS = 2
GROUP = ATTN_HEADS // KV_HEADS
ATTN_WIDTH = ATTN_HEADS * HEAD_DIM
KV_WIDTH = KV_HEADS * HEAD_DIM
WINDOW = 128
M_HEADS = 4
M_HEAD_DIM = 128
M_WIDTH = M_HEADS * M_HEAD_DIM
M_PAIRS = M_HEADS // 2
D_FF = 2816
EPS = 1e-6

LANES = 128
SUBLANES = 8
GATE_PAD = LANES
BF16_SUBLANES = 16
AUG_ROWS = M_HEAD_DIM + BF16_SUBLANES
IN_MAIN = ATTN_WIDTH + 2 * KV_WIDTH + 4 * M_WIDTH
IN_PAD = IN_MAIN + GATE_PAD
VMEM_LIMIT = 56 * 1024 * 1024

HEAD_ORDER = tuple(h for j in range(GROUP) for h in (j, j + GROUP))

TOKEN_TILE = 512
FFN_CHUNK = 256
ROW_GROUPS = 2
FFN_GROUP_ROWS = (256, 256)
FFN_GROUP_LAG = (0, 1)
WEIGHT_SLAB = 32
SAMPLE_BATCH = 16


def _rms(x, g):
    return x * lax.rsqrt(jnp.mean(x * x, axis=-1, keepdims=True) + EPS) * g


def _const_spec(shape):
    nd = len(shape)
    return pl.BlockSpec(shape, lambda i: (0,) * nd, pipeline_mode=pl.Buffered(1))


def _params(semantics):
    return pltpu.CompilerParams(dimension_semantics=semantics, vmem_limit_bytes=VMEM_LIMIT)


OFF_QA, OFF_KA, OFF_VA = 0, ATTN_WIDTH, ATTN_WIDTH + KV_WIDTH
OFF_QM = ATTN_WIDTH + 2 * KV_WIDTH
OFF_KM, OFF_VM, OFF_OM = OFF_QM + M_WIDTH, OFF_QM + 2 * M_WIDTH, OFF_QM + 3 * M_WIDTH


def _inproj_kernel(x_ref, g_ref, w_ref, qa_ref, ka_ref, va_ref, qm_ref, km_ref, vm_ref, om_ref, gt_ref):
    tm = x_ref.shape[0]
    step = tm // ROW_GROUPS
    for r0 in range(0, tm, step):
        rs = slice(r0, r0 + step)
        h = _rms(x_ref[rs, :], g_ref[...]).astype(BF16)

        def proj(off, n):
            return jnp.dot(h, w_ref[:, off:off + n], preferred_element_type=F32)

        qa_ref[rs, :] = proj(OFF_QA, ATTN_WIDTH) * (HEAD_DIM ** -0.5)
        ka_ref[rs, :] = proj(OFF_KA, KV_WIDTH)
        va_ref[rs, :] = proj(OFF_VA, KV_WIDTH)
        qm_ref[rs, :] = proj(OFF_QM, M_WIDTH)
        km_ref[rs, :] = (proj(OFF_KM, M_WIDTH) * (M_HEAD_DIM ** -0.5)).astype(km_ref.dtype)
        vm_ref[rs, :] = proj(OFF_VM, M_WIDTH).astype(vm_ref.dtype)
        om_ref[rs, :] = proj(OFF_OM, M_WIDTH)
        gt_ref[rs, :] = proj(IN_MAIN, GATE_PAD)


PROJ_WIDTHS = (ATTN_WIDTH, KV_WIDTH, KV_WIDTH, M_WIDTH, M_WIDTH, M_WIDTH, M_WIDTH, GATE_PAD)
PROJ_DTYPES = (F32, F32, F32, F32, BF16, BF16, F32, F32)


def _stack_heads(q_tiles):
    lane = lax.broadcasted_iota(jnp.int32, q_tiles[0].shape, 1)
    lo = lane < HEAD_DIM
    zero = jnp.zeros_like(q_tiles[0])
    parts = []
    for qt in q_tiles:
        parts += [jnp.where(lo, qt, zero), jnp.where(lo, zero, qt)]
    return jnp.concatenate(parts, axis=0)


def _unstack_heads(o, rows):
    lane = lax.broadcasted_iota(jnp.int32, (rows, LANES), 1)
    lo = lane < HEAD_DIM
    return [jnp.where(lo, o[(2 * j) * rows:(2 * j + 1) * rows, :], o[(2 * j + 1) * rows:(2 * j + 2) * rows, :])
            for j in range(GROUP)]


def _sink_rows(sink_ref, rows):
    return jnp.concatenate([jnp.full((rows, LANES), sink_ref[0, h], F32) for h in HEAD_ORDER], axis=0)


def _attn_sample_kernel(sink_ref, q_ref, kn_ref, vn_ref, ck_ref, cv_ref, o_ref, ko_ref, vo_ref, *, dec_seq):
    t = dec_seq
    bb = SAMPLE_BATCH
    nrow = ATTN_HEADS * t
    ts, ns = t.bit_length() - 1, nrow.bit_length() - 1
    r_c = lax.broadcasted_iota(jnp.int32, (nrow, WINDOW), 0) & (t - 1)
    c_c = lax.broadcasted_iota(jnp.int32, (nrow, WINDOW), 1)
    vis_cache = c_c > r_c
    r_n = lax.broadcasted_iota(jnp.int32, (bb * nrow, bb * t), 0)
    c_n = lax.broadcasted_iota(jnp.int32, (bb * nrow, bb * t), 1)
    vis_new = jnp.logical_and((r_n >> ns) == (c_n >> ts), (c_n & (t - 1)) <= (r_n & (t - 1)))
    sink = _sink_rows(sink_ref, t)
    nt = (((1,), (1,)), ((), ()))
    kn_all, vn_all = kn_ref[...], vn_ref[...]
    lane_w = lax.broadcasted_iota(jnp.int32, (KV_WIDTH, WINDOW), 1)
    is_new = lane_w >= WINDOW - t
    zero_rows = jnp.zeros((WINDOW - t, KV_WIDTH), F32)
    state_shape = (KV_HEADS, HEAD_DIM, WINDOW)

    def slide(old_t, new_rows):
        new_t = jnp.concatenate([zero_rows, new_rows], axis=0).T
        return jnp.where(is_new, new_t, pltpu.roll(old_t, WINDOW - t, axis=1)).reshape(state_shape)

    qs, s_c, cvs = [], [], []
    for b in range(bb):
        rows = slice(b * t, (b + 1) * t)
        ck = ck_ref[b].reshape(KV_WIDTH, WINDOW)
        cvs.append(cv_ref[b].reshape(KV_WIDTH, WINDOW))
        ko_ref[b] = slide(ck, kn_all[rows, :])
        vo_ref[b] = slide(cvs[b], vn_all[rows, :])
        qs.append(_stack_heads([q_ref[rows, j * LANES:(j + 1) * LANES] for j in range(GROUP)]).astype(BF16))
        s_c.append(jnp.where(vis_cache, jnp.dot(qs[b], ck.astype(BF16), preferred_element_type=F32), -jnp.inf))
    s_n = jnp.where(vis_new, lax.dot_general(jnp.concatenate(qs, axis=0), kn_all.astype(BF16), nt,
                                             preferred_element_type=F32), -jnp.inf)
    p_c, p_n, rden = [], [], []
    for b in range(bb):
        s_nb = s_n[b * nrow:(b + 1) * nrow, :]
        m = jnp.maximum(jnp.maximum(jnp.max(s_c[b], axis=-1, keepdims=True), jnp.max(s_nb, axis=-1, keepdims=True)),
                        sink)
        p_c.append(jnp.exp(s_c[b] - m))
        p_n.append(jnp.exp(s_nb - m[:, :bb * t]))
        rden.append(1.0 / (jnp.sum(p_c[b], axis=-1, keepdims=True) + jnp.sum(p_n[b], axis=-1, keepdims=True)
                           + jnp.exp(sink - m)))
    o_n = jnp.dot(jnp.concatenate(p_n, axis=0).astype(BF16), vn_all.astype(BF16), preferred_element_type=F32)
    outs = [[] for _ in range(GROUP)]
    for b in range(bb):
        o = (lax.dot_general(p_c[b].astype(BF16), cvs[b].astype(BF16), nt, preferred_element_type=F32)
             + o_n[b * nrow:(b + 1) * nrow, :]) * rden[b]
        for j, tile in enumerate(_unstack_heads(o, t)):
            outs[j].append(tile)
    for j, parts in enumerate(outs):
        o_ref[:, j * LANES:(j + 1) * LANES] = jnp.concatenate(parts, axis=0).astype(o_ref.dtype)


def _block_diag(a, b):
    za, zb = jnp.zeros_like(a), jnp.zeros_like(b)
    return jnp.concatenate([jnp.concatenate([a, zb], axis=1), jnp.concatenate([za, b], axis=1)], axis=0)


def _head_out(hh, mnorm_row, om):
    y = hh * lax.rsqrt(jnp.mean(hh * hh, axis=-1, keepdims=True) + EPS) * mnorm_row
    return (jax.nn.sigmoid(om) * y).astype(BF16)


def _mixer_prompt_kernel(sink_ref, bi_ref, bf_ref, x_ref, g_ref, win_ref, mnorm_ref, wg_ref, wu_ref, wd_ref, wo_ref,
                         mix_ref, kw_ref, vw_ref, c_ref, m_ref, w_ref, wgb_ref, wub_ref, wdb_ref, wob_ref,
                         kwt_ref, vwt_ref, ct_ref, nt_ref, mt_ref, bias_ref):
    step = pl.program_id(0)

    @pl.when(step == 0)
    def _():
        kw_ref[...] = jnp.zeros_like(kw_ref)
        vw_ref[...] = jnp.zeros_like(vw_ref)
        c_ref[...] = jnp.zeros_like(c_ref)
        m_ref[...] = jnp.zeros_like(m_ref)
        for j in range(GROUP):
            lo = win_ref[j * HEAD_DIM:(j + 1) * HEAD_DIM, :]
            hi = win_ref[(j + GROUP) * HEAD_DIM:(j + GROUP + 1) * HEAD_DIM, :]
            w_ref[:, j * LANES:(j + 1) * LANES] = jnp.concatenate([lo, hi], axis=0).T.astype(BF16)
        for c0 in range(ATTN_WIDTH, IN_MAIN, M_WIDTH):
            c1 = min(c0 + M_WIDTH, IN_MAIN)
            w_ref[:, c0:c1] = win_ref[c0:c1, :].T.astype(BF16)
        gw = jnp.concatenate([win_ref[IN_MAIN:IN_MAIN + 2 * M_HEADS, :],
                              jnp.zeros((GATE_PAD - 2 * M_HEADS, D_MODEL), F32)], axis=0)
        w_ref[:, IN_MAIN:] = gw.T.astype(BF16)

    wgb_ref[...] = wg_ref[...].astype(BF16)
    wub_ref[...] = wu_ref[...].astype(BF16)
    wdb_ref[...] = wd_ref[...].astype(BF16)
    wob_ref[...] = wo_ref[...].astype(BF16)

    tm = x_ref.shape[0]
    ln = WINDOW
    groups = [slice(r0, r0 + tm // ROW_GROUPS) for r0 in range(0, tm, tm // ROW_GROUPS)]
    blocks = [slice(r0, r0 + ln) for r0 in range(0, tm, ln)]
    hcols = [slice(h * M_HEAD_DIM, (h + 1) * M_HEAD_DIM) for h in range(M_HEADS)]
    pcols = [slice(2 * p * M_HEAD_DIM, 2 * (p + 1) * M_HEAD_DIM) for p in range(M_PAIRS)]
    nt = (((1,), (1,)), ((), ()))
    hs = [_rms(x_ref[rs, :], g_ref[...]).astype(BF16) for rs in groups]

    def proj(off, n):
        return jnp.concatenate([jnp.dot(h, w_ref[:, off:off + n], preferred_element_type=F32) for h in hs], axis=0)

    qt = proj(OFF_QM, M_WIDTH).astype(BF16).T
    gates = proj(IN_MAIN, GATE_PAD).T
    km = (proj(OFF_KM, M_WIDTH) * (M_HEAD_DIM ** -0.5)).astype(BF16)
    r = lax.broadcasted_iota(jnp.int32, (ln, ln), 0)
    c = lax.broadcasted_iota(jnp.int32, (ln, ln), 1)
    causal_t = r <= c
    upper = causal_t.astype(F32)
    lane8 = lax.broadcasted_iota(jnp.int32, (SUBLANES, ln), 1)
    ones_rows = jnp.ones((AUG_ROWS - M_HEAD_DIM, ln), BF16)
    zrows = jnp.zeros((ln - SUBLANES, ln), F32)
    row8 = lax.broadcasted_iota(jnp.int32, (SUBLANES, ln), 0)
    bias = jnp.zeros((SUBLANES, ln), F32)
    for h in range(M_HEADS):
        bias = jnp.where(row8 == h, bi_ref[0, h], jnp.where(row8 == M_HEADS + h, bf_ref[0, h], bias))
    gis = [gates[0:SUBLANES, rows] + bias for rows in blocks]
    prefix = jnp.dot(jnp.concatenate([jax.nn.log_sigmoid(gi) for gi in gis], axis=0), upper, precision=HIGHEST,
                     preferred_element_type=F32)
    pre = []
    for blk, gi in enumerate(gis):
        b = pltpu.roll(prefix[blk * SUBLANES:(blk + 1) * SUBLANES, :], M_HEADS, axis=0)
        g = gi - b
        cm0 = g
        sh = 1
        while sh < ln:
            cm0 = jnp.maximum(cm0, jnp.where(lane8 >= sh, pltpu.roll(cm0, sh, axis=1), -jnp.inf))
            sh *= 2
        b_last = jnp.broadcast_to(b[:, ln - 1:ln], b.shape)
        g_max = jnp.broadcast_to(cm0[:, ln - 1:ln], b.shape)
        g_cols = jnp.concatenate([g, zrows], axis=0).T
        pre.append((b, g_cols, cm0, b_last, g_max, (b_last - b) + gi))
    scores_m = [[jnp.dot(km[rows, pc], _block_diag(qt[hcols[2 * p], rows], qt[hcols[2 * p + 1], rows]),
                         preferred_element_type=F32) for p, pc in enumerate(pcols)] for rows in blocks]
    vt = proj(OFF_VM, M_WIDTH).astype(BF16).T

    qa = (proj(OFF_QA, ATTN_WIDTH) * (HEAD_DIM ** -0.5)).astype(BF16)
    kv = proj(OFF_KA, 2 * KV_WIDTH)
    ka, va = kv[:, :KV_WIDTH], kv[:, KV_WIDTH:]
    nrow = ATTN_HEADS * ln
    ra = lax.broadcasted_iota(jnp.int32, (nrow, ln), 0) & (ln - 1)
    ca = lax.broadcasted_iota(jnp.int32, (nrow, ln), 1)
    own = ca <= ra
    sink = _sink_rows(sink_ref, ln)
    no_prev = jnp.where(step == 0, -jnp.inf, 0.0)
    ones = jnp.ones((2 * ln, LANES), BF16)
    kcats, v_augs = [], []
    for blk, rows in enumerate(blocks):
        if blk == 0:
            k_prev, v_prev = kw_ref[...], vw_ref[...]
        else:
            k_prev, v_prev = ka[blocks[blk - 1], :], va[blocks[blk - 1], :]
        kcats.append(jnp.concatenate([ka[rows, :], k_prev], axis=0).astype(BF16))
        vcat = jnp.concatenate([va[rows, :], v_prev], axis=0).astype(BF16)
        v_augs.append(jnp.concatenate([vcat, ones], axis=1))
    kw_ref[...] = ka[blocks[-1], :]
    vw_ref[...] = va[blocks[-1], :]

    m_prev = m_ref[...]
    scal = []
    for b, _, cm0, b_last, g_max, w_end_arg in pre:
        cm = jnp.maximum(cm0, m_prev)
        m_end = b_last + jnp.maximum(g_max, m_prev)
        scal.append((cm, jnp.exp(m_prev - cm), jnp.exp(-(b + cm)), jnp.exp(b_last + m_prev - m_end),
                     jnp.exp(w_end_arg - m_end)))
        m_prev = m_end
    m_ref[...] = m_prev
    gated = []
    for rows, s_t, (_, g_cols, *_), (cm, _, _, _, w_end) in zip(blocks, scores_m, pre, scal):
        vts = [jnp.concatenate([vt[hc, rows], ones_rows], axis=0) for hc in hcols]
        sqks = [(jnp.exp(jnp.where(causal_t, g_cols[:, h:h + 1] - cm[h:h + 1, :], -jnp.inf))
                 * s_t[h // 2][:, (h % 2) * ln:(h % 2 + 1) * ln]).astype(BF16) for h in range(M_HEADS)]
        kvws = [(vts[h].astype(F32) * w_end[h:h + 1, :]).astype(BF16) for h in range(M_HEADS)]
        gated.append((vts, sqks, kvws))
    pairs = range(M_PAIRS)
    upds = [[jnp.dot(jnp.concatenate([kvws[2 * p], kvws[2 * p + 1]], axis=1),
                     _block_diag(km[rows, hcols[2 * p]], km[rows, hcols[2 * p + 1]]),
                     preferred_element_type=F32) for p in pairs] for rows, (_, _, kvws) in zip(blocks, gated)]
    intras = [[jnp.dot(jnp.concatenate([vts[2 * p], vts[2 * p + 1]], axis=1),
                       _block_diag(sqks[2 * p], sqks[2 * p + 1]), preferred_element_type=F32) for p in pairs]
              for vts, sqks, _ in gated]

    scores_a = [lax.dot_general(_stack_heads([qa[rows, j * LANES:(j + 1) * LANES] for j in range(GROUP)]), kcat, nt,
                                preferred_element_type=F32) for rows, kcat in zip(blocks, kcats)]
    probs, maxes = [], []
    for blk, s in enumerate(scores_a):
        s_prev = s[:, ln:]
        if blk == 0:
            s_prev = s_prev + no_prev
        sc = jnp.where(own, s[:, :ln], s_prev)
        mx = jnp.maximum(jnp.max(sc, axis=-1, keepdims=True), sink)
        p = jnp.exp(sc - mx)
        zero = jnp.zeros_like(p)
        probs.append(jnp.concatenate([jnp.where(own, p, zero), jnp.where(own, zero, p)], axis=1).astype(BF16))
        maxes.append(mx)
    om = proj(OFF_OM, M_WIDTH)

    mem = [jnp.concatenate([c_ref[2 * p], c_ref[2 * p + 1]], axis=1) for p in range(M_PAIRS)]
    pair_row = lambda x, p: jnp.concatenate([x[2 * p:2 * p + 1, :], x[2 * p + 1:2 * p + 2, :]], axis=1)
    for rows, upd, intra, (_, w_inter, e_negm, dec, _), p2, v_aug, mx in zip(
            blocks, upds, intras, scal, probs, v_augs, maxes):
        inter = [jnp.dot(mem[p].astype(BF16), _block_diag(qt[hcols[2 * p], rows], qt[hcols[2 * p + 1], rows]),
                         preferred_element_type=F32) for p in pairs]
        mem = [pair_row(dec, p) * mem[p] + upd[p] for p in pairs]
        o = jnp.dot(p2, v_aug, preferred_element_type=F32)
        o = o[:, :LANES] * (1.0 / (o[:, LANES:] + jnp.exp(sink - mx)))
        for j, tile in enumerate(_unstack_heads(o, ln)):
            mix_ref[rows, j * LANES:(j + 1) * LANES] = tile.astype(mix_ref.dtype)
        for p in pairs:
            num = inter[p] * pair_row(w_inter, p) + intra[p]
            den = jnp.maximum(jnp.abs(num[M_HEAD_DIM:M_HEAD_DIM + 1, :]), pair_row(e_negm, p))
            hh = num[:M_HEAD_DIM, :] * (1.0 / den)
            y = hh * lax.rsqrt(jnp.mean(hh * hh, axis=0, keepdims=True) + EPS)
            for h in (2 * p, 2 * p + 1):
                y_h = y[:, (h % 2) * ln:(h % 2 + 1) * ln]
                mcols = slice(ATTN_WIDTH + h * M_HEAD_DIM, ATTN_WIDTH + (h + 1) * M_HEAD_DIM)
                mix_ref[rows, mcols] = (jax.nn.sigmoid(om[rows, hcols[h]])
                                        * (y_h.T * mnorm_ref[:, hcols[h]])).astype(mix_ref.dtype)
    for p in range(M_PAIRS):
        c_ref[2 * p] = mem[p][:, :M_HEAD_DIM]
        c_ref[2 * p + 1] = mem[p][:, M_HEAD_DIM:]

    @pl.when(step == pl.num_programs(0) - 1)
    def _():
        kwt_ref[...] = kw_ref[...].T
        vwt_ref[...] = vw_ref[...].T
        for h in range(M_HEADS):
            ct_ref[h] = c_ref[h][:M_HEAD_DIM, :].T
            nt_ref[h:h + 1, :] = c_ref[h][M_HEAD_DIM:M_HEAD_DIM + 1, :]
        diag = jnp.sum(jnp.where(row8 == lane8, m_ref[...], 0.0), axis=0, keepdims=True)
        mt_ref[...] = diag[:, :M_HEADS]
        lane = lax.broadcasted_iota(jnp.int32, bias_ref.shape, 1)
        gate_bias = jnp.zeros(bias_ref.shape, F32)
        for h in range(M_HEADS):
            gate_bias = jnp.where(lane == h, bi_ref[0, h], jnp.where(lane == M_HEADS + h, bf_ref[0, h], gate_bias))
        bias_ref[...] = gate_bias


def _out_row_block(i):
    per_head = (ATTN_WIDTH // ATTN_HEADS) // WEIGHT_SLAB
    j, part = i // per_head, i % per_head
    head = (j % 2) * GROUP + j // 2
    return jnp.where(i < ATTN_HEADS * per_head, head * per_head + part, i)


def _mixer_prompt(sink, b_i, b_f, x, g_pre, w_in_t, mnorm, w_gate, w_up, w_down, w_out):
    m = x.shape[0]
    tm = TOKEN_TILE
    steps = m // tm
    assert D_MODEL == steps * WEIGHT_SLAB and D_FF % (steps // 2) == 0 and (D_FF // (steps // 2)) % BF16_SUBLANES == 0
    down_slab = D_FF // (steps // 2)
    row = lambda n: pl.BlockSpec((tm, n), lambda i: (i, 0))
    whole = lambda shape: pl.BlockSpec(shape, lambda i: (0,) * len(shape))
    smem = pl.BlockSpec(memory_space=pltpu.SMEM)
    slab = lambda n: pl.BlockSpec((WEIGHT_SLAB, n), lambda i: (i, 0))
    down = pl.BlockSpec((down_slab, D_MODEL), lambda i: (i // 2, 0))
    c_shape = (M_HEADS, AUG_ROWS, M_HEAD_DIM)
    w_shape = (WINDOW, KV_WIDTH)
    wt_shape = (KV_WIDTH, WINDOW)
    ct_shape = (M_HEADS, M_HEAD_DIM, M_HEAD_DIM)
    n_shape = (M_HEADS, M_HEAD_DIM)
    b_shape = (1, GATE_PAD)
    s_shape = (SUBLANES, LANES)
    sds = jax.ShapeDtypeStruct
    return pl.pallas_call(
        _mixer_prompt_kernel,
        grid=(steps,),
        in_specs=[smem, smem, smem, row(D_MODEL), _const_spec((1, D_MODEL)), _const_spec(w_in_t.shape),
                  _const_spec((1, M_WIDTH)), slab(D_FF), slab(D_FF), down,
                  pl.BlockSpec((WEIGHT_SLAB, D_MODEL), lambda i: (_out_row_block(i), 0))],
        out_specs=[row(ATTN_WIDTH + M_WIDTH), whole(w_shape), whole(w_shape), whole(c_shape), whole(s_shape),
                   whole((D_MODEL, IN_PAD)), slab(D_FF), slab(D_FF), down, slab(D_MODEL),
                   whole(wt_shape), whole(wt_shape), whole(ct_shape), whole(n_shape),
                   whole((1, M_HEADS)), whole(b_shape)],
        out_shape=[sds((m, ATTN_WIDTH + M_WIDTH), BF16), sds(w_shape, F32), sds(w_shape, F32), sds(c_shape, F32),
                   sds(s_shape, F32), sds((D_MODEL, IN_PAD), BF16), sds(w_gate.shape, BF16), sds(w_up.shape, BF16),
                   sds(w_down.shape, BF16), sds(w_out.shape, BF16),
                   sds(wt_shape, F32), sds(wt_shape, F32), sds(ct_shape, F32), sds(n_shape, F32),
                   sds((1, M_HEADS), F32), sds(b_shape, F32)],
        compiler_params=_params(("arbitrary",)),
        name="mixer_prompt",
    )(sink, b_i, b_f, x, g_pre, w_in_t, mnorm, w_gate, w_up, w_down, w_out)


def _gates(g_blk, bias_row, cum):
    pre = g_blk + bias_row
    lane = lax.broadcasted_iota(jnp.int32, pre.shape, 1)
    a = jnp.where(lane < M_HEADS, pre, jax.nn.log_sigmoid(pre))
    b = jnp.dot(cum, a, precision=HIGHEST, preferred_element_type=F32)
    return a, b, a.T, b.T


def _col(x, j):
    return jnp.broadcast_to(x[:, j:j + 1], x.shape)


def _mlstm_gated_scores(q, k, a, b, at, bt, h, mask, m_prev, last):
    bc, ic = _col(b, M_HEADS + h), _col(a, h)
    br, ir = bt[M_HEADS + h:M_HEADS + h + 1, :], at[h:h + 1, :]
    d = jnp.where(mask, (bc - br) + ir, -jnp.inf)
    inter = bc + m_prev
    m_t = jnp.maximum(inter, jnp.max(d, axis=-1, keepdims=True))
    w_inter = jnp.exp(inter - m_t)
    sqk = jnp.exp(d - m_t) * lax.dot_general(q, k, (((1,), (1,)), ((), ())), preferred_element_type=F32)
    m_end = last(m_t)
    bl = last(bc)
    dec = jnp.exp(bl + m_prev - m_end)
    w_end = jnp.exp((bl - bc) + ic - m_end)
    kw = k.astype(F32) * w_end
    return w_inter, sqk, m_t, m_end, dec, kw


def _mlstm_sample_kernel(q_ref, k_ref, v_ref, om_ref, g_ref, m0_ref, bias_ref, mnorm_ref, c_ref, n_ref,
                         o_ref, c_out, n_out, m_out, *, dec_seq):
    t = dec_seq
    bb = SAMPLE_BATCH
    ln = bb * t
    r = lax.broadcasted_iota(jnp.int32, (ln, ln), 0)
    c = lax.broadcasted_iota(jnp.int32, (ln, ln), 1)
    shift = t.bit_length() - 1
    same = (r >> shift) == (c >> shift)
    mask = jnp.logical_and(same, c <= r)
    cum = mask.astype(F32)
    expand = (c == (r >> shift)).astype(F32)
    gather = ((c >> shift) == r).astype(F32)
    is_last = (r & (t - 1)) == t - 1
    is_first = (r & (t - 1)) == 0

    def last(x):
        y = jnp.where(is_last, x, 0.0)
        step = 1
        while step < t:
            y = y + pltpu.roll(y, ln - step, axis=0)
            step *= 2
        return y

    a, b, at, bt = _gates(g_ref[...], bias_ref[...], cum)
    m0 = m0_ref[...]
    lane = lax.broadcasted_iota(jnp.int32, (ln, LANES), 1)
    zpad = jnp.zeros((ln - bb, M_HEAD_DIM), F32)
    heads = range(M_HEADS)
    hcols = [slice(h * M_HEAD_DIM, (h + 1) * M_HEAD_DIM) for h in heads]
    qfs = [q_ref[:, hc] for hc in hcols]
    vs = [v_ref[:, hc] for hc in hcols]
    qcs = [jnp.concatenate(
        [jnp.dot(qfs[h][s * t:(s + 1) * t, :], c_ref[s * M_HEADS + h].astype(BF16).astype(F32),
                 preferred_element_type=F32) for s in range(bb)], axis=0) for h in heads]
    n_exps = [jnp.dot(expand, jnp.concatenate([n_ref[:, hc], zpad], axis=0), precision=HIGHEST,
                      preferred_element_type=F32) for hc in hcols]
    parts = [_mlstm_gated_scores(qfs[h].astype(BF16), k_ref[:, hcols[h]], a, b, at, bt, h, mask,
                                 _col(m0, h), last) for h in heads]
    intras = [jnp.dot(parts[h][1].astype(BF16), vs[h], preferred_element_type=F32) for h in heads]
    m_cols = jnp.zeros((ln, LANES), F32)
    kwts = []
    for h in heads:
        w_inter, sqk, m_t, m_end, dec, kw = parts[h]
        num = w_inter * qcs[h] + intras[h]
        nq = w_inter * jnp.sum(qfs[h] * n_exps[h], axis=-1, keepdims=True) + jnp.sum(sqk, axis=-1, keepdims=True)
        hh = num / jnp.maximum(jnp.abs(nq), jnp.exp(-m_t))
        o_ref[:, hcols[h]] = _head_out(hh, mnorm_ref[:, hcols[h]], om_ref[:, hcols[h]])
        kwts.append(kw.T.astype(BF16))
        m_cols = jnp.where(lane == h, m_end, m_cols)
    m_out[...] = m_cols
    for h in heads:
        dec = parts[h][4]
        for s in range(bb):
            lhs = jnp.where((c >> shift) == s, kwts[h], jnp.zeros_like(kwts[h]))
            upd = jnp.dot(lhs, vs[h], preferred_element_type=F32)
            c_out[s * M_HEADS + h] = dec[s * t:s * t + 1, :] * c_ref[s * M_HEADS + h] + upd
    for h in heads:
        dec, kw = parts[h][4], parts[h][5]
        n_new = jnp.dot(gather, jnp.where(is_first, dec * n_exps[h], 0.0) + kw, precision=HIGHEST,
                        preferred_element_type=F32)
        n_out[:, hcols[h]] = n_new[:bb, :]


def _mixer_sample_kernel(sink_ref, x_ref, g_ref, w_ref, ck_ref, cv_ref, m0_ref, bias_ref, mnorm_ref, c_ref, n_ref,
                         mix_ref, ko_ref, vo_ref, c_out, n_out, m_out, *proj_refs, dec_seq):
    step = pl.program_id(0)
    tm = mix_ref.shape[0]

    def project(chunk):
        rows = pl.ds(pl.multiple_of(chunk * tm, tm), tm)
        _inproj_kernel(x_ref.at[rows, :], g_ref, w_ref, *[r.at[rows, :] for r in proj_refs])

    @pl.when(step == 0)
    def _():
        project(0)

    rows = pl.ds(pl.multiple_of(step * tm, tm), tm)
    qa_ref, kn_ref, vn_ref, qm_ref, km_ref, vm_ref, om_ref, gt_ref = [r.at[rows, :] for r in proj_refs]
    _attn_sample_kernel(sink_ref, qa_ref, kn_ref, vn_ref, ck_ref, cv_ref, mix_ref.at[:, pl.ds(0, ATTN_WIDTH)],
                        ko_ref, vo_ref, dec_seq=dec_seq)
    _mlstm_sample_kernel(qm_ref, km_ref, vm_ref, om_ref, gt_ref, m0_ref, bias_ref, mnorm_ref, c_ref, n_ref,
                         mix_ref.at[:, pl.ds(ATTN_WIDTH, M_WIDTH)], c_out, n_out, m_out, dec_seq=dec_seq)
    project(lax.rem(step + 1, pl.num_programs(0)))


def _mixer_sample(sink, x, g_pre, w_pad, cache_k, cache_v, m0_rows, bias_row, mnorm, c_in, n_in, dec_seq):
    m = x.shape[0]
    bb = SAMPLE_BATCH
    tm = bb * dec_seq
    nb = m // dec_seq
    row = lambda n: pl.BlockSpec((tm, n), lambda i: (i, 0))
    whole = lambda shape: pl.BlockSpec(shape, lambda i: (0,) * len(shape))
    cache = pl.BlockSpec((bb, KV_HEADS, HEAD_DIM, WINDOW), lambda i: (i, 0, 0, 0))
    c_spec = pl.BlockSpec((bb * M_HEADS, M_HEAD_DIM, M_HEAD_DIM), lambda i: (i, 0, 0))
    n_spec = pl.BlockSpec((bb, M_WIDTH), lambda i: (i, 0))
    sds = jax.ShapeDtypeStruct
    return pl.pallas_call(
        functools.partial(_mixer_sample_kernel, dec_seq=dec_seq),
        grid=(nb // bb,),
        in_specs=[pl.BlockSpec(memory_space=pltpu.SMEM), _const_spec(x.shape), _const_spec((1, D_MODEL)),
                  _const_spec(w_pad.shape), cache, cache, row(LANES),
                  whole((1, GATE_PAD)), whole((1, M_WIDTH)), c_spec, n_spec],
        out_specs=[row(ATTN_WIDTH + M_WIDTH), cache, cache, c_spec, n_spec, row(LANES)],
        out_shape=[sds((m, ATTN_WIDTH + M_WIDTH), BF16), sds(cache_k.shape, F32), sds(cache_v.shape, F32),
                   sds(c_in.shape, F32), sds(n_in.shape, F32), sds((m, LANES), F32)],
        scratch_shapes=[pltpu.VMEM((m, n), dt) for n, dt in zip(PROJ_WIDTHS, PROJ_DTYPES)],
        compiler_params=_params(("arbitrary",)),
        name="mixer_sample",
    )(sink, x, g_pre, w_pad, cache_k, cache_v, m0_rows, bias_row, mnorm, c_in, n_in)


def _out_ffn_kernel(x_ref, mix_ref, wo_ref, g1_ref, g2_ref, wg_ref, wu_ref, wd_ref, g3_ref, o_ref):
    assert sum(FFN_GROUP_ROWS) == x_ref.shape[0]
    starts = [sum(FFN_GROUP_ROWS[:i]) for i in range(len(FFN_GROUP_ROWS))]
    groups = [slice(r, r + n) for r, n in zip(starts, FFN_GROUP_ROWS)]
    ys = [jnp.dot(mix_ref[rs, :], wo_ref[...], preferred_element_type=F32) for rs in groups]
    x1s = [x_ref[rs, :] + _rms(y, g1_ref[...]) for rs, y in zip(groups, ys)]
    fs = [_rms(x1, g2_ref[...]).astype(BF16) for x1 in x1s]
    accs = [None] * len(groups)
    acts = {}
    chunks = D_FF // FFN_CHUNK
    for rnd in range(chunks + max(FFN_GROUP_LAG) + 1):
        for i, f in enumerate(fs):
            c = rnd - FFN_GROUP_LAG[i]
            if 0 <= c < chunks:
                cols = slice(c * FFN_CHUNK, (c + 1) * FFN_CHUNK)
                g = jnp.dot(f, wg_ref[:, cols], preferred_element_type=F32)
                u = jnp.dot(f, wu_ref[:, cols], preferred_element_type=F32)
                acts[i, c] = (g * jax.nn.sigmoid(g) * u).astype(BF16)
        for i in range(len(groups)):
            c = rnd - FFN_GROUP_LAG[i] - 1
            if 0 <= c < chunks:
                part = jnp.dot(acts.pop((i, c)), wd_ref[c * FFN_CHUNK:(c + 1) * FFN_CHUNK, :],
                               preferred_element_type=F32)
                accs[i] = part if accs[i] is None else accs[i] + part
    for rs, x1, acc in zip(groups, x1s, accs):
        o_ref[rs, :] = x1 + _rms(acc, g3_ref[...])


def _out_ffn(x, mix, w_out, g_post_mix, g_pre_ffn, w_gate, w_up, w_down, g_post_ffn):
    m = x.shape[0]
    tm = TOKEN_TILE
    row = lambda n: pl.BlockSpec((tm, n), lambda i: (i, 0))
    vec = _const_spec((1, D_MODEL))
    return pl.pallas_call(
        _out_ffn_kernel,
        grid=(m // tm,),
        in_specs=[row(D_MODEL), row(ATTN_WIDTH + M_WIDTH), _const_spec((D_MODEL, D_MODEL)), vec, vec,
                  _const_spec((D_MODEL, D_FF)), _const_spec((D_MODEL, D_FF)), _const_spec((D_FF, D_MODEL)), vec],
        out_specs=row(D_MODEL),
        out_shape=jax.ShapeDtypeStruct((m, D_MODEL), F32),
        compiler_params=_params(("parallel",)),
        name="out_ffn",
    )(x, mix, w_out, g_post_mix, g_pre_ffn, w_gate, w_up, w_down, g_post_ffn)


def _layer(xp, xs, cache_k, cache_v, state_c, state_n, state_m, w_in, b_i, b_f, attn_sink, m_norm, w_out,
           g_pre_mix, g_post_mix, g_pre_ffn, g_post_ffn, w_gate, w_up, w_down):
    bp, sp, _ = xp.shape
    bs, ts, _ = xs.shape
    assert bp == 1 and sp % TOKEN_TILE == 0 and TOKEN_TILE % WINDOW == 0
    assert ts & (ts - 1) == 0 and (bs * ts) % TOKEN_TILE == 0 and bs % SAMPLE_BATCH == 0

    row = lambda v: v.reshape(1, -1)
    sink = row(attn_sink)

    x2 = xp.reshape(sp, D_MODEL)
    mix, _, _, _, _, w_pad, wg, wu, wd, wo, k_wt, v_wt, c_p, n_p, m_p, bias_row = _mixer_prompt(
        sink, row(b_i), row(b_f), x2, row(g_pre_mix), w_in.T, row(m_norm), w_gate, w_up, w_down, w_out)
    ffn = (wo, row(g_post_mix), row(g_pre_ffn), wg, wu, wd, row(g_post_ffn))
    yp = _out_ffn(x2, mix, *ffn).reshape(xp.shape)
    window_major = lambda c: jnp.transpose(c, (0, 3, 1, 2))
    k_p = window_major(k_wt.reshape(1, KV_HEADS, HEAD_DIM, WINDOW))
    v_p = window_major(v_wt.reshape(1, KV_HEADS, HEAD_DIM, WINDOW))
    c_p = c_p.reshape(1, M_HEADS, M_HEAD_DIM, M_HEAD_DIM)
    n_p = n_p.reshape(1, M_HEADS, M_HEAD_DIM)

    x2 = xs.reshape(bs * ts, D_MODEL)
    feature_major = lambda c: jnp.transpose(c, (0, 2, 3, 1))
    m0_rows = jnp.pad(jnp.repeat(state_m, ts, axis=0), ((0, 0), (0, LANES - M_HEADS)))
    mix, k_s, v_s, c_s, n_s, m_rows = _mixer_sample(
        sink, x2, row(g_pre_mix), w_pad, feature_major(cache_k), feature_major(cache_v), m0_rows, bias_row,
        row(m_norm), state_c.reshape(bs * M_HEADS, M_HEAD_DIM, M_HEAD_DIM), state_n.reshape(bs, M_WIDTH), ts)
    ys = _out_ffn(x2, mix, *ffn).reshape(xs.shape)
    k_s, v_s = window_major(k_s), window_major(v_s)
    c_s = c_s.reshape(bs, M_HEADS, M_HEAD_DIM, M_HEAD_DIM)
    n_s = n_s.reshape(bs, M_HEADS, M_HEAD_DIM)
    m_s = m_rows[ts - 1::ts, :M_HEADS]
    return yp, ys, (k_p, v_p, c_p, n_p, m_p), (k_s, v_s, c_s, n_s, m_s)


def kernel(x_prompt, x_sample, cache_k, cache_v, state_C, state_n, state_m, w_in, b_i, b_f, attn_sink, m_norm,
           w_out, g_pre_mix, g_post_mix, g_pre_ffn, g_post_ffn, w_gate, w_up, w_down):
    depth = w_in.shape[0]
    xp, xs = x_prompt, x_sample
    prompt_states, sample_states = [], []
    for l in range(depth):
        xp, xs, st_p, st_s = _layer(xp, xs, cache_k[l], cache_v[l], state_C[l], state_n[l], state_m[l],
                                    w_in[l], b_i[l], b_f[l], attn_sink[l], m_norm[l], w_out[l],
                                    g_pre_mix[l], g_post_mix[l], g_pre_ffn[l], g_post_ffn[l],
                                    w_gate[l], w_up[l], w_down[l])
        prompt_states.append(st_p)
        sample_states.append(st_s)
    stack = lambda states, i: jnp.stack([s[i] for s in states], axis=0)
    return (xp, xs) + tuple(stack(prompt_states, i) for i in range(5)) + tuple(stack(sample_states, i) for i in range(5))
```

```python
import functools

import jax
import jax.numpy as jnp
from jax import lax
from jax.experimental import pallas as pl
from jax.experimental.pallas import tpu as pltpu

F32 = jnp.float32
BF16 = jnp.bfloat16
HIGHEST = lax.Precision.HIGHEST

D_MODEL = 1024
HEAD_DIM = 64
ATTN_HEADS = 8
KV_HEADS = 2
GROUP = ATTN_HEADS // KV_HEADS
ATTN_WIDTH = ATTN_HEADS * HEAD_DIM
KV_WIDTH = KV_HEADS * HEAD_DIM
WINDOW = 128
M_HEADS = 4
M_HEAD_DIM = 128
M_WIDTH = M_HEADS * M_HEAD_DIM
M_PAIRS = M_HEADS // 2
D_FF = 2816
EPS = 1e-6

LANES = 128
SUBLANES = 8
GATE_PAD = LANES
BF16_SUBLANES = 16
AUG_ROWS = M_HEAD_DIM + BF16_SUBLANES
IN_MAIN = ATTN_WIDTH + 2 * KV_WIDTH + 4 * M_WIDTH
IN_PAD = IN_MAIN + GATE_PAD
VMEM_LIMIT = 56 * 1024 * 1024

HEAD_ORDER = tuple(h for j in range(GROUP) for h in (j, j + GROUP))

TOKEN_TILE = 512
FFN_CHUNK = 256
ROW_GROUPS = 2
FFN_GROUP_ROWS = (256, 256)
FFN_GROUP_LAG = (0, 1)
WEIGHT_SLAB = 32
SAMPLE_BATCH = 16


def _rms(x, g):
    return x * lax.rsqrt(jnp.mean(x * x, axis=-1, keepdims=True) + EPS) * g


def _const_spec(shape):
    nd = len(shape)
    return pl.BlockSpec(shape, lambda i: (0,) * nd, pipeline_mode=pl.Buffered(1))


def _params(semantics):
    return pltpu.CompilerParams(dimension_semantics=semantics, vmem_limit_bytes=VMEM_LIMIT)


OFF_QA, OFF_KA, OFF_VA = 0, ATTN_WIDTH, ATTN_WIDTH + KV_WIDTH
OFF_QM = ATTN_WIDTH + 2 * KV_WIDTH
OFF_KM, OFF_VM, OFF_OM = OFF_QM + M_WIDTH, OFF_QM + 2 * M_WIDTH, OFF_QM + 3 * M_WIDTH


def _inproj_kernel(x_ref, g_ref, w_ref, qa_ref, ka_ref, va_ref, qm_ref, km_ref, vm_ref, om_ref, gt_ref):
    tm = x_ref.shape[0]
    step = tm // ROW_GROUPS
    for r0 in range(0, tm, step):
        rs = slice(r0, r0 + step)
        h = _rms(x_ref[rs, :], g_ref[...]).astype(BF16)

        def proj(off, n):
            return jnp.dot(h, w_ref[:, off:off + n], preferred_element_type=F32)

        qa_ref[rs, :] = proj(OFF_QA, ATTN_WIDTH) * (HEAD_DIM ** -0.5)
        ka_ref[rs, :] = proj(OFF_KA, KV_WIDTH)
        va_ref[rs, :] = proj(OFF_VA, KV_WIDTH)
        qm_ref[rs, :] = proj(OFF_QM, M_WIDTH)
        km_ref[rs, :] = (proj(OFF_KM, M_WIDTH) * (M_HEAD_DIM ** -0.5)).astype(km_ref.dtype)
        vm_ref[rs, :] = proj(OFF_VM, M_WIDTH).astype(vm_ref.dtype)
        om_ref[rs, :] = proj(OFF_OM, M_WIDTH)
        gt_ref[rs, :] = proj(IN_MAIN, GATE_PAD)


PROJ_WIDTHS = (ATTN_WIDTH, KV_WIDTH, KV_WIDTH, M_WIDTH, M_WIDTH, M_WIDTH, M_WIDTH, GATE_PAD)
PROJ_DTYPES = (F32, F32, F32, F32, BF16, BF16, F32, F32)


def _stack_heads(q_tiles):
    lane = lax.broadcasted_iota(jnp.int32, q_tiles[0].shape, 1)
    lo = lane < HEAD_DIM
    zero = jnp.zeros_like(q_tiles[0])
    parts = []
    for qt in q_tiles:
        parts += [jnp.where(lo, qt, zero), jnp.where(lo, zero, qt)]
    return jnp.concatenate(parts, axis=0)


def _unstack_heads(o, rows):
    lane = lax.broadcasted_iota(jnp.int32, (rows, LANES), 1)
    lo = lane < HEAD_DIM
    return [jnp.where(lo, o[(2 * j) * rows:(2 * j + 1) * rows, :], o[(2 * j + 1) * rows:(2 * j + 2) * rows, :])
            for j in range(GROUP)]


def _sink_rows(sink_ref, rows):
    return jnp.concatenate([jnp.full((rows, LANES), sink_ref[0, h], F32) for h in HEAD_ORDER], axis=0)


def _attn_sample_kernel(sink_ref, q_ref, kn_ref, vn_ref, ck_ref, cv_ref, o_ref, ko_ref, vo_ref, *, dec_seq):
    t = dec_seq
    bb = SAMPLE_BATCH
    nrow = ATTN_HEADS * t
    ts, ns = t.bit_length() - 1, nrow.bit_length() - 1
    r_c = lax.broadcasted_iota(jnp.int32, (nrow, WINDOW), 0) & (t - 1)
    c_c = lax.broadcasted_iota(jnp.int32, (nrow, WINDOW), 1)
    vis_cache = c_c > r_c
    r_n = lax.broadcasted_iota(jnp.int32, (bb * nrow, bb * t), 0)
    c_n = lax.broadcasted_iota(jnp.int32, (bb * nrow, bb * t), 1)
    vis_new = jnp.logical_and((r_n >> ns) == (c_n >> ts), (c_n & (t - 1)) <= (r_n & (t - 1)))
    sink = _sink_rows(sink_ref, t)
    nt = (((1,), (1,)), ((), ()))
    kn_all, vn_all = kn_ref[...], vn_ref[...]
    lane_w = lax.broadcasted_iota(jnp.int32, (KV_WIDTH, WINDOW), 1)
    is_new = lane_w >= WINDOW - t
    zero_rows = jnp.zeros((WINDOW - t, KV_WIDTH), F32)
    state_shape = (KV_HEADS, HEAD_DIM, WINDOW)

    def slide(old_t, new_rows):
        new_t = jnp.concatenate([zero_rows, new_rows], axis=0).T
        return jnp.where(is_new, new_t, pltpu.roll(old_t, WINDOW - t, axis=1)).reshape(state_shape)

    qs, s_c, cvs = [], [], []
    for b in range(bb):
        rows = slice(b * t, (b + 1) * t)
        ck = ck_ref[b].reshape(KV_WIDTH, WINDOW)
        cvs.append(cv_ref[b].reshape(KV_WIDTH, WINDOW))
        ko_ref[b] = slide(ck, kn_all[rows, :])
        vo_ref[b] = slide(cvs[b], vn_all[rows, :])
        qs.append(_stack_heads([q_ref[rows, j * LANES:(j + 1) * LANES] for j in range(GROUP)]).astype(BF16))
        s_c.append(jnp.where(vis_cache, jnp.dot(qs[b], ck.astype(BF16), preferred_element_type=F32), -jnp.inf))
    s_n = jnp.where(vis_new, lax.dot_general(jnp.concatenate(qs, axis=0), kn_all.astype(BF16), nt,
                                             preferred_element_type=F32), -jnp.inf)
    p_c, p_n, rden = [], [], []
    for b in range(bb):
        s_nb = s_n[b * nrow:(b + 1) * nrow, :]
        m = jnp.maximum(jnp.maximum(jnp.max(s_c[b], axis=-1, keepdims=True), jnp.max(s_nb, axis=-1, keepdims=True)),
                        sink)
        p_c.append(jnp.exp(s_c[b] - m))
        p_n.append(jnp.exp(s_nb - m[:, :bb * t]))
        rden.append(1.0 / (jnp.sum(p_c[b], axis=-1, keepdims=True) + jnp.sum(p_n[b], axis=-1, keepdims=True)
                           + jnp.exp(sink - m)))
    o_n = jnp.dot(jnp.concatenate(p_n, axis=0).astype(BF16), vn_all.astype(BF16), preferred_element_type=F32)
    outs = [[] for _ in range(GROUP)]
    for b in range(bb):
        o = (lax.dot_general(p_c[b].astype(BF16), cvs[b].astype(BF16), nt, preferred_element_type=F32)
             + o_n[b * nrow:(b + 1) * nrow, :]) * rden[b]
        for j, tile in enumerate(_unstack_heads(o, t)):
            outs[j].append(tile)
    for j, parts in enumerate(outs):
        o_ref[:, j * LANES:(j + 1) * LANES] = jnp.concatenate(parts, axis=0).astype(o_ref.dtype)


def _block_diag(a, b):
    za, zb = jnp.zeros_like(a), jnp.zeros_like(b)
    return jnp.concatenate([jnp.concatenate([a, zb], axis=1), jnp.concatenate([za, b], axis=1)], axis=0)


def _head_out(hh, mnorm_row, om):
    y = hh * lax.rsqrt(jnp.mean(hh * hh, axis=-1, keepdims=True) + EPS) * mnorm_row
    return (jax.nn.sigmoid(om) * y).astype(BF16)


def _mixer_prompt_kernel(sink_ref, bi_ref, bf_ref, x_ref, g_ref, win_ref, mnorm_ref, wg_ref, wu_ref, wd_ref, wo_ref,
                         mix_ref, kw_ref, vw_ref, c_ref, m_ref, w_ref, wgb_ref, wub_ref, wdb_ref, wob_ref,
                         kwt_ref, vwt_ref, ct_ref, nt_ref, mt_ref, bias_ref):
    step = pl.program_id(0)

    @pl.when(step == 0)
    def _():
        kw_ref[...] = jnp.zeros_like(kw_ref)
        vw_ref[...] = jnp.zeros_like(vw_ref)
        c_ref[...] = jnp.zeros_like(c_ref)
        m_ref[...] = jnp.zeros_like(m_ref)
        for j in range(GROUP):
            lo = win_ref[j * HEAD_DIM:(j + 1) * HEAD_DIM, :]
            hi = win_ref[(j + GROUP) * HEAD_DIM:(j + GROUP + 1) * HEAD_DIM, :]
            w_ref[:, j * LANES:(j + 1) * LANES] = jnp.concatenate([lo, hi], axis=0).T.astype(BF16)
        for c0 in range(ATTN_WIDTH, IN_MAIN, M_WIDTH):
            c1 = min(c0 + M_WIDTH, IN_MAIN)
            w_ref[:, c0:c1] = win_ref[c0:c1, :].T.astype(BF16)
        gw = jnp.concatenate([win_ref[IN_MAIN:IN_MAIN + 2 * M_HEADS, :],
                              jnp.zeros((GATE_PAD - 2 * M_HEADS, D_MODEL), F32)], axis=0)
        w_ref[:, IN_MAIN:] = gw.T.astype(BF16)

    wgb_ref[...] = wg_ref[...].astype(BF16)
    wub_ref[...] = wu_ref[...].astype(BF16)
    wdb_ref[...] = wd_ref[...].astype(BF16)
    wob_ref[...] = wo_ref[...].astype(BF16)

    tm = x_ref.shape[0]
    ln = WINDOW
    groups = [slice(r0, r0 + tm // ROW_GROUPS) for r0 in range(0, tm, tm // ROW_GROUPS)]
    blocks = [slice(r0, r0 + ln) for r0 in range(0, tm, ln)]
    hcols = [slice(h * M_HEAD_DIM, (h + 1) * M_HEAD_DIM) for h in range(M_HEADS)]
    pcols = [slice(2 * p * M_HEAD_DIM, 2 * (p + 1) * M_HEAD_DIM) for p in range(M_PAIRS)]
    nt = (((1,), (1,)), ((), ()))
    hs = [_rms(x_ref[rs, :], g_ref[...]).astype(BF16) for rs in groups]

    def proj(off, n):
        return jnp.concatenate([jnp.dot(h, w_ref[:, off:off + n], preferred_element_type=F32) for h in hs], axis=0)

    qt = proj(OFF_QM, M_WIDTH).astype(BF16).T
    gates = proj(IN_MAIN, GATE_PAD).T
    km = (proj(OFF_KM, M_WIDTH) * (M_HEAD_DIM ** -0.5)).astype(BF16)
    r = lax.broadcasted_iota(jnp.int32, (ln, ln), 0)
    c = lax.broadcasted_iota(jnp.int32, (ln, ln), 1)
    causal_t = r <= c
    upper = causal_t.astype(F32)
    lane8 = lax.broadcasted_iota(jnp.int32, (SUBLANES, ln), 1)
    ones_rows = jnp.ones((AUG_ROWS - M_HEAD_DIM, ln), BF16)
    zrows = jnp.zeros((ln - SUBLANES, ln), F32)
    row8 = lax.broadcasted_iota(jnp.int32, (SUBLANES, ln), 0)
    bias = jnp.zeros((SUBLANES, ln), F32)
    for h in range(M_HEADS):
        bias = jnp.where(row8 == h, bi_ref[0, h], jnp.where(row8 == M_HEADS + h, bf_ref[0, h], bias))
    gis = [gates[0:SUBLANES, rows] + bias for rows in blocks]
    prefix = jnp.dot(jnp.concatenate([jax.nn.log_sigmoid(gi) for gi in gis], axis=0), upper, precision=HIGHEST,
                     preferred_element_type=F32)
    pre = []
    for blk, gi in enumerate(gis):
        b = pltpu.roll(prefix[blk * SUBLANES:(blk + 1) * SUBLANES, :], M_HEADS, axis=0)
        g = gi - b
        cm0 = g
        sh = 1
        while sh < ln:
            cm0 = jnp.maximum(cm0, jnp.where(lane8 >= sh, pltpu.roll(cm0, sh, axis=1), -jnp.inf))
            sh *= 2
        b_last = jnp.broadcast_to(b[:, ln - 1:ln], b.shape)
        g_max = jnp.broadcast_to(cm0[:, ln - 1:ln], b.shape)
        g_cols = jnp.concatenate([g, zrows], axis=0).T
        pre.append((b, g_cols, cm0, b_last, g_max, (b_last - b) + gi))
    scores_m = [[jnp.dot(km[rows, pc], _block_diag(qt[hcols[2 * p], rows], qt[hcols[2 * p + 1], rows]),
                         preferred_element_type=F32) for p, pc in enumerate(pcols)] for rows in blocks]
    vt = proj(OFF_VM, M_WIDTH).astype(BF16).T

    qa = (proj(OFF_QA, ATTN_WIDTH) * (HEAD_DIM ** -0.5)).astype(BF16)
    kv = proj(OFF_KA, 2 * KV_WIDTH)
    ka, va = kv[:, :KV_WIDTH], kv[:, KV_WIDTH:]
    nrow = ATTN_HEADS * ln
    ra = lax.broadcasted_iota(jnp.int32, (nrow, ln), 0) & (ln - 1)
    ca = lax.broadcasted_iota(jnp.int32, (nrow, ln), 1)
    own = ca <= ra
    sink = _sink_rows(sink_ref, ln)
    no_prev = jnp.where(step == 0, -jnp.inf, 0.0)
    ones = jnp.ones((2 * ln, LANES), BF16)
    kcats, v_augs = [], []
    for blk, rows in enumerate(blocks):
        if blk == 0:
            k_prev, v_prev = kw_ref[...], vw_ref[...]
        else:
            k_prev, v_prev = ka[blocks[blk - 1], :], va[blocks[blk - 1], :]
        kcats.append(jnp.concatenate([ka[rows, :], k_prev], axis=0).astype(BF16))
        vcat = jnp.concatenate([va[rows, :], v_prev], axis=0).astype(BF16)
        v_augs.append(jnp.concatenate([vcat, ones], axis=1))
    kw_ref[...] = ka[blocks[-1], :]
    vw_ref[...] = va[blocks[-1], :]

    m_prev = m_ref[...]
    scal = []
    for b, _, cm0, b_last, g_max, w_end_arg in pre:
        cm = jnp.maximum(cm0, m_prev)
        m_end = b_last + jnp.maximum(g_max, m_prev)
        scal.append((cm, jnp.exp(m_prev - cm), jnp.exp(-(b + cm)), jnp.exp(b_last + m_prev - m_end),
                     jnp.exp(w_end_arg - m_end)))
        m_prev = m_end
    m_ref[...] = m_prev
    gated = []
    for rows, s_t, (_, g_cols, *_), (cm, _, _, _, w_end) in zip(blocks, scores_m, pre, scal):
        vts = [jnp.concatenate([vt[hc, rows], ones_rows], axis=0) for hc in hcols]
        sqks = [(jnp.exp(jnp.where(causal_t, g_cols[:, h:h + 1] - cm[h:h + 1, :], -jnp.inf))
                 * s_t[h // 2][:, (h % 2) * ln:(h % 2 + 1) * ln]).astype(BF16) for h in range(M_HEADS)]
        kvws = [(vts[h].astype(F32) * w_end[h:h + 1, :]).astype(BF16) for h in range(M_HEADS)]
        gated.append((vts, sqks, kvws))
    pairs = range(M_PAIRS)
    upds = [[jnp.dot(jnp.concatenate([kvws[2 * p], kvws[2 * p + 1]], axis=1),
                     _block_diag(km[rows, hcols[2 * p]], km[rows, hcols[2 * p + 1]]),
                     preferred_element_type=F32) for p in pairs] for rows, (_, _, kvws) in zip(blocks, gated)]
    intras = [[jnp.dot(jnp.concatenate([vts[2 * p], vts[2 * p + 1]], axis=1),
                       _block_diag(sqks[2 * p], sqks[2 * p + 1]), preferred_element_type=F32) for p in pairs]
              for vts, sqks, _ in gated]

    scores_a = [lax.dot_general(_stack_heads([qa[rows, j * LANES:(j + 1) * LANES] for j in range(GROUP)]), kcat, nt,
                                preferred_element_type=F32) for rows, kcat in zip(blocks, kcats)]
    probs, maxes = [], []
    for blk, s in enumerate(scores_a):
        s_prev = s[:, ln:]
        if blk == 0:
            s_prev = s_prev + no_prev
        sc = jnp.where(own, s[:, :ln], s_prev)
        mx = jnp.maximum(jnp.max(sc, axis=-1, keepdims=True), sink)
        p = jnp.exp(sc - mx)
        zero = jnp.zeros_like(p)
        probs.append(jnp.concatenate([jnp.where(own, p, zero), jnp.where(own, zero, p)], axis=1).astype(BF16))
        maxes.append(mx)
    om = proj(OFF_OM, M_WIDTH)

    mem = [jnp.concatenate([c_ref[2 * p], c_ref[2 * p + 1]], axis=1) for p in range(M_PAIRS)]
    pair_row = lambda x, p: jnp.concatenate([x[2 * p:2 * p + 1, :], x[2 * p + 1:2 * p + 2, :]], axis=1)
    for rows, upd, intra, (_, w_inter, e_negm, dec, _), p2, v_aug, mx in zip(
            blocks, upds, intras, scal, probs, v_augs, maxes):
        inter = [jnp.dot(mem[p].astype(BF16), _block_diag(qt[hcols[2 * p], rows], qt[hcols[2 * p + 1], rows]),
                         preferred_element_type=F32) for p in pairs]
        mem = [pair_row(dec, p) * mem[p] + upd[p] for p in pairs]
        o = jnp.dot(p2, v_aug, preferred_element_type=F32)
        o = o[:, :LANES] * (1.0 / (o[:, LANES:] + jnp.exp(sink - mx)))
        for j, tile in enumerate(_unstack_heads(o, ln)):
            mix_ref[rows, j * LANES:(j + 1) * LANES] = tile.astype(mix_ref.dtype)
        for p in pairs:
            num = inter[p] * pair_row(w_inter, p) + intra[p]
            den = jnp.maximum(jnp.abs(num[M_HEAD_DIM:M_HEAD_DIM + 1, :]), pair_row(e_negm, p))
            hh = num[:M_HEAD_DIM, :] * (1.0 / den)
            y = hh * lax.rsqrt(jnp.mean(hh * hh, axis=0, keepdims=True) + EPS)
            for h in (2 * p, 2 * p + 1):
                y_h = y[:, (h % 2) * ln:(h % 2 + 1) * ln]
                mcols = slice(ATTN_WIDTH + h * M_HEAD_DIM, ATTN_WIDTH + (h + 1) * M_HEAD_DIM)
                mix_ref[rows, mcols] = (jax.nn.sigmoid(om[rows, hcols[h]])
                                        * (y_h.T * mnorm_ref[:, hcols[h]])).astype(mix_ref.dtype)
    for p in range(M_PAIRS):
        c_ref[2 * p] = mem[p][:, :M_HEAD_DIM]
        c_ref[2 * p + 1] = mem[p][:, M_HEAD_DIM:]

    @pl.when(step == pl.num_programs(0) - 1)
    def _():
        kwt_ref[...] = kw_ref[...].T
        vwt_ref[...] = vw_ref[...].T
        for h in range(M_HEADS):
            ct_ref[h] = c_ref[h][:M_HEAD_DIM, :].T
            nt_ref[h:h + 1, :] = c_ref[h][M_HEAD_DIM:M_HEAD_DIM + 1, :]
        diag = jnp.sum(jnp.where(row8 == lane8, m_ref[...], 0.0), axis=0, keepdims=True)
        mt_ref[...] = diag[:, :M_HEADS]
        lane = lax.broadcasted_iota(jnp.int32, bias_ref.shape, 1)
        gate_bias = jnp.zeros(bias_ref.shape, F32)
        for h in range(M_HEADS):
            gate_bias = jnp.where(lane == h, bi_ref[0, h], jnp.where(lane == M_HEADS + h, bf_ref[0, h], gate_bias))
        bias_ref[...] = gate_bias


def _out_row_block(i):
    per_head = (ATTN_WIDTH // ATTN_HEADS) // WEIGHT_SLAB
    j, part = i // per_head, i % per_head
    head = (j % 2) * GROUP + j // 2
    return jnp.where(i < ATTN_HEADS * per_head, head * per_head + part, i)


def _mixer_prompt(sink, b_i, b_f, x, g_pre, w_in_t, mnorm, w_gate, w_up, w_down, w_out):
    m = x.shape[0]
    tm = TOKEN_TILE
    steps = m // tm
    assert D_MODEL == steps * WEIGHT_SLAB and D_FF % (steps // 2) == 0 and (D_FF // (steps // 2)) % BF16_SUBLANES == 0
    down_slab = D_FF // (steps // 2)
    row = lambda n: pl.BlockSpec((tm, n), lambda i: (i, 0))
    whole = lambda shape: pl.BlockSpec(shape, lambda i: (0,) * len(shape))
    smem = pl.BlockSpec(memory_space=pltpu.SMEM)
    slab = lambda n: pl.BlockSpec((WEIGHT_SLAB, n), lambda i: (i, 0))
    down = pl.BlockSpec((down_slab, D_MODEL), lambda i: (i // 2, 0))
    c_shape = (M_HEADS, AUG_ROWS, M_HEAD_DIM)
    w_shape = (WINDOW, KV_WIDTH)
    wt_shape = (KV_WIDTH, WINDOW)
    ct_shape = (M_HEADS, M_HEAD_DIM, M_HEAD_DIM)
    n_shape = (M_HEADS, M_HEAD_DIM)
    b_shape = (1, GATE_PAD)
    s_shape = (SUBLANES, LANES)
    sds = jax.ShapeDtypeStruct
    return pl.pallas_call(
        _mixer_prompt_kernel,
        grid=(steps,),
        in_specs=[smem, smem, smem, row(D_MODEL), _const_spec((1, D_MODEL)), _const_spec(w_in_t.shape),
                  _const_spec((1, M_WIDTH)), slab(D_FF), slab(D_FF), down,
                  pl.BlockSpec((WEIGHT_SLAB, D_MODEL), lambda i: (_out_row_block(i), 0))],
        out_specs=[row(ATTN_WIDTH + M_WIDTH), whole(w_shape), whole(w_shape), whole(c_shape), whole(s_shape),
                   whole((D_MODEL, IN_PAD)), slab(D_FF), slab(D_FF), down, slab(D_MODEL),
                   whole(wt_shape), whole(wt_shape), whole(ct_shape), whole(n_shape),
                   whole((1, M_HEADS)), whole(b_shape)],
        out_shape=[sds((m, ATTN_WIDTH + M_WIDTH), BF16), sds(w_shape, F32), sds(w_shape, F32), sds(c_shape, F32),
                   sds(s_shape, F32), sds((D_MODEL, IN_PAD), BF16), sds(w_gate.shape, BF16), sds(w_up.shape, BF16),
                   sds(w_down.shape, BF16), sds(w_out.shape, BF16),
                   sds(wt_shape, F32), sds(wt_shape, F32), sds(ct_shape, F32), sds(n_shape, F32),
                   sds((1, M_HEADS), F32), sds(b_shape, F32)],
        compiler_params=_params(("arbitrary",)),
        name="mixer_prompt",
    )(sink, b_i, b_f, x, g_pre, w_in_t, mnorm, w_gate, w_up, w_down, w_out)


def _gates(g_blk, bias_row, cum):
    pre = g_blk + bias_row
    lane = lax.broadcasted_iota(jnp.int32, pre.shape, 1)
    a = jnp.where(lane < M_HEADS, pre, jax.nn.log_sigmoid(pre))
    b = jnp.dot(cum, a, precision=HIGHEST, preferred_element_type=F32)
    return a, b, a.T, b.T


def _col(x, j):
    return jnp.broadcast_to(x[:, j:j + 1], x.shape)


def _mlstm_gated_scores(q, k, a, b, at, bt, h, mask, m_prev, last):
    bc, ic = _col(b, M_HEADS + h), _col(a, h)
    br, ir = bt[M_HEADS + h:M_HEADS + h + 1, :], at[h:h + 1, :]
    d = jnp.where(mask, (bc - br) + ir, -jnp.inf)
    inter = bc + m_prev
    m_t = jnp.maximum(inter, jnp.max(d, axis=-1, keepdims=True))
    w_inter = jnp.exp(inter - m_t)
    sqk = jnp.exp(d - m_t) * lax.dot_general(q, k, (((1,), (1,)), ((), ())), preferred_element_type=F32)
    m_end = last(m_t)
    bl = last(bc)
    dec = jnp.exp(bl + m_prev - m_end)
    w_end = jnp.exp((bl - bc) + ic - m_end)
    kw = k.astype(F32) * w_end
    return w_inter, sqk, m_t, m_end, dec, kw


def _mlstm_sample_kernel(q_ref, k_ref, v_ref, om_ref, g_ref, m0_ref, bias_ref, mnorm_ref, c_ref, n_ref,
                         o_ref, c_out, n_out, m_out, *, dec_seq):
    t = dec_seq
    bb = SAMPLE_BATCH
    ln = bb * t
    r = lax.broadcasted_iota(jnp.int32, (ln, ln), 0)
    c = lax.broadcasted_iota(jnp.int32, (ln, ln), 1)
    shift = t.bit_length() - 1
    same = (r >> shift) == (c >> shift)
    mask = jnp.logical_and(same, c <= r)
    cum = mask.astype(F32)
    expand = (c == (r >> shift)).astype(F32)
    gather = ((c >> shift) == r).astype(F32)
    is_last = (r & (t - 1)) == t - 1
    is_first = (r & (t - 1)) == 0

    def last(x):
        y = jnp.where(is_last, x, 0.0)
        step = 1
        while step < t:
            y = y + pltpu.roll(y, ln - step, axis=0)
            step *= 2
        return y

    a, b, at, bt = _gates(g_ref[...], bias_ref[...], cum)
    m0 = m0_ref[...]
    lane = lax.broadcasted_iota(jnp.int32, (ln, LANES), 1)
    zpad = jnp.zeros((ln - bb, M_HEAD_DIM), F32)
    heads = range(M_HEADS)
    hcols = [slice(h * M_HEAD_DIM, (h + 1) * M_HEAD_DIM) for h in heads]
    qfs = [q_ref[:, hc] for hc in hcols]
    vs = [v_ref[:, hc] for hc in hcols]
    qcs = [jnp.concatenate(
        [jnp.dot(qfs[h][s * t:(s + 1) * t, :], c_ref[s * M_HEADS + h].astype(BF16).astype(F32),
                 preferred_element_type=F32) for s in range(bb)], axis=0) for h in heads]
    n_exps = [jnp.dot(expand, jnp.concatenate([n_ref[:, hc], zpad], axis=0), precision=HIGHEST,
                      preferred_element_type=F32) for hc in hcols]
    parts = [_mlstm_gated_scores(qfs[h].astype(BF16), k_ref[:, hcols[h]], a, b, at, bt, h, mask,
                                 _col(m0, h), last) for h in heads]
    intras = [jnp.dot(parts[h][1].astype(BF16), vs[h], preferred_element_type=F32) for h in heads]
    m_cols = jnp.zeros((ln, LANES), F32)
    kwts = []
    for h in heads:
        w_inter, sqk, m_t, m_end, dec, kw = parts[h]
        num = w_inter * qcs[h] + intras[h]
        nq = w_inter * jnp.sum(qfs[h] * n_exps[h], axis=-1, keepdims=True) + jnp.sum(sqk, axis=-1, keepdims=True)
        hh = num / jnp.maximum(jnp.abs(nq), jnp.exp(-m_t))
        o_ref[:, hcols[h]] = _head_out(hh, mnorm_ref[:, hcols[h]], om_ref[:, hcols[h]])
        kwts.append(kw.T.astype(BF16))
        m_cols = jnp.where(lane == h, m_end, m_cols)
    m_out[...] = m_cols
    for h in heads:
        dec = parts[h][4]
        for s in range(bb):
            lhs = jnp.where((c >> shift) == s, kwts[h], jnp.zeros_like(kwts[h]))
            upd = jnp.dot(lhs, vs[h], preferred_element_type=F32)
            c_out[s * M_HEADS + h] = dec[s * t:s * t + 1, :] * c_ref[s * M_HEADS + h] + upd
    for h in heads:
        dec, kw = parts[h][4], parts[h][5]
        n_new = jnp.dot(gather, jnp.where(is_first, dec * n_exps[h], 0.0) + kw, precision=HIGHEST,
                        preferred_element_type=F32)
        n_out[:, hcols[h]] = n_new[:bb, :]


def _mixer_sample_kernel(sink_ref, x_ref, g_ref, w_ref, ck_ref, cv_ref, m0_ref, bias_ref, mnorm_ref, c_ref, n_ref,
                         mix_ref, ko_ref, vo_ref, c_out, n_out, m_out, *proj_refs, dec_seq):
    step = pl.program_id(0)

    @pl.when(step == 0)
    def _():
        _inproj_kernel(x_ref, g_ref, w_ref, *proj_refs)

    tm = mix_ref.shape[0]
    rows = pl.ds(pl.multiple_of(step * tm, tm), tm)
    qa_ref, kn_ref, vn_ref, qm_ref, km_ref, vm_ref, om_ref, gt_ref = [r.at[rows, :] for r in proj_refs]
    _attn_sample_kernel(sink_ref, qa_ref, kn_ref, vn_ref, ck_ref, cv_ref, mix_ref.at[:, pl.ds(0, ATTN_WIDTH)],
                        ko_ref, vo_ref, dec_seq=dec_seq)
    _mlstm_sample_kernel(qm_ref, km_ref, vm_ref, om_ref, gt_ref, m0_ref, bias_ref, mnorm_ref, c_ref, n_ref,
                         mix_ref.at[:, pl.ds(ATTN_WIDTH, M_WIDTH)], c_out, n_out, m_out, dec_seq=dec_seq)


def _mixer_sample(sink, x, g_pre, w_pad, cache_k, cache_v, m0_rows, bias_row, mnorm, c_in, n_in, dec_seq):
    m = x.shape[0]
    bb = SAMPLE_BATCH
    tm = bb * dec_seq
    nb = m // dec_seq
    row = lambda n: pl.BlockSpec((tm, n), lambda i: (i, 0))
    whole = lambda shape: pl.BlockSpec(shape, lambda i: (0,) * len(shape))
    cache = pl.BlockSpec((bb, KV_HEADS, HEAD_DIM, WINDOW), lambda i: (i, 0, 0, 0))
    c_spec = pl.BlockSpec((bb * M_HEADS, M_HEAD_DIM, M_HEAD_DIM), lambda i: (i, 0, 0))
    n_spec = pl.BlockSpec((bb, M_WIDTH), lambda i: (i, 0))
    sds = jax.ShapeDtypeStruct
    return pl.pallas_call(
        functools.partial(_mixer_sample_kernel, dec_seq=dec_seq),
        grid=(nb // bb,),
        in_specs=[pl.BlockSpec(memory_space=pltpu.SMEM), _const_spec(x.shape), _const_spec((1, D_MODEL)),
                  _const_spec(w_pad.shape), cache, cache, row(LANES),
                  whole((1, GATE_PAD)), whole((1, M_WIDTH)), c_spec, n_spec],
        out_specs=[row(ATTN_WIDTH + M_WIDTH), cache, cache, c_spec, n_spec, row(LANES)],
        out_shape=[sds((m, ATTN_WIDTH + M_WIDTH), BF16), sds(cache_k.shape, F32), sds(cache_v.shape, F32),
                   sds(c_in.shape, F32), sds(n_in.shape, F32), sds((m, LANES), F32)],
        scratch_shapes=[pltpu.VMEM((m, n), dt) for n, dt in zip(PROJ_WIDTHS, PROJ_DTYPES)],
        compiler_params=_params(("arbitrary",)),
        name="mixer_sample",
    )(sink, x, g_pre, w_pad, cache_k, cache_v, m0_rows, bias_row, mnorm, c_in, n_in)


def _out_ffn_kernel(x_ref, mix_ref, wo_ref, g1_ref, g2_ref, wg_ref, wu_ref, wd_ref, g3_ref, o_ref):
    assert sum(FFN_GROUP_ROWS) == x_ref.shape[0]
    starts = [sum(FFN_GROUP_ROWS[:i]) for i in range(len(FFN_GROUP_ROWS))]
    groups = [slice(r, r + n) for r, n in zip(starts, FFN_GROUP_ROWS)]
    halves = [slice(rs.start + h * ((rs.stop - rs.start) // 2), rs.start + (h + 1) * ((rs.stop - rs.start) // 2))
              for rs in groups for h in range(2)]
    ys = [jnp.dot(mix_ref[rs, :], wo_ref[...], preferred_element_type=F32) for rs in halves]
    x1h = [x_ref[rs, :] + _rms(y, g1_ref[...]) for rs, y in zip(halves, ys)]
    fh = [_rms(x1, g2_ref[...]).astype(BF16) for x1 in x1h]
    x1s = [jnp.concatenate(x1h[2 * i:2 * i + 2], axis=0) for i in range(len(groups))]
    fs = [jnp.concatenate(fh[2 * i:2 * i + 2], axis=0) for i in range(len(groups))]
    accs = [None] * len(groups)
    acts = {}
    chunks = D_FF // FFN_CHUNK
    for rnd in range(chunks + max(FFN_GROUP_LAG) + 1):
        for i, f in enumerate(fs):
            c = rnd - FFN_GROUP_LAG[i]
            if 0 <= c < chunks:
                cols = slice(c * FFN_CHUNK, (c + 1) * FFN_CHUNK)
                g = jnp.dot(f, wg_ref[:, cols], preferred_element_type=F32)
                u = jnp.dot(f, wu_ref[:, cols], preferred_element_type=F32)
                acts[i, c] = (g * jax.nn.sigmoid(g) * u).astype(BF16)
        for i in range(len(groups)):
            c = rnd - FFN_GROUP_LAG[i] - 1
            if 0 <= c < chunks:
                part = jnp.dot(acts.pop((i, c)), wd_ref[c * FFN_CHUNK:(c + 1) * FFN_CHUNK, :],
                               preferred_element_type=F32)
                accs[i] = part if accs[i] is None else accs[i] + part
    for rs, x1, acc in zip(groups, x1s, accs):
        o_ref[rs, :] = x1 + _rms(acc, g3_ref[...])


def _out_ffn(x, mix, w_out, g_post_mix, g_pre_ffn, w_gate, w_up, w_down, g_post_ffn):
    m = x.shape[0]
    tm = TOKEN_TILE
    row = lambda n: pl.BlockSpec((tm, n), lambda i: (i, 0))
    vec = _const_spec((1, D_MODEL))
    return pl.pallas_call(
        _out_ffn_kernel,
        grid=(m // tm,),
        in_specs=[row(D_MODEL), row(ATTN_WIDTH + M_WIDTH), _const_spec((D_MODEL, D_MODEL)), vec, vec,
                  _const_spec((D_MODEL, D_FF)), _const_spec((D_MODEL, D_FF)), _const_spec((D_FF, D_MODEL)), vec],
        out_specs=row(D_MODEL),
        out_shape=jax.ShapeDtypeStruct((m, D_MODEL), F32),
        compiler_params=_params(("parallel",)),
        name="out_ffn",
    )(x, mix, w_out, g_post_mix, g_pre_ffn, w_gate, w_up, w_down, g_post_ffn)


def _layer(xp, xs, cache_k, cache_v, state_c, state_n, state_m, w_in, b_i, b_f, attn_sink, m_norm, w_out,
           g_pre_mix, g_post_mix, g_pre_ffn, g_post_ffn, w_gate, w_up, w_down):
    bp, sp, _ = xp.shape
    bs, ts, _ = xs.shape
    assert bp == 1 and sp % TOKEN_TILE == 0 and TOKEN_TILE % WINDOW == 0
    assert ts & (ts - 1) == 0 and (bs * ts) % TOKEN_TILE == 0 and bs % SAMPLE_BATCH == 0

    row = lambda v: v.reshape(1, -1)
    sink = row(attn_sink)

    x2 = xp.reshape(sp, D_MODEL)
    mix, _, _, _, _, w_pad, wg, wu, wd, wo, k_wt, v_wt, c_p, n_p, m_p, bias_row = _mixer_prompt(
        sink, row(b_i), row(b_f), x2, row(g_pre_mix), w_in.T, row(m_norm), w_gate, w_up, w_down, w_out)
    ffn = (wo, row(g_post_mix), row(g_pre_ffn), wg, wu, wd, row(g_post_ffn))
    yp = _out_ffn(x2, mix, *ffn).reshape(xp.shape)
    window_major = lambda c: jnp.transpose(c, (0, 3, 1, 2))
    k_p = window_major(k_wt.reshape(1, KV_HEADS, HEAD_DIM, WINDOW))
    v_p = window_major(v_wt.reshape(1, KV_HEADS, HEAD_DIM, WINDOW))
    c_p = c_p.reshape(1, M_HEADS, M_HEAD_DIM, M_HEAD_DIM)
    n_p = n_p.reshape(1, M_HEADS, M_HEAD_DIM)

    x2 = xs.reshape(bs * ts, D_MODEL)
    feature_major = lambda c: jnp.transpose(c, (0, 2, 3, 1))
    m0_rows = jnp.pad(jnp.repeat(state_m, ts, axis=0), ((0, 0), (0, LANES - M_HEADS)))
    mix, k_s, v_s, c_s, n_s, m_rows = _mixer_sample(
        sink, x2, row(g_pre_mix), w_pad, feature_major(cache_k), feature_major(cache_v), m0_rows, bias_row,
        row(m_norm), state_c.reshape(bs * M_HEADS, M_HEAD_DIM, M_HEAD_DIM), state_n.reshape(bs, M_WIDTH), ts)
    ys = _out_ffn(x2, mix, *ffn).reshape(xs.shape)
    k_s, v_s = window_major(k_s), window_major(v_s)
    c_s = c_s.reshape(bs, M_HEADS, M_HEAD_DIM, M_HEAD_DIM)
    n_s = n_s.reshape(bs, M_HEADS, M_HEAD_DIM)
    m_s = m_rows[ts - 1::ts, :M_HEADS]
    return yp, ys, (k_p, v_p, c_p, n_p, m_p), (k_s, v_s, c_s, n_s, m_s)


def kernel(x_prompt, x_sample, cache_k, cache_v, state_C, state_n, state_m, w_in, b_i, b_f, attn_sink, m_norm,
           w_out, g_pre_mix, g_post_mix, g_pre_ffn, g_post_ffn, w_gate, w_up, w_down):
    depth = w_in.shape[0]
    xp, xs = x_prompt, x_sample
    prompt_states, sample_states = [], []
    for l in range(depth):
        xp, xs, st_p, st_s = _layer(xp, xs, cache_k[l], cache_v[l], state_C[l], state_n[l], state_m[l],
                                    w_in[l], b_i[l], b_f[l], attn_sink[l], m_norm[l], w_out[l],
                                    g_pre_mix[l], g_post_mix[l], g_pre_ffn[l], g_post_ffn[l],
                                    w_gate[l], w_up[l], w_down[l])
        prompt_states.append(st_p)
        sample_states.append(st_s)
    stack = lambda states, i: jnp.stack([s[i] for s in states], axis=0)
    return (xp, xs) + tuple(stack(prompt_states, i) for i in range(5)) + tuple(stack(sample_states, i) for i in range(5))
```

```python
import functools

import jax
import jax.numpy as jnp
from jax import lax
from jax.experimental import pallas as pl
from jax.experimental.pallas import tpu as pltpu

F32 = jnp.float32
BF16 = jnp.bfloat16
HIGHEST = lax.Precision.HIGHEST

D_MODEL = 1024
HEAD_DIM = 64
ATTN_HEADS = 8
KV_HEADS = 2
GROUP = ATTN_HEADS // KV_HEADS
ATTN_WIDTH = ATTN_HEADS * HEAD_DIM
KV_WIDTH = KV_HEADS * HEAD_DIM
WINDOW = 128
M_HEADS = 4
M_HEAD_DIM = 128
M_WIDTH = M_HEADS * M_HEAD_DIM
M_PAIRS = M_HEADS // 2
D_FF = 2816
EPS = 1e-6

LANES = 128
SUBLANES = 8
GATE_PAD = LANES
BF16_SUBLANES = 16
AUG_ROWS = M_HEAD_DIM + BF16_SUBLANES
IN_MAIN = ATTN_WIDTH + 2 * KV_WIDTH + 4 * M_WIDTH
IN_PAD = IN_MAIN + GATE_PAD
VMEM_LIMIT = 56 * 1024 * 1024

HEAD_ORDER = tuple(h for j in range(GROUP) for h in (j, j + GROUP))

TOKEN_TILE = 512
FFN_CHUNK = 256
ROW_GROUPS = 2
FFN_GROUP_ROWS = (256, 256)
FFN_GROUP_LAG = (0, 1)
MIXER_TILE = 1024
WEIGHT_SLAB = 64
SAMPLE_BATCH = 16


def _rms(x, g):
    return x * lax.rsqrt(jnp.mean(x * x, axis=-1, keepdims=True) + EPS) * g


def _const_spec(shape):
    nd = len(shape)
    return pl.BlockSpec(shape, lambda i: (0,) * nd, pipeline_mode=pl.Buffered(1))


def _params(semantics):
    return pltpu.CompilerParams(dimension_semantics=semantics, vmem_limit_bytes=VMEM_LIMIT)


OFF_QA, OFF_KA, OFF_VA = 0, ATTN_WIDTH, ATTN_WIDTH + KV_WIDTH
OFF_QM = ATTN_WIDTH + 2 * KV_WIDTH
OFF_KM, OFF_VM, OFF_OM = OFF_QM + M_WIDTH, OFF_QM + 2 * M_WIDTH, OFF_QM + 3 * M_WIDTH


def _inproj_kernel(x_ref, g_ref, w_ref, qa_ref, ka_ref, va_ref, qm_ref, km_ref, vm_ref, om_ref, gt_ref):
    tm = x_ref.shape[0]
    step = tm // ROW_GROUPS
    for r0 in range(0, tm, step):
        rs = slice(r0, r0 + step)
        h = _rms(x_ref[rs, :], g_ref[...]).astype(BF16)

        def proj(off, n):
            return jnp.dot(h, w_ref[:, off:off + n], preferred_element_type=F32)

        qa_ref[rs, :] = proj(OFF_QA, ATTN_WIDTH) * (HEAD_DIM ** -0.5)
        ka_ref[rs, :] = proj(OFF_KA, KV_WIDTH)
        va_ref[rs, :] = proj(OFF_VA, KV_WIDTH)
        qm_ref[rs, :] = proj(OFF_QM, M_WIDTH)
        km_ref[rs, :] = (proj(OFF_KM, M_WIDTH) * (M_HEAD_DIM ** -0.5)).astype(km_ref.dtype)
        vm_ref[rs, :] = proj(OFF_VM, M_WIDTH).astype(vm_ref.dtype)
        om_ref[rs, :] = proj(OFF_OM, M_WIDTH)
        gt_ref[rs, :] = proj(IN_MAIN, GATE_PAD)


PROJ_WIDTHS = (ATTN_WIDTH, KV_WIDTH, KV_WIDTH, M_WIDTH, M_WIDTH, M_WIDTH, M_WIDTH, GATE_PAD)
PROJ_DTYPES = (F32, F32, F32, F32, BF16, BF16, F32, F32)


def _stack_heads(q_tiles):
    lane = lax.broadcasted_iota(jnp.int32, q_tiles[0].shape, 1)
    lo = lane < HEAD_DIM
    zero = jnp.zeros_like(q_tiles[0])
    parts = []
    for qt in q_tiles:
        parts += [jnp.where(lo, qt, zero), jnp.where(lo, zero, qt)]
    return jnp.concatenate(parts, axis=0)


def _unstack_heads(o, rows):
    lane = lax.broadcasted_iota(jnp.int32, (rows, LANES), 1)
    lo = lane < HEAD_DIM
    return [jnp.where(lo, o[(2 * j) * rows:(2 * j + 1) * rows, :], o[(2 * j + 1) * rows:(2 * j + 2) * rows, :])
            for j in range(GROUP)]


def _sink_rows(sink_ref, rows):
    return jnp.concatenate([jnp.full((rows, LANES), sink_ref[0, h], F32) for h in HEAD_ORDER], axis=0)


def _attn_sample_kernel(sink_ref, q_ref, kn_ref, vn_ref, ck_ref, cv_ref, o_ref, ko_ref, vo_ref, *, dec_seq):
    t = dec_seq
    bb = SAMPLE_BATCH
    nrow = ATTN_HEADS * t
    ts, ns = t.bit_length() - 1, nrow.bit_length() - 1
    r_c = lax.broadcasted_iota(jnp.int32, (nrow, WINDOW), 0) & (t - 1)
    c_c = lax.broadcasted_iota(jnp.int32, (nrow, WINDOW), 1)
    vis_cache = c_c > r_c
    r_n = lax.broadcasted_iota(jnp.int32, (bb * nrow, bb * t), 0)
    c_n = lax.broadcasted_iota(jnp.int32, (bb * nrow, bb * t), 1)
    vis_new = jnp.logical_and((r_n >> ns) == (c_n >> ts), (c_n & (t - 1)) <= (r_n & (t - 1)))
    sink = _sink_rows(sink_ref, t)
    nt = (((1,), (1,)), ((), ()))
    kn_all, vn_all = kn_ref[...], vn_ref[...]
    lane_w = lax.broadcasted_iota(jnp.int32, (KV_WIDTH, WINDOW), 1)
    is_new = lane_w >= WINDOW - t
    zero_rows = jnp.zeros((WINDOW - t, KV_WIDTH), F32)
    state_shape = (KV_HEADS, HEAD_DIM, WINDOW)

    def slide(old_t, new_rows):
        new_t = jnp.concatenate([zero_rows, new_rows], axis=0).T
        return jnp.where(is_new, new_t, pltpu.roll(old_t, WINDOW - t, axis=1)).reshape(state_shape)

    qs, s_c, cvs = [], [], []
    for b in range(bb):
        rows = slice(b * t, (b + 1) * t)
        ck = ck_ref[b].reshape(KV_WIDTH, WINDOW)
        cvs.append(cv_ref[b].reshape(KV_WIDTH, WINDOW))
        ko_ref[b] = slide(ck, kn_all[rows, :])
        vo_ref[b] = slide(cvs[b], vn_all[rows, :])
        qs.append(_stack_heads([q_ref[rows, j * LANES:(j + 1) * LANES] for j in range(GROUP)]).astype(BF16))
        s_c.append(jnp.where(vis_cache, jnp.dot(qs[b], ck.astype(BF16), preferred_element_type=F32), -jnp.inf))
    s_n = jnp.where(vis_new, lax.dot_general(jnp.concatenate(qs, axis=0), kn_all.astype(BF16), nt,
                                             preferred_element_type=F32), -jnp.inf)
    p_c, p_n, rden = [], [], []
    for b in range(bb):
        s_nb = s_n[b * nrow:(b + 1) * nrow, :]
        m = jnp.maximum(jnp.maximum(jnp.max(s_c[b], axis=-1, keepdims=True), jnp.max(s_nb, axis=-1, keepdims=True)),
                        sink)
        p_c.append(jnp.exp(s_c[b] - m))
        p_n.append(jnp.exp(s_nb - m[:, :bb * t]))
        rden.append(1.0 / (jnp.sum(p_c[b], axis=-1, keepdims=True) + jnp.sum(p_n[b], axis=-1, keepdims=True)
                           + jnp.exp(sink - m)))
    o_n = jnp.dot(jnp.concatenate(p_n, axis=0).astype(BF16), vn_all.astype(BF16), preferred_element_type=F32)
    outs = [[] for _ in range(GROUP)]
    for b in range(bb):
        o = (lax.dot_general(p_c[b].astype(BF16), cvs[b].astype(BF16), nt, preferred_element_type=F32)
             + o_n[b * nrow:(b + 1) * nrow, :]) * rden[b]
        for j, tile in enumerate(_unstack_heads(o, t)):
            outs[j].append(tile)
    for j, parts in enumerate(outs):
        o_ref[:, j * LANES:(j + 1) * LANES] = jnp.concatenate(parts, axis=0).astype(o_ref.dtype)


def _block_diag(a, b):
    za, zb = jnp.zeros_like(a), jnp.zeros_like(b)
    return jnp.concatenate([jnp.concatenate([a, zb], axis=1), jnp.concatenate([za, b], axis=1)], axis=0)


def _head_out(hh, mnorm_row, om):
    y = hh * lax.rsqrt(jnp.mean(hh * hh, axis=-1, keepdims=True) + EPS) * mnorm_row
    return (jax.nn.sigmoid(om) * y).astype(BF16)


def _mixer_prompt_kernel(sink_ref, bi_ref, bf_ref, x_ref, g_ref, win_ref, mnorm_ref, wg_ref, wu_ref, wd_ref, wo_ref,
                         mix_ref, kw_ref, vw_ref, c_ref, m_ref, w_ref, wgb_ref, wub_ref, wdb_ref, wob_ref,
                         kwt_ref, vwt_ref, ct_ref, nt_ref, mt_ref, bias_ref):
    step = pl.program_id(0)

    @pl.when(step == 0)
    def _():
        kw_ref[...] = jnp.zeros_like(kw_ref)
        vw_ref[...] = jnp.zeros_like(vw_ref)
        c_ref[...] = jnp.zeros_like(c_ref)
        m_ref[...] = jnp.zeros_like(m_ref)
        for j in range(GROUP):
            lo = win_ref[j * HEAD_DIM:(j + 1) * HEAD_DIM, :]
            hi = win_ref[(j + GROUP) * HEAD_DIM:(j + GROUP + 1) * HEAD_DIM, :]
            w_ref[:, j * LANES:(j + 1) * LANES] = jnp.concatenate([lo, hi], axis=0).T.astype(BF16)
        for c0 in range(ATTN_WIDTH, IN_MAIN, M_WIDTH):
            c1 = min(c0 + M_WIDTH, IN_MAIN)
            w_ref[:, c0:c1] = win_ref[c0:c1, :].T.astype(BF16)
        gw = jnp.concatenate([win_ref[IN_MAIN:IN_MAIN + 2 * M_HEADS, :],
                              jnp.zeros((GATE_PAD - 2 * M_HEADS, D_MODEL), F32)], axis=0)
        w_ref[:, IN_MAIN:] = gw.T.astype(BF16)

    wgb_ref[...] = wg_ref[...].astype(BF16)
    wub_ref[...] = wu_ref[...].astype(BF16)
    wdb_ref[...] = wd_ref[...].astype(BF16)
    wob_ref[...] = wo_ref[...].astype(BF16)

    tm = x_ref.shape[0]
    ln = WINDOW
    groups = [slice(r0, r0 + tm // ROW_GROUPS) for r0 in range(0, tm, tm // ROW_GROUPS)]
    blocks = [slice(r0, r0 + ln) for r0 in range(0, tm, ln)]
    hcols = [slice(h * M_HEAD_DIM, (h + 1) * M_HEAD_DIM) for h in range(M_HEADS)]
    pcols = [slice(2 * p * M_HEAD_DIM, 2 * (p + 1) * M_HEAD_DIM) for p in range(M_PAIRS)]
    nt = (((1,), (1,)), ((), ()))
    hs = [_rms(x_ref[rs, :], g_ref[...]).astype(BF16) for rs in groups]

    def proj(off, n):
        return jnp.concatenate([jnp.dot(h, w_ref[:, off:off + n], preferred_element_type=F32) for h in hs], axis=0)

    qt = proj(OFF_QM, M_WIDTH).astype(BF16).T
    gates = proj(IN_MAIN, GATE_PAD).T
    km = (proj(OFF_KM, M_WIDTH) * (M_HEAD_DIM ** -0.5)).astype(BF16)
    r = lax.broadcasted_iota(jnp.int32, (ln, ln), 0)
    c = lax.broadcasted_iota(jnp.int32, (ln, ln), 1)
    causal_t = r <= c
    upper = causal_t.astype(F32)
    lane8 = lax.broadcasted_iota(jnp.int32, (SUBLANES, ln), 1)
    ones_rows = jnp.ones((AUG_ROWS - M_HEAD_DIM, ln), BF16)
    zrows = jnp.zeros((ln - SUBLANES, ln), F32)
    row8 = lax.broadcasted_iota(jnp.int32, (SUBLANES, ln), 0)
    bias = jnp.zeros((SUBLANES, ln), F32)
    for h in range(M_HEADS):
        bias = jnp.where(row8 == h, bi_ref[0, h], jnp.where(row8 == M_HEADS + h, bf_ref[0, h], bias))
    gis = [gates[0:SUBLANES, rows] + bias for rows in blocks]
    prefix = jnp.dot(jnp.concatenate([jax.nn.log_sigmoid(gi) for gi in gis], axis=0), upper, precision=HIGHEST,
                     preferred_element_type=F32)
    pre = []
    for blk, gi in enumerate(gis):
        b = pltpu.roll(prefix[blk * SUBLANES:(blk + 1) * SUBLANES, :], M_HEADS, axis=0)
        g = gi - b
        cm0 = g
        sh = 1
        while sh < ln:
            cm0 = jnp.maximum(cm0, jnp.where(lane8 >= sh, pltpu.roll(cm0, sh, axis=1), -jnp.inf))
            sh *= 2
        b_last = jnp.broadcast_to(b[:, ln - 1:ln], b.shape)
        g_max = jnp.broadcast_to(cm0[:, ln - 1:ln], b.shape)
        g_cols = jnp.concatenate([g, zrows], axis=0).T
        pre.append((b, g_cols, cm0, b_last, g_max, (b_last - b) + gi))
    scores_m = [[jnp.dot(km[rows, pc], _block_diag(qt[hcols[2 * p], rows], qt[hcols[2 * p + 1], rows]),
                         preferred_element_type=F32) for p, pc in enumerate(pcols)] for rows in blocks]
    vt = proj(OFF_VM, M_WIDTH).astype(BF16).T

    qa = (proj(OFF_QA, ATTN_WIDTH) * (HEAD_DIM ** -0.5)).astype(BF16)
    kv = proj(OFF_KA, 2 * KV_WIDTH)
    ka, va = kv[:, :KV_WIDTH], kv[:, KV_WIDTH:]
    nrow = ATTN_HEADS * ln
    ra = lax.broadcasted_iota(jnp.int32, (nrow, ln), 0) & (ln - 1)
    ca = lax.broadcasted_iota(jnp.int32, (nrow, ln), 1)
    own = ca <= ra
    sink = _sink_rows(sink_ref, ln)
    no_prev = jnp.where(step == 0, -jnp.inf, 0.0)
    ones = jnp.ones((2 * ln, LANES), BF16)
    kcats, v_augs = [], []
    for blk, rows in enumerate(blocks):
        if blk == 0:
            k_prev, v_prev = kw_ref[...], vw_ref[...]
        else:
            k_prev, v_prev = ka[blocks[blk - 1], :], va[blocks[blk - 1], :]
        kcats.append(jnp.concatenate([ka[rows, :], k_prev], axis=0).astype(BF16))
        vcat = jnp.concatenate([va[rows, :], v_prev], axis=0).astype(BF16)
        v_augs.append(jnp.concatenate([vcat, ones], axis=1))
    kw_ref[...] = ka[blocks[-1], :]
    vw_ref[...] = va[blocks[-1], :]

    m_prev = m_ref[...]
    scal = []
    for b, _, cm0, b_last, g_max, w_end_arg in pre:
        cm = jnp.maximum(cm0, m_prev)
        m_end = b_last + jnp.maximum(g_max, m_prev)
        scal.append((cm, jnp.exp(m_prev - cm), jnp.exp(-(b + cm)), jnp.exp(b_last + m_prev - m_end),
                     jnp.exp(w_end_arg - m_end)))
        m_prev = m_end
    m_ref[...] = m_prev
    gated = []
    for rows, s_t, (_, g_cols, *_), (cm, _, _, _, w_end) in zip(blocks, scores_m, pre, scal):
        vts = [jnp.concatenate([vt[hc, rows], ones_rows], axis=0) for hc in hcols]
        sqks = [(jnp.exp(jnp.where(causal_t, g_cols[:, h:h + 1] - cm[h:h + 1, :], -jnp.inf))
                 * s_t[h // 2][:, (h % 2) * ln:(h % 2 + 1) * ln]).astype(BF16) for h in range(M_HEADS)]
        kvws = [(vts[h].astype(F32) * w_end[h:h + 1, :]).astype(BF16) for h in range(M_HEADS)]
        gated.append((vts, sqks, kvws))
    pairs = range(M_PAIRS)
    upds = [[jnp.dot(jnp.concatenate([kvws[2 * p], kvws[2 * p + 1]], axis=1),
                     _block_diag(km[rows, hcols[2 * p]], km[rows, hcols[2 * p + 1]]),
                     preferred_element_type=F32) for p in pairs] for rows, (_, _, kvws) in zip(blocks, gated)]
    intras = [[jnp.dot(jnp.concatenate([vts[2 * p], vts[2 * p + 1]], axis=1),
                       _block_diag(sqks[2 * p], sqks[2 * p + 1]), preferred_element_type=F32) for p in pairs]
              for vts, sqks, _ in gated]

    scores_a = [lax.dot_general(_stack_heads([qa[rows, j * LANES:(j + 1) * LANES] for j in range(GROUP)]), kcat, nt,
                                preferred_element_type=F32) for rows, kcat in zip(blocks, kcats)]
    probs, maxes = [], []
    for blk, s in enumerate(scores_a):
        s_prev = s[:, ln:]
        if blk == 0:
            s_prev = s_prev + no_prev
        sc = jnp.where(own, s[:, :ln], s_prev)
        mx = jnp.maximum(jnp.max(sc, axis=-1, keepdims=True), sink)
        p = jnp.exp(sc - mx)
        zero = jnp.zeros_like(p)
        probs.append(jnp.concatenate([jnp.where(own, p, zero), jnp.where(own, zero, p)], axis=1).astype(BF16))
        maxes.append(mx)
    om = proj(OFF_OM, M_WIDTH)

    mem = [jnp.concatenate([c_ref[2 * p], c_ref[2 * p + 1]], axis=1) for p in range(M_PAIRS)]
    pair_row = lambda x, p: jnp.concatenate([x[2 * p:2 * p + 1, :], x[2 * p + 1:2 * p + 2, :]], axis=1)
    for rows, upd, intra, (_, w_inter, e_negm, dec, _), p2, v_aug, mx in zip(
            blocks, upds, intras, scal, probs, v_augs, maxes):
        inter = [jnp.dot(mem[p].astype(BF16), _block_diag(qt[hcols[2 * p], rows], qt[hcols[2 * p + 1], rows]),
                         preferred_element_type=F32) for p in pairs]
        mem = [pair_row(dec, p) * mem[p] + upd[p] for p in pairs]
        o = jnp.dot(p2, v_aug, preferred_element_type=F32)
        o = o[:, :LANES] * (1.0 / (o[:, LANES:] + jnp.exp(sink - mx)))
        for j, tile in enumerate(_unstack_heads(o, ln)):
            mix_ref[rows, j * LANES:(j + 1) * LANES] = tile.astype(mix_ref.dtype)
        for p in pairs:
            num = inter[p] * pair_row(w_inter, p) + intra[p]
            den = jnp.maximum(jnp.abs(num[M_HEAD_DIM:M_HEAD_DIM + 1, :]), pair_row(e_negm, p))
            hh = num[:M_HEAD_DIM, :] * (1.0 / den)
            y = hh * lax.rsqrt(jnp.mean(hh * hh, axis=0, keepdims=True) + EPS)
            for h in (2 * p, 2 * p + 1):
                y_h = y[:, (h % 2) * ln:(h % 2 + 1) * ln]
                mcols = slice(ATTN_WIDTH + h * M_HEAD_DIM, ATTN_WIDTH + (h + 1) * M_HEAD_DIM)
                mix_ref[rows, mcols] = (jax.nn.sigmoid(om[rows, hcols[h]])
                                        * (y_h.T * mnorm_ref[:, hcols[h]])).astype(mix_ref.dtype)
    for p in range(M_PAIRS):
        c_ref[2 * p] = mem[p][:, :M_HEAD_DIM]
        c_ref[2 * p + 1] = mem[p][:, M_HEAD_DIM:]

    @pl.when(step == pl.num_programs(0) - 1)
    def _():
        kwt_ref[...] = kw_ref[...].T
        vwt_ref[...] = vw_ref[...].T
        for h in range(M_HEADS):
            ct_ref[h] = c_ref[h][:M_HEAD_DIM, :].T
            nt_ref[h:h + 1, :] = c_ref[h][M_HEAD_DIM:M_HEAD_DIM + 1, :]
        diag = jnp.sum(jnp.where(row8 == lane8, m_ref[...], 0.0), axis=0, keepdims=True)
        mt_ref[...] = diag[:, :M_HEADS]
        lane = lax.broadcasted_iota(jnp.int32, bias_ref.shape, 1)
        gate_bias = jnp.zeros(bias_ref.shape, F32)
        for h in range(M_HEADS):
            gate_bias = jnp.where(lane == h, bi_ref[0, h], jnp.where(lane == M_HEADS + h, bf_ref[0, h], gate_bias))
        bias_ref[...] = gate_bias


def _out_row_block(i):
    per_head = (ATTN_WIDTH // ATTN_HEADS) // WEIGHT_SLAB
    j, part = i // per_head, i % per_head
    head = (j % 2) * GROUP + j // 2
    return jnp.where(i < ATTN_HEADS * per_head, head * per_head + part, i)


def _mixer_prompt(sink, b_i, b_f, x, g_pre, w_in_t, mnorm, w_gate, w_up, w_down, w_out):
    m = x.shape[0]
    tm = MIXER_TILE
    steps = m // tm
    assert D_MODEL == steps * WEIGHT_SLAB and D_FF % (steps // 2) == 0 and (D_FF // (steps // 2)) % BF16_SUBLANES == 0
    down_slab = D_FF // (steps // 2)
    row = lambda n: pl.BlockSpec((tm, n), lambda i: (i, 0))
    whole = lambda shape: pl.BlockSpec(shape, lambda i: (0,) * len(shape))
    smem = pl.BlockSpec(memory_space=pltpu.SMEM)
    slab = lambda n: pl.BlockSpec((WEIGHT_SLAB, n), lambda i: (i, 0))
    down = pl.BlockSpec((down_slab, D_MODEL), lambda i: (i // 2, 0))
    c_shape = (M_HEADS, AUG_ROWS, M_HEAD_DIM)
    w_shape = (WINDOW, KV_WIDTH)
    wt_shape = (KV_WIDTH, WINDOW)
    ct_shape = (M_HEADS, M_HEAD_DIM, M_HEAD_DIM)
    n_shape = (M_HEADS, M_HEAD_DIM)
    b_shape = (1, GATE_PAD)
    s_shape = (SUBLANES, LANES)
    sds = jax.ShapeDtypeStruct
    return pl.pallas_call(
        _mixer_prompt_kernel,
        grid=(steps,),
        in_specs=[smem, smem, smem, row(D_MODEL), _const_spec((1, D_MODEL)), _const_spec(w_in_t.shape),
                  _const_spec((1, M_WIDTH)), slab(D_FF), slab(D_FF), down,
                  pl.BlockSpec((WEIGHT_SLAB, D_MODEL), lambda i: (_out_row_block(i), 0))],
        out_specs=[row(ATTN_WIDTH + M_WIDTH), whole(w_shape), whole(w_shape), whole(c_shape), whole(s_shape),
                   whole((D_MODEL, IN_PAD)), slab(D_FF), slab(D_FF), down, slab(D_MODEL),
                   whole(wt_shape), whole(wt_shape), whole(ct_shape), whole(n_shape),
                   whole((1, M_HEADS)), whole(b_shape)],
        out_shape=[sds((m, ATTN_WIDTH + M_WIDTH), BF16), sds(w_shape, F32), sds(w_shape, F32), sds(c_shape, F32),
                   sds(s_shape, F32), sds((D_MODEL, IN_PAD), BF16), sds(w_gate.shape, BF16), sds(w_up.shape, BF16),
                   sds(w_down.shape, BF16), sds(w_out.shape, BF16),
                   sds(wt_shape, F32), sds(wt_shape, F32), sds(ct_shape, F32), sds(n_shape, F32),
                   sds((1, M_HEADS), F32), sds(b_shape, F32)],
        compiler_params=_params(("arbitrary",)),
        name="mixer_prompt",
    )(sink, b_i, b_f, x, g_pre, w_in_t, mnorm, w_gate, w_up, w_down, w_out)


def _gates(g_blk, bias_row, cum):
    pre = g_blk + bias_row
    lane = lax.broadcasted_iota(jnp.int32, pre.shape, 1)
    a = jnp.where(lane < M_HEADS, pre, jax.nn.log_sigmoid(pre))
    b = jnp.dot(cum, a, precision=HIGHEST, preferred_element_type=F32)
    return a, b, a.T, b.T


def _col(x, j):
    return jnp.broadcast_to(x[:, j:j + 1], x.shape)


def _mlstm_gated_scores(q, k, a, b, at, bt, h, mask, m_prev, last):
    bc, ic = _col(b, M_HEADS + h), _col(a, h)
    br, ir = bt[M_HEADS + h:M_HEADS + h + 1, :], at[h:h + 1, :]
    d = jnp.where(mask, (bc - br) + ir, -jnp.inf)
    inter = bc + m_prev
    m_t = jnp.maximum(inter, jnp.max(d, axis=-1, keepdims=True))
    w_inter = jnp.exp(inter - m_t)
    sqk = jnp.exp(d - m_t) * lax.dot_general(q, k, (((1,), (1,)), ((), ())), preferred_element_type=F32)
    m_end = last(m_t)
    bl = last(bc)
    dec = jnp.exp(bl + m_prev - m_end)
    w_end = jnp.exp((bl - bc) + ic - m_end)
    kw = k.astype(F32) * w_end
    return w_inter, sqk, m_t, m_end, dec, kw


def _mlstm_sample_kernel(q_ref, k_ref, v_ref, om_ref, g_ref, m0_ref, bias_ref, mnorm_ref, c_ref, n_ref,
                         o_ref, c_out, n_out, m_out, *, dec_seq):
    t = dec_seq
    bb = SAMPLE_BATCH
    ln = bb * t
    r = lax.broadcasted_iota(jnp.int32, (ln, ln), 0)
    c = lax.broadcasted_iota(jnp.int32, (ln, ln), 1)
    shift = t.bit_length() - 1
    same = (r >> shift) == (c >> shift)
    mask = jnp.logical_and(same, c <= r)
    cum = mask.astype(F32)
    expand = (c == (r >> shift)).astype(F32)
    gather = ((c >> shift) == r).astype(F32)
    is_last = (r & (t - 1)) == t - 1
    is_first = (r & (t - 1)) == 0

    def last(x):
        y = jnp.where(is_last, x, 0.0)
        step = 1
        while step < t:
            y = y + pltpu.roll(y, ln - step, axis=0)
            step *= 2
        return y

    a, b, at, bt = _gates(g_ref[...], bias_ref[...], cum)
    m0 = m0_ref[...]
    lane = lax.broadcasted_iota(jnp.int32, (ln, LANES), 1)
    zpad = jnp.zeros((ln - bb, M_HEAD_DIM), F32)
    heads = range(M_HEADS)
    hcols = [slice(h * M_HEAD_DIM, (h + 1) * M_HEAD_DIM) for h in heads]
    qfs = [q_ref[:, hc] for hc in hcols]
    vs = [v_ref[:, hc] for hc in hcols]
    qcs = [jnp.concatenate(
        [jnp.dot(qfs[h][s * t:(s + 1) * t, :], c_ref[s * M_HEADS + h].astype(BF16).astype(F32),
                 preferred_element_type=F32) for s in range(bb)], axis=0) for h in heads]
    n_exps = [jnp.dot(expand, jnp.concatenate([n_ref[:, hc], zpad], axis=0), precision=HIGHEST,
                      preferred_element_type=F32) for hc in hcols]
    parts = [_mlstm_gated_scores(qfs[h].astype(BF16), k_ref[:, hcols[h]], a, b, at, bt, h, mask,
                                 _col(m0, h), last) for h in heads]
    intras = [jnp.dot(parts[h][1].astype(BF16), vs[h], preferred_element_type=F32) for h in heads]
    m_cols = jnp.zeros((ln, LANES), F32)
    kwts = []
    for h in heads:
        w_inter, sqk, m_t, m_end, dec, kw = parts[h]
        num = w_inter * qcs[h] + intras[h]
        nq = w_inter * jnp.sum(qfs[h] * n_exps[h], axis=-1, keepdims=True) + jnp.sum(sqk, axis=-1, keepdims=True)
        hh = num / jnp.maximum(jnp.abs(nq), jnp.exp(-m_t))
        o_ref[:, hcols[h]] = _head_out(hh, mnorm_ref[:, hcols[h]], om_ref[:, hcols[h]])
        kwts.append(kw.T.astype(BF16))
        m_cols = jnp.where(lane == h, m_end, m_cols)
    m_out[...] = m_cols
    for h in heads:
        dec = parts[h][4]
        for s in range(bb):
            lhs = jnp.where((c >> shift) == s, kwts[h], jnp.zeros_like(kwts[h]))
            upd = jnp.dot(lhs, vs[h], preferred_element_type=F32)
            c_out[s * M_HEADS + h] = dec[s * t:s * t + 1, :] * c_ref[s * M_HEADS + h] + upd
    for h in heads:
        dec, kw = parts[h][4], parts[h][5]
        n_new = jnp.dot(gather, jnp.where(is_first, dec * n_exps[h], 0.0) + kw, precision=HIGHEST,
                        preferred_element_type=F32)
        n_out[:, hcols[h]] = n_new[:bb, :]


def _mixer_sample_kernel(sink_ref, x_ref, g_ref, w_ref, ck_ref, cv_ref, m0_ref, bias_ref, mnorm_ref, c_ref, n_ref,
                         mix_ref, ko_ref, vo_ref, c_out, n_out, m_out, *proj_refs, dec_seq):
    step = pl.program_id(0)

    @pl.when(step == 0)
    def _():
        _inproj_kernel(x_ref, g_ref, w_ref, *proj_refs)

    tm = mix_ref.shape[0]
    rows = pl.ds(pl.multiple_of(step * tm, tm), tm)
    qa_ref, kn_ref, vn_ref, qm_ref, km_ref, vm_ref, om_ref, gt_ref = [r.at[rows, :] for r in proj_refs]
    _attn_sample_kernel(sink_ref, qa_ref, kn_ref, vn_ref, ck_ref, cv_ref, mix_ref.at[:, pl.ds(0, ATTN_WIDTH)],
                        ko_ref, vo_ref, dec_seq=dec_seq)
    _mlstm_sample_kernel(qm_ref, km_ref, vm_ref, om_ref, gt_ref, m0_ref, bias_ref, mnorm_ref, c_ref, n_ref,
                         mix_ref.at[:, pl.ds(ATTN_WIDTH, M_WIDTH)], c_out, n_out, m_out, dec_seq=dec_seq)


def _mixer_sample(sink, x, g_pre, w_pad, cache_k, cache_v, m0_rows, bias_row, mnorm, c_in, n_in, dec_seq):
    m = x.shape[0]
    bb = SAMPLE_BATCH
    tm = bb * dec_seq
    nb = m // dec_seq
    row = lambda n: pl.BlockSpec((tm, n), lambda i: (i, 0))
    whole = lambda shape: pl.BlockSpec(shape, lambda i: (0,) * len(shape))
    cache = pl.BlockSpec((bb, KV_HEADS, HEAD_DIM, WINDOW), lambda i: (i, 0, 0, 0))
    c_spec = pl.BlockSpec((bb * M_HEADS, M_HEAD_DIM, M_HEAD_DIM), lambda i: (i, 0, 0))
    n_spec = pl.BlockSpec((bb, M_WIDTH), lambda i: (i, 0))
    sds = jax.ShapeDtypeStruct
    return pl.pallas_call(
        functools.partial(_mixer_sample_kernel, dec_seq=dec_seq),
        grid=(nb // bb,),
        in_specs=[pl.BlockSpec(memory_space=pltpu.SMEM), _const_spec(x.shape), _const_spec((1, D_MODEL)),
                  _const_spec(w_pad.shape), cache, cache, row(LANES),
                  whole((1, GATE_PAD)), whole((1, M_WIDTH)), c_spec, n_spec],
        out_specs=[row(ATTN_WIDTH + M_WIDTH), cache, cache, c_spec, n_spec, row(LANES)],
        out_shape=[sds((m, ATTN_WIDTH + M_WIDTH), BF16), sds(cache_k.shape, F32), sds(cache_v.shape, F32),
                   sds(c_in.shape, F32), sds(n_in.shape, F32), sds((m, LANES), F32)],
        scratch_shapes=[pltpu.VMEM((m, n), dt) for n, dt in zip(PROJ_WIDTHS, PROJ_DTYPES)],
        compiler_params=_params(("arbitrary",)),
        name="mixer_sample",
    )(sink, x, g_pre, w_pad, cache_k, cache_v, m0_rows, bias_row, mnorm, c_in, n_in)


def _out_ffn_kernel(x_ref, mix_ref, wo_ref, g1_ref, g2_ref, wg_ref, wu_ref, wd_ref, g3_ref, o_ref):
    assert sum(FFN_GROUP_ROWS) == x_ref.shape[0]
    starts = [sum(FFN_GROUP_ROWS[:i]) for i in range(len(FFN_GROUP_ROWS))]
    groups = [slice(r, r + n) for r, n in zip(starts, FFN_GROUP_ROWS)]
    halves = [slice(rs.start + h * ((rs.stop - rs.start) // 2), rs.start + (h + 1) * ((rs.stop - rs.start) // 2))
              for rs in groups for h in range(2)]
    ys = [jnp.dot(mix_ref[rs, :], wo_ref[...], preferred_element_type=F32) for rs in halves]
    x1h = [x_ref[rs, :] + _rms(y, g1_ref[...]) for rs, y in zip(halves, ys)]
    fh = [_rms(x1, g2_ref[...]).astype(BF16) for x1 in x1h]
    x1s = [jnp.concatenate(x1h[2 * i:2 * i + 2], axis=0) for i in range(len(groups))]
    fs = [jnp.concatenate(fh[2 * i:2 * i + 2], axis=0) for i in range(len(groups))]
    accs = [None] * len(groups)
    acts = {}
    chunks = D_FF // FFN_CHUNK
    for rnd in range(chunks + max(FFN_GROUP_LAG) + 1):
        for i, f in enumerate(fs):
            c = rnd - FFN_GROUP_LAG[i]
            if 0 <= c < chunks:
                cols = slice(c * FFN_CHUNK, (c + 1) * FFN_CHUNK)
                g = jnp.dot(f, wg_ref[:, cols], preferred_element_type=F32)
                u = jnp.dot(f, wu_ref[:, cols], preferred_element_type=F32)
                acts[i, c] = (g * jax.nn.sigmoid(g) * u).astype(BF16)
        for i in range(len(groups)):
            c = rnd - FFN_GROUP_LAG[i] - 1
            if 0 <= c < chunks:
                part = jnp.dot(acts.pop((i, c)), wd_ref[c * FFN_CHUNK:(c + 1) * FFN_CHUNK, :],
                               preferred_element_type=F32)
                accs[i] = part if accs[i] is None else accs[i] + part
    for rs, x1, acc in zip(groups, x1s, accs):
        o_ref[rs, :] = x1 + _rms(acc, g3_ref[...])


def _out_ffn(x, mix, w_out, g_post_mix, g_pre_ffn, w_gate, w_up, w_down, g_post_ffn):
    m = x.shape[0]
    tm = TOKEN_TILE
    row = lambda n: pl.BlockSpec((tm, n), lambda i: (i, 0))
    vec = _const_spec((1, D_MODEL))
    return pl.pallas_call(
        _out_ffn_kernel,
        grid=(m // tm,),
        in_specs=[row(D_MODEL), row(ATTN_WIDTH + M_WIDTH), _const_spec((D_MODEL, D_MODEL)), vec, vec,
                  _const_spec((D_MODEL, D_FF)), _const_spec((D_MODEL, D_FF)), _const_spec((D_FF, D_MODEL)), vec],
        out_specs=row(D_MODEL),
        out_shape=jax.ShapeDtypeStruct((m, D_MODEL), F32),
        compiler_params=_params(("parallel",)),
        name="out_ffn",
    )(x, mix, w_out, g_post_mix, g_pre_ffn, w_gate, w_up, w_down, g_post_ffn)


def _layer(xp, xs, cache_k, cache_v, state_c, state_n, state_m, w_in, b_i, b_f, attn_sink, m_norm, w_out,
           g_pre_mix, g_post_mix, g_pre_ffn, g_post_ffn, w_gate, w_up, w_down):
    bp, sp, _ = xp.shape
    bs, ts, _ = xs.shape
    assert bp == 1 and sp % TOKEN_TILE == 0 and TOKEN_TILE % WINDOW == 0
    assert ts & (ts - 1) == 0 and (bs * ts) % TOKEN_TILE == 0 and bs % SAMPLE_BATCH == 0

    row = lambda v: v.reshape(1, -1)
    sink = row(attn_sink)

    x2 = xp.reshape(sp, D_MODEL)
    mix, _, _, _, _, w_pad, wg, wu, wd, wo, k_wt, v_wt, c_p, n_p, m_p, bias_row = _mixer_prompt(
        sink, row(b_i), row(b_f), x2, row(g_pre_mix), w_in.T, row(m_norm), w_gate, w_up, w_down, w_out)
    ffn = (wo, row(g_post_mix), row(g_pre_ffn), wg, wu, wd, row(g_post_ffn))
    yp = _out_ffn(x2, mix, *ffn).reshape(xp.shape)
    window_major = lambda c: jnp.transpose(c, (0, 3, 1, 2))
    k_p = window_major(k_wt.reshape(1, KV_HEADS, HEAD_DIM, WINDOW))
    v_p = window_major(v_wt.reshape(1, KV_HEADS, HEAD_DIM, WINDOW))
    c_p = c_p.reshape(1, M_HEADS, M_HEAD_DIM, M_HEAD_DIM)
    n_p = n_p.reshape(1, M_HEADS, M_HEAD_DIM)

    x2 = xs.reshape(bs * ts, D_MODEL)
    feature_major = lambda c: jnp.transpose(c, (0, 2, 3, 1))
    m0_rows = jnp.pad(jnp.repeat(state_m, ts, axis=0), ((0, 0), (0, LANES - M_HEADS)))
    mix, k_s, v_s, c_s, n_s, m_rows = _mixer_sample(
        sink, x2, row(g_pre_mix), w_pad, feature_major(cache_k), feature_major(cache_v), m0_rows, bias_row,
        row(m_norm), state_c.reshape(bs * M_HEADS, M_HEAD_DIM, M_HEAD_DIM), state_n.reshape(bs, M_WIDTH), ts)
    ys = _out_ffn(x2, mix, *ffn).reshape(xs.shape)
    k_s, v_s = window_major(k_s), window_major(v_s)
    c_s = c_s.reshape(bs, M_HEADS, M_HEAD_DIM, M_HEAD_DIM)
    n_s = n_s.reshape(bs, M_HEADS, M_HEAD_DIM)
    m_s = m_rows[ts - 1::ts, :M_HEADS]
    return yp, ys, (k_p, v_p, c_p, n_p, m_p), (k_s, v_s, c_s, n_s, m_s)


def kernel(x_prompt, x_sample, cache_k, cache_v, state_C, state_n, state_m, w_in, b_i, b_f, attn_sink, m_norm,
           w_out, g_pre_mix, g_post_mix, g_pre_ffn, g_post_ffn, w_gate, w_up, w_down):
    depth = w_in.shape[0]
    xp, xs = x_prompt, x_sample
    prompt_states, sample_states = [], []
    for l in range(depth):
        xp, xs, st_p, st_s = _layer(xp, xs, cache_k[l], cache_v[l], state_C[l], state_n[l], state_m[l],
                                    w_in[l], b_i[l], b_f[l], attn_sink[l], m_norm[l], w_out[l],
                                    g_pre_mix[l], g_post_mix[l], g_pre_ffn[l], g_post_ffn[l],
                                    w_gate[l], w_up[l], w_down[l])
        prompt_states.append(st_p)
        sample_states.append(st_s)
    stack = lambda states, i: jnp.stack([s[i] for s in states], axis=0)
    return (xp, xs) + tuple(stack(prompt_states, i) for i in range(5)) + tuple(stack(sample_states, i) for i in range(5))
```

```python
import functools

import jax
import jax.numpy as jnp
from jax import lax
from jax.experimental import pallas as pl
from jax.experimental.pallas import tpu as pltpu

F32 = jnp.float32
BF16 = jnp.bfloat16
HIGHEST = lax.Precision.HIGHEST

D_MODEL = 1024
HEAD_DIM = 64
ATTN_HEADS = 8
KV_HEADS = 2
GROUP = ATTN_HEADS // KV_HEADS
ATTN_WIDTH = ATTN_HEADS * HEAD_DIM
KV_WIDTH = KV_HEADS * HEAD_DIM
WINDOW = 128
M_HEADS = 4
M_HEAD_DIM = 128
M_WIDTH = M_HEADS * M_HEAD_DIM
M_PAIRS = M_HEADS // 2
D_FF = 2816
EPS = 1e-6

LANES = 128
SUBLANES = 8
GATE_PAD = LANES
BF16_SUBLANES = 16
AUG_ROWS = M_HEAD_DIM + BF16_SUBLANES
IN_MAIN = ATTN_WIDTH + 2 * KV_WIDTH + 4 * M_WIDTH
IN_PAD = IN_MAIN + GATE_PAD
VMEM_LIMIT = 56 * 1024 * 1024

HEAD_ORDER = tuple(h for j in range(GROUP) for h in (j, j + GROUP))

TOKEN_TILE = 512
FFN_CHUNK = 256
ROW_GROUPS = 2
FFN_TILE = 1024
FFN_GROUP = 256
MIXER_TILE = 1024
WEIGHT_SLAB = 64
SAMPLE_BATCH = 16


def _rms(x, g):
    return x * lax.rsqrt(jnp.mean(x * x, axis=-1, keepdims=True) + EPS) * g


def _const_spec(shape):
    nd = len(shape)
    return pl.BlockSpec(shape, lambda i: (0,) * nd, pipeline_mode=pl.Buffered(1))


def _params(semantics):
    return pltpu.CompilerParams(dimension_semantics=semantics, vmem_limit_bytes=VMEM_LIMIT)


OFF_QA, OFF_KA, OFF_VA = 0, ATTN_WIDTH, ATTN_WIDTH + KV_WIDTH
OFF_QM = ATTN_WIDTH + 2 * KV_WIDTH
OFF_KM, OFF_VM, OFF_OM = OFF_QM + M_WIDTH, OFF_QM + 2 * M_WIDTH, OFF_QM + 3 * M_WIDTH


def _inproj_kernel(x_ref, g_ref, w_ref, qa_ref, ka_ref, va_ref, qm_ref, km_ref, vm_ref, om_ref, gt_ref):
    tm = x_ref.shape[0]
    step = tm // ROW_GROUPS
    for r0 in range(0, tm, step):
        rs = slice(r0, r0 + step)
        h = _rms(x_ref[rs, :], g_ref[...]).astype(BF16)

        def proj(off, n):
            return jnp.dot(h, w_ref[:, off:off + n], preferred_element_type=F32)

        qa_ref[rs, :] = proj(OFF_QA, ATTN_WIDTH) * (HEAD_DIM ** -0.5)
        ka_ref[rs, :] = proj(OFF_KA, KV_WIDTH)
        va_ref[rs, :] = proj(OFF_VA, KV_WIDTH)
        qm_ref[rs, :] = proj(OFF_QM, M_WIDTH)
        km_ref[rs, :] = (proj(OFF_KM, M_WIDTH) * (M_HEAD_DIM ** -0.5)).astype(km_ref.dtype)
        vm_ref[rs, :] = proj(OFF_VM, M_WIDTH).astype(vm_ref.dtype)
        om_ref[rs, :] = proj(OFF_OM, M_WIDTH)
        gt_ref[rs, :] = proj(IN_MAIN, GATE_PAD)


PROJ_WIDTHS = (ATTN_WIDTH, KV_WIDTH, KV_WIDTH, M_WIDTH, M_WIDTH, M_WIDTH, M_WIDTH, GATE_PAD)
PROJ_DTYPES = (F32, F32, F32, F32, BF16, BF16, F32, F32)


def _stack_heads(q_tiles):
    lane = lax.broadcasted_iota(jnp.int32, q_tiles[0].shape, 1)
    lo = lane < HEAD_DIM
    zero = jnp.zeros_like(q_tiles[0])
    parts = []
    for qt in q_tiles:
        parts += [jnp.where(lo, qt, zero), jnp.where(lo, zero, qt)]
    return jnp.concatenate(parts, axis=0)


def _unstack_heads(o, rows):
    lane = lax.broadcasted_iota(jnp.int32, (rows, LANES), 1)
    lo = lane < HEAD_DIM
    return [jnp.where(lo, o[(2 * j) * rows:(2 * j + 1) * rows, :], o[(2 * j + 1) * rows:(2 * j + 2) * rows, :])
            for j in range(GROUP)]


def _sink_rows(sink_ref, rows):
    return jnp.concatenate([jnp.full((rows, LANES), sink_ref[0, h], F32) for h in HEAD_ORDER], axis=0)


def _attn_sample_kernel(sink_ref, q_ref, kn_ref, vn_ref, ck_ref, cv_ref, o_ref, ko_ref, vo_ref, *, dec_seq):
    t = dec_seq
    bb = SAMPLE_BATCH
    nrow = ATTN_HEADS * t
    ts, ns = t.bit_length() - 1, nrow.bit_length() - 1
    r_c = lax.broadcasted_iota(jnp.int32, (nrow, WINDOW), 0) & (t - 1)
    c_c = lax.broadcasted_iota(jnp.int32, (nrow, WINDOW), 1)
    vis_cache = c_c > r_c
    r_n = lax.broadcasted_iota(jnp.int32, (bb * nrow, bb * t), 0)
    c_n = lax.broadcasted_iota(jnp.int32, (bb * nrow, bb * t), 1)
    vis_new = jnp.logical_and((r_n >> ns) == (c_n >> ts), (c_n & (t - 1)) <= (r_n & (t - 1)))
    sink = _sink_rows(sink_ref, t)
    nt = (((1,), (1,)), ((), ()))
    kn_all, vn_all = kn_ref[...], vn_ref[...]
    lane_w = lax.broadcasted_iota(jnp.int32, (KV_WIDTH, WINDOW), 1)
    is_new = lane_w >= WINDOW - t
    zero_rows = jnp.zeros((WINDOW - t, KV_WIDTH), F32)
    state_shape = (KV_HEADS, HEAD_DIM, WINDOW)

    def slide(old_t, new_rows):
        new_t = jnp.concatenate([zero_rows, new_rows], axis=0).T
        return jnp.where(is_new, new_t, pltpu.roll(old_t, WINDOW - t, axis=1)).reshape(state_shape)

    qs, s_c, cvs = [], [], []
    for b in range(bb):
        rows = slice(b * t, (b + 1) * t)
        ck = ck_ref[b].reshape(KV_WIDTH, WINDOW)
        cvs.append(cv_ref[b].reshape(KV_WIDTH, WINDOW))
        ko_ref[b] = slide(ck, kn_all[rows, :])
        vo_ref[b] = slide(cvs[b], vn_all[rows, :])
        qs.append(_stack_heads([q_ref[rows, j * LANES:(j + 1) * LANES] for j in range(GROUP)]).astype(BF16))
        s_c.append(jnp.where(vis_cache, jnp.dot(qs[b], ck.astype(BF16), preferred_element_type=F32), -jnp.inf))
    s_n = jnp.where(vis_new, lax.dot_general(jnp.concatenate(qs, axis=0), kn_all.astype(BF16), nt,
                                             preferred_element_type=F32), -jnp.inf)
    p_c, p_n, rden = [], [], []
    for b in range(bb):
        s_nb = s_n[b * nrow:(b + 1) * nrow, :]
        m = jnp.maximum(jnp.maximum(jnp.max(s_c[b], axis=-1, keepdims=True), jnp.max(s_nb, axis=-1, keepdims=True)),
                        sink)
        p_c.append(jnp.exp(s_c[b] - m))
        p_n.append(jnp.exp(s_nb - m[:, :bb * t]))
        rden.append(1.0 / (jnp.sum(p_c[b], axis=-1, keepdims=True) + jnp.sum(p_n[b], axis=-1, keepdims=True)
                           + jnp.exp(sink - m)))
    o_n = jnp.dot(jnp.concatenate(p_n, axis=0).astype(BF16), vn_all.astype(BF16), preferred_element_type=F32)
    outs = [[] for _ in range(GROUP)]
    for b in range(bb):
        o = (lax.dot_general(p_c[b].astype(BF16), cvs[b].astype(BF16), nt, preferred_element_type=F32)
             + o_n[b * nrow:(b + 1) * nrow, :]) * rden[b]
        for j, tile in enumerate(_unstack_heads(o, t)):
            outs[j].append(tile)
    for j, parts in enumerate(outs):
        o_ref[:, j * LANES:(j + 1) * LANES] = jnp.concatenate(parts, axis=0).astype(o_ref.dtype)


def _block_diag(a, b):
    za, zb = jnp.zeros_like(a), jnp.zeros_like(b)
    return jnp.concatenate([jnp.concatenate([a, zb], axis=1), jnp.concatenate([za, b], axis=1)], axis=0)


def _head_out(hh, mnorm_row, om):
    y = hh * lax.rsqrt(jnp.mean(hh * hh, axis=-1, keepdims=True) + EPS) * mnorm_row
    return (jax.nn.sigmoid(om) * y).astype(BF16)


def _mixer_prompt_kernel(sink_ref, bi_ref, bf_ref, x_ref, g_ref, win_ref, mnorm_ref, wg_ref, wu_ref, wd_ref, wo_ref,
                         mix_ref, kw_ref, vw_ref, c_ref, m_ref, w_ref, wgb_ref, wub_ref, wdb_ref, wob_ref,
                         kwt_ref, vwt_ref, ct_ref, nt_ref, mt_ref, bias_ref):
    step = pl.program_id(0)

    @pl.when(step == 0)
    def _():
        kw_ref[...] = jnp.zeros_like(kw_ref)
        vw_ref[...] = jnp.zeros_like(vw_ref)
        c_ref[...] = jnp.zeros_like(c_ref)
        m_ref[...] = jnp.zeros_like(m_ref)
        for j in range(GROUP):
            lo = win_ref[j * HEAD_DIM:(j + 1) * HEAD_DIM, :]
            hi = win_ref[(j + GROUP) * HEAD_DIM:(j + GROUP + 1) * HEAD_DIM, :]
            w_ref[:, j * LANES:(j + 1) * LANES] = jnp.concatenate([lo, hi], axis=0).T.astype(BF16)
        for c0 in range(ATTN_WIDTH, IN_MAIN, M_WIDTH):
            c1 = min(c0 + M_WIDTH, IN_MAIN)
            w_ref[:, c0:c1] = win_ref[c0:c1, :].T.astype(BF16)
        gw = jnp.concatenate([win_ref[IN_MAIN:IN_MAIN + 2 * M_HEADS, :],
                              jnp.zeros((GATE_PAD - 2 * M_HEADS, D_MODEL), F32)], axis=0)
        w_ref[:, IN_MAIN:] = gw.T.astype(BF16)

    wgb_ref[...] = wg_ref[...].astype(BF16)
    wub_ref[...] = wu_ref[...].astype(BF16)
    wdb_ref[...] = wd_ref[...].astype(BF16)
    wob_ref[...] = wo_ref[...].astype(BF16)

    tm = x_ref.shape[0]
    ln = WINDOW
    groups = [slice(r0, r0 + tm // ROW_GROUPS) for r0 in range(0, tm, tm // ROW_GROUPS)]
    blocks = [slice(r0, r0 + ln) for r0 in range(0, tm, ln)]
    hcols = [slice(h * M_HEAD_DIM, (h + 1) * M_HEAD_DIM) for h in range(M_HEADS)]
    pcols = [slice(2 * p * M_HEAD_DIM, 2 * (p + 1) * M_HEAD_DIM) for p in range(M_PAIRS)]
    nt = (((1,), (1,)), ((), ()))
    hs = [_rms(x_ref[rs, :], g_ref[...]).astype(BF16) for rs in groups]

    def proj(off, n):
        return jnp.concatenate([jnp.dot(h, w_ref[:, off:off + n], preferred_element_type=F32) for h in hs], axis=0)

    qt = proj(OFF_QM, M_WIDTH).astype(BF16).T
    gates = proj(IN_MAIN, GATE_PAD).T
    km = (proj(OFF_KM, M_WIDTH) * (M_HEAD_DIM ** -0.5)).astype(BF16)
    r = lax.broadcasted_iota(jnp.int32, (ln, ln), 0)
    c = lax.broadcasted_iota(jnp.int32, (ln, ln), 1)
    causal_t = r <= c
    upper = causal_t.astype(F32)
    lane8 = lax.broadcasted_iota(jnp.int32, (SUBLANES, ln), 1)
    ones_rows = jnp.ones((AUG_ROWS - M_HEAD_DIM, ln), BF16)
    zrows = jnp.zeros((ln - SUBLANES, ln), F32)
    row8 = lax.broadcasted_iota(jnp.int32, (SUBLANES, ln), 0)
    bias = jnp.zeros((SUBLANES, ln), F32)
    for h in range(M_HEADS):
        bias = jnp.where(row8 == h, bi_ref[0, h], jnp.where(row8 == M_HEADS + h, bf_ref[0, h], bias))
    gis = [gates[0:SUBLANES, rows] + bias for rows in blocks]
    prefix = jnp.dot(jnp.concatenate([jax.nn.log_sigmoid(gi) for gi in gis], axis=0), upper, precision=HIGHEST,
                     preferred_element_type=F32)
    pre = []
    for blk, gi in enumerate(gis):
        b = pltpu.roll(prefix[blk * SUBLANES:(blk + 1) * SUBLANES, :], M_HEADS, axis=0)
        g = gi - b
        cm0 = g
        sh = 1
        while sh < ln:
            cm0 = jnp.maximum(cm0, jnp.where(lane8 >= sh, pltpu.roll(cm0, sh, axis=1), -jnp.inf))
            sh *= 2
        b_last = jnp.broadcast_to(b[:, ln - 1:ln], b.shape)
        g_max = jnp.broadcast_to(cm0[:, ln - 1:ln], b.shape)
        g_cols = jnp.concatenate([g, zrows], axis=0).T
        pre.append((b, g_cols, cm0, b_last, g_max, (b_last - b) + gi))
    scores_m = [[jnp.dot(km[rows, pc], _block_diag(qt[hcols[2 * p], rows], qt[hcols[2 * p + 1], rows]),
                         preferred_element_type=F32) for p, pc in enumerate(pcols)] for rows in blocks]
    vt = proj(OFF_VM, M_WIDTH).astype(BF16).T

    qa = (proj(OFF_QA, ATTN_WIDTH) * (HEAD_DIM ** -0.5)).astype(BF16)
    kv = proj(OFF_KA, 2 * KV_WIDTH)
    ka, va = kv[:, :KV_WIDTH], kv[:, KV_WIDTH:]
    nrow = ATTN_HEADS * ln
    ra = lax.broadcasted_iota(jnp.int32, (nrow, ln), 0) & (ln - 1)
    ca = lax.broadcasted_iota(jnp.int32, (nrow, ln), 1)
    own = ca <= ra
    sink = _sink_rows(sink_ref, ln)
    no_prev = jnp.where(step == 0, -jnp.inf, 0.0)
    ones = jnp.ones((2 * ln, LANES), BF16)
    kcats, v_augs = [], []
    for blk, rows in enumerate(blocks):
        if blk == 0:
            k_prev, v_prev = kw_ref[...], vw_ref[...]
        else:
            k_prev, v_prev = ka[blocks[blk - 1], :], va[blocks[blk - 1], :]
        kcats.append(jnp.concatenate([ka[rows, :], k_prev], axis=0).astype(BF16))
        vcat = jnp.concatenate([va[rows, :], v_prev], axis=0).astype(BF16)
        v_augs.append(jnp.concatenate([vcat, ones], axis=1))
    kw_ref[...] = ka[blocks[-1], :]
    vw_ref[...] = va[blocks[-1], :]

    m_prev = m_ref[...]
    scal = []
    for b, _, cm0, b_last, g_max, w_end_arg in pre:
        cm = jnp.maximum(cm0, m_prev)
        m_end = b_last + jnp.maximum(g_max, m_prev)
        scal.append((cm, jnp.exp(m_prev - cm), jnp.exp(-(b + cm)), jnp.exp(b_last + m_prev - m_end),
                     jnp.exp(w_end_arg - m_end)))
        m_prev = m_end
    m_ref[...] = m_prev
    gated = []
    for rows, s_t, (_, g_cols, *_), (cm, _, _, _, w_end) in zip(blocks, scores_m, pre, scal):
        vts = [jnp.concatenate([vt[hc, rows], ones_rows], axis=0) for hc in hcols]
        sqks = [(jnp.exp(jnp.where(causal_t, g_cols[:, h:h + 1] - cm[h:h + 1, :], -jnp.inf))
                 * s_t[h // 2][:, (h % 2) * ln:(h % 2 + 1) * ln]).astype(BF16) for h in range(M_HEADS)]
        kvws = [(vts[h].astype(F32) * w_end[h:h + 1, :]).astype(BF16) for h in range(M_HEADS)]
        gated.append((vts, sqks, kvws))
    pairs = range(M_PAIRS)
    upds = [[jnp.dot(jnp.concatenate([kvws[2 * p], kvws[2 * p + 1]], axis=1),
                     _block_diag(km[rows, hcols[2 * p]], km[rows, hcols[2 * p + 1]]),
                     preferred_element_type=F32) for p in pairs] for rows, (_, _, kvws) in zip(blocks, gated)]
    intras = [[jnp.dot(jnp.concatenate([vts[2 * p], vts[2 * p + 1]], axis=1),
                       _block_diag(sqks[2 * p], sqks[2 * p + 1]), preferred_element_type=F32) for p in pairs]
              for vts, sqks, _ in gated]

    scores_a = [lax.dot_general(_stack_heads([qa[rows, j * LANES:(j + 1) * LANES] for j in range(GROUP)]), kcat, nt,
                                preferred_element_type=F32) for rows, kcat in zip(blocks, kcats)]
    probs, maxes = [], []
    for blk, s in enumerate(scores_a):
        s_prev = s[:, ln:]
        if blk == 0:
            s_prev = s_prev + no_prev
        sc = jnp.where(own, s[:, :ln], s_prev)
        mx = jnp.maximum(jnp.max(sc, axis=-1, keepdims=True), sink)
        p = jnp.exp(sc - mx)
        zero = jnp.zeros_like(p)
        probs.append(jnp.concatenate([jnp.where(own, p, zero), jnp.where(own, zero, p)], axis=1).astype(BF16))
        maxes.append(mx)
    om = proj(OFF_OM, M_WIDTH)

    mem = [jnp.concatenate([c_ref[2 * p], c_ref[2 * p + 1]], axis=1) for p in range(M_PAIRS)]
    pair_row = lambda x, p: jnp.concatenate([x[2 * p:2 * p + 1, :], x[2 * p + 1:2 * p + 2, :]], axis=1)
    for rows, upd, intra, (_, w_inter, e_negm, dec, _), p2, v_aug, mx in zip(
            blocks, upds, intras, scal, probs, v_augs, maxes):
        inter = [jnp.dot(mem[p].astype(BF16), _block_diag(qt[hcols[2 * p], rows], qt[hcols[2 * p + 1], rows]),
                         preferred_element_type=F32) for p in pairs]
        mem = [pair_row(dec, p) * mem[p] + upd[p] for p in pairs]
        o = jnp.dot(p2, v_aug, preferred_element_type=F32)
        o = o[:, :LANES] * (1.0 / (o[:, LANES:] + jnp.exp(sink - mx)))
        for j, tile in enumerate(_unstack_heads(o, ln)):
            mix_ref[rows, j * LANES:(j + 1) * LANES] = tile.astype(mix_ref.dtype)
        for p in pairs:
            num = inter[p] * pair_row(w_inter, p) + intra[p]
            den = jnp.maximum(jnp.abs(num[M_HEAD_DIM:M_HEAD_DIM + 1, :]), pair_row(e_negm, p))
            hh = num[:M_HEAD_DIM, :] * (1.0 / den)
            y = hh * lax.rsqrt(jnp.mean(hh * hh, axis=0, keepdims=True) + EPS)
            for h in (2 * p, 2 * p + 1):
                y_h = y[:, (h % 2) * ln:(h % 2 + 1) * ln]
                mcols = slice(ATTN_WIDTH + h * M_HEAD_DIM, ATTN_WIDTH + (h + 1) * M_HEAD_DIM)
                mix_ref[rows, mcols] = (jax.nn.sigmoid(om[rows, hcols[h]])
                                        * (y_h.T * mnorm_ref[:, hcols[h]])).astype(mix_ref.dtype)
    for p in range(M_PAIRS):
        c_ref[2 * p] = mem[p][:, :M_HEAD_DIM]
        c_ref[2 * p + 1] = mem[p][:, M_HEAD_DIM:]

    @pl.when(step == pl.num_programs(0) - 1)
    def _():
        kwt_ref[...] = kw_ref[...].T
        vwt_ref[...] = vw_ref[...].T
        for h in range(M_HEADS):
            ct_ref[h] = c_ref[h][:M_HEAD_DIM, :].T
            nt_ref[h:h + 1, :] = c_ref[h][M_HEAD_DIM:M_HEAD_DIM + 1, :]
        diag = jnp.sum(jnp.where(row8 == lane8, m_ref[...], 0.0), axis=0, keepdims=True)
        mt_ref[...] = diag[:, :M_HEADS]
        lane = lax.broadcasted_iota(jnp.int32, bias_ref.shape, 1)
        gate_bias = jnp.zeros(bias_ref.shape, F32)
        for h in range(M_HEADS):
            gate_bias = jnp.where(lane == h, bi_ref[0, h], jnp.where(lane == M_HEADS + h, bf_ref[0, h], gate_bias))
        bias_ref[...] = gate_bias


def _out_row_block(i):
    per_head = (ATTN_WIDTH // ATTN_HEADS) // WEIGHT_SLAB
    j, part = i // per_head, i % per_head
    head = (j % 2) * GROUP + j // 2
    return jnp.where(i < ATTN_HEADS * per_head, head * per_head + part, i)


def _mixer_prompt(sink, b_i, b_f, x, g_pre, w_in_t, mnorm, w_gate, w_up, w_down, w_out):
    m = x.shape[0]
    tm = MIXER_TILE
    steps = m // tm
    assert D_MODEL == steps * WEIGHT_SLAB and D_FF % (steps // 2) == 0 and (D_FF // (steps // 2)) % BF16_SUBLANES == 0
    down_slab = D_FF // (steps // 2)
    row = lambda n: pl.BlockSpec((tm, n), lambda i: (i, 0))
    whole = lambda shape: pl.BlockSpec(shape, lambda i: (0,) * len(shape))
    smem = pl.BlockSpec(memory_space=pltpu.SMEM)
    slab = lambda n: pl.BlockSpec((WEIGHT_SLAB, n), lambda i: (i, 0))
    down = pl.BlockSpec((down_slab, D_MODEL), lambda i: (i // 2, 0))
    c_shape = (M_HEADS, AUG_ROWS, M_HEAD_DIM)
    w_shape = (WINDOW, KV_WIDTH)
    wt_shape = (KV_WIDTH, WINDOW)
    ct_shape = (M_HEADS, M_HEAD_DIM, M_HEAD_DIM)
    n_shape = (M_HEADS, M_HEAD_DIM)
    b_shape = (1, GATE_PAD)
    s_shape = (SUBLANES, LANES)
    sds = jax.ShapeDtypeStruct
    return pl.pallas_call(
        _mixer_prompt_kernel,
        grid=(steps,),
        in_specs=[smem, smem, smem, row(D_MODEL), _const_spec((1, D_MODEL)), _const_spec(w_in_t.shape),
                  _const_spec((1, M_WIDTH)), slab(D_FF), slab(D_FF), down,
                  pl.BlockSpec((WEIGHT_SLAB, D_MODEL), lambda i: (_out_row_block(i), 0))],
        out_specs=[row(ATTN_WIDTH + M_WIDTH), whole(w_shape), whole(w_shape), whole(c_shape), whole(s_shape),
                   whole((D_MODEL, IN_PAD)), slab(D_FF), slab(D_FF), down, slab(D_MODEL),
                   whole(wt_shape), whole(wt_shape), whole(ct_shape), whole(n_shape),
                   whole((1, M_HEADS)), whole(b_shape)],
        out_shape=[sds((m, ATTN_WIDTH + M_WIDTH), BF16), sds(w_shape, F32), sds(w_shape, F32), sds(c_shape, F32),
                   sds(s_shape, F32), sds((D_MODEL, IN_PAD), BF16), sds(w_gate.shape, BF16), sds(w_up.shape, BF16),
                   sds(w_down.shape, BF16), sds(w_out.shape, BF16),
                   sds(wt_shape, F32), sds(wt_shape, F32), sds(ct_shape, F32), sds(n_shape, F32),
                   sds((1, M_HEADS), F32), sds(b_shape, F32)],
        compiler_params=_params(("arbitrary",)),
        name="mixer_prompt",
    )(sink, b_i, b_f, x, g_pre, w_in_t, mnorm, w_gate, w_up, w_down, w_out)


def _gates(g_blk, bias_row, cum):
    pre = g_blk + bias_row
    lane = lax.broadcasted_iota(jnp.int32, pre.shape, 1)
    a = jnp.where(lane < M_HEADS, pre, jax.nn.log_sigmoid(pre))
    b = jnp.dot(cum, a, precision=HIGHEST, preferred_element_type=F32)
    return a, b, a.T, b.T


def _col(x, j):
    return jnp.broadcast_to(x[:, j:j + 1], x.shape)


def _mlstm_gated_scores(q, k, a, b, at, bt, h, mask, m_prev, last):
    bc, ic = _col(b, M_HEADS + h), _col(a, h)
    br, ir = bt[M_HEADS + h:M_HEADS + h + 1, :], at[h:h + 1, :]
    d = jnp.where(mask, (bc - br) + ir, -jnp.inf)
    inter = bc + m_prev
    m_t = jnp.maximum(inter, jnp.max(d, axis=-1, keepdims=True))
    w_inter = jnp.exp(inter - m_t)
    sqk = jnp.exp(d - m_t) * lax.dot_general(q, k, (((1,), (1,)), ((), ())), preferred_element_type=F32)
    m_end = last(m_t)
    bl = last(bc)
    dec = jnp.exp(bl + m_prev - m_end)
    w_end = jnp.exp((bl - bc) + ic - m_end)
    kw = k.astype(F32) * w_end
    return w_inter, sqk, m_t, m_end, dec, kw


def _mlstm_sample_kernel(q_ref, k_ref, v_ref, om_ref, g_ref, m0_ref, bias_ref, mnorm_ref, c_ref, n_ref,
                         o_ref, c_out, n_out, m_out, *, dec_seq):
    t = dec_seq
    bb = SAMPLE_BATCH
    ln = bb * t
    r = lax.broadcasted_iota(jnp.int32, (ln, ln), 0)
    c = lax.broadcasted_iota(jnp.int32, (ln, ln), 1)
    shift = t.bit_length() - 1
    same = (r >> shift) == (c >> shift)
    mask = jnp.logical_and(same, c <= r)
    cum = mask.astype(F32)
    expand = (c == (r >> shift)).astype(F32)
    gather = ((c >> shift) == r).astype(F32)
    is_last = (r & (t - 1)) == t - 1
    is_first = (r & (t - 1)) == 0

    def last(x):
        y = jnp.where(is_last, x, 0.0)
        step = 1
        while step < t:
            y = y + pltpu.roll(y, ln - step, axis=0)
            step *= 2
        return y

    a, b, at, bt = _gates(g_ref[...], bias_ref[...], cum)
    m0 = m0_ref[...]
    lane = lax.broadcasted_iota(jnp.int32, (ln, LANES), 1)
    zpad = jnp.zeros((ln - bb, M_HEAD_DIM), F32)
    heads = range(M_HEADS)
    hcols = [slice(h * M_HEAD_DIM, (h + 1) * M_HEAD_DIM) for h in heads]
    qfs = [q_ref[:, hc] for hc in hcols]
    vs = [v_ref[:, hc] for hc in hcols]
    qcs = [jnp.concatenate(
        [jnp.dot(qfs[h][s * t:(s + 1) * t, :], c_ref[s * M_HEADS + h].astype(BF16).astype(F32),
                 preferred_element_type=F32) for s in range(bb)], axis=0) for h in heads]
    n_exps = [jnp.dot(expand, jnp.concatenate([n_ref[:, hc], zpad], axis=0), precision=HIGHEST,
                      preferred_element_type=F32) for hc in hcols]
    parts = [_mlstm_gated_scores(qfs[h].astype(BF16), k_ref[:, hcols[h]], a, b, at, bt, h, mask,
                                 _col(m0, h), last) for h in heads]
    intras = [jnp.dot(parts[h][1].astype(BF16), vs[h], preferred_element_type=F32) for h in heads]
    m_cols = jnp.zeros((ln, LANES), F32)
    kwts = []
    for h in heads:
        w_inter, sqk, m_t, m_end, dec, kw = parts[h]
        num = w_inter * qcs[h] + intras[h]
        nq = w_inter * jnp.sum(qfs[h] * n_exps[h], axis=-1, keepdims=True) + jnp.sum(sqk, axis=-1, keepdims=True)
        hh = num / jnp.maximum(jnp.abs(nq), jnp.exp(-m_t))
        o_ref[:, hcols[h]] = _head_out(hh, mnorm_ref[:, hcols[h]], om_ref[:, hcols[h]])
        kwts.append(kw.T.astype(BF16))
        m_cols = jnp.where(lane == h, m_end, m_cols)
    m_out[...] = m_cols
    for h in heads:
        dec = parts[h][4]
        for s in range(bb):
            lhs = jnp.where((c >> shift) == s, kwts[h], jnp.zeros_like(kwts[h]))
            upd = jnp.dot(lhs, vs[h], preferred_element_type=F32)
            c_out[s * M_HEADS + h] = dec[s * t:s * t + 1, :] * c_ref[s * M_HEADS + h] + upd
    for h in heads:
        dec, kw = parts[h][4], parts[h][5]
        n_new = jnp.dot(gather, jnp.where(is_first, dec * n_exps[h], 0.0) + kw, precision=HIGHEST,
                        preferred_element_type=F32)
        n_out[:, hcols[h]] = n_new[:bb, :]


def _mixer_sample_kernel(sink_ref, x_ref, g_ref, w_ref, ck_ref, cv_ref, m0_ref, bias_ref, mnorm_ref, c_ref, n_ref,
                         mix_ref, ko_ref, vo_ref, c_out, n_out, m_out, *proj_refs, dec_seq):
    step = pl.program_id(0)

    @pl.when(step == 0)
    def _():
        _inproj_kernel(x_ref, g_ref, w_ref, *proj_refs)

    tm = mix_ref.shape[0]
    rows = pl.ds(pl.multiple_of(step * tm, tm), tm)
    qa_ref, kn_ref, vn_ref, qm_ref, km_ref, vm_ref, om_ref, gt_ref = [r.at[rows, :] for r in proj_refs]
    _attn_sample_kernel(sink_ref, qa_ref, kn_ref, vn_ref, ck_ref, cv_ref, mix_ref.at[:, pl.ds(0, ATTN_WIDTH)],
                        ko_ref, vo_ref, dec_seq=dec_seq)
    _mlstm_sample_kernel(qm_ref, km_ref, vm_ref, om_ref, gt_ref, m0_ref, bias_ref, mnorm_ref, c_ref, n_ref,
                         mix_ref.at[:, pl.ds(ATTN_WIDTH, M_WIDTH)], c_out, n_out, m_out, dec_seq=dec_seq)


def _mixer_sample(sink, x, g_pre, w_pad, cache_k, cache_v, m0_rows, bias_row, mnorm, c_in, n_in, dec_seq):
    m = x.shape[0]
    bb = SAMPLE_BATCH
    tm = bb * dec_seq
    nb = m // dec_seq
    row = lambda n: pl.BlockSpec((tm, n), lambda i: (i, 0))
    whole = lambda shape: pl.BlockSpec(shape, lambda i: (0,) * len(shape))
    cache = pl.BlockSpec((bb, KV_HEADS, HEAD_DIM, WINDOW), lambda i: (i, 0, 0, 0))
    c_spec = pl.BlockSpec((bb * M_HEADS, M_HEAD_DIM, M_HEAD_DIM), lambda i: (i, 0, 0))
    n_spec = pl.BlockSpec((bb, M_WIDTH), lambda i: (i, 0))
    sds = jax.ShapeDtypeStruct
    return pl.pallas_call(
        functools.partial(_mixer_sample_kernel, dec_seq=dec_seq),
        grid=(nb // bb,),
        in_specs=[pl.BlockSpec(memory_space=pltpu.SMEM), _const_spec(x.shape), _const_spec((1, D_MODEL)),
                  _const_spec(w_pad.shape), cache, cache, row(LANES),
                  whole((1, GATE_PAD)), whole((1, M_WIDTH)), c_spec, n_spec],
        out_specs=[row(ATTN_WIDTH + M_WIDTH), cache, cache, c_spec, n_spec, row(LANES)],
        out_shape=[sds((m, ATTN_WIDTH + M_WIDTH), BF16), sds(cache_k.shape, F32), sds(cache_v.shape, F32),
                   sds(c_in.shape, F32), sds(n_in.shape, F32), sds((m, LANES), F32)],
        scratch_shapes=[pltpu.VMEM((m, n), dt) for n, dt in zip(PROJ_WIDTHS, PROJ_DTYPES)],
        compiler_params=_params(("arbitrary",)),
        name="mixer_sample",
    )(sink, x, g_pre, w_pad, cache_k, cache_v, m0_rows, bias_row, mnorm, c_in, n_in)


def _out_ffn_kernel(x_ref, mix_ref, wo_ref, g1_ref, g2_ref, wg_ref, wu_ref, wd_ref, g3_ref, o_ref):
    tm = x_ref.shape[0]
    assert tm % FFN_GROUP == 0
    groups = [slice(r, r + FFN_GROUP) for r in range(0, tm, FFN_GROUP)]
    halves = [slice(rs.start + h * ((rs.stop - rs.start) // 2), rs.start + (h + 1) * ((rs.stop - rs.start) // 2))
              for rs in groups for h in range(2)]
    ys = [jnp.dot(mix_ref[rs, :], wo_ref[...], preferred_element_type=F32) for rs in halves]
    x1h = [x_ref[rs, :] + _rms(y, g1_ref[...]) for rs, y in zip(halves, ys)]
    fh = [_rms(x1, g2_ref[...]).astype(BF16) for x1 in x1h]
    x1s = [jnp.concatenate(x1h[2 * i:2 * i + 2], axis=0) for i in range(len(groups))]
    fs = [jnp.concatenate(fh[2 * i:2 * i + 2], axis=0) for i in range(len(groups))]
    accs = [None] * len(groups)
    acts = {}
    chunks = D_FF // FFN_CHUNK
    for rnd in range(chunks + len(groups)):
        for i, f in enumerate(fs):
            c = rnd - i
            if 0 <= c < chunks:
                cols = slice(c * FFN_CHUNK, (c + 1) * FFN_CHUNK)
                g = jnp.dot(f, wg_ref[:, cols], preferred_element_type=F32)
                u = jnp.dot(f, wu_ref[:, cols], preferred_element_type=F32)
                acts[i, c] = (g * jax.nn.sigmoid(g) * u).astype(BF16)
        for i in range(len(groups)):
            c = rnd - i - 1
            if 0 <= c < chunks:
                part = jnp.dot(acts.pop((i, c)), wd_ref[c * FFN_CHUNK:(c + 1) * FFN_CHUNK, :],
                               preferred_element_type=F32)
                accs[i] = part if accs[i] is None else accs[i] + part
    for rs, x1, acc in zip(groups, x1s, accs):
        o_ref[rs, :] = x1 + _rms(acc, g3_ref[...])


def _out_ffn(x, mix, w_out, g_post_mix, g_pre_ffn, w_gate, w_up, w_down, g_post_ffn):
    m = x.shape[0]
    tm = min(FFN_TILE, m // 2)
    row = lambda n: pl.BlockSpec((tm, n), lambda i: (i, 0))
    vec = _const_spec((1, D_MODEL))
    return pl.pallas_call(
        _out_ffn_kernel,
        grid=(m // tm,),
        in_specs=[row(D_MODEL), row(ATTN_WIDTH + M_WIDTH), _const_spec((D_MODEL, D_MODEL)), vec, vec,
                  _const_spec((D_MODEL, D_FF)), _const_spec((D_MODEL, D_FF)), _const_spec((D_FF, D_MODEL)), vec],
        out_specs=row(D_MODEL),
        out_shape=jax.ShapeDtypeStruct((m, D_MODEL), F32),
        compiler_params=_params(("parallel",)),
        name="out_ffn",
    )(x, mix, w_out, g_post_mix, g_pre_ffn, w_gate, w_up, w_down, g_post_ffn)


def _layer(xp, xs, cache_k, cache_v, state_c, state_n, state_m, w_in, b_i, b_f, attn_sink, m_norm, w_out,
           g_pre_mix, g_post_mix, g_pre_ffn, g_post_ffn, w_gate, w_up, w_down):
    bp, sp, _ = xp.shape
    bs, ts, _ = xs.shape
    assert bp == 1 and sp % TOKEN_TILE == 0 and TOKEN_TILE % WINDOW == 0
    assert ts & (ts - 1) == 0 and (bs * ts) % TOKEN_TILE == 0 and bs % SAMPLE_BATCH == 0

    row = lambda v: v.reshape(1, -1)
    sink = row(attn_sink)

    x2 = xp.reshape(sp, D_MODEL)
    mix, _, _, _, _, w_pad, wg, wu, wd, wo, k_wt, v_wt, c_p, n_p, m_p, bias_row = _mixer_prompt(
        sink, row(b_i), row(b_f), x2, row(g_pre_mix), w_in.T, row(m_norm), w_gate, w_up, w_down, w_out)
    ffn = (wo, row(g_post_mix), row(g_pre_ffn), wg, wu, wd, row(g_post_ffn))
    yp = _out_ffn(x2, mix, *ffn).reshape(xp.shape)
    window_major = lambda c: jnp.transpose(c, (0, 3, 1, 2))
    k_p = window_major(k_wt.reshape(1, KV_HEADS, HEAD_DIM, WINDOW))
    v_p = window_major(v_wt.reshape(1, KV_HEADS, HEAD_DIM, WINDOW))
    c_p = c_p.reshape(1, M_HEADS, M_HEAD_DIM, M_HEAD_DIM)
    n_p = n_p.reshape(1, M_HEADS, M_HEAD_DIM)

    x2 = xs.reshape(bs * ts, D_MODEL)
    feature_major = lambda c: jnp.transpose(c, (0, 2, 3, 1))
    m0_rows = jnp.pad(jnp.repeat(state_m, ts, axis=0), ((0, 0), (0, LANES - M_HEADS)))
    mix, k_s, v_s, c_s, n_s, m_rows = _mixer_sample(
        sink, x2, row(g_pre_mix), w_pad, feature_major(cache_k), feature_major(cache_v), m0_rows, bias_row,
        row(m_norm), state_c.reshape(bs * M_HEADS, M_HEAD_DIM, M_HEAD_DIM), state_n.reshape(bs, M_WIDTH), ts)
    ys = _out_ffn(x2, mix, *ffn).reshape(xs.shape)
    k_s, v_s = window_major(k_s), window_major(v_s)
    c_s = c_s.reshape(bs, M_HEADS, M_HEAD_DIM, M_HEAD_DIM)
    n_s = n_s.reshape(bs, M_HEADS, M_HEAD_DIM)
    m_s = m_rows[ts - 1::ts, :M_HEADS]
    return yp, ys, (k_p, v_p, c_p, n_p, m_p), (k_s, v_s, c_s, n_s, m_s)


def kernel(x_prompt, x_sample, cache_k, cache_v, state_C, state_n, state_m, w_in, b_i, b_f, attn_sink, m_norm,
           w_out, g_pre_mix, g_post_mix, g_pre_ffn, g_post_ffn, w_gate, w_up, w_down):
    depth = w_in.shape[0]
    xp, xs = x_prompt, x_sample
    prompt_states, sample_states = [], []
    for l in range(depth):
        xp, xs, st_p, st_s = _layer(xp, xs, cache_k[l], cache_v[l], state_C[l], state_n[l], state_m[l],
                                    w_in[l], b_i[l], b_f[l], attn_sink[l], m_norm[l], w_out[l],
                                    g_pre_mix[l], g_post_mix[l], g_pre_ffn[l], g_post_ffn[l],
                                    w_gate[l], w_up[l], w_down[l])
        prompt_states.append(st_p)
        sample_states.append(st_s)
    stack = lambda states, i: jnp.stack([s[i] for s in states], axis=0)
    return (xp, xs) + tuple(stack(prompt_states, i) for i in range(5)) + tuple(stack(sample_states, i) for i in range(5))
```

```python
import functools

import jax
import jax.numpy as jnp
from jax import lax
from jax.experimental import pallas as pl
from jax.experimental.pallas import tpu as pltpu

F32 = jnp.float32
BF16 = jnp.bfloat16
HIGHEST = lax.Precision.HIGHEST

D_MODEL = 1024
HEAD_DIM = 64
ATTN_HEADS = 8
KV_HEADS = 2
GROUP = ATTN_HEADS // KV_HEADS
ATTN_WIDTH = ATTN_HEADS * HEAD_DIM
KV_WIDTH = KV_HEADS * HEAD_DIM
WINDOW = 128
M_HEADS = 4
M_HEAD_DIM = 128
M_WIDTH = M_HEADS * M_HEAD_DIM
M_PAIRS = M_HEADS // 2
D_FF = 2816
EPS = 1e-6

LANES = 128
SUBLANES = 8
GATE_PAD = LANES
BF16_SUBLANES = 16
AUG_ROWS = M_HEAD_DIM + BF16_SUBLANES
IN_MAIN = ATTN_WIDTH + 2 * KV_WIDTH + 4 * M_WIDTH
IN_PAD = IN_MAIN + GATE_PAD
VMEM_LIMIT = 56 * 1024 * 1024

HEAD_ORDER = tuple(h for j in range(GROUP) for h in (j, j + GROUP))

TOKEN_TILE = 512
FFN_CHUNK = 256
ROW_GROUPS = 2
FFN_GROUP_ROWS = (256, 256)
FFN_GROUP_LAG = (0, 1)
FFN_WAIT_STAGES = (0, 2, 5)
MIXER_TILE = 1024
WEIGHT_SLAB = 64
SAMPLE_BATCH = 16


def _rms(x, g):
    return x * lax.rsqrt(jnp.mean(x * x, axis=-1, keepdims=True) + EPS) * g


def _const_spec(shape):
    nd = len(shape)
    return pl.BlockSpec(shape, lambda i: (0,) * nd, pipeline_mode=pl.Buffered(1))


def _params(semantics):
    return pltpu.CompilerParams(dimension_semantics=semantics, vmem_limit_bytes=VMEM_LIMIT)


OFF_QA, OFF_KA, OFF_VA = 0, ATTN_WIDTH, ATTN_WIDTH + KV_WIDTH
OFF_QM = ATTN_WIDTH + 2 * KV_WIDTH
OFF_KM, OFF_VM, OFF_OM = OFF_QM + M_WIDTH, OFF_QM + 2 * M_WIDTH, OFF_QM + 3 * M_WIDTH


def _inproj_kernel(x_ref, g_ref, w_ref, qa_ref, ka_ref, va_ref, qm_ref, km_ref, vm_ref, om_ref, gt_ref):
    tm = x_ref.shape[0]
    step = tm // ROW_GROUPS
    for r0 in range(0, tm, step):
        rs = slice(r0, r0 + step)
        h = _rms(x_ref[rs, :], g_ref[...]).astype(BF16)

        def proj(off, n):
            return jnp.dot(h, w_ref[:, off:off + n], preferred_element_type=F32)

        qa_ref[rs, :] = proj(OFF_QA, ATTN_WIDTH) * (HEAD_DIM ** -0.5)
        ka_ref[rs, :] = proj(OFF_KA, KV_WIDTH)
        va_ref[rs, :] = proj(OFF_VA, KV_WIDTH)
        qm_ref[rs, :] = proj(OFF_QM, M_WIDTH)
        km_ref[rs, :] = (proj(OFF_KM, M_WIDTH) * (M_HEAD_DIM ** -0.5)).astype(km_ref.dtype)
        vm_ref[rs, :] = proj(OFF_VM, M_WIDTH).astype(vm_ref.dtype)
        om_ref[rs, :] = proj(OFF_OM, M_WIDTH)
        gt_ref[rs, :] = proj(IN_MAIN, GATE_PAD)


PROJ_WIDTHS = (ATTN_WIDTH, KV_WIDTH, KV_WIDTH, M_WIDTH, M_WIDTH, M_WIDTH, M_WIDTH, GATE_PAD)
PROJ_DTYPES = (F32, F32, F32, F32, BF16, BF16, F32, F32)


def _stack_heads(q_tiles):
    lane = lax.broadcasted_iota(jnp.int32, q_tiles[0].shape, 1)
    lo = lane < HEAD_DIM
    zero = jnp.zeros_like(q_tiles[0])
    parts = []
    for qt in q_tiles:
        parts += [jnp.where(lo, qt, zero), jnp.where(lo, zero, qt)]
    return jnp.concatenate(parts, axis=0)


def _unstack_heads(o, rows):
    lane = lax.broadcasted_iota(jnp.int32, (rows, LANES), 1)
    lo = lane < HEAD_DIM
    return [jnp.where(lo, o[(2 * j) * rows:(2 * j + 1) * rows, :], o[(2 * j + 1) * rows:(2 * j + 2) * rows, :])
            for j in range(GROUP)]


def _sink_rows(sink_ref, rows):
    return jnp.concatenate([jnp.full((rows, LANES), sink_ref[0, h], F32) for h in HEAD_ORDER], axis=0)


def _attn_sample_kernel(sink_ref, q_ref, kn_ref, vn_ref, ck_ref, cv_ref, o_ref, ko_ref, vo_ref, *, dec_seq):
    t = dec_seq
    bb = SAMPLE_BATCH
    nrow = ATTN_HEADS * t
    ts, ns = t.bit_length() - 1, nrow.bit_length() - 1
    r_c = lax.broadcasted_iota(jnp.int32, (nrow, WINDOW), 0) & (t - 1)
    c_c = lax.broadcasted_iota(jnp.int32, (nrow, WINDOW), 1)
    vis_cache = c_c > r_c
    r_n = lax.broadcasted_iota(jnp.int32, (bb * nrow, bb * t), 0)
    c_n = lax.broadcasted_iota(jnp.int32, (bb * nrow, bb * t), 1)
    vis_new = jnp.logical_and((r_n >> ns) == (c_n >> ts), (c_n & (t - 1)) <= (r_n & (t - 1)))
    sink = _sink_rows(sink_ref, t)
    nt = (((1,), (1,)), ((), ()))
    kn_all, vn_all = kn_ref[...], vn_ref[...]
    lane_w = lax.broadcasted_iota(jnp.int32, (KV_WIDTH, WINDOW), 1)
    is_new = lane_w >= WINDOW - t
    zero_rows = jnp.zeros((WINDOW - t, KV_WIDTH), F32)
    state_shape = (KV_HEADS, HEAD_DIM, WINDOW)

    def slide(old_t, new_rows):
        new_t = jnp.concatenate([zero_rows, new_rows], axis=0).T
        return jnp.where(is_new, new_t, pltpu.roll(old_t, WINDOW - t, axis=1)).reshape(state_shape)

    qs, s_c, cvs = [], [], []
    for b in range(bb):
        rows = slice(b * t, (b + 1) * t)
        ck = ck_ref[b].reshape(KV_WIDTH, WINDOW)
        cvs.append(cv_ref[b].reshape(KV_WIDTH, WINDOW))
        ko_ref[b] = slide(ck, kn_all[rows, :])
        vo_ref[b] = slide(cvs[b], vn_all[rows, :])
        qs.append(_stack_heads([q_ref[rows, j * LANES:(j + 1) * LANES] for j in range(GROUP)]).astype(BF16))
        s_c.append(jnp.where(vis_cache, jnp.dot(qs[b], ck.astype(BF16), preferred_element_type=F32), -jnp.inf))
    s_n = jnp.where(vis_new, lax.dot_general(jnp.concatenate(qs, axis=0), kn_all.astype(BF16), nt,
                                             preferred_element_type=F32), -jnp.inf)
    p_c, p_n, rden = [], [], []
    for b in range(bb):
        s_nb = s_n[b * nrow:(b + 1) * nrow, :]
        m = jnp.maximum(jnp.maximum(jnp.max(s_c[b], axis=-1, keepdims=True), jnp.max(s_nb, axis=-1, keepdims=True)),
                        sink)
        p_c.append(jnp.exp(s_c[b] - m))
        p_n.append(jnp.exp(s_nb - m[:, :bb * t]))
        rden.append(1.0 / (jnp.sum(p_c[b], axis=-1, keepdims=True) + jnp.sum(p_n[b], axis=-1, keepdims=True)
                           + jnp.exp(sink - m)))
    o_n = jnp.dot(jnp.concatenate(p_n, axis=0).astype(BF16), vn_all.astype(BF16), preferred_element_type=F32)
    outs = [[] for _ in range(GROUP)]
    for b in range(bb):
        o = (lax.dot_general(p_c[b].astype(BF16), cvs[b].astype(BF16), nt, preferred_element_type=F32)
             + o_n[b * nrow:(b + 1) * nrow, :]) * rden[b]
        for j, tile in enumerate(_unstack_heads(o, t)):
            outs[j].append(tile)
    for j, parts in enumerate(outs):
        o_ref[:, j * LANES:(j + 1) * LANES] = jnp.concatenate(parts, axis=0).astype(o_ref.dtype)


def _block_diag(a, b):
    za, zb = jnp.zeros_like(a), jnp.zeros_like(b)
    return jnp.concatenate([jnp.concatenate([a, zb], axis=1), jnp.concatenate([za, b], axis=1)], axis=0)


def _head_out(hh, mnorm_row, om):
    y = hh * lax.rsqrt(jnp.mean(hh * hh, axis=-1, keepdims=True) + EPS) * mnorm_row
    return (jax.nn.sigmoid(om) * y).astype(BF16)


def _mixer_prompt_kernel(sink_ref, bi_ref, bf_ref, x_ref, g_ref, win_ref, mnorm_ref, wg_ref, wu_ref, wd_ref, wo_ref,
                         mix_ref, kw_ref, vw_ref, c_ref, m_ref, w_ref, wgb_ref, wub_ref, wdb_ref, wob_ref,
                         kwt_ref, vwt_ref, ct_ref, nt_ref, mt_ref, bias_ref):
    step = pl.program_id(0)

    @pl.when(step == 0)
    def _():
        kw_ref[...] = jnp.zeros_like(kw_ref)
        vw_ref[...] = jnp.zeros_like(vw_ref)
        c_ref[...] = jnp.zeros_like(c_ref)
        m_ref[...] = jnp.zeros_like(m_ref)
        for j in range(GROUP):
            lo = win_ref[j * HEAD_DIM:(j + 1) * HEAD_DIM, :]
            hi = win_ref[(j + GROUP) * HEAD_DIM:(j + GROUP + 1) * HEAD_DIM, :]
            w_ref[:, j * LANES:(j + 1) * LANES] = jnp.concatenate([lo, hi], axis=0).T.astype(BF16)
        for c0 in range(ATTN_WIDTH, IN_MAIN, M_WIDTH):
            c1 = min(c0 + M_WIDTH, IN_MAIN)
            w_ref[:, c0:c1] = win_ref[c0:c1, :].T.astype(BF16)
        gw = jnp.concatenate([win_ref[IN_MAIN:IN_MAIN + 2 * M_HEADS, :],
                              jnp.zeros((GATE_PAD - 2 * M_HEADS, D_MODEL), F32)], axis=0)
        w_ref[:, IN_MAIN:] = gw.T.astype(BF16)

    wgb_ref[...] = wg_ref[...].astype(BF16)
    wub_ref[...] = wu_ref[...].astype(BF16)
    wdb_ref[...] = wd_ref[...].astype(BF16)
    wob_ref[...] = wo_ref[...].astype(BF16)

    tm = x_ref.shape[0]
    ln = WINDOW
    groups = [slice(r0, r0 + tm // ROW_GROUPS) for r0 in range(0, tm, tm // ROW_GROUPS)]
    blocks = [slice(r0, r0 + ln) for r0 in range(0, tm, ln)]
    hcols = [slice(h * M_HEAD_DIM, (h + 1) * M_HEAD_DIM) for h in range(M_HEADS)]
    pcols = [slice(2 * p * M_HEAD_DIM, 2 * (p + 1) * M_HEAD_DIM) for p in range(M_PAIRS)]
    nt = (((1,), (1,)), ((), ()))
    hs = [_rms(x_ref[rs, :], g_ref[...]).astype(BF16) for rs in groups]

    def proj(off, n):
        return jnp.concatenate([jnp.dot(h, w_ref[:, off:off + n], preferred_element_type=F32) for h in hs], axis=0)

    qt = proj(OFF_QM, M_WIDTH).astype(BF16).T
    gates = proj(IN_MAIN, GATE_PAD).T
    km = (proj(OFF_KM, M_WIDTH) * (M_HEAD_DIM ** -0.5)).astype(BF16)
    r = lax.broadcasted_iota(jnp.int32, (ln, ln), 0)
    c = lax.broadcasted_iota(jnp.int32, (ln, ln), 1)
    causal_t = r <= c
    upper = causal_t.astype(F32)
    lane8 = lax.broadcasted_iota(jnp.int32, (SUBLANES, ln), 1)
    ones_rows = jnp.ones((AUG_ROWS - M_HEAD_DIM, ln), BF16)
    zrows = jnp.zeros((ln - SUBLANES, ln), F32)
    row8 = lax.broadcasted_iota(jnp.int32, (SUBLANES, ln), 0)
    bias = jnp.zeros((SUBLANES, ln), F32)
    for h in range(M_HEADS):
        bias = jnp.where(row8 == h, bi_ref[0, h], jnp.where(row8 == M_HEADS + h, bf_ref[0, h], bias))
    gis = [gates[0:SUBLANES, rows] + bias for rows in blocks]
    prefix = jnp.dot(jnp.concatenate([jax.nn.log_sigmoid(gi) for gi in gis], axis=0), upper, precision=HIGHEST,
                     preferred_element_type=F32)
    pre = []
    for blk, gi in enumerate(gis):
        b = pltpu.roll(prefix[blk * SUBLANES:(blk + 1) * SUBLANES, :], M_HEADS, axis=0)
        g = gi - b
        cm0 = g
        sh = 1
        while sh < ln:
            cm0 = jnp.maximum(cm0, jnp.where(lane8 >= sh, pltpu.roll(cm0, sh, axis=1), -jnp.inf))
            sh *= 2
        b_last = jnp.broadcast_to(b[:, ln - 1:ln], b.shape)
        g_max = jnp.broadcast_to(cm0[:, ln - 1:ln], b.shape)
        g_cols = jnp.concatenate([g, zrows], axis=0).T
        pre.append((b, g_cols, cm0, b_last, g_max, (b_last - b) + gi))
    scores_m = [[jnp.dot(km[rows, pc], _block_diag(qt[hcols[2 * p], rows], qt[hcols[2 * p + 1], rows]),
                         preferred_element_type=F32) for p, pc in enumerate(pcols)] for rows in blocks]
    vt = proj(OFF_VM, M_WIDTH).astype(BF16).T

    qa = (proj(OFF_QA, ATTN_WIDTH) * (HEAD_DIM ** -0.5)).astype(BF16)
    kv = proj(OFF_KA, 2 * KV_WIDTH)
    ka, va = kv[:, :KV_WIDTH], kv[:, KV_WIDTH:]
    nrow = ATTN_HEADS * ln
    ra = lax.broadcasted_iota(jnp.int32, (nrow, ln), 0) & (ln - 1)
    ca = lax.broadcasted_iota(jnp.int32, (nrow, ln), 1)
    own = ca <= ra
    sink = _sink_rows(sink_ref, ln)
    no_prev = jnp.where(step == 0, -jnp.inf, 0.0)
    ones = jnp.ones((2 * ln, LANES), BF16)
    kcats, v_augs = [], []
    for blk, rows in enumerate(blocks):
        if blk == 0:
            k_prev, v_prev = kw_ref[...], vw_ref[...]
        else:
            k_prev, v_prev = ka[blocks[blk - 1], :], va[blocks[blk - 1], :]
        kcats.append(jnp.concatenate([ka[rows, :], k_prev], axis=0).astype(BF16))
        vcat = jnp.concatenate([va[rows, :], v_prev], axis=0).astype(BF16)
        v_augs.append(jnp.concatenate([vcat, ones], axis=1))
    kw_ref[...] = ka[blocks[-1], :]
    vw_ref[...] = va[blocks[-1], :]

    m_prev = m_ref[...]
    scal = []
    for b, _, cm0, b_last, g_max, w_end_arg in pre:
        cm = jnp.maximum(cm0, m_prev)
        m_end = b_last + jnp.maximum(g_max, m_prev)
        scal.append((cm, jnp.exp(m_prev - cm), jnp.exp(-(b + cm)), jnp.exp(b_last + m_prev - m_end),
                     jnp.exp(w_end_arg - m_end)))
        m_prev = m_end
    m_ref[...] = m_prev
    gated = []
    for rows, s_t, (_, g_cols, *_), (cm, _, _, _, w_end) in zip(blocks, scores_m, pre, scal):
        vts = [jnp.concatenate([vt[hc, rows], ones_rows], axis=0) for hc in hcols]
        sqks = [(jnp.exp(jnp.where(causal_t, g_cols[:, h:h + 1] - cm[h:h + 1, :], -jnp.inf))
                 * s_t[h // 2][:, (h % 2) * ln:(h % 2 + 1) * ln]).astype(BF16) for h in range(M_HEADS)]
        kvws = [(vts[h].astype(F32) * w_end[h:h + 1, :]).astype(BF16) for h in range(M_HEADS)]
        gated.append((vts, sqks, kvws))
    pairs = range(M_PAIRS)
    upds = [[jnp.dot(jnp.concatenate([kvws[2 * p], kvws[2 * p + 1]], axis=1),
                     _block_diag(km[rows, hcols[2 * p]], km[rows, hcols[2 * p + 1]]),
                     preferred_element_type=F32) for p in pairs] for rows, (_, _, kvws) in zip(blocks, gated)]
    intras = [[jnp.dot(jnp.concatenate([vts[2 * p], vts[2 * p + 1]], axis=1),
                       _block_diag(sqks[2 * p], sqks[2 * p + 1]), preferred_element_type=F32) for p in pairs]
              for vts, sqks, _ in gated]

    scores_a = [lax.dot_general(_stack_heads([qa[rows, j * LANES:(j + 1) * LANES] for j in range(GROUP)]), kcat, nt,
                                preferred_element_type=F32) for rows, kcat in zip(blocks, kcats)]
    probs, maxes = [], []
    for blk, s in enumerate(scores_a):
        s_prev = s[:, ln:]
        if blk == 0:
            s_prev = s_prev + no_prev
        sc = jnp.where(own, s[:, :ln], s_prev)
        mx = jnp.maximum(jnp.max(sc, axis=-1, keepdims=True), sink)
        p = jnp.exp(sc - mx)
        zero = jnp.zeros_like(p)
        probs.append(jnp.concatenate([jnp.where(own, p, zero), jnp.where(own, zero, p)], axis=1).astype(BF16))
        maxes.append(mx)
    om = proj(OFF_OM, M_WIDTH)

    mem = [jnp.concatenate([c_ref[2 * p], c_ref[2 * p + 1]], axis=1) for p in range(M_PAIRS)]
    pair_row = lambda x, p: jnp.concatenate([x[2 * p:2 * p + 1, :], x[2 * p + 1:2 * p + 2, :]], axis=1)
    for rows, upd, intra, (_, w_inter, e_negm, dec, _), p2, v_aug, mx in zip(
            blocks, upds, intras, scal, probs, v_augs, maxes):
        inter = [jnp.dot(mem[p].astype(BF16), _block_diag(qt[hcols[2 * p], rows], qt[hcols[2 * p + 1], rows]),
                         preferred_element_type=F32) for p in pairs]
        mem = [pair_row(dec, p) * mem[p] + upd[p] for p in pairs]
        o = jnp.dot(p2, v_aug, preferred_element_type=F32)
        o = o[:, :LANES] * (1.0 / (o[:, LANES:] + jnp.exp(sink - mx)))
        for j, tile in enumerate(_unstack_heads(o, ln)):
            mix_ref[rows, j * LANES:(j + 1) * LANES] = tile.astype(mix_ref.dtype)
        for p in pairs:
            num = inter[p] * pair_row(w_inter, p) + intra[p]
            den = jnp.maximum(jnp.abs(num[M_HEAD_DIM:M_HEAD_DIM + 1, :]), pair_row(e_negm, p))
            hh = num[:M_HEAD_DIM, :] * (1.0 / den)
            y = hh * lax.rsqrt(jnp.mean(hh * hh, axis=0, keepdims=True) + EPS)
            for h in (2 * p, 2 * p + 1):
                y_h = y[:, (h % 2) * ln:(h % 2 + 1) * ln]
                mcols = slice(ATTN_WIDTH + h * M_HEAD_DIM, ATTN_WIDTH + (h + 1) * M_HEAD_DIM)
                mix_ref[rows, mcols] = (jax.nn.sigmoid(om[rows, hcols[h]])
                                        * (y_h.T * mnorm_ref[:, hcols[h]])).astype(mix_ref.dtype)
    for p in range(M_PAIRS):
        c_ref[2 * p] = mem[p][:, :M_HEAD_DIM]
        c_ref[2 * p + 1] = mem[p][:, M_HEAD_DIM:]

    @pl.when(step == pl.num_programs(0) - 1)
    def _():
        kwt_ref[...] = kw_ref[...].T
        vwt_ref[...] = vw_ref[...].T
        for h in range(M_HEADS):
            ct_ref[h] = c_ref[h][:M_HEAD_DIM, :].T
            nt_ref[h:h + 1, :] = c_ref[h][M_HEAD_DIM:M_HEAD_DIM + 1, :]
        diag = jnp.sum(jnp.where(row8 == lane8, m_ref[...], 0.0), axis=0, keepdims=True)
        mt_ref[...] = diag[:, :M_HEADS]
        lane = lax.broadcasted_iota(jnp.int32, bias_ref.shape, 1)
        gate_bias = jnp.zeros(bias_ref.shape, F32)
        for h in range(M_HEADS):
            gate_bias = jnp.where(lane == h, bi_ref[0, h], jnp.where(lane == M_HEADS + h, bf_ref[0, h], gate_bias))
        bias_ref[...] = gate_bias


def _out_row_block(i):
    per_head = (ATTN_WIDTH // ATTN_HEADS) // WEIGHT_SLAB
    j, part = i // per_head, i % per_head
    head = (j % 2) * GROUP + j // 2
    return jnp.where(i < ATTN_HEADS * per_head, head * per_head + part, i)


def _mixer_prompt(sink, b_i, b_f, x, g_pre, w_in_t, mnorm, w_gate, w_up, w_down, w_out):
    m = x.shape[0]
    tm = MIXER_TILE
    steps = m // tm
    assert D_MODEL == steps * WEIGHT_SLAB and D_FF % (steps // 2) == 0 and (D_FF // (steps // 2)) % BF16_SUBLANES == 0
    down_slab = D_FF // (steps // 2)
    row = lambda n: pl.BlockSpec((tm, n), lambda i: (i, 0))
    whole = lambda shape: pl.BlockSpec(shape, lambda i: (0,) * len(shape))
    smem = pl.BlockSpec(memory_space=pltpu.SMEM)
    slab = lambda n: pl.BlockSpec((WEIGHT_SLAB, n), lambda i: (i, 0))
    down = pl.BlockSpec((down_slab, D_MODEL), lambda i: (i // 2, 0))
    c_shape = (M_HEADS, AUG_ROWS, M_HEAD_DIM)
    w_shape = (WINDOW, KV_WIDTH)
    wt_shape = (KV_WIDTH, WINDOW)
    ct_shape = (M_HEADS, M_HEAD_DIM, M_HEAD_DIM)
    n_shape = (M_HEADS, M_HEAD_DIM)
    b_shape = (1, GATE_PAD)
    s_shape = (SUBLANES, LANES)
    sds = jax.ShapeDtypeStruct
    return pl.pallas_call(
        _mixer_prompt_kernel,
        grid=(steps,),
        in_specs=[smem, smem, smem, row(D_MODEL), _const_spec((1, D_MODEL)), _const_spec(w_in_t.shape),
                  _const_spec((1, M_WIDTH)), slab(D_FF), slab(D_FF), down,
                  pl.BlockSpec((WEIGHT_SLAB, D_MODEL), lambda i: (_out_row_block(i), 0))],
        out_specs=[row(ATTN_WIDTH + M_WIDTH), whole(w_shape), whole(w_shape), whole(c_shape), whole(s_shape),
                   whole((D_MODEL, IN_PAD)), slab(D_FF), slab(D_FF), down, slab(D_MODEL),
                   whole(wt_shape), whole(wt_shape), whole(ct_shape), whole(n_shape),
                   whole((1, M_HEADS)), whole(b_shape)],
        out_shape=[sds((m, ATTN_WIDTH + M_WIDTH), BF16), sds(w_shape, F32), sds(w_shape, F32), sds(c_shape, F32),
                   sds(s_shape, F32), sds((D_MODEL, IN_PAD), BF16), sds(w_gate.shape, BF16), sds(w_up.shape, BF16),
                   sds(w_down.shape, BF16), sds(w_out.shape, BF16),
                   sds(wt_shape, F32), sds(wt_shape, F32), sds(ct_shape, F32), sds(n_shape, F32),
                   sds((1, M_HEADS), F32), sds(b_shape, F32)],
        compiler_params=_params(("arbitrary",)),
        name="mixer_prompt",
    )(sink, b_i, b_f, x, g_pre, w_in_t, mnorm, w_gate, w_up, w_down, w_out)


def _gates(g_blk, bias_row, cum):
    pre = g_blk + bias_row
    lane = lax.broadcasted_iota(jnp.int32, pre.shape, 1)
    a = jnp.where(lane < M_HEADS, pre, jax.nn.log_sigmoid(pre))
    b = jnp.dot(cum, a, precision=HIGHEST, preferred_element_type=F32)
    return a, b, a.T, b.T


def _col(x, j):
    return jnp.broadcast_to(x[:, j:j + 1], x.shape)


def _mlstm_gated_scores(q, k, a, b, at, bt, h, mask, m_prev, last):
    bc, ic = _col(b, M_HEADS + h), _col(a, h)
    br, ir = bt[M_HEADS + h:M_HEADS + h + 1, :], at[h:h + 1, :]
    d = jnp.where(mask, (bc - br) + ir, -jnp.inf)
    inter = bc + m_prev
    m_t = jnp.maximum(inter, jnp.max(d, axis=-1, keepdims=True))
    w_inter = jnp.exp(inter - m_t)
    sqk = jnp.exp(d - m_t) * lax.dot_general(q, k, (((1,), (1,)), ((), ())), preferred_element_type=F32)
    m_end = last(m_t)
    bl = last(bc)
    dec = jnp.exp(bl + m_prev - m_end)
    w_end = jnp.exp((bl - bc) + ic - m_end)
    kw = k.astype(F32) * w_end
    return w_inter, sqk, m_t, m_end, dec, kw


def _mlstm_sample_kernel(q_ref, k_ref, v_ref, om_ref, g_ref, m0_ref, bias_ref, mnorm_ref, c_ref, n_ref,
                         o_ref, c_out, n_out, m_out, *, dec_seq):
    t = dec_seq
    bb = SAMPLE_BATCH
    ln = bb * t
    r = lax.broadcasted_iota(jnp.int32, (ln, ln), 0)
    c = lax.broadcasted_iota(jnp.int32, (ln, ln), 1)
    shift = t.bit_length() - 1
    same = (r >> shift) == (c >> shift)
    mask = jnp.logical_and(same, c <= r)
    cum = mask.astype(F32)
    expand = (c == (r >> shift)).astype(F32)
    gather = ((c >> shift) == r).astype(F32)
    is_last = (r & (t - 1)) == t - 1
    is_first = (r & (t - 1)) == 0

    def last(x):
        y = jnp.where(is_last, x, 0.0)
        step = 1
        while step < t:
            y = y + pltpu.roll(y, ln - step, axis=0)
            step *= 2
        return y

    a, b, at, bt = _gates(g_ref[...], bias_ref[...], cum)
    m0 = m0_ref[...]
    lane = lax.broadcasted_iota(jnp.int32, (ln, LANES), 1)
    zpad = jnp.zeros((ln - bb, M_HEAD_DIM), F32)
    heads = range(M_HEADS)
    hcols = [slice(h * M_HEAD_DIM, (h + 1) * M_HEAD_DIM) for h in heads]
    qfs = [q_ref[:, hc] for hc in hcols]
    vs = [v_ref[:, hc] for hc in hcols]
    qcs = [jnp.concatenate(
        [jnp.dot(qfs[h][s * t:(s + 1) * t, :], c_ref[s * M_HEADS + h].astype(BF16).astype(F32),
                 preferred_element_type=F32) for s in range(bb)], axis=0) for h in heads]
    n_exps = [jnp.dot(expand, jnp.concatenate([n_ref[:, hc], zpad], axis=0), precision=HIGHEST,
                      preferred_element_type=F32) for hc in hcols]
    parts = [_mlstm_gated_scores(qfs[h].astype(BF16), k_ref[:, hcols[h]], a, b, at, bt, h, mask,
                                 _col(m0, h), last) for h in heads]
    intras = [jnp.dot(parts[h][1].astype(BF16), vs[h], preferred_element_type=F32) for h in heads]
    m_cols = jnp.zeros((ln, LANES), F32)
    kwts = []
    for h in heads:
        w_inter, sqk, m_t, m_end, dec, kw = parts[h]
        num = w_inter * qcs[h] + intras[h]
        nq = w_inter * jnp.sum(qfs[h] * n_exps[h], axis=-1, keepdims=True) + jnp.sum(sqk, axis=-1, keepdims=True)
        hh = num / jnp.maximum(jnp.abs(nq), jnp.exp(-m_t))
        o_ref[:, hcols[h]] = _head_out(hh, mnorm_ref[:, hcols[h]], om_ref[:, hcols[h]])
        kwts.append(kw.T.astype(BF16))
        m_cols = jnp.where(lane == h, m_end, m_cols)
    m_out[...] = m_cols
    for h in heads:
        dec = parts[h][4]
        for s in range(bb):
            lhs = jnp.where((c >> shift) == s, kwts[h], jnp.zeros_like(kwts[h]))
            upd = jnp.dot(lhs, vs[h], preferred_element_type=F32)
            c_out[s * M_HEADS + h] = dec[s * t:s * t + 1, :] * c_ref[s * M_HEADS + h] + upd
    for h in heads:
        dec, kw = parts[h][4], parts[h][5]
        n_new = jnp.dot(gather, jnp.where(is_first, dec * n_exps[h], 0.0) + kw, precision=HIGHEST,
                        preferred_element_type=F32)
        n_out[:, hcols[h]] = n_new[:bb, :]


def _mixer_sample_kernel(sink_ref, x_ref, g_ref, w_ref, ck_ref, cv_ref, m0_ref, bias_ref, mnorm_ref, c_ref, n_ref,
                         mix_ref, ko_ref, vo_ref, c_out, n_out, m_out, *proj_refs, dec_seq):
    step = pl.program_id(0)

    @pl.when(step == 0)
    def _():
        _inproj_kernel(x_ref, g_ref, w_ref, *proj_refs)

    tm = mix_ref.shape[0]
    rows = pl.ds(pl.multiple_of(step * tm, tm), tm)
    qa_ref, kn_ref, vn_ref, qm_ref, km_ref, vm_ref, om_ref, gt_ref = [r.at[rows, :] for r in proj_refs]
    _attn_sample_kernel(sink_ref, qa_ref, kn_ref, vn_ref, ck_ref, cv_ref, mix_ref.at[:, pl.ds(0, ATTN_WIDTH)],
                        ko_ref, vo_ref, dec_seq=dec_seq)
    _mlstm_sample_kernel(qm_ref, km_ref, vm_ref, om_ref, gt_ref, m0_ref, bias_ref, mnorm_ref, c_ref, n_ref,
                         mix_ref.at[:, pl.ds(ATTN_WIDTH, M_WIDTH)], c_out, n_out, m_out, dec_seq=dec_seq)


def _mixer_sample(sink, x, g_pre, w_pad, cache_k, cache_v, m0_rows, bias_row, mnorm, c_in, n_in, dec_seq):
    m = x.shape[0]
    bb = SAMPLE_BATCH
    tm = bb * dec_seq
    nb = m // dec_seq
    row = lambda n: pl.BlockSpec((tm, n), lambda i: (i, 0))
    whole = lambda shape: pl.BlockSpec(shape, lambda i: (0,) * len(shape))
    cache = pl.BlockSpec((bb, KV_HEADS, HEAD_DIM, WINDOW), lambda i: (i, 0, 0, 0))
    c_spec = pl.BlockSpec((bb * M_HEADS, M_HEAD_DIM, M_HEAD_DIM), lambda i: (i, 0, 0))
    n_spec = pl.BlockSpec((bb, M_WIDTH), lambda i: (i, 0))
    sds = jax.ShapeDtypeStruct
    return pl.pallas_call(
        functools.partial(_mixer_sample_kernel, dec_seq=dec_seq),
        grid=(nb // bb,),
        in_specs=[pl.BlockSpec(memory_space=pltpu.SMEM), _const_spec(x.shape), _const_spec((1, D_MODEL)),
                  _const_spec(w_pad.shape), cache, cache, row(LANES),
                  whole((1, GATE_PAD)), whole((1, M_WIDTH)), c_spec, n_spec],
        out_specs=[row(ATTN_WIDTH + M_WIDTH), cache, cache, c_spec, n_spec, row(LANES)],
        out_shape=[sds((m, ATTN_WIDTH + M_WIDTH), BF16), sds(cache_k.shape, F32), sds(cache_v.shape, F32),
                   sds(c_in.shape, F32), sds(n_in.shape, F32), sds((m, LANES), F32)],
        scratch_shapes=[pltpu.VMEM((m, n), dt) for n, dt in zip(PROJ_WIDTHS, PROJ_DTYPES)],
        compiler_params=_params(("arbitrary",)),
        name="mixer_sample",
    )(sink, x, g_pre, w_pad, cache_k, cache_v, m0_rows, bias_row, mnorm, c_in, n_in)


def _out_ffn_kernel(x_ref, mix_ref, wo_ref, g1_ref, g2_ref, wg_ref, wu_ref, wd_ref, g3_ref, o_ref,
                    group_rows=FFN_GROUP_ROWS, group_lag=FFN_GROUP_LAG, before_out=None, before_gate_up=None,
                    before_down=None):
    assert sum(group_rows) == x_ref.shape[0]
    starts = [sum(group_rows[:i]) for i in range(len(group_rows))]
    groups = [slice(r, r + n) for r, n in zip(starts, group_rows)]
    if before_out is not None:
        before_out()
    halves = [slice(rs.start + h * ((rs.stop - rs.start) // 2), rs.start + (h + 1) * ((rs.stop - rs.start) // 2))
              for rs in groups for h in range(2)]
    ys = [jnp.dot(mix_ref[rs, :], wo_ref[...], preferred_element_type=F32) for rs in halves]
    x1h = [x_ref[rs, :] + _rms(y, g1_ref[...]) for rs, y in zip(halves, ys)]
    fh = [_rms(x1, g2_ref[...]).astype(BF16) for x1 in x1h]
    x1s = [jnp.concatenate(x1h[2 * i:2 * i + 2], axis=0) for i in range(len(groups))]
    fs = [jnp.concatenate(fh[2 * i:2 * i + 2], axis=0) for i in range(len(groups))]
    accs = [None] * len(groups)
    acts = {}
    chunks = D_FF // FFN_CHUNK
    for rnd in range(chunks + max(group_lag) + 1):
        for i, f in enumerate(fs):
            c = rnd - group_lag[i]
            if 0 <= c < chunks:
                if before_gate_up is not None and group_lag[i] == 0:
                    before_gate_up(c)
                cols = slice(c * FFN_CHUNK, (c + 1) * FFN_CHUNK)
                g = jnp.dot(f, wg_ref[:, cols], preferred_element_type=F32)
                u = jnp.dot(f, wu_ref[:, cols], preferred_element_type=F32)
                acts[i, c] = (g * jax.nn.sigmoid(g) * u).astype(BF16)
        for i in range(len(groups)):
            c = rnd - group_lag[i] - 1
            if 0 <= c < chunks:
                if before_down is not None and group_lag[i] == 0:
                    before_down(c)
                part = jnp.dot(acts.pop((i, c)), wd_ref[c * FFN_CHUNK:(c + 1) * FFN_CHUNK, :],
                               preferred_element_type=F32)
                accs[i] = part if accs[i] is None else accs[i] + part
    for rs, x1, acc in zip(groups, x1s, accs):
        o_ref[rs, :] = x1 + _rms(acc, g3_ref[...])


def _out_ffn_streamed_kernel(x_ref, mix_ref, wo_hbm, g1_ref, g2_ref, wg_hbm, wu_hbm, wd_hbm, g3_ref, o_ref,
                             wo_ref, wg_ref, wu_ref, wd_ref, sems):
    chunks = D_FF // FFN_CHUNK
    cols = lambda c: pl.ds(c * FFN_CHUNK, FFN_CHUNK)
    out_copy = pltpu.make_async_copy(wo_hbm, wo_ref, sems.at[0])
    gate = [pltpu.make_async_copy(wg_hbm.at[:, cols(c)], wg_ref.at[:, cols(c)], sems.at[1 + 3 * c])
            for c in range(chunks)]
    up = [pltpu.make_async_copy(wu_hbm.at[:, cols(c)], wu_ref.at[:, cols(c)], sems.at[2 + 3 * c])
          for c in range(chunks)]
    down = [pltpu.make_async_copy(wd_hbm.at[cols(c), :], wd_ref.at[cols(c), :], sems.at[3 + 3 * c])
            for c in range(chunks)]
    out_copy.start()
    for c in range(chunks):
        gate[c].start()
        up[c].start()
        down[c].start()

    def before_gate_up(c):
        if c in FFN_WAIT_STAGES:
            stage_end = min([s for s in FFN_WAIT_STAGES if s > c] + [chunks])
            for j in range(c, stage_end):
                gate[j].wait()
                up[j].wait()
                down[j].wait()

    n_groups = x_ref.shape[0] // FFN_GROUP_ROWS[0]
    _out_ffn_kernel(x_ref, mix_ref, wo_ref, g1_ref, g2_ref, wg_ref, wu_ref, wd_ref, g3_ref, o_ref,
                    group_rows=(FFN_GROUP_ROWS[0],) * n_groups, group_lag=tuple(range(n_groups)),
                    before_out=out_copy.wait, before_gate_up=before_gate_up)


def _out_ffn_streamed(x, mix, w_out, g_post_mix, g_pre_ffn, w_gate, w_up, w_down, g_post_ffn):
    m = x.shape[0]
    chunks = D_FF // FFN_CHUNK
    whole = lambda shape: pl.BlockSpec(shape, lambda i: (0,) * len(shape))
    hbm = pl.BlockSpec(memory_space=pl.ANY)
    vec = whole((1, D_MODEL))
    return pl.pallas_call(
        _out_ffn_streamed_kernel,
        grid=(1,),
        in_specs=[whole((m, D_MODEL)), whole((m, ATTN_WIDTH + M_WIDTH)), hbm, vec, vec, hbm, hbm, hbm, vec],
        out_specs=whole((m, D_MODEL)),
        out_shape=jax.ShapeDtypeStruct((m, D_MODEL), F32),
        scratch_shapes=[pltpu.VMEM(w_out.shape, BF16), pltpu.VMEM(w_gate.shape, BF16), pltpu.VMEM(w_up.shape, BF16),
                        pltpu.VMEM(w_down.shape, BF16), pltpu.SemaphoreType.DMA((1 + 3 * chunks,))],
        compiler_params=_params(("arbitrary",)),
        name="out_ffn_streamed",
    )(x, mix, w_out, g_post_mix, g_pre_ffn, w_gate, w_up, w_down, g_post_ffn)


def _out_ffn(x, mix, w_out, g_post_mix, g_pre_ffn, w_gate, w_up, w_down, g_post_ffn):
    m = x.shape[0]
    tm = TOKEN_TILE
    row = lambda n: pl.BlockSpec((tm, n), lambda i: (i, 0))
    vec = _const_spec((1, D_MODEL))
    return pl.pallas_call(
        _out_ffn_kernel,
        grid=(m // tm,),
        in_specs=[row(D_MODEL), row(ATTN_WIDTH + M_WIDTH), _const_spec((D_MODEL, D_MODEL)), vec, vec,
                  _const_spec((D_MODEL, D_FF)), _const_spec((D_MODEL, D_FF)), _const_spec((D_FF, D_MODEL)), vec],
        out_specs=row(D_MODEL),
        out_shape=jax.ShapeDtypeStruct((m, D_MODEL), F32),
        compiler_params=_params(("parallel",)),
        name="out_ffn",
    )(x, mix, w_out, g_post_mix, g_pre_ffn, w_gate, w_up, w_down, g_post_ffn)


def _layer(xp, xs, cache_k, cache_v, state_c, state_n, state_m, w_in, b_i, b_f, attn_sink, m_norm, w_out,
           g_pre_mix, g_post_mix, g_pre_ffn, g_post_ffn, w_gate, w_up, w_down):
    bp, sp, _ = xp.shape
    bs, ts, _ = xs.shape
    assert bp == 1 and sp % TOKEN_TILE == 0 and TOKEN_TILE % WINDOW == 0
    assert ts & (ts - 1) == 0 and (bs * ts) % TOKEN_TILE == 0 and bs % SAMPLE_BATCH == 0

    row = lambda v: v.reshape(1, -1)
    sink = row(attn_sink)

    x2 = xp.reshape(sp, D_MODEL)
    mix, _, _, _, _, w_pad, wg, wu, wd, wo, k_wt, v_wt, c_p, n_p, m_p, bias_row = _mixer_prompt(
        sink, row(b_i), row(b_f), x2, row(g_pre_mix), w_in.T, row(m_norm), w_gate, w_up, w_down, w_out)
    ffn = (wo, row(g_post_mix), row(g_pre_ffn), wg, wu, wd, row(g_post_ffn))
    yp = _out_ffn(x2, mix, *ffn).reshape(xp.shape)
    window_major = lambda c: jnp.transpose(c, (0, 3, 1, 2))
    k_p = window_major(k_wt.reshape(1, KV_HEADS, HEAD_DIM, WINDOW))
    v_p = window_major(v_wt.reshape(1, KV_HEADS, HEAD_DIM, WINDOW))
    c_p = c_p.reshape(1, M_HEADS, M_HEAD_DIM, M_HEAD_DIM)
    n_p = n_p.reshape(1, M_HEADS, M_HEAD_DIM)

    x2 = xs.reshape(bs * ts, D_MODEL)
    feature_major = lambda c: jnp.transpose(c, (0, 2, 3, 1))
    m0_rows = jnp.pad(jnp.repeat(state_m, ts, axis=0), ((0, 0), (0, LANES - M_HEADS)))
    mix, k_s, v_s, c_s, n_s, m_rows = _mixer_sample(
        sink, x2, row(g_pre_mix), w_pad, feature_major(cache_k), feature_major(cache_v), m0_rows, bias_row,
        row(m_norm), state_c.reshape(bs * M_HEADS, M_HEAD_DIM, M_HEAD_DIM), state_n.reshape(bs, M_WIDTH), ts)
    ys = _out_ffn_streamed(x2, mix, *ffn).reshape(xs.shape)
    k_s, v_s = window_major(k_s), window_major(v_s)
    c_s = c_s.reshape(bs, M_HEADS, M_HEAD_DIM, M_HEAD_DIM)
    n_s = n_s.reshape(bs, M_HEADS, M_HEAD_DIM)
    m_s = m_rows[ts - 1::ts, :M_HEADS]
    return yp, ys, (k_p, v_p, c_p, n_p, m_p), (k_s, v_s, c_s, n_s, m_s)


def kernel(x_prompt, x_sample, cache_k, cache_v, state_C, state_n, state_m, w_in, b_i, b_f, attn_sink, m_norm,
           w_out, g_pre_mix, g_post_mix, g_pre_ffn, g_post_ffn, w_gate, w_up, w_down):
    depth = w_in.shape[0]
    xp, xs = x_prompt, x_sample
    prompt_states, sample_states = [], []
    for l in range(depth):
        xp, xs, st_p, st_s = _layer(xp, xs, cache_k[l], cache_v[l], state_C[l], state_n[l], state_m[l],
                                    w_in[l], b_i[l], b_f[l], attn_sink[l], m_norm[l], w_out[l],
                                    g_pre_mix[l], g_post_mix[l], g_pre_ffn[l], g_post_ffn[l],
                                    w_gate[l], w_up[l], w_down[l])
        prompt_states.append(st_p)
        sample_states.append(st_s)
    stack = lambda states, i: jnp.stack([s[i] for s in states], axis=0)
    return (xp, xs) + tuple(stack(prompt_states, i) for i in range(5)) + tuple(stack(sample_states, i) for i in range(5))
```
